```python
import math
import jax, jax.numpy as jnp
from jax import lax
import numpy as np

D_MODEL = 1024
BATCH = 16
SEQ = 256
DEPTH = 1
DEC_BATCH = 2
DEC_SEQ = 4096
PAST_LEN = 256

GRID_W = 64
S5_WIDTH = 512
S5_GROUP = 16
S5_GROUPS = S5_WIDTH // S5_GROUP
S5_STATE = 64
FFT_WIDTH = D_MODEL - S5_WIDTH
FFT_GROUPS = 4
FFT_GROUP = FFT_WIDTH // FFT_GROUPS
N_BRANCH = 2
IN_WIDTH = S5_WIDTH + FFT_WIDTH + N_BRANCH * D_MODEL
_FF_RAW = -(-8 * D_MODEL // 3)
D_FF = -(-_FF_RAW // 256) * 256
N_MOD = 6
EPS = 1e-6

kernel_name = "hybrid_s5_fnet_flow_step"


def rmsnorm(x, g):
    x32 = x.astype(jnp.float32)
    y = x32 * lax.rsqrt(jnp.mean(x32 * x32, axis=-1, keepdims=True) + EPS)
    return (y * g.astype(jnp.float32)).astype(x.dtype)


def adaln(cvec, w_ada, b_ada):
    m = jax.nn.silu(cvec) @ w_ada + b_ada
    return m.reshape(cvec.shape[0], N_MOD, D_MODEL)


def _cmul(ar, ai, br, bi):
    return ar * br - ai * bi, ar * bi + ai * br


def _scan_op(e1, e2):
    a1r, a1i, b1r, b1i = e1
    a2r, a2i, b2r, b2i = e2
    ar, ai = _cmul(a1r, a1i, a2r, a2i)
    br, bi = _cmul(a2r, a2i, b1r, b1i)
    return ar, ai, br + b2r, bi + b2i


def s5_discretise(lam_re, lam_im, log_step, b_re, b_im):
    step = jnp.exp(log_step)[:, None]
    mag = jnp.exp(lam_re * step)
    ab_re = mag * jnp.cos(lam_im * step)
    ab_im = mag * jnp.sin(lam_im * step)
    nr = ab_re - 1.0
    ni = ab_im
    den = lam_re * lam_re + lam_im * lam_im
    fr = (nr * lam_re + ni * lam_im) / den
    fi = (ni * lam_re - nr * lam_im) / den
    bb_re = fr[..., None] * b_re - fi[..., None] * b_im
    bb_im = fr[..., None] * b_im + fi[..., None] * b_re
    return ab_re, ab_im, bb_re, bb_im


def s5_direction(u, lam_re, lam_im, log_step, b_re, b_im, c_re, c_im, h0_re, h0_im, reverse):
    ab_re, ab_im, bb_re, bb_im = s5_discretise(lam_re, lam_im, log_step, b_re, b_im)
    bu_re = jnp.einsum("nlgh,gph->nlgp", u, bb_re)
    bu_im = jnp.einsum("nlgh,gph->nlgp", u, bb_im)
    a_re = jnp.broadcast_to(ab_re, bu_re.shape)
    a_im = jnp.broadcast_to(ab_im, bu_im.shape)
    acum_re, acum_im, s_re, s_im = lax.associative_scan(
        _scan_op, (a_re, a_im, bu_re, bu_im), reverse=reverse, axis=1)
    hr, hi = _cmul(acum_re, acum_im, h0_re[:, None], h0_im[:, None])
    s_re = s_re + hr
    s_im = s_im + hi
    y = jnp.einsum("nlgp,ghp->nlgh", s_re, c_re) - jnp.einsum("nlgp,ghp->nlgh", s_im, c_im)
    end = 0 if reverse else -1
    return y, s_re[:, end], s_im[:, end]


def s5_branch(u, h0, p):
    n, l, _ = u.shape
    f32 = jnp.float32
    v = u.astype(f32).reshape(n, l, S5_GROUPS, S5_GROUP)
    h0 = h0.astype(f32)
    y = p["s5_d"].astype(f32).reshape(S5_GROUPS, S5_GROUP) * v
    finals = []
    for d in range(2):
        y_d, f_re, f_im = s5_direction(
            v,
            p["s5_lambda_re"][d].astype(f32), p["s5_lambda_im"][d].astype(f32),
            p["s5_log_step"][d].astype(f32),
            p["s5_b_re"][d].astype(f32), p["s5_b_im"][d].astype(f32),
            p["s5_c_re"][d].astype(f32), p["s5_c_im"][d].astype(f32),
            h0[:, d, 0], h0[:, d, 1], reverse=(d == 1))
        y = y + y_d
        finals.append(jnp.stack([f_re, f_im], axis=1))
    y = y.reshape(n, l, S5_WIDTH).astype(u.dtype)
    z = jax.nn.gelu(y)
    z = z * jax.nn.sigmoid(z @ p["w_glu"] + p["b_glu"])
    return z, jnp.stack(finals, axis=1)


def fourier_branch(u):
    n, l, _ = u.shape
    v = u.astype(jnp.float32).reshape(n, l, FFT_GROUPS, FFT_GROUP)
    f = jnp.fft.fft2(v, axes=(1, 3), norm="ortho").real
    return f.reshape(n, l, FFT_WIDTH).astype(u.dtype)


def layer(x, mod, h0, p):
    shift1, scale1, gate1, shift2, scale2, gate2 = [mod[:, k][:, None] for k in range(N_MOD)]
    h = rmsnorm(x, p["norm1_g"]) * (1.0 + scale1) + shift1
    z = h @ p["w_in"]
    o = 0
    u_s5 = z[..., o:o + S5_WIDTH]; o += S5_WIDTH
    u_f = z[..., o:o + FFT_WIDTH]; o += FFT_WIDTH
    g_s5 = z[..., o:o + D_MODEL]; o += D_MODEL
    g_f = z[..., o:o + D_MODEL]
    y_s5, finals = s5_branch(u_s5, h0, p)
    y_f = fourier_branch(u_f)
    m = (jax.nn.sigmoid(g_s5) * (y_s5 @ p["w_proj_s5"])
         + jax.nn.sigmoid(g_f) * (y_f @ p["w_proj_fft"]))
    x = x + gate1 * (m @ p["w_out"])
    h2 = rmsnorm(x, p["norm2_g"]) * (1.0 + scale2) + shift2
    ff = (jax.nn.silu(h2 @ p["w_ffn_gate"]) * (h2 @ p["w_ffn_up"])) @ p["w_ffn_down"]
    x = x + gate2 * ff
    return x, finals


def setup_inputs(seed: int = 0) -> dict:
    key = jax.random.key(seed)
    ks = iter(jax.random.split(key, 40))
    f32 = jnp.float32
    D, G, P, H = D_MODEL, S5_GROUPS, S5_STATE, S5_GROUP

    def nrm(shape, scale):
        return jax.random.normal(next(ks), shape, f32) * scale

    lam_im_base = jnp.pi * jnp.arange(P, dtype=f32)
    return {
        "x_prompt": nrm((BATCH, SEQ, D), 1.0),
        "x_sample": nrm((DEC_BATCH, DEC_SEQ, D), 1.0),
        "state_s5": nrm((DEC_BATCH, DEPTH, 2, 2, G, P), 0.5),
        "c": nrm((DEC_BATCH, D), 1.0),
        "c_ctx": nrm((D,), 1.0),
        "norm1_g": 1.0 + nrm((DEPTH, D), 0.02),
        "norm2_g": 1.0 + nrm((DEPTH, D), 0.02),
        "w_ada": nrm((DEPTH, D, N_MOD * D), 0.5 * D ** -0.5),
        "b_ada": nrm((DEPTH, N_MOD * D), 0.01),
        "w_in": nrm((DEPTH, D, IN_WIDTH), D ** -0.5),
        "s5_lambda_re": -0.5 + nrm((DEPTH, 2, G, P), 0.01),
        "s5_lambda_im": lam_im_base + nrm((DEPTH, 2, G, P), 0.01),
        "s5_log_step": jax.random.uniform(next(ks), (DEPTH, 2, G), f32,
                                          math.log(1e-3), math.log(1e-1)),
        "s5_b_re": nrm((DEPTH, 2, G, P, H), (2 * H) ** -0.5),
        "s5_b_im": nrm((DEPTH, 2, G, P, H), (2 * H) ** -0.5),
        "s5_c_re": nrm((DEPTH, 2, G, H, P), P ** -0.5),
        "s5_c_im": nrm((DEPTH, 2, G, H, P), P ** -0.5),
        "s5_d": nrm((DEPTH, S5_WIDTH), 1.0),
        "w_glu": nrm((DEPTH, S5_WIDTH, S5_WIDTH), S5_WIDTH ** -0.5),
        "b_glu": nrm((DEPTH, S5_WIDTH), 0.01),
        "w_proj_s5": nrm((DEPTH, S5_WIDTH, D), S5_WIDTH ** -0.5),
        "w_proj_fft": nrm((DEPTH, FFT_WIDTH, D), FFT_WIDTH ** -0.5),
        "w_out": nrm((DEPTH, D, D), D ** -0.5),
        "w_ffn_gate": nrm((DEPTH, D, D_FF), D ** -0.5),
        "w_ffn_up": nrm((DEPTH, D, D_FF), D ** -0.5),
        "w_ffn_down": nrm((DEPTH, D_FF, D), D_FF ** -0.5),
        "final_norm_g": 1.0 + nrm((D,), 0.02),
    }


def reference(x_prompt, x_sample, state_s5, c, c_ctx, norm1_g, norm2_g, w_ada, b_ada, w_in,
              s5_lambda_re, s5_lambda_im, s5_log_step, s5_b_re, s5_b_im, s5_c_re, s5_c_im,
              s5_d, w_glu, b_glu, w_proj_s5, w_proj_fft, w_out, w_ffn_gate, w_ffn_up,
              w_ffn_down, final_norm_g):
    xp = x_prompt
    xs = x_sample
    n_ctx = x_prompt.shape[0]
    zero_state = jnp.zeros((n_ctx, 2, 2, S5_GROUPS, S5_STATE), jnp.float32)
    new_states = []
    for i in range(DEPTH):
        p = {
            "norm1_g": norm1_g[i], "norm2_g": norm2_g[i], "w_in": w_in[i],
            "s5_lambda_re": s5_lambda_re[i], "s5_lambda_im": s5_lambda_im[i],
            "s5_log_step": s5_log_step[i], "s5_b_re": s5_b_re[i], "s5_b_im": s5_b_im[i],
            "s5_c_re": s5_c_re[i], "s5_c_im": s5_c_im[i], "s5_d": s5_d[i],
            "w_glu": w_glu[i], "b_glu": b_glu[i], "w_proj_s5": w_proj_s5[i],
            "w_proj_fft": w_proj_fft[i], "w_out": w_out[i], "w_ffn_gate": w_ffn_gate[i],
            "w_ffn_up": w_ffn_up[i], "w_ffn_down": w_ffn_down[i],
        }
        mod_ctx = adaln(c_ctx[None], w_ada[i], b_ada[i])
        mod_lat = adaln(c, w_ada[i], b_ada[i])
        xp, finals_ctx = layer(xp, mod_ctx, zero_state, p)
        new_states.append(finals_ctx)
        xs, _ = layer(xs, mod_lat, state_s5[:, i], p)
    y_prompt = rmsnorm(xp, final_norm_g)
    y_sample = rmsnorm(xs, final_norm_g)
    new_state_s5 = jnp.stack(new_states, axis=1)
    return (y_prompt, y_sample, new_state_s5)
```

```python
import functools
import math

import numpy as np
import jax
import jax.numpy as jnp
from jax import lax
from jax.experimental import pallas as pl
from jax.experimental.pallas import tpu as pltpu

F32 = jnp.float32
BF16 = jnp.bfloat16
HIGHEST = lax.Precision.HIGHEST

D_MODEL = 1024
S5_WIDTH = 512
S5_GROUPS = 32
S5_GROUP = 16
S5_STATE = 64
FFT_WIDTH = 512
FFT_GROUPS = 4
FFT_GROUP = 128
D_FF = 2816
N_MOD = 6
EPS = 1e-6

CHUNK = 16
SEG_CHUNKS = 16
SEG = CHUNK * SEG_CHUNKS
GP = S5_GROUPS * S5_STATE
PAIRS = S5_GROUPS // 2
CH = CHUNK * S5_GROUP

VMEM_LIMIT = 56 * 1024 * 1024


def _cparams(sem):
    return pltpu.CompilerParams(dimension_semantics=sem, vmem_limit_bytes=VMEM_LIMIT)


def _ada_kernel(c_ref, w_ref, b_ref, o_ref):
    c = c_ref[...]
    s = c * jax.nn.sigmoid(c)
    o_ref[...] = jnp.dot(s, w_ref[...], preferred_element_type=F32, precision=HIGHEST) + b_ref[...]


def _adaln(cvec8, w_ada, b_ada):
    n_out = w_ada.shape[1]
    tn = 1024
    return pl.pallas_call(
        _ada_kernel,
        grid=(n_out // tn,),
        in_specs=[pl.BlockSpec((8, D_MODEL), lambda j: (0, 0)),
                  pl.BlockSpec((D_MODEL, tn), lambda j: (0, j)),
                  pl.BlockSpec((1, tn), lambda j: (0, j))],
        out_specs=pl.BlockSpec((8, tn), lambda j: (0, j)),
        out_shape=jax.ShapeDtypeStruct((8, n_out), F32),
        compiler_params=_cparams(("arbitrary",)),
        name="adaln",
    )(cvec8, w_ada, b_ada.reshape(1, n_out))


def _prep_kernel(lre_ref, lim_ref, lst_ref, bre_ref, bim_ref, cre_ref, cim_ref, dexp_ref, ind_ref,
                 apre_ref, apim_ref, gbre_ref, gbim_ref, care_ref, caimn_ref, t_ref,
                 bbre_scr, bbim_scr, prod_scr, k_scr):
    lre = lre_ref[...]
    lim = lim_ref[...]
    step = jnp.exp(lst_ref[...])
    mag = jnp.exp(lre * step)
    are = mag * jnp.cos(lim * step)
    aim = mag * jnp.sin(lim * step)
    nr = are - 1.0
    den = lre * lre + lim * lim
    fr = (nr * lre + aim * lim) / den
    fi = (aim * lre - nr * lim) / den
    for d in range(2):
        bbre_scr[d] = fr[d:d + 1] * bre_ref[d] - fi[d:d + 1] * bim_ref[d]
        bbim_scr[d] = fr[d:d + 1] * bim_ref[d] + fi[d:d + 1] * bre_ref[d]

    pr = [jnp.ones_like(are)]
    pi = [jnp.zeros_like(are)]
    for _ in range(CHUNK):
        r, i = pr[-1], pi[-1]
        pr.append(r * are - i * aim)
        pi.append(r * aim + i * are)
    a16r, a16i = pr[CHUNK], pi[CHUNK]
    qr, qi = jnp.ones_like(are), jnp.zeros_like(are)
    for c in range(SEG_CHUNKS + 1):
        apre_ref[c] = qr
        apim_ref[c] = qi
        qr, qi = qr * a16r - qi * a16i, qr * a16i + qi * a16r

    for d in range(2):
        for k in range(CHUNK + 1):
            r, i = pr[k][d:d + 1], pi[k][d:d + 1]
            if k < CHUNK:
                gbre_ref[d, k] = r * bbre_scr[d] - i * bbim_scr[d]
                gbim_ref[d, k] = r * bbim_scr[d] + i * bbre_scr[d]
            care_ref[d, k] = r * cre_ref[d] - i * cim_ref[d]
            caimn_ref[d, k] = -(r * cim_ref[d] + i * cre_ref[d])

    def body(it, carry):
        d = it // CHUNK
        k = it % CHUNK
        for h in range(S5_GROUP):
            cr = care_ref[d, k, pl.ds(h, 1), :]
            cn = caimn_ref[d, k, pl.ds(h, 1), :]
            prod_scr[h * S5_GROUP:(h + 1) * S5_GROUP, :] = cr * bbre_scr[d] + cn * bbim_scr[d]
        k_scr[it] = jnp.dot(prod_scr[...], ind_ref[...], preferred_element_type=F32, precision=HIGHEST)
        return carry

    lax.fori_loop(0, 2 * CHUNK, body, 0)

    t_ref[CHUNK - 1] = k_scr[0] + k_scr[CHUNK] + dexp_ref[...]
    for k in range(1, CHUNK):
        t_ref[CHUNK - 1 + k] = k_scr[k]
        t_ref[CHUNK - 1 - k] = k_scr[CHUNK + k]


def _s5_tables(lam_re, lam_im, log_step, b_re, b_im, c_re, c_im, s5_d):
    lre = lam_re.reshape(2, GP)
    lim = lam_im.reshape(2, GP)
    lst = jnp.repeat(log_step, S5_STATE, axis=-1)
    bre = b_re.transpose(0, 3, 1, 2).reshape(2, S5_GROUP, GP)
    bim = b_im.transpose(0, 3, 1, 2).reshape(2, S5_GROUP, GP)
    cre = c_re.transpose(0, 2, 1, 3).reshape(2, S5_GROUP, GP)
    cim = c_im.transpose(0, 2, 1, 3).reshape(2, S5_GROUP, GP)
    d_hg = jnp.pad(s5_d.reshape(S5_GROUPS, S5_GROUP).T, ((0, 0), (0, 128 - S5_GROUPS)))
    eye = jnp.asarray(np.eye(S5_GROUP, dtype=np.float32))
    dexp = (eye[:, :, None] * d_hg[:, None, :]).reshape(S5_GROUP * S5_GROUP, 128)
    ind_np = np.zeros((GP, 128), np.float32)
    ind_np[np.arange(GP), np.arange(GP) // S5_STATE] = 1.0
    ind = jnp.asarray(ind_np)

    tab = lambda n: jax.ShapeDtypeStruct((2, n, S5_GROUP, GP), F32)
    vm = pl.BlockSpec(memory_space=pltpu.VMEM)
    apre, apim, gbre, gbim, care, caimn, t = pl.pallas_call(
        _prep_kernel,
        in_specs=[vm] * 9,
        out_specs=[vm] * 7,
        out_shape=[jax.ShapeDtypeStruct((SEG_CHUNKS + 1, 2, GP), F32),
                   jax.ShapeDtypeStruct((SEG_CHUNKS + 1, 2, GP), F32),
                   tab(CHUNK), tab(CHUNK), tab(CHUNK + 1), tab(CHUNK + 1),
                   jax.ShapeDtypeStruct((2 * CHUNK - 1, S5_GROUP * S5_GROUP, 128), F32)],
        scratch_shapes=[pltpu.VMEM((2, S5_GROUP, GP), F32),
                        pltpu.VMEM((2, S5_GROUP, GP), F32),
                        pltpu.VMEM((S5_GROUP * S5_GROUP, GP), F32),
                        pltpu.VMEM((2 * CHUNK, S5_GROUP * S5_GROUP, 128), F32)],
        compiler_params=pltpu.CompilerParams(vmem_limit_bytes=VMEM_LIMIT),
        name="s5_tables",
    )(lre, lim, lst, bre, bim, cre, cim, dexp, ind)

    eye2 = jnp.asarray(np.eye(2, dtype=np.float32))
    tg = t[:, :, :S5_GROUPS].reshape(2 * CHUNK - 1, S5_GROUP, S5_GROUP, S5_GROUPS).transpose(3, 0, 1, 2)
    lag = (CHUNK - 1) + np.arange(CHUNK)[None, :] - np.arange(CHUNK)[:, None]
    m = tg[:, lag]
    m = m.transpose(0, 1, 4, 2, 3).reshape(S5_GROUPS, CH, CH).astype(BF16)
    gb = jnp.stack([jnp.stack([gbre[0, ::-1], gbre[1]]), jnp.stack([gbim[0, ::-1], gbim[1]])])
    wc = gb.reshape(2, 2, CHUNK, S5_GROUP, PAIRS, 2, S5_STATE).transpose(4, 5, 2, 3, 1, 0, 6)
    w = wc[:, :, :, :, :, :, None, :] * eye2[None, :, None, None, None, None, :, None]
    w = w.reshape(PAIRS, 2 * CH, 4 * 128).astype(BF16)
    ca = jnp.stack([jnp.stack([care[0, 1:], care[1, :0:-1]]), jnp.stack([caimn[0, 1:], caimn[1, :0:-1]])])
    ec = ca.reshape(2, 2, CHUNK, S5_GROUP, PAIRS, 2, S5_STATE).transpose(4, 1, 0, 5, 6, 2, 3)
    e = ec[:, :, :, :, :, None, :, :] * eye2[None, None, None, :, None, :, None, None]
    e = e.reshape(PAIRS, 4 * 128, 2 * CH).astype(BF16)
    return m, w, e, apre, apim


def _rms(x, g):
    return x * lax.rsqrt(jnp.mean(x * x, axis=-1, keepdims=True) + EPS) * g


def _inproj_kernel(x_ref, mod_ref, g_ref, win_ref, bdc_ref, bds_ref,
                   us_ref, fc_ref, fs_ref, gs_ref, gf_ref):
    mod = mod_ref[0]
    h = _rms(x_ref[...], g_ref[...]) * (1.0 + mod[1:2]) + mod[0:1]
    hb = h.astype(BF16)
    dot = functools.partial(jnp.dot, preferred_element_type=F32)
    us_ref[...] = dot(hb, win_ref[:, 0:S5_WIDTH]).astype(BF16)
    uf = dot(hb, win_ref[:, S5_WIDTH:2 * S5_WIDTH]).astype(BF16)
    fc_ref[...] = dot(uf, bdc_ref[...]).astype(BF16)
    fs_ref[...] = dot(uf, bds_ref[...]).astype(BF16)
    o = 2 * S5_WIDTH
    gs_ref[...] = jax.nn.sigmoid(dot(hb, win_ref[:, o:o + D_MODEL])).astype(BF16)
    gf_ref[...] = jax.nn.sigmoid(dot(hb, win_ref[:, o + D_MODEL:o + 2 * D_MODEL])).astype(BF16)


def _channel_dft_mats():
    j = np.arange(FFT_GROUP)
    ang = 2.0 * np.pi * ((j[:, None] * j[None, :]) % FFT_GROUP) / FFT_GROUP
    blk_c = np.cos(ang) / math.sqrt(FFT_GROUP)
    blk_s = np.sin(ang) / math.sqrt(FFT_GROUP)
    bdc = np.kron(np.eye(FFT_GROUPS), blk_c)
    bds = np.kron(np.eye(FFT_GROUPS), blk_s)
    return jnp.asarray(bdc, F32).astype(BF16), jnp.asarray(bds, F32).astype(BF16)


def _const_spec(shape):
    nd = len(shape)
    return pl.BlockSpec(shape, lambda i: (0,) * nd, pipeline_mode=pl.Buffered(1))


def _inproj(x2d, mod, mod_row, norm_g, w_in_b, tm):
    t = x2d.shape[0]
    bdc, bds = _channel_dft_mats()
    tok = lambda w: pl.BlockSpec((tm, w), lambda i: (i, 0))
    out = lambda w: jax.ShapeDtypeStruct((t, w), BF16)
    return pl.pallas_call(
        _inproj_kernel,
        grid=(t // tm,),
        in_specs=[tok(D_MODEL),
                  pl.BlockSpec((1, N_MOD, D_MODEL), lambda i: (mod_row(i), 0, 0)),
                  _const_spec((1, D_MODEL)),
                  _const_spec(w_in_b.shape),
                  _const_spec(bdc.shape),
                  _const_spec(bds.shape)],
        out_specs=[tok(S5_WIDTH), tok(FFT_WIDTH), tok(FFT_WIDTH), tok(D_MODEL), tok(D_MODEL)],
        out_shape=[out(S5_WIDTH), out(FFT_WIDTH), out(FFT_WIDTH), out(D_MODEL), out(D_MODEL)],
        compiler_params=_cparams(("parallel",)),
        name="inproj",
    )(x2d, mod, norm_g.reshape(1, D_MODEL), w_in_b, bdc, bds)


N_PROMPT_SEG = 16
N_SAMPLE_SEG = 32
N_SEG = N_PROMPT_SEG + N_SAMPLE_SEG
ROWS = SEG_CHUNKS * N_SEG


def _cmul_add(ar, ai, hr, hi, sr, si):
    return ar * hr - ai * hi + sr, ar * hi + ai * hr + si


def _s5_kernel(x_ref, m_ref, w_ref, e_ref, apre_ref, apim_ref, h0_ref,
               y_ref, fin_ref, s_scr, hin_scr, f_scr, hs_scr):
    dot = functools.partial(jnp.dot, preferred_element_type=F32)
    x0 = x_ref[0]
    x1 = x_ref[1]
    s_scr[...] = dot(x0, w_ref[0, 0:CH, :]) + dot(x1, w_ref[0, CH:2 * CH, :])

    blk = lambda c: pl.ds(c * N_SEG, N_SEG)
    zero = jnp.zeros((N_SEG, 128), F32)
    ar, ai = apre_ref[1, 0:1, :], apim_ref[1, 0:1, :]
    hr, hi = zero, zero
    for c in range(SEG_CHUNKS):
        hin_scr[0, blk(c), :] = hr
        hin_scr[1, blk(c), :] = hi
        hr, hi = _cmul_add(ar, ai, hr, hi, s_scr[blk(c), 0:128], s_scr[blk(c), 128:256])
    fin_ref[0, 0] = hr[0:N_PROMPT_SEG]
    fin_ref[0, 1] = hi[0:N_PROMPT_SEG]
    f_scr[0] = hr[N_PROMPT_SEG:]
    f_scr[1] = hi[N_PROMPT_SEG:]
    br, bi = apre_ref[1, 1:2, :], apim_ref[1, 1:2, :]
    gr, gi = zero, zero
    for c in range(SEG_CHUNKS - 1, -1, -1):
        hin_scr[2, blk(c), :] = gr
        hin_scr[3, blk(c), :] = gi
        gr, gi = _cmul_add(br, bi, gr, gi, s_scr[blk(c), 256:384], s_scr[blk(c), 384:512])
    fin_ref[0, 2] = gr[0:N_PROMPT_SEG]
    fin_ref[0, 3] = gi[0:N_PROMPT_SEG]
    f_scr[2] = gr[N_PROMPT_SEG:]
    f_scr[3] = gi[N_PROMPT_SEG:]

    nseq = N_SAMPLE_SEG // SEG_CHUNKS
    a2r, a2i = apre_ref[SEG_CHUNKS, 0:1, :], apim_ref[SEG_CHUNKS, 0:1, :]
    hr, hi = h0_ref[0], h0_ref[1]
    for j in range(SEG_CHUNKS):
        rows = pl.ds(j * nseq, nseq)
        hs_scr[0, rows, :] = hr
        hs_scr[1, rows, :] = hi
        hr, hi = _cmul_add(a2r, a2i, hr, hi, f_scr[0, rows, :], f_scr[1, rows, :])
    b2r, b2i = apre_ref[SEG_CHUNKS, 1:2, :], apim_ref[SEG_CHUNKS, 1:2, :]
    gr, gi = h0_ref[2], h0_ref[3]
    for j in range(SEG_CHUNKS - 1, -1, -1):
        rows = pl.ds(j * nseq, nseq)
        hs_scr[2, rows, :] = gr
        hs_scr[3, rows, :] = gi
        gr, gi = _cmul_add(b2r, b2i, gr, gi, f_scr[2, rows, :], f_scr[3, rows, :])
    for c in range(SEG_CHUNKS):
        rows = pl.ds(c * N_SEG + N_PROMPT_SEG, N_SAMPLE_SEG)
        pr, pi = apre_ref[c, 0:1, :], apim_ref[c, 0:1, :]
        hr, hi = _cmul_add(pr, pi, hs_scr[0], hs_scr[1], hin_scr[0, rows, :], hin_scr[1, rows, :])
        hin_scr[0, rows, :] = hr
        hin_scr[1, rows, :] = hi
        cb = SEG_CHUNKS - 1 - c
        pr, pi = apre_ref[cb, 1:2, :], apim_ref[cb, 1:2, :]
        gr, gi = _cmul_add(pr, pi, hs_scr[2], hs_scr[3], hin_scr[2, rows, :], hin_scr[3, rows, :])
        hin_scr[2, rows, :] = gr
        hin_scr[3, rows, :] = gi

    carry = dot(hin_scr[0].astype(BF16), e_ref[0, 0:128, :])
    for k in range(1, 4):
        carry = carry + dot(hin_scr[k].astype(BF16), e_ref[0, k * 128:(k + 1) * 128, :])
    y_ref[0, :, 0:CH] = dot(x0, m_ref[0]) + carry[:, 0:CH]
    y_ref[0, :, CH:2 * CH] = dot(x1, m_ref[1]) + carry[:, CH:2 * CH]


def _s5_conv(xg, m, w, e, apre, apim, h0):
    return pl.pallas_call(
        _s5_kernel,
        grid=(PAIRS,),
        in_specs=[pl.BlockSpec((2, ROWS, CH), lambda q: (q, 0, 0)),
                  pl.BlockSpec((2, CH, CH), lambda q: (q, 0, 0)),
                  pl.BlockSpec((1, 2 * CH, 512), lambda q: (q, 0, 0)),
                  pl.BlockSpec((1, 512, 2 * CH), lambda q: (q, 0, 0)),
                  pl.BlockSpec((SEG_CHUNKS + 1, 2, 128), lambda q: (0, 0, q)),
                  pl.BlockSpec((SEG_CHUNKS + 1, 2, 128), lambda q: (0, 0, q)),
                  pl.BlockSpec((4, N_SAMPLE_SEG // SEG_CHUNKS, 128), lambda q: (0, 0, q))],
        out_specs=[pl.BlockSpec((1, ROWS, 2 * CH), lambda q: (q, 0, 0)),
                   pl.BlockSpec((1, 4, N_PROMPT_SEG, 128), lambda q: (q, 0, 0, 0))],
        out_shape=[jax.ShapeDtypeStruct((PAIRS, ROWS, 2 * CH), F32),
                   jax.ShapeDtypeStruct((PAIRS, 4, N_PROMPT_SEG, 128), F32)],
        scratch_shapes=[pltpu.VMEM((ROWS, 512), F32),
                        pltpu.VMEM((4, ROWS, 128), F32),
                        pltpu.VMEM((4, N_SAMPLE_SEG, 128), F32),
                        pltpu.VMEM((4, N_SAMPLE_SEG, 128), F32)],
        compiler_params=_cparams(("parallel",)),
        name="s5_conv",
    )(xg, m, w, e, apre, apim, h0)


def _pos_dft_kernel(cm_ref, sm_ref, fc_ref, fs_ref, o_ref):
    dot = functools.partial(jnp.dot, preferred_element_type=F32)
    o_ref[0] = (dot(cm_ref[...], fc_ref[0]) - dot(sm_ref[...], fs_ref[0])).astype(BF16)


def _pos_dft(fc, fs, tk):
    n, l, wdt = fc.shape
    k = jnp.arange(l, dtype=jnp.int32)
    ang = ((k[:, None] * k[None, :]) % l).astype(F32) * (2.0 * math.pi / l)
    cm = (jnp.cos(ang) * (1.0 / math.sqrt(l))).astype(BF16)
    sm = (jnp.sin(ang) * (1.0 / math.sqrt(l))).astype(BF16)
    return pl.pallas_call(
        _pos_dft_kernel,
        grid=(n, l // tk),
        in_specs=[pl.BlockSpec((tk, l), lambda b, i: (i, 0)),
                  pl.BlockSpec((tk, l), lambda b, i: (i, 0)),
                  pl.BlockSpec((1, l, wdt), lambda b, i: (b, 0, 0)),
                  pl.BlockSpec((1, l, wdt), lambda b, i: (b, 0, 0))],
        out_specs=pl.BlockSpec((1, tk, wdt), lambda b, i: (b, i, 0)),
        out_shape=jax.ShapeDtypeStruct((n, l, wdt), BF16),
        compiler_params=_cparams(("parallel", "parallel")),
        name="pos_dft",
    )(cm, sm, fc, fs)


def _mix_ffn_kernel(x_ref, mod_ref, ys_ref, yf_ref, gs_ref, gf_ref,
                    wglu_ref, bglu_ref, wps_ref, wpf_ref, wout_ref, n2_ref,
                    wg_ref, wu_ref, wd_ref, fn_ref, o_ref):
    dot = functools.partial(jnp.dot, preferred_element_type=F32)
    mod = mod_ref[0]
    z = jax.nn.gelu(ys_ref[...])
    z = z * jax.nn.sigmoid(dot(z.astype(BF16), wglu_ref[...]) + bglu_ref[...])
    m = (gs_ref[...].astype(F32) * dot(z.astype(BF16), wps_ref[...])
         + gf_ref[...].astype(F32) * dot(yf_ref[...], wpf_ref[...]))
    x1 = x_ref[...] + mod[2:3] * dot(m.astype(BF16), wout_ref[...])
    h2 = (_rms(x1, n2_ref[...]) * (1.0 + mod[4:5]) + mod[3:4]).astype(BF16)
    gate = dot(h2, wg_ref[...])
    up = dot(h2, wu_ref[...])
    ff = dot((gate * jax.nn.sigmoid(gate) * up).astype(BF16), wd_ref[...])
    x2 = x1 + mod[5:6] * ff
    o_ref[...] = _rms(x2, fn_ref[...])


def _mix_ffn(x2d, mod, mod_row, ys, yf, gs, gf, wts, tm):
    t = x2d.shape[0]
    tok = lambda w: pl.BlockSpec((tm, w), lambda i: (i, 0))
    return pl.pallas_call(
        _mix_ffn_kernel,
        grid=(t // tm,),
        in_specs=[tok(D_MODEL),
                  pl.BlockSpec((1, N_MOD, D_MODEL), lambda i: (mod_row(i), 0, 0)),
                  tok(S5_WIDTH), tok(FFT_WIDTH), tok(D_MODEL), tok(D_MODEL)]
                 + [_const_spec(w.shape) for w in wts],
        out_specs=tok(D_MODEL),
        out_shape=jax.ShapeDtypeStruct((t, D_MODEL), F32),
        compiler_params=_cparams(("parallel",)),
        name="mix_ffn",
    )(x2d, mod, ys, yf, gs, gf, *wts)


def kernel(x_prompt, x_sample, state_s5, c, c_ctx, norm1_g, norm2_g, w_ada, b_ada, w_in,
           s5_lambda_re, s5_lambda_im, s5_log_step, s5_b_re, s5_b_im, s5_c_re, s5_c_im,
           s5_d, w_glu, b_glu, w_proj_s5, w_proj_fft, w_out, w_ffn_gate, w_ffn_up,
           w_ffn_down, final_norm_g):
    nb, sl, _ = x_prompt.shape
    db, dl, _ = x_sample.shape
    assert w_in.shape[0] == 1 and sl == SEG and nb == N_PROMPT_SEG and db * (dl // SEG) == N_SAMPLE_SEG
    nseg_s = dl // SEG

    cvec = jnp.concatenate([c_ctx[None], c, jnp.zeros((8 - 1 - db, D_MODEL), F32)], axis=0)
    mod = _adaln(cvec, w_ada[0], b_ada[0]).reshape(8, N_MOD, D_MODEL)

    m, w, e, apre, apim = _s5_tables(s5_lambda_re[0], s5_lambda_im[0], s5_log_step[0], s5_b_re[0],
                                     s5_b_im[0], s5_c_re[0], s5_c_im[0], s5_d[0])

    tm = 512
    xp = x_prompt.reshape(nb * sl, D_MODEL)
    xs = x_sample.reshape(db * dl, D_MODEL)
    row_p = lambda i: 0
    row_s = lambda i: 1 + i // (dl // tm)
    w_in_b = w_in[0].astype(BF16)
    us_p, fc_p, fs_p, gs_p, gf_p = _inproj(xp, mod, row_p, norm1_g[0], w_in_b, tm)
    us_s, fc_s, fs_s, gs_s, gf_s = _inproj(xs, mod, row_s, norm1_g[0], w_in_b, tm)

    xg_p = us_p.reshape(nb, SEG_CHUNKS, CHUNK, S5_GROUPS, S5_GROUP).transpose(3, 1, 0, 2, 4)
    xg_s = us_s.reshape(db, nseg_s, SEG_CHUNKS, CHUNK, S5_GROUPS, S5_GROUP).transpose(4, 2, 1, 0, 3, 5)
    xg = jnp.concatenate([xg_p, xg_s.reshape(S5_GROUPS, SEG_CHUNKS, N_SAMPLE_SEG, CHUNK, S5_GROUP)], axis=2)
    xg = xg.reshape(S5_GROUPS, ROWS, CH)
    h0 = state_s5[:, 0].astype(F32).transpose(1, 2, 0, 3, 4).reshape(4, db, GP)
    y, fin = _s5_conv(xg, m, w, e, apre, apim, h0)
    y = y.reshape(PAIRS, SEG_CHUNKS, N_SEG, 2, CHUNK, S5_GROUP)
    ys_p = y[:, :, :nb].transpose(2, 1, 4, 0, 3, 5).reshape(nb * sl, S5_WIDTH)
    ys_s = y[:, :, nb:].reshape(PAIRS, SEG_CHUNKS, nseg_s, db, 2, CHUNK, S5_GROUP)
    ys_s = ys_s.transpose(3, 2, 1, 5, 0, 4, 6).reshape(db * dl, S5_WIDTH)
    new_state = fin.reshape(PAIRS, 2, 2, nb, 2, S5_STATE).transpose(3, 1, 2, 0, 4, 5)
    new_state = new_state.reshape(nb, 1, 2, 2, S5_GROUPS, S5_STATE)

    yf_p = _pos_dft(fc_p.reshape(nb, sl, FFT_WIDTH), fs_p.reshape(nb, sl, FFT_WIDTH), sl)
    yf_s = _pos_dft(fc_s.reshape(db, dl, FFT_WIDTH), fs_s.reshape(db, dl, FFT_WIDTH), 512)

    wts = (w_glu[0].astype(BF16), b_glu[0].reshape(1, S5_WIDTH), w_proj_s5[0].astype(BF16),
           w_proj_fft[0].astype(BF16), w_out[0].astype(BF16), norm2_g[0].reshape(1, D_MODEL),
           w_ffn_gate[0].astype(BF16), w_ffn_up[0].astype(BF16), w_ffn_down[0].astype(BF16),
           final_norm_g.reshape(1, D_MODEL))
    tm2 = 256
    row_s2 = lambda i: 1 + i // (dl // tm2)
    y_p = _mix_ffn(xp, mod, row_p, ys_p, yf_p.reshape(nb * sl, FFT_WIDTH), gs_p, gf_p, wts, tm2)
    y_s = _mix_ffn(xs, mod, row_s2, ys_s, yf_s.reshape(db * dl, FFT_WIDTH), gs_s, gf_s, wts, tm2)
    return (y_p.reshape(nb, sl, D_MODEL), y_s.reshape(db, dl, D_MODEL), new_state)
```

```python
import functools
import math

import numpy as np
import jax
import jax.numpy as jnp
from jax import lax
from jax.experimental import pallas as pl
from jax.experimental.pallas import tpu as pltpu

F32 = jnp.float32
BF16 = jnp.bfloat16
HIGHEST = lax.Precision.HIGHEST

D_MODEL = 1024
S5_WIDTH = 512
S5_GROUPS = 32
S5_GROUP = 16
S5_STATE = 64
FFT_WIDTH = 512
FFT_GROUPS = 4
FFT_GROUP = 128
D_FF = 2816
N_MOD = 6
EPS = 1e-6

LANES = 128
CHUNK = 16
SEG_CHUNKS = 16
SEG = CHUNK * SEG_CHUNKS
GP = S5_GROUPS * S5_STATE
PAIRS = S5_GROUPS // 2
CH = CHUNK * S5_GROUP
OCT = LANES // S5_GROUP

VMEM_LIMIT = 56 * 1024 * 1024


def _cparams(sem):
    return pltpu.CompilerParams(dimension_semantics=sem, vmem_limit_bytes=VMEM_LIMIT)


def _ada_kernel(c_ref, w_ref, b_ref, o_ref):
    c = c_ref[...]
    s = c * jax.nn.sigmoid(c)
    o_ref[...] = jnp.dot(s, w_ref[...], preferred_element_type=F32, precision=HIGHEST) + b_ref[...]


def _adaln(cvec8, w_ada, b_ada):
    n_out = w_ada.shape[1]
    tn = 1024
    return pl.pallas_call(
        _ada_kernel,
        grid=(n_out // tn,),
        in_specs=[pl.BlockSpec((8, D_MODEL), lambda j: (0, 0)),
                  pl.BlockSpec((D_MODEL, tn), lambda j: (0, j)),
                  pl.BlockSpec((1, tn), lambda j: (0, j))],
        out_specs=pl.BlockSpec((8, tn), lambda j: (0, j)),
        out_shape=jax.ShapeDtypeStruct((8, n_out), F32),
        compiler_params=_cparams(("arbitrary",)),
        name="adaln",
    )(cvec8, w_ada, b_ada.reshape(1, n_out))


def _prep_kernel(lre_ref, lim_ref, lst_ref, bre_ref, bim_ref, cre_ref, cim_ref, dexp_ref, ind_ref,
                 apre_ref, apim_ref, gbre_ref, gbim_ref, care_ref, caimn_ref, t_ref,
                 bbre_scr, bbim_scr, prod_scr, k_scr):
    lre = lre_ref[...]
    lim = lim_ref[...]
    step = jnp.exp(lst_ref[...])
    mag = jnp.exp(lre * step)
    are = mag * jnp.cos(lim * step)
    aim = mag * jnp.sin(lim * step)
    nr = are - 1.0
    den = lre * lre + lim * lim
    fr = (nr * lre + aim * lim) / den
    fi = (aim * lre - nr * lim) / den
    for d in range(2):
        bbre_scr[d] = fr[d:d + 1] * bre_ref[d] - fi[d:d + 1] * bim_ref[d]
        bbim_scr[d] = fr[d:d + 1] * bim_ref[d] + fi[d:d + 1] * bre_ref[d]

    pr = [jnp.ones_like(are)]
    pi = [jnp.zeros_like(are)]
    for _ in range(CHUNK):
        r, i = pr[-1], pi[-1]
        pr.append(r * are - i * aim)
        pi.append(r * aim + i * are)
    a16r, a16i = pr[CHUNK], pi[CHUNK]
    qr, qi = jnp.ones_like(are), jnp.zeros_like(are)
    for c in range(SEG_CHUNKS + 1):
        apre_ref[c] = qr
        apim_ref[c] = qi
        qr, qi = qr * a16r - qi * a16i, qr * a16i + qi * a16r

    for d in range(2):
        for k in range(CHUNK + 1):
            r, i = pr[k][d:d + 1], pi[k][d:d + 1]
            if k < CHUNK:
                gbre_ref[d, k] = r * bbre_scr[d] - i * bbim_scr[d]
                gbim_ref[d, k] = r * bbim_scr[d] + i * bbre_scr[d]
            care_ref[d, k] = r * cre_ref[d] - i * cim_ref[d]
            caimn_ref[d, k] = -(r * cim_ref[d] + i * cre_ref[d])

    def body(it, carry):
        d = it // CHUNK
        k = it % CHUNK
        for h in range(S5_GROUP):
            cr = care_ref[d, k, pl.ds(h, 1), :]
            cn = caimn_ref[d, k, pl.ds(h, 1), :]
            prod_scr[h * S5_GROUP:(h + 1) * S5_GROUP, :] = cr * bbre_scr[d] + cn * bbim_scr[d]
        k_scr[it] = jnp.dot(prod_scr[...], ind_ref[...], preferred_element_type=F32, precision=HIGHEST)
        return carry

    lax.fori_loop(0, 2 * CHUNK, body, 0)

    t_ref[CHUNK - 1] = k_scr[0] + k_scr[CHUNK] + dexp_ref[...]
    for k in range(1, CHUNK):
        t_ref[CHUNK - 1 + k] = k_scr[k]
        t_ref[CHUNK - 1 - k] = k_scr[CHUNK + k]


def _s5_tables(lam_re, lam_im, log_step, b_re, b_im, c_re, c_im, s5_d):
    lre = lam_re.reshape(2, GP)
    lim = lam_im.reshape(2, GP)
    lst = jnp.repeat(log_step, S5_STATE, axis=-1)
    bre = b_re.transpose(0, 3, 1, 2).reshape(2, S5_GROUP, GP)
    bim = b_im.transpose(0, 3, 1, 2).reshape(2, S5_GROUP, GP)
    cre = c_re.transpose(0, 2, 1, 3).reshape(2, S5_GROUP, GP)
    cim = c_im.transpose(0, 2, 1, 3).reshape(2, S5_GROUP, GP)
    d_hg = jnp.pad(s5_d.reshape(S5_GROUPS, S5_GROUP).T, ((0, 0), (0, LANES - S5_GROUPS)))
    eye = jnp.asarray(np.eye(S5_GROUP, dtype=np.float32))
    dexp = (eye[:, :, None] * d_hg[:, None, :]).reshape(S5_GROUP * S5_GROUP, LANES)
    ind_np = np.zeros((GP, LANES), np.float32)
    ind_np[np.arange(GP), np.arange(GP) // S5_STATE] = 1.0
    ind = jnp.asarray(ind_np)

    tab = lambda n: jax.ShapeDtypeStruct((2, n, S5_GROUP, GP), F32)
    vm = pl.BlockSpec(memory_space=pltpu.VMEM)
    apre, apim, gbre, gbim, care, caimn, t = pl.pallas_call(
        _prep_kernel,
        in_specs=[vm] * 9,
        out_specs=[vm] * 7,
        out_shape=[jax.ShapeDtypeStruct((SEG_CHUNKS + 1, 2, GP), F32),
                   jax.ShapeDtypeStruct((SEG_CHUNKS + 1, 2, GP), F32),
                   tab(CHUNK), tab(CHUNK), tab(CHUNK + 1), tab(CHUNK + 1),
                   jax.ShapeDtypeStruct((2 * CHUNK - 1, S5_GROUP * S5_GROUP, LANES), F32)],
        scratch_shapes=[pltpu.VMEM((2, S5_GROUP, GP), F32),
                        pltpu.VMEM((2, S5_GROUP, GP), F32),
                        pltpu.VMEM((S5_GROUP * S5_GROUP, GP), F32),
                        pltpu.VMEM((2 * CHUNK, S5_GROUP * S5_GROUP, LANES), F32)],
        compiler_params=pltpu.CompilerParams(vmem_limit_bytes=VMEM_LIMIT),
        name="s5_tables",
    )(lre, lim, lst, bre, bim, cre, cim, dexp, ind)

    eye2 = jnp.asarray(np.eye(2, dtype=np.float32))
    tg = t[:, :, :S5_GROUPS].reshape(2 * CHUNK - 1, S5_GROUP, S5_GROUP, S5_GROUPS).transpose(3, 0, 1, 2)
    lag = (CHUNK - 1) + np.arange(CHUNK)[None, :] - np.arange(CHUNK)[:, None]
    m = tg[:, lag]
    m = m.transpose(0, 1, 4, 2, 3).reshape(S5_GROUPS, CH, CH).astype(BF16)
    gb = jnp.stack([jnp.stack([gbre[0, ::-1], gbre[1]]), jnp.stack([gbim[0, ::-1], gbim[1]])])
    wc = gb.reshape(2, 2, CHUNK, S5_GROUP, PAIRS, 2, S5_STATE).transpose(4, 5, 2, 3, 1, 0, 6)
    w = wc[:, :, :, :, :, :, None, :] * eye2[None, :, None, None, None, None, :, None]
    w = w.reshape(PAIRS, 2 * CH, 4 * LANES).astype(BF16)
    ca = jnp.stack([jnp.stack([care[0, 1:], care[1, :0:-1]]), jnp.stack([caimn[0, 1:], caimn[1, :0:-1]])])
    ec = ca.reshape(2, 2, CHUNK, S5_GROUP, PAIRS, 2, S5_STATE).transpose(4, 1, 0, 5, 6, 2, 3)
    e = ec[:, :, :, :, :, None, :, :] * eye2[None, None, None, :, None, :, None, None]
    e = e.reshape(PAIRS, 4 * LANES, 2 * CH).astype(BF16)
    return m, w, e, apre, apim


def _rms(x, g):
    return x * lax.rsqrt(jnp.mean(x * x, axis=-1, keepdims=True) + EPS) * g


def _lane_slot_masks(rows):
    slot = lax.broadcasted_iota(jnp.int32, (rows, LANES), 1) // S5_GROUP
    return [slot == s for s in range(OCT)]


def _inproj_kernel(x_ref, mod_ref, g_ref, win_ref, bdc_ref, bds_ref,
                   xg_ref, fc_ref, fs_ref, gs_ref, gf_ref, h_scr, hb_scr, zs_scr):
    tm = x_ref.shape[0]
    mod = mod_ref[0]
    h = _rms(x_ref[...], g_ref[...]) * (1.0 + mod[1:2]) + mod[0:1]
    for k in range(D_MODEL // LANES):
        h_scr[k] = h[:, k * LANES:(k + 1) * LANES]
    for j in range(tm // SEG):
        for s in range(CHUNK):
            r0 = j * SEG + s * SEG_CHUNKS
            for k in range(D_MODEL // LANES):
                hb_scr[r0:r0 + SEG_CHUNKS, k * LANES:(k + 1) * LANES] = (
                    h_scr[k, pl.ds(j * SEG + s, SEG_CHUNKS, stride=CHUNK), :].astype(BF16))
    hb = hb_scr[...]
    dot = functools.partial(jnp.dot, preferred_element_type=F32)
    zs_scr[...] = dot(hb, win_ref[:, 0:S5_WIDTH])
    masks = _lane_slot_masks(SEG_CHUNKS)
    for j in range(tm // SEG):
        for b in range(S5_WIDTH // LANES):
            z = [zs_scr[j * SEG + s * SEG_CHUNKS:j * SEG + (s + 1) * SEG_CHUNKS, b * LANES:(b + 1) * LANES]
                 for s in range(CHUNK)]
            for i in range(OCT):
                for hf in range(CHUNK // OCT):
                    acc = None
                    for s in range(OCT):
                        v = z[OCT * hf + s]
                        sh = ((s - i) % OCT) * S5_GROUP
                        if sh:
                            v = pltpu.roll(v, sh, axis=1)
                        acc = v if acc is None else jnp.where(masks[s], v, acc)
                    xg_ref[OCT * b + i, j * SEG_CHUNKS:(j + 1) * SEG_CHUNKS,
                           hf * LANES:(hf + 1) * LANES] = acc.astype(BF16)
    uf = dot(hb, win_ref[:, S5_WIDTH:2 * S5_WIDTH]).astype(BF16)
    fc_ref[...] = dot(uf, bdc_ref[...]).astype(BF16)
    fs_ref[...] = dot(uf, bds_ref[...]).astype(BF16)
    o = 2 * S5_WIDTH
    gs_ref[...] = jax.nn.sigmoid(dot(hb, win_ref[:, o:o + D_MODEL])).astype(BF16)
    gf_ref[...] = jax.nn.sigmoid(dot(hb, win_ref[:, o + D_MODEL:o + 2 * D_MODEL])).astype(BF16)


def _channel_dft_mats():
    j = np.arange(FFT_GROUP)
    ang = 2.0 * np.pi * ((j[:, None] * j[None, :]) % FFT_GROUP) / FFT_GROUP
    blk_c = np.cos(ang) / math.sqrt(FFT_GROUP)
    blk_s = np.sin(ang) / math.sqrt(FFT_GROUP)
    bdc = np.kron(np.eye(FFT_GROUPS), blk_c)
    bds = np.kron(np.eye(FFT_GROUPS), blk_s)
    return jnp.asarray(bdc, F32).astype(BF16), jnp.asarray(bds, F32).astype(BF16)


def _const_spec(shape):
    nd = len(shape)
    return pl.BlockSpec(shape, lambda i: (0,) * nd, pipeline_mode=pl.Buffered(1))


def _inproj(x2d, mod, mod_row, norm_g, w_in_b, tm):
    t = x2d.shape[0]
    bdc, bds = _channel_dft_mats()
    tok = lambda w: pl.BlockSpec((tm, w), lambda i: (i, 0))
    out = lambda w: jax.ShapeDtypeStruct((t, w), BF16)
    return pl.pallas_call(
        _inproj_kernel,
        grid=(t // tm,),
        in_specs=[tok(D_MODEL),
                  pl.BlockSpec((1, N_MOD, D_MODEL), lambda i: (mod_row(i), 0, 0)),
                  _const_spec((1, D_MODEL)),
                  _const_spec(w_in_b.shape),
                  _const_spec(bdc.shape),
                  _const_spec(bds.shape)],
        out_specs=[pl.BlockSpec((S5_GROUPS, tm // CHUNK, CH), lambda i: (0, i, 0)),
                   tok(FFT_WIDTH), tok(FFT_WIDTH), tok(D_MODEL), tok(D_MODEL)],
        out_shape=[jax.ShapeDtypeStruct((S5_GROUPS, t // CHUNK, CH), BF16),
                   out(FFT_WIDTH), out(FFT_WIDTH), out(D_MODEL), out(D_MODEL)],
        scratch_shapes=[pltpu.VMEM((D_MODEL // LANES, tm, LANES), F32),
                        pltpu.VMEM((tm, D_MODEL), BF16),
                        pltpu.VMEM((tm, S5_WIDTH), F32)],
        compiler_params=_cparams(("parallel",)),
        name="inproj",
    )(x2d, mod, norm_g.reshape(1, D_MODEL), w_in_b, bdc, bds)


N_PROMPT_SEG = 16
N_SAMPLE_SEG = 32
N_SAMPLE_SEQ = 2
N_SEG = N_PROMPT_SEG + N_SAMPLE_SEG
ROWS = SEG_CHUNKS * N_SEG
ROWS_P = SEG_CHUNKS * N_PROMPT_SEG
OCT_PAIRS = OCT // 2


def _cmul_add(ar, ai, hr, hi, sr, si):
    return ar * hr - ai * hi + sr, ar * hi + ai * hr + si


def _s5_kernel(xp_ref, xs_ref, m_ref, w_ref, e_ref, apre_ref, apim_ref, h0_ref,
               yp_ref, ys_ref, fin_ref, s_scr, hin_scr, hinp_scr, f_scr, hs_scr, y_scr):
    dot = functools.partial(jnp.dot, preferred_element_type=F32)
    parts = ((xp_ref, 0, ROWS_P), (xs_ref, ROWS_P, ROWS - ROWS_P))
    seg_rows = lambda c: pl.ds(c, N_SEG, stride=SEG_CHUNKS)
    blk = lambda c: pl.ds(c * N_SEG, N_SEG)
    seq_rows = lambda j: pl.ds(j, N_SAMPLE_SEQ, stride=SEG_CHUNKS)
    lat = lambda c: pl.ds(c * N_SEG + N_PROMPT_SEG, N_SAMPLE_SEG)
    zero = jnp.zeros((N_SEG, LANES), F32)

    for pr in range(OCT_PAIRS):
        ln = slice(pr * LANES, (pr + 1) * LANES)
        g0, g1 = 2 * pr, 2 * pr + 1
        for x_ref, r0, nr in parts:
            s = dot(x_ref[g0], w_ref[pr, 0:CH, :]) + dot(x_ref[g1], w_ref[pr, CH:2 * CH, :])
            for k in range(4):
                s_scr[k, r0:r0 + nr, :] = s[:, k * LANES:(k + 1) * LANES]

        ar, ai = apre_ref[1, 0:1, ln], apim_ref[1, 0:1, ln]
        hr, hi = zero, zero
        for c in range(SEG_CHUNKS):
            hin_scr[0, blk(c), :] = hr
            hin_scr[1, blk(c), :] = hi
            hr, hi = _cmul_add(ar, ai, hr, hi, s_scr[0, seg_rows(c), :], s_scr[1, seg_rows(c), :])
        fin_ref[pr, 0] = hr[0:N_PROMPT_SEG]
        fin_ref[pr, 1] = hi[0:N_PROMPT_SEG]
        f_scr[0] = hr[N_PROMPT_SEG:]
        f_scr[1] = hi[N_PROMPT_SEG:]
        br, bi = apre_ref[1, 1:2, ln], apim_ref[1, 1:2, ln]
        gr, gi = zero, zero
        for c in range(SEG_CHUNKS - 1, -1, -1):
            hin_scr[2, blk(c), :] = gr
            hin_scr[3, blk(c), :] = gi
            gr, gi = _cmul_add(br, bi, gr, gi, s_scr[2, seg_rows(c), :], s_scr[3, seg_rows(c), :])
        fin_ref[pr, 2] = gr[0:N_PROMPT_SEG]
        fin_ref[pr, 3] = gi[0:N_PROMPT_SEG]
        f_scr[2] = gr[N_PROMPT_SEG:]
        f_scr[3] = gi[N_PROMPT_SEG:]

        a2r, a2i = apre_ref[SEG_CHUNKS, 0:1, ln], apim_ref[SEG_CHUNKS, 0:1, ln]
        hr, hi = h0_ref[0, :, ln], h0_ref[1, :, ln]
        for j in range(SEG_CHUNKS):
            hs_scr[0, seq_rows(j), :] = hr
            hs_scr[1, seq_rows(j), :] = hi
            hr, hi = _cmul_add(a2r, a2i, hr, hi, f_scr[0, seq_rows(j), :], f_scr[1, seq_rows(j), :])
        b2r, b2i = apre_ref[SEG_CHUNKS, 1:2, ln], apim_ref[SEG_CHUNKS, 1:2, ln]
        gr, gi = h0_ref[2, :, ln], h0_ref[3, :, ln]
        for j in range(SEG_CHUNKS - 1, -1, -1):
            hs_scr[2, seq_rows(j), :] = gr
            hs_scr[3, seq_rows(j), :] = gi
            gr, gi = _cmul_add(b2r, b2i, gr, gi, f_scr[2, seq_rows(j), :], f_scr[3, seq_rows(j), :])
        for c in range(SEG_CHUNKS):
            p_r, p_i = apre_ref[c, 0:1, ln], apim_ref[c, 0:1, ln]
            hr, hi = _cmul_add(p_r, p_i, hs_scr[0], hs_scr[1], hin_scr[0, lat(c), :], hin_scr[1, lat(c), :])
            hin_scr[0, lat(c), :] = hr
            hin_scr[1, lat(c), :] = hi
            cb = SEG_CHUNKS - 1 - c
            p_r, p_i = apre_ref[cb, 1:2, ln], apim_ref[cb, 1:2, ln]
            gr, gi = _cmul_add(p_r, p_i, hs_scr[2], hs_scr[3], hin_scr[2, lat(c), :], hin_scr[3, lat(c), :])
            hin_scr[2, lat(c), :] = gr
            hin_scr[3, lat(c), :] = gi

        for k in range(4):
            for sg in range(N_SEG):
                hinp_scr[sg * SEG_CHUNKS:(sg + 1) * SEG_CHUNKS, k * LANES:(k + 1) * LANES] = (
                    hin_scr[k, pl.ds(sg, SEG_CHUNKS, stride=N_SEG), :].astype(BF16))
        for gi_, g in ((0, g0), (1, g1)):
            for x_ref, r0, nr in parts:
                y_scr[g, r0:r0 + nr, :] = (dot(x_ref[g], m_ref[g])
                                           + dot(hinp_scr[r0:r0 + nr, :], e_ref[pr, :, gi_ * CH:(gi_ + 1) * CH]))

    rb_rows = LANES
    masks = _lane_slot_masks(rb_rows)
    for rb in range(ROWS // rb_rows):
        for t in range(CHUNK):
            hf, sl = divmod(t, OCT)
            acc = None
            for i in range(OCT):
                v = y_scr[i, rb * rb_rows:(rb + 1) * rb_rows, hf * LANES:(hf + 1) * LANES]
                sh = ((i - sl) % OCT) * S5_GROUP
                if sh:
                    v = pltpu.roll(v, sh, axis=1)
                acc = v if acc is None else jnp.where(masks[i], v, acc)
            for sg in range(rb_rows // SEG_CHUNKS):
                seg = rb * (rb_rows // SEG_CHUNKS) + sg
                piece = acc[sg * SEG_CHUNKS:(sg + 1) * SEG_CHUNKS]
                if seg < N_PROMPT_SEG:
                    r0 = seg * SEG + t * SEG_CHUNKS
                    yp_ref[0, r0:r0 + SEG_CHUNKS, :] = piece
                else:
                    r0 = (seg - N_PROMPT_SEG) * SEG + t * SEG_CHUNKS
                    ys_ref[0, r0:r0 + SEG_CHUNKS, :] = piece


def _s5_conv(xg_p, xg_s, m, w, e, apre, apim, h0):
    n_oct = S5_GROUPS // OCT
    tp, ts = xg_p.shape[1] * CHUNK, xg_s.shape[1] * CHUNK
    return pl.pallas_call(
        _s5_kernel,
        grid=(n_oct,),
        in_specs=[pl.BlockSpec((OCT, ROWS_P, CH), lambda o: (o, 0, 0)),
                  pl.BlockSpec((OCT, ROWS - ROWS_P, CH), lambda o: (o, 0, 0)),
                  pl.BlockSpec((OCT, CH, CH), lambda o: (o, 0, 0)),
                  pl.BlockSpec((OCT_PAIRS, 2 * CH, 4 * LANES), lambda o: (o, 0, 0)),
                  pl.BlockSpec((OCT_PAIRS, 4 * LANES, 2 * CH), lambda o: (o, 0, 0)),
                  pl.BlockSpec((SEG_CHUNKS + 1, 2, OCT_PAIRS * LANES), lambda o: (0, 0, o)),
                  pl.BlockSpec((SEG_CHUNKS + 1, 2, OCT_PAIRS * LANES), lambda o: (0, 0, o)),
                  pl.BlockSpec((4, N_SAMPLE_SEQ, OCT_PAIRS * LANES), lambda o: (0, 0, o))],
        out_specs=[pl.BlockSpec((1, tp, LANES), lambda o: (o, 0, 0)),
                   pl.BlockSpec((1, ts, LANES), lambda o: (o, 0, 0)),
                   pl.BlockSpec((OCT_PAIRS, 4, N_PROMPT_SEG, LANES), lambda o: (o, 0, 0, 0))],
        out_shape=[jax.ShapeDtypeStruct((n_oct, tp, LANES), F32),
                   jax.ShapeDtypeStruct((n_oct, ts, LANES), F32),
                   jax.ShapeDtypeStruct((PAIRS, 4, N_PROMPT_SEG, LANES), F32)],
        scratch_shapes=[pltpu.VMEM((4, ROWS, LANES), F32),
                        pltpu.VMEM((4, ROWS, LANES), F32),
                        pltpu.VMEM((ROWS, 4 * LANES), BF16),
                        pltpu.VMEM((4, N_SAMPLE_SEG, LANES), F32),
                        pltpu.VMEM((4, N_SAMPLE_SEG, LANES), F32),
                        pltpu.VMEM((OCT, ROWS, CH), F32)],
        compiler_params=_cparams(("parallel",)),
        name="s5_conv",
    )(xg_p, xg_s, m, w, e, apre, apim, h0)


def _tile_order_positions(l):
    r = np.arange(l)
    tile, wi = r // SEG, r % SEG
    return tile * SEG + (wi % SEG_CHUNKS) * CHUNK + wi // SEG_CHUNKS


def _pos_dft_kernel(cm_ref, sm_ref, fc_ref, fs_ref, o_ref):
    dot = functools.partial(jnp.dot, preferred_element_type=F32)
    o_ref[0] = (dot(cm_ref[...], fc_ref[0]) - dot(sm_ref[...], fs_ref[0])).astype(BF16)


def _pos_dft(fc, fs, tk):
    n, l, wdt = fc.shape
    k = jnp.asarray(_tile_order_positions(l), jnp.int32)
    ang = ((k[:, None] * k[None, :]) % l).astype(F32) * (2.0 * math.pi / l)
    cm = (jnp.cos(ang) * (1.0 / math.sqrt(l))).astype(BF16)
    sm = (jnp.sin(ang) * (1.0 / math.sqrt(l))).astype(BF16)
    return pl.pallas_call(
        _pos_dft_kernel,
        grid=(n, l // tk),
        in_specs=[pl.BlockSpec((tk, l), lambda b, i: (i, 0)),
                  pl.BlockSpec((tk, l), lambda b, i: (i, 0)),
                  pl.BlockSpec((1, l, wdt), lambda b, i: (b, 0, 0)),
                  pl.BlockSpec((1, l, wdt), lambda b, i: (b, 0, 0))],
        out_specs=pl.BlockSpec((1, tk, wdt), lambda b, i: (b, i, 0)),
        out_shape=jax.ShapeDtypeStruct((n, l, wdt), BF16),
        compiler_params=_cparams(("parallel", "parallel")),
        name="pos_dft",
    )(cm, sm, fc, fs)


def _mix_ffn_kernel(x_ref, mod_ref, ys_ref, yf_ref, gs_ref, gf_ref,
                    wglu_ref, bglu_ref, wps_ref, wpf_ref, wout_ref, n2_ref,
                    wg_ref, wu_ref, wd_ref, fn_ref, o_ref, m_scr, mb_scr):
    tm = x_ref.shape[0]
    dot = functools.partial(jnp.dot, preferred_element_type=F32)
    mod = mod_ref[0]
    y = jnp.concatenate([ys_ref[b] for b in range(S5_WIDTH // LANES)], axis=1)
    z = jax.nn.gelu(y)
    z = z * jax.nn.sigmoid(dot(z.astype(BF16), wglu_ref[...]) + bglu_ref[...])
    m = (gs_ref[...].astype(F32) * dot(z.astype(BF16), wps_ref[...])
         + gf_ref[...].astype(F32) * dot(yf_ref[...], wpf_ref[...]))
    for k in range(D_MODEL // LANES):
        m_scr[k] = m[:, k * LANES:(k + 1) * LANES]
    for j in range(tm // SEG):
        for c in range(SEG_CHUNKS):
            r0 = j * SEG + c * CHUNK
            for k in range(D_MODEL // LANES):
                mb_scr[r0:r0 + CHUNK, k * LANES:(k + 1) * LANES] = (
                    m_scr[k, pl.ds(j * SEG + c, CHUNK, stride=SEG_CHUNKS), :].astype(BF16))
    x1 = x_ref[...] + mod[2:3] * dot(mb_scr[...], wout_ref[...])
    h2 = (_rms(x1, n2_ref[...]) * (1.0 + mod[4:5]) + mod[3:4]).astype(BF16)
    gate = dot(h2, wg_ref[...])
    up = dot(h2, wu_ref[...])
    ff = dot((gate * jax.nn.sigmoid(gate) * up).astype(BF16), wd_ref[...])
    x2 = x1 + mod[5:6] * ff
    o_ref[...] = _rms(x2, fn_ref[...])


def _mix_ffn(x2d, mod, mod_row, ys, yf, gs, gf, wts, tm):
    t = x2d.shape[0]
    tok = lambda w: pl.BlockSpec((tm, w), lambda i: (i, 0))
    return pl.pallas_call(
        _mix_ffn_kernel,
        grid=(t // tm,),
        in_specs=[tok(D_MODEL),
                  pl.BlockSpec((1, N_MOD, D_MODEL), lambda i: (mod_row(i), 0, 0)),
                  pl.BlockSpec((S5_WIDTH // LANES, tm, LANES), lambda i: (0, i, 0)),
                  tok(FFT_WIDTH), tok(D_MODEL), tok(D_MODEL)]
                 + [_const_spec(w.shape) for w in wts],
        out_specs=tok(D_MODEL),
        out_shape=jax.ShapeDtypeStruct((t, D_MODEL), F32),
        scratch_shapes=[pltpu.VMEM((D_MODEL // LANES, tm, LANES), F32),
                        pltpu.VMEM((tm, D_MODEL), BF16)],
        compiler_params=_cparams(("parallel",)),
        name="mix_ffn",
    )(x2d, mod, ys, yf, gs, gf, *wts)


def kernel(x_prompt, x_sample, state_s5, c, c_ctx, norm1_g, norm2_g, w_ada, b_ada, w_in,
           s5_lambda_re, s5_lambda_im, s5_log_step, s5_b_re, s5_b_im, s5_c_re, s5_c_im,
           s5_d, w_glu, b_glu, w_proj_s5, w_proj_fft, w_out, w_ffn_gate, w_ffn_up,
           w_ffn_down, final_norm_g):
    nb, sl, _ = x_prompt.shape
    db, dl, _ = x_sample.shape
    assert w_in.shape[0] == 1 and sl == SEG and nb == N_PROMPT_SEG
    assert db == N_SAMPLE_SEQ and dl == SEG * SEG_CHUNKS

    cvec = jnp.concatenate([c_ctx[None], c, jnp.zeros((8 - 1 - db, D_MODEL), F32)], axis=0)
    mod = _adaln(cvec, w_ada[0], b_ada[0]).reshape(8, N_MOD, D_MODEL)

    m, w, e, apre, apim = _s5_tables(s5_lambda_re[0], s5_lambda_im[0], s5_log_step[0], s5_b_re[0],
                                     s5_b_im[0], s5_c_re[0], s5_c_im[0], s5_d[0])

    tm = 512
    xp = x_prompt.reshape(nb * sl, D_MODEL)
    xs = x_sample.reshape(db * dl, D_MODEL)
    row_p = lambda i: 0
    row_s = lambda i: 1 + i // (dl // tm)
    w_in_b = w_in[0].astype(BF16)
    xg_p, fc_p, fs_p, gs_p, gf_p = _inproj(xp, mod, row_p, norm1_g[0], w_in_b, tm)
    xg_s, fc_s, fs_s, gs_s, gf_s = _inproj(xs, mod, row_s, norm1_g[0], w_in_b, tm)

    h0 = state_s5[:, 0].astype(F32).transpose(1, 2, 0, 3, 4).reshape(4, db, GP)
    ys_p, ys_s, fin = _s5_conv(xg_p, xg_s, m, w, e, apre, apim, h0)
    new_state = fin.reshape(PAIRS, 2, 2, nb, 2, S5_STATE).transpose(3, 1, 2, 0, 4, 5)
    new_state = new_state.reshape(nb, 1, 2, 2, S5_GROUPS, S5_STATE)

    yf_p = _pos_dft(fc_p.reshape(nb, sl, FFT_WIDTH), fs_p.reshape(nb, sl, FFT_WIDTH), sl)
    yf_s = _pos_dft(fc_s.reshape(db, dl, FFT_WIDTH), fs_s.reshape(db, dl, FFT_WIDTH), 512)

    wts = (w_glu[0].astype(BF16), b_glu[0].reshape(1, S5_WIDTH), w_proj_s5[0].astype(BF16),
           w_proj_fft[0].astype(BF16), w_out[0].astype(BF16), norm2_g[0].reshape(1, D_MODEL),
           w_ffn_gate[0].astype(BF16), w_ffn_up[0].astype(BF16), w_ffn_down[0].astype(BF16),
           final_norm_g.reshape(1, D_MODEL))
    tm2 = SEG
    row_s2 = lambda i: 1 + i // (dl // tm2)
    y_p = _mix_ffn(xp, mod, row_p, ys_p, yf_p.reshape(nb * sl, FFT_WIDTH), gs_p, gf_p, wts, tm2)
    y_s = _mix_ffn(xs, mod, row_s2, ys_s, yf_s.reshape(db * dl, FFT_WIDTH), gs_s, gf_s, wts, tm2)
    return (y_p.reshape(nb, sl, D_MODEL), y_s.reshape(db, dl, D_MODEL), new_state)
```

```python
import functools
import math

import numpy as np
import jax
import jax.numpy as jnp
from jax import lax
from jax.experimental import pallas as pl
from jax.experimental.pallas import tpu as pltpu

F32 = jnp.float32
BF16 = jnp.bfloat16
HIGHEST = lax.Precision.HIGHEST

D_MODEL = 1024
S5_WIDTH = 512
S5_GROUPS = 32
S5_GROUP = 16
S5_STATE = 64
FFT_WIDTH = 512
FFT_GROUPS = 4
FFT_GROUP = 128
D_FF = 2816
N_MOD = 6
EPS = 1e-6

LANES = 128
CHUNK = 16
SEG_CHUNKS = 16
SEG = CHUNK * SEG_CHUNKS
GP = S5_GROUPS * S5_STATE
PAIRS = S5_GROUPS // 2
CH = CHUNK * S5_GROUP
OCT = LANES // S5_GROUP

VMEM_LIMIT = 56 * 1024 * 1024


def _cparams(sem):
    return pltpu.CompilerParams(dimension_semantics=sem, vmem_limit_bytes=VMEM_LIMIT)


def _ada_kernel(c_ref, w_ref, b_ref, o_ref):
    c = c_ref[...]
    s = c * jax.nn.sigmoid(c)
    o_ref[...] = jnp.dot(s, w_ref[...], preferred_element_type=F32, precision=HIGHEST) + b_ref[...]


def _adaln(cvec8, w_ada, b_ada):
    n_out = w_ada.shape[1]
    tn = 1024
    return pl.pallas_call(
        _ada_kernel,
        grid=(n_out // tn,),
        in_specs=[pl.BlockSpec((8, D_MODEL), lambda j: (0, 0)),
                  pl.BlockSpec((D_MODEL, tn), lambda j: (0, j)),
                  pl.BlockSpec((1, tn), lambda j: (0, j))],
        out_specs=pl.BlockSpec((8, tn), lambda j: (0, j)),
        out_shape=jax.ShapeDtypeStruct((8, n_out), F32),
        compiler_params=_cparams(("arbitrary",)),
        name="adaln",
    )(cvec8, w_ada, b_ada.reshape(1, n_out))


def _prep_kernel(lre_ref, lim_ref, lst_ref, bre_ref, bim_ref, cre_ref, cim_ref, dexp_ref, ind_ref,
                 apre_ref, apim_ref, gbre_ref, gbim_ref, care_ref, caimn_ref, t_ref,
                 bbre_scr, bbim_scr, prod_scr, k_scr):
    lre = lre_ref[...]
    lim = lim_ref[...]
    step = jnp.exp(lst_ref[...])
    mag = jnp.exp(lre * step)
    are = mag * jnp.cos(lim * step)
    aim = mag * jnp.sin(lim * step)
    nr = are - 1.0
    den = lre * lre + lim * lim
    fr = (nr * lre + aim * lim) / den
    fi = (aim * lre - nr * lim) / den
    for d in range(2):
        bbre_scr[d] = fr[d:d + 1] * bre_ref[d] - fi[d:d + 1] * bim_ref[d]
        bbim_scr[d] = fr[d:d + 1] * bim_ref[d] + fi[d:d + 1] * bre_ref[d]

    pr = [jnp.ones_like(are)]
    pi = [jnp.zeros_like(are)]
    for _ in range(CHUNK):
        r, i = pr[-1], pi[-1]
        pr.append(r * are - i * aim)
        pi.append(r * aim + i * are)
    a16r, a16i = pr[CHUNK], pi[CHUNK]
    qr, qi = jnp.ones_like(are), jnp.zeros_like(are)
    for c in range(SEG_CHUNKS + 1):
        apre_ref[c] = qr
        apim_ref[c] = qi
        qr, qi = qr * a16r - qi * a16i, qr * a16i + qi * a16r

    for d in range(2):
        for k in range(CHUNK + 1):
            r, i = pr[k][d:d + 1], pi[k][d:d + 1]
            if k < CHUNK:
                gbre_ref[d, k] = r * bbre_scr[d] - i * bbim_scr[d]
                gbim_ref[d, k] = r * bbim_scr[d] + i * bbre_scr[d]
            care_ref[d, k] = r * cre_ref[d] - i * cim_ref[d]
            caimn_ref[d, k] = -(r * cim_ref[d] + i * cre_ref[d])

    def body(it, carry):
        d = it // CHUNK
        k = it % CHUNK
        for h in range(S5_GROUP):
            cr = care_ref[d, k, pl.ds(h, 1), :]
            cn = caimn_ref[d, k, pl.ds(h, 1), :]
            prod_scr[h * S5_GROUP:(h + 1) * S5_GROUP, :] = cr * bbre_scr[d] + cn * bbim_scr[d]
        k_scr[it] = jnp.dot(prod_scr[...], ind_ref[...], preferred_element_type=F32, precision=HIGHEST)
        return carry

    lax.fori_loop(0, 2 * CHUNK, body, 0)

    t_ref[CHUNK - 1] = k_scr[0] + k_scr[CHUNK] + dexp_ref[...]
    for k in range(1, CHUNK):
        t_ref[CHUNK - 1 + k] = k_scr[k]
        t_ref[CHUNK - 1 - k] = k_scr[CHUNK + k]


def _s5_tables(lam_re, lam_im, log_step, b_re, b_im, c_re, c_im, s5_d):
    lre = lam_re.reshape(2, GP)
    lim = lam_im.reshape(2, GP)
    lst = jnp.repeat(log_step, S5_STATE, axis=-1)
    bre = b_re.transpose(0, 3, 1, 2).reshape(2, S5_GROUP, GP)
    bim = b_im.transpose(0, 3, 1, 2).reshape(2, S5_GROUP, GP)
    cre = c_re.transpose(0, 2, 1, 3).reshape(2, S5_GROUP, GP)
    cim = c_im.transpose(0, 2, 1, 3).reshape(2, S5_GROUP, GP)
    d_hg = jnp.pad(s5_d.reshape(S5_GROUPS, S5_GROUP).T, ((0, 0), (0, LANES - S5_GROUPS)))
    eye = jnp.asarray(np.eye(S5_GROUP, dtype=np.float32))
    dexp = (eye[:, :, None] * d_hg[:, None, :]).reshape(S5_GROUP * S5_GROUP, LANES)
    ind_np = np.zeros((GP, LANES), np.float32)
    ind_np[np.arange(GP), np.arange(GP) // S5_STATE] = 1.0
    ind = jnp.asarray(ind_np)

    tab = lambda n: jax.ShapeDtypeStruct((2, n, S5_GROUP, GP), F32)
    vm = pl.BlockSpec(memory_space=pltpu.VMEM)
    apre, apim, gbre, gbim, care, caimn, t = pl.pallas_call(
        _prep_kernel,
        in_specs=[vm] * 9,
        out_specs=[vm] * 7,
        out_shape=[jax.ShapeDtypeStruct((SEG_CHUNKS + 1, 2, GP), F32),
                   jax.ShapeDtypeStruct((SEG_CHUNKS + 1, 2, GP), F32),
                   tab(CHUNK), tab(CHUNK), tab(CHUNK + 1), tab(CHUNK + 1),
                   jax.ShapeDtypeStruct((2 * CHUNK - 1, S5_GROUP * S5_GROUP, LANES), F32)],
        scratch_shapes=[pltpu.VMEM((2, S5_GROUP, GP), F32),
                        pltpu.VMEM((2, S5_GROUP, GP), F32),
                        pltpu.VMEM((S5_GROUP * S5_GROUP, GP), F32),
                        pltpu.VMEM((2 * CHUNK, S5_GROUP * S5_GROUP, LANES), F32)],
        compiler_params=pltpu.CompilerParams(vmem_limit_bytes=VMEM_LIMIT),
        name="s5_tables",
    )(lre, lim, lst, bre, bim, cre, cim, dexp, ind)

    eye2 = jnp.asarray(np.eye(2, dtype=np.float32))
    tg = t[:, :, :S5_GROUPS].reshape(2 * CHUNK - 1, S5_GROUP, S5_GROUP, S5_GROUPS).transpose(3, 0, 1, 2)
    lag = (CHUNK - 1) + np.arange(CHUNK)[None, :] - np.arange(CHUNK)[:, None]
    m = tg[:, lag]
    m = m.transpose(0, 1, 4, 2, 3).reshape(S5_GROUPS, CH, CH).astype(BF16)
    gb = jnp.stack([jnp.stack([gbre[0, ::-1], gbre[1]]), jnp.stack([gbim[0, ::-1], gbim[1]])])
    wc = gb.reshape(2, 2, CHUNK, S5_GROUP, PAIRS, 2, S5_STATE).transpose(4, 5, 2, 3, 1, 0, 6)
    w = wc[:, :, :, :, :, :, None, :] * eye2[None, :, None, None, None, None, :, None]
    w = w.reshape(PAIRS, 2 * CH, 4 * LANES).astype(BF16)
    ca = jnp.stack([jnp.stack([care[0, 1:], care[1, :0:-1]]), jnp.stack([caimn[0, 1:], caimn[1, :0:-1]])])
    ec = ca.reshape(2, 2, CHUNK, S5_GROUP, PAIRS, 2, S5_STATE).transpose(4, 1, 0, 5, 6, 2, 3)
    e = ec[:, :, :, :, :, None, :, :] * eye2[None, None, None, :, None, :, None, None]
    e = e.reshape(PAIRS, 4 * LANES, 2 * CH).astype(BF16)
    return m, w, e, apre, apim


def _rms(x, g):
    return x * lax.rsqrt(jnp.mean(x * x, axis=-1, keepdims=True) + EPS) * g


def _lane_slot_masks(rows):
    slot = lax.broadcasted_iota(jnp.int32, (rows, LANES), 1) // S5_GROUP
    return [slot == s for s in range(OCT)]


def _inproj_kernel(x_ref, mod_ref, g_ref, win_ref, bdc_ref, bds_ref,
                   xg_ref, fc_ref, fs_ref, gs_ref, gf_ref, h_scr, hb_scr, zs_scr):
    tm = x_ref.shape[0]
    mod = mod_ref[0]
    h = _rms(x_ref[...], g_ref[...]) * (1.0 + mod[1:2]) + mod[0:1]
    for k in range(D_MODEL // LANES):
        h_scr[k] = h[:, k * LANES:(k + 1) * LANES]
    for j in range(tm // SEG):
        for s in range(CHUNK):
            r0 = j * SEG + s * SEG_CHUNKS
            for k in range(D_MODEL // LANES):
                hb_scr[r0:r0 + SEG_CHUNKS, k * LANES:(k + 1) * LANES] = (
                    h_scr[k, pl.ds(j * SEG + s, SEG_CHUNKS, stride=CHUNK), :].astype(BF16))
    hb = hb_scr[...]
    dot = functools.partial(jnp.dot, preferred_element_type=F32)
    zs_scr[...] = dot(hb, win_ref[:, 0:S5_WIDTH])
    masks = _lane_slot_masks(SEG_CHUNKS)
    for j in range(tm // SEG):
        for b in range(S5_WIDTH // LANES):
            z = [zs_scr[j * SEG + s * SEG_CHUNKS:j * SEG + (s + 1) * SEG_CHUNKS, b * LANES:(b + 1) * LANES]
                 for s in range(CHUNK)]
            for i in range(OCT):
                for hf in range(CHUNK // OCT):
                    acc = None
                    for s in range(OCT):
                        v = z[OCT * hf + s]
                        sh = ((s - i) % OCT) * S5_GROUP
                        if sh:
                            v = pltpu.roll(v, sh, axis=1)
                        acc = v if acc is None else jnp.where(masks[s], v, acc)
                    xg_ref[OCT * b + i, j * SEG_CHUNKS:(j + 1) * SEG_CHUNKS,
                           hf * LANES:(hf + 1) * LANES] = acc.astype(BF16)
    uf = dot(hb, win_ref[:, S5_WIDTH:2 * S5_WIDTH]).astype(BF16)
    fc_ref[...] = dot(uf, bdc_ref[...]).astype(BF16)
    fs_ref[...] = dot(uf, bds_ref[...]).astype(BF16)
    o = 2 * S5_WIDTH
    gs_ref[...] = jax.nn.sigmoid(dot(hb, win_ref[:, o:o + D_MODEL])).astype(BF16)
    gf_ref[...] = jax.nn.sigmoid(dot(hb, win_ref[:, o + D_MODEL:o + 2 * D_MODEL])).astype(BF16)


def _channel_dft_mats():
    j = np.arange(FFT_GROUP)
    ang = 2.0 * np.pi * ((j[:, None] * j[None, :]) % FFT_GROUP) / FFT_GROUP
    blk_c = np.cos(ang) / math.sqrt(FFT_GROUP)
    blk_s = np.sin(ang) / math.sqrt(FFT_GROUP)
    bdc = np.kron(np.eye(FFT_GROUPS), blk_c)
    bds = np.kron(np.eye(FFT_GROUPS), blk_s)
    return jnp.asarray(bdc, F32).astype(BF16), jnp.asarray(bds, F32).astype(BF16)


def _const_spec(shape):
    nd = len(shape)
    return pl.BlockSpec(shape, lambda i: (0,) * nd, pipeline_mode=pl.Buffered(1))


def _inproj(x2d, mod, mod_row, norm_g, w_in_b, tm):
    t = x2d.shape[0]
    bdc, bds = _channel_dft_mats()
    tok = lambda w: pl.BlockSpec((tm, w), lambda i: (i, 0))
    out = lambda w: jax.ShapeDtypeStruct((t, w), BF16)
    return pl.pallas_call(
        _inproj_kernel,
        grid=(t // tm,),
        in_specs=[tok(D_MODEL),
                  pl.BlockSpec((1, N_MOD, D_MODEL), lambda i: (mod_row(i), 0, 0)),
                  _const_spec((1, D_MODEL)),
                  _const_spec(w_in_b.shape),
                  _const_spec(bdc.shape),
                  _const_spec(bds.shape)],
        out_specs=[pl.BlockSpec((S5_GROUPS, tm // CHUNK, CH), lambda i: (0, i, 0)),
                   tok(FFT_WIDTH), tok(FFT_WIDTH), tok(D_MODEL), tok(D_MODEL)],
        out_shape=[jax.ShapeDtypeStruct((S5_GROUPS, t // CHUNK, CH), BF16),
                   out(FFT_WIDTH), out(FFT_WIDTH), out(D_MODEL), out(D_MODEL)],
        scratch_shapes=[pltpu.VMEM((D_MODEL // LANES, tm, LANES), F32),
                        pltpu.VMEM((tm, D_MODEL), BF16),
                        pltpu.VMEM((tm, S5_WIDTH), F32)],
        compiler_params=_cparams(("parallel",)),
        name="inproj",
    )(x2d, mod, norm_g.reshape(1, D_MODEL), w_in_b, bdc, bds)


N_PROMPT_SEG = 16
N_SAMPLE_SEG = 32
N_SAMPLE_SEQ = 2
N_SEG = N_PROMPT_SEG + N_SAMPLE_SEG
ROWS = SEG_CHUNKS * N_SEG
ROWS_P = SEG_CHUNKS * N_PROMPT_SEG
OCT_PAIRS = OCT // 2


def _cmul_add(ar, ai, hr, hi, sr, si):
    return ar * hr - ai * hi + sr, ar * hi + ai * hr + si


def _s5_kernel(xp_ref, xs_ref, m_ref, w_ref, e_ref, apre_ref, apim_ref, h0_ref,
               yp_ref, ys_ref, fin_ref, s_scr, hin_scr, hinp_scr, f_scr, hs_scr, y_scr):
    dot = functools.partial(jnp.dot, preferred_element_type=F32)
    parts = ((xp_ref, 0, ROWS_P), (xs_ref, ROWS_P, ROWS - ROWS_P))
    seg_rows = lambda c: pl.ds(c, N_SEG, stride=SEG_CHUNKS)
    blk = lambda c: pl.ds(c * N_SEG, N_SEG)
    seq_rows = lambda j: pl.ds(j, N_SAMPLE_SEQ, stride=SEG_CHUNKS)
    lat = lambda c: pl.ds(c * N_SEG + N_PROMPT_SEG, N_SAMPLE_SEG)
    zero = jnp.zeros((N_SEG, LANES), F32)

    for pr in range(OCT_PAIRS):
        ln = slice(pr * LANES, (pr + 1) * LANES)
        g0, g1 = 2 * pr, 2 * pr + 1
        for x_ref, r0, nr in parts:
            s = dot(x_ref[g0], w_ref[pr, 0:CH, :]) + dot(x_ref[g1], w_ref[pr, CH:2 * CH, :])
            for k in range(4):
                s_scr[k, r0:r0 + nr, :] = s[:, k * LANES:(k + 1) * LANES]

        ar, ai = apre_ref[1, 0:1, ln], apim_ref[1, 0:1, ln]
        hr, hi = zero, zero
        for c in range(SEG_CHUNKS):
            hin_scr[0, blk(c), :] = hr
            hin_scr[1, blk(c), :] = hi
            hr, hi = _cmul_add(ar, ai, hr, hi, s_scr[0, seg_rows(c), :], s_scr[1, seg_rows(c), :])
        fin_ref[pr, 0] = hr[0:N_PROMPT_SEG]
        fin_ref[pr, 1] = hi[0:N_PROMPT_SEG]
        f_scr[0] = hr[N_PROMPT_SEG:]
        f_scr[1] = hi[N_PROMPT_SEG:]
        br, bi = apre_ref[1, 1:2, ln], apim_ref[1, 1:2, ln]
        gr, gi = zero, zero
        for c in range(SEG_CHUNKS - 1, -1, -1):
            hin_scr[2, blk(c), :] = gr
            hin_scr[3, blk(c), :] = gi
            gr, gi = _cmul_add(br, bi, gr, gi, s_scr[2, seg_rows(c), :], s_scr[3, seg_rows(c), :])
        fin_ref[pr, 2] = gr[0:N_PROMPT_SEG]
        fin_ref[pr, 3] = gi[0:N_PROMPT_SEG]
        f_scr[2] = gr[N_PROMPT_SEG:]
        f_scr[3] = gi[N_PROMPT_SEG:]

        a2r, a2i = apre_ref[SEG_CHUNKS, 0:1, ln], apim_ref[SEG_CHUNKS, 0:1, ln]
        hr, hi = h0_ref[0, :, ln], h0_ref[1, :, ln]
        for j in range(SEG_CHUNKS):
            hs_scr[0, seq_rows(j), :] = hr
            hs_scr[1, seq_rows(j), :] = hi
            hr, hi = _cmul_add(a2r, a2i, hr, hi, f_scr[0, seq_rows(j), :], f_scr[1, seq_rows(j), :])
        b2r, b2i = apre_ref[SEG_CHUNKS, 1:2, ln], apim_ref[SEG_CHUNKS, 1:2, ln]
        gr, gi = h0_ref[2, :, ln], h0_ref[3, :, ln]
        for j in range(SEG_CHUNKS - 1, -1, -1):
            hs_scr[2, seq_rows(j), :] = gr
            hs_scr[3, seq_rows(j), :] = gi
            gr, gi = _cmul_add(b2r, b2i, gr, gi, f_scr[2, seq_rows(j), :], f_scr[3, seq_rows(j), :])
        for c in range(SEG_CHUNKS):
            p_r, p_i = apre_ref[c, 0:1, ln], apim_ref[c, 0:1, ln]
            hr, hi = _cmul_add(p_r, p_i, hs_scr[0], hs_scr[1], hin_scr[0, lat(c), :], hin_scr[1, lat(c), :])
            hin_scr[0, lat(c), :] = hr
            hin_scr[1, lat(c), :] = hi
            cb = SEG_CHUNKS - 1 - c
            p_r, p_i = apre_ref[cb, 1:2, ln], apim_ref[cb, 1:2, ln]
            gr, gi = _cmul_add(p_r, p_i, hs_scr[2], hs_scr[3], hin_scr[2, lat(c), :], hin_scr[3, lat(c), :])
            hin_scr[2, lat(c), :] = gr
            hin_scr[3, lat(c), :] = gi

        for k in range(4):
            for sg in range(N_SEG):
                hinp_scr[sg * SEG_CHUNKS:(sg + 1) * SEG_CHUNKS, k * LANES:(k + 1) * LANES] = (
                    hin_scr[k, pl.ds(sg, SEG_CHUNKS, stride=N_SEG), :].astype(BF16))
        for gi_, g in ((0, g0), (1, g1)):
            for x_ref, r0, nr in parts:
                y_scr[g, r0:r0 + nr, :] = (dot(x_ref[g], m_ref[g])
                                           + dot(hinp_scr[r0:r0 + nr, :], e_ref[pr, :, gi_ * CH:(gi_ + 1) * CH]))

    rb_rows = LANES
    masks = _lane_slot_masks(rb_rows)
    for rb in range(ROWS // rb_rows):
        for t in range(CHUNK):
            hf, sl = divmod(t, OCT)
            acc = None
            for i in range(OCT):
                v = y_scr[i, rb * rb_rows:(rb + 1) * rb_rows, hf * LANES:(hf + 1) * LANES]
                sh = ((i - sl) % OCT) * S5_GROUP
                if sh:
                    v = pltpu.roll(v, sh, axis=1)
                acc = v if acc is None else jnp.where(masks[i], v, acc)
            for sg in range(rb_rows // SEG_CHUNKS):
                seg = rb * (rb_rows // SEG_CHUNKS) + sg
                piece = acc[sg * SEG_CHUNKS:(sg + 1) * SEG_CHUNKS]
                if seg < N_PROMPT_SEG:
                    r0 = seg * SEG + t * SEG_CHUNKS
                    yp_ref[0, r0:r0 + SEG_CHUNKS, :] = piece
                else:
                    r0 = (seg - N_PROMPT_SEG) * SEG + t * SEG_CHUNKS
                    ys_ref[0, r0:r0 + SEG_CHUNKS, :] = piece


def _s5_conv(xg_p, xg_s, m, w, e, apre, apim, h0):
    n_oct = S5_GROUPS // OCT
    tp, ts = xg_p.shape[1] * CHUNK, xg_s.shape[1] * CHUNK
    return pl.pallas_call(
        _s5_kernel,
        grid=(n_oct,),
        in_specs=[pl.BlockSpec((OCT, ROWS_P, CH), lambda o: (o, 0, 0)),
                  pl.BlockSpec((OCT, ROWS - ROWS_P, CH), lambda o: (o, 0, 0)),
                  pl.BlockSpec((OCT, CH, CH), lambda o: (o, 0, 0)),
                  pl.BlockSpec((OCT_PAIRS, 2 * CH, 4 * LANES), lambda o: (o, 0, 0)),
                  pl.BlockSpec((OCT_PAIRS, 4 * LANES, 2 * CH), lambda o: (o, 0, 0)),
                  pl.BlockSpec((SEG_CHUNKS + 1, 2, OCT_PAIRS * LANES), lambda o: (0, 0, o)),
                  pl.BlockSpec((SEG_CHUNKS + 1, 2, OCT_PAIRS * LANES), lambda o: (0, 0, o)),
                  pl.BlockSpec((4, N_SAMPLE_SEQ, OCT_PAIRS * LANES), lambda o: (0, 0, o))],
        out_specs=[pl.BlockSpec((1, tp, LANES), lambda o: (o, 0, 0)),
                   pl.BlockSpec((1, ts, LANES), lambda o: (o, 0, 0)),
                   pl.BlockSpec((OCT_PAIRS, 4, N_PROMPT_SEG, LANES), lambda o: (o, 0, 0, 0))],
        out_shape=[jax.ShapeDtypeStruct((n_oct, tp, LANES), F32),
                   jax.ShapeDtypeStruct((n_oct, ts, LANES), F32),
                   jax.ShapeDtypeStruct((PAIRS, 4, N_PROMPT_SEG, LANES), F32)],
        scratch_shapes=[pltpu.VMEM((4, ROWS, LANES), F32),
                        pltpu.VMEM((4, ROWS, LANES), F32),
                        pltpu.VMEM((ROWS, 4 * LANES), BF16),
                        pltpu.VMEM((4, N_SAMPLE_SEG, LANES), F32),
                        pltpu.VMEM((4, N_SAMPLE_SEG, LANES), F32),
                        pltpu.VMEM((OCT, ROWS, CH), F32)],
        compiler_params=_cparams(("parallel",)),
        name="s5_conv",
    )(xg_p, xg_s, m, w, e, apre, apim, h0)


def _tile_order_positions(l):
    r = np.arange(l)
    tile, wi = r // SEG, r % SEG
    return tile * SEG + (wi % SEG_CHUNKS) * CHUNK + wi // SEG_CHUNKS


def _tile_dft_tables(scale, rows_in_tile_order=True):
    pos = _tile_order_positions(SEG)
    freq = pos if rows_in_tile_order else np.arange(SEG)
    ang = 2.0 * np.pi * ((freq[:, None] * pos[None, :]) % SEG) / SEG
    return jnp.asarray(np.cos(ang) * scale, F32), jnp.asarray(np.sin(ang) * scale, F32)


def _pos_dft_tile_kernel(cm_ref, sm_ref, fc_ref, fs_ref, o_ref):
    dot = functools.partial(jnp.dot, preferred_element_type=F32)
    o_ref[0] = (dot(cm_ref[...].astype(BF16), fc_ref[0]) - dot(sm_ref[...].astype(BF16), fs_ref[0])).astype(BF16)


def _pos_dft_tile(fc, fs):
    n, l, wdt = fc.shape
    cm, sm = _tile_dft_tables(1.0 / math.sqrt(l))
    seq = pl.BlockSpec((1, l, wdt), lambda b: (b, 0, 0))
    return pl.pallas_call(
        _pos_dft_tile_kernel,
        grid=(n,),
        in_specs=[_const_spec((l, l)), _const_spec((l, l)), seq, seq],
        out_specs=seq,
        out_shape=jax.ShapeDtypeStruct((n, l, wdt), BF16),
        compiler_params=_cparams(("parallel",)),
        name="pos_dft_tile",
    )(cm, sm, fc, fs)


N_TILES = 16


def _fft16(xr, xi):
    n = len(xr)
    rev = [int(format(i, "04b")[::-1], 2) for i in range(n)]
    ar = [xr[r] for r in rev]
    ai = [xi[r] for r in rev]
    size = 2
    while size <= n:
        half = size // 2
        for start in range(0, n, size):
            for k in range(half):
                wr = math.cos(2.0 * math.pi * k / size)
                wi = -math.sin(2.0 * math.pi * k / size)
                i0, i1 = start + k, start + k + half
                if k == 0:
                    tr, ti = ar[i1], ai[i1]
                elif 4 * k == size:
                    tr, ti = ai[i1], -ar[i1]
                else:
                    tr = ar[i1] * wr - ai[i1] * wi
                    ti = ar[i1] * wi + ai[i1] * wr
                ar[i1], ai[i1] = ar[i0] - tr, ai[i0] - ti
                ar[i0], ai[i0] = ar[i0] + tr, ai[i0] + ti
        size *= 2
    return ar, ai


def _tiles_fft_kernel(fc_ref, fs_ref, a_ref):
    rb, wdt = fc_ref.shape[1], fc_ref.shape[2]
    for r in range(rb // CHUNK):
        for b in range(wdt // LANES):
            rows, lns = slice(r * CHUNK, (r + 1) * CHUNK), slice(b * LANES, (b + 1) * LANES)
            xr = [fc_ref[j, rows, lns].astype(F32) for j in range(N_TILES)]
            xi = [-fs_ref[j, rows, lns].astype(F32) for j in range(N_TILES)]
            ar, ai = _fft16(xr, xi)
            for k in range(N_TILES):
                a_ref[k, 0, rows, lns] = ar[k].astype(BF16)
                a_ref[k, 1, rows, lns] = ai[k].astype(BF16)


def _twiddled_dft_kernel(cb_ref, sb_ref, ca_ref, sa_ref, a_ref, o_ref):
    k1 = pl.program_id(1)
    dot = functools.partial(jnp.dot, preferred_element_type=F32)
    ca, sa = ca_ref[pl.ds(k1, 1), :], sa_ref[pl.ds(k1, 1), :]
    cb, sb = cb_ref[...], sb_ref[...]
    dc = (cb * ca - sb * sa).astype(BF16)
    ds = (sb * ca + cb * sa).astype(BF16)
    out = dot(dc, a_ref[0]) + dot(ds, a_ref[1])
    o_ref[...] = out.astype(BF16).reshape(o_ref.shape)


def _pos_dft_long(fc, fs):
    n, l, wdt = fc.shape
    assert l == N_TILES * SEG
    rb = 32
    tiles = pl.BlockSpec((None, N_TILES, rb, wdt), lambda b, r: (b, 0, r, 0))
    a = pl.pallas_call(
        _tiles_fft_kernel,
        grid=(n, SEG // rb),
        in_specs=[tiles, tiles],
        out_specs=pl.BlockSpec((None, N_TILES, 2, rb, wdt), lambda b, r: (b, 0, 0, r, 0)),
        out_shape=jax.ShapeDtypeStruct((n, N_TILES, 2, SEG, wdt), BF16),
        compiler_params=_cparams(("parallel", "parallel")),
        name="tiles_fft",
    )(fc.reshape(n, N_TILES, SEG, wdt), fs.reshape(n, N_TILES, SEG, wdt))

    cb, sb = _tile_dft_tables(1.0 / math.sqrt(l), rows_in_tile_order=False)
    pos = _tile_order_positions(SEG)
    ang = 2.0 * np.pi * (np.arange(N_TILES)[:, None] * pos[None, :]) / l
    ca, sa = jnp.asarray(np.cos(ang), F32), jnp.asarray(np.sin(ang), F32)
    const2 = lambda shape: pl.BlockSpec(shape, lambda b, k: (0, 0), pipeline_mode=pl.Buffered(1))
    out = pl.pallas_call(
        _twiddled_dft_kernel,
        grid=(n, N_TILES),
        in_specs=[const2((SEG, SEG)), const2((SEG, SEG)), const2((N_TILES, SEG)), const2((N_TILES, SEG)),
                  pl.BlockSpec((None, None, 2, SEG, wdt), lambda b, k: (b, k, 0, 0, 0))],
        out_specs=pl.BlockSpec((None, N_TILES, None, SEG_CHUNKS, wdt), lambda b, k: (b, 0, k, 0, 0)),
        out_shape=jax.ShapeDtypeStruct((n, N_TILES, CHUNK, SEG_CHUNKS, wdt), BF16),
        compiler_params=_cparams(("parallel", "parallel")),
        name="twiddled_dft",
    )(cb, sb, ca, sa, a)
    return out.reshape(n, l, wdt)


def _mix_ffn_kernel(x_ref, mod_ref, ys_ref, yf_ref, gs_ref, gf_ref,
                    wglu_ref, bglu_ref, wps_ref, wpf_ref, wout_ref, n2_ref,
                    wg_ref, wu_ref, wd_ref, fn_ref, o_ref, m_scr, mb_scr):
    tm = x_ref.shape[0]
    dot = functools.partial(jnp.dot, preferred_element_type=F32)
    mod = mod_ref[0]
    y = jnp.concatenate([ys_ref[b] for b in range(S5_WIDTH // LANES)], axis=1)
    z = jax.nn.gelu(y)
    z = z * jax.nn.sigmoid(dot(z.astype(BF16), wglu_ref[...]) + bglu_ref[...])
    m = (gs_ref[...].astype(F32) * dot(z.astype(BF16), wps_ref[...])
         + gf_ref[...].astype(F32) * dot(yf_ref[...], wpf_ref[...]))
    for k in range(D_MODEL // LANES):
        m_scr[k] = m[:, k * LANES:(k + 1) * LANES]
    for j in range(tm // SEG):
        for c in range(SEG_CHUNKS):
            r0 = j * SEG + c * CHUNK
            for k in range(D_MODEL // LANES):
                mb_scr[r0:r0 + CHUNK, k * LANES:(k + 1) * LANES] = (
                    m_scr[k, pl.ds(j * SEG + c, CHUNK, stride=SEG_CHUNKS), :].astype(BF16))
    x1 = x_ref[...] + mod[2:3] * dot(mb_scr[...], wout_ref[...])
    h2 = (_rms(x1, n2_ref[...]) * (1.0 + mod[4:5]) + mod[3:4]).astype(BF16)
    gate = dot(h2, wg_ref[...])
    up = dot(h2, wu_ref[...])
    ff = dot((gate * jax.nn.sigmoid(gate) * up).astype(BF16), wd_ref[...])
    x2 = x1 + mod[5:6] * ff
    o_ref[...] = _rms(x2, fn_ref[...])


def _mix_ffn(x2d, mod, mod_row, ys, yf, gs, gf, wts, tm):
    t = x2d.shape[0]
    tok = lambda w: pl.BlockSpec((tm, w), lambda i: (i, 0))
    return pl.pallas_call(
        _mix_ffn_kernel,
        grid=(t // tm,),
        in_specs=[tok(D_MODEL),
                  pl.BlockSpec((1, N_MOD, D_MODEL), lambda i: (mod_row(i), 0, 0)),
                  pl.BlockSpec((S5_WIDTH // LANES, tm, LANES), lambda i: (0, i, 0)),
                  tok(FFT_WIDTH), tok(D_MODEL), tok(D_MODEL)]
                 + [_const_spec(w.shape) for w in wts],
        out_specs=tok(D_MODEL),
        out_shape=jax.ShapeDtypeStruct((t, D_MODEL), F32),
        scratch_shapes=[pltpu.VMEM((D_MODEL // LANES, tm, LANES), F32),
                        pltpu.VMEM((tm, D_MODEL), BF16)],
        compiler_params=_cparams(("parallel",)),
        name="mix_ffn",
    )(x2d, mod, ys, yf, gs, gf, *wts)


def kernel(x_prompt, x_sample, state_s5, c, c_ctx, norm1_g, norm2_g, w_ada, b_ada, w_in,
           s5_lambda_re, s5_lambda_im, s5_log_step, s5_b_re, s5_b_im, s5_c_re, s5_c_im,
           s5_d, w_glu, b_glu, w_proj_s5, w_proj_fft, w_out, w_ffn_gate, w_ffn_up,
           w_ffn_down, final_norm_g):
    nb, sl, _ = x_prompt.shape
    db, dl, _ = x_sample.shape
    assert w_in.shape[0] == 1 and sl == SEG and nb == N_PROMPT_SEG
    assert db == N_SAMPLE_SEQ and dl == SEG * SEG_CHUNKS

    cvec = jnp.concatenate([c_ctx[None], c, jnp.zeros((8 - 1 - db, D_MODEL), F32)], axis=0)
    mod = _adaln(cvec, w_ada[0], b_ada[0]).reshape(8, N_MOD, D_MODEL)

    m, w, e, apre, apim = _s5_tables(s5_lambda_re[0], s5_lambda_im[0], s5_log_step[0], s5_b_re[0],
                                     s5_b_im[0], s5_c_re[0], s5_c_im[0], s5_d[0])

    tm = 512
    xp = x_prompt.reshape(nb * sl, D_MODEL)
    xs = x_sample.reshape(db * dl, D_MODEL)
    row_p = lambda i: 0
    row_s = lambda i: 1 + i // (dl // tm)
    w_in_b = w_in[0].astype(BF16)
    xg_p, fc_p, fs_p, gs_p, gf_p = _inproj(xp, mod, row_p, norm1_g[0], w_in_b, tm)
    xg_s, fc_s, fs_s, gs_s, gf_s = _inproj(xs, mod, row_s, norm1_g[0], w_in_b, tm)

    h0 = state_s5[:, 0].astype(F32).transpose(1, 2, 0, 3, 4).reshape(4, db, GP)
    ys_p, ys_s, fin = _s5_conv(xg_p, xg_s, m, w, e, apre, apim, h0)
    new_state = fin.reshape(PAIRS, 2, 2, nb, 2, S5_STATE).transpose(3, 1, 2, 0, 4, 5)
    new_state = new_state.reshape(nb, 1, 2, 2, S5_GROUPS, S5_STATE)

    yf_p = _pos_dft_tile(fc_p.reshape(nb, sl, FFT_WIDTH), fs_p.reshape(nb, sl, FFT_WIDTH))
    yf_s = _pos_dft_long(fc_s.reshape(db, dl, FFT_WIDTH), fs_s.reshape(db, dl, FFT_WIDTH))

    wts = (w_glu[0].astype(BF16), b_glu[0].reshape(1, S5_WIDTH), w_proj_s5[0].astype(BF16),
           w_proj_fft[0].astype(BF16), w_out[0].astype(BF16), norm2_g[0].reshape(1, D_MODEL),
           w_ffn_gate[0].astype(BF16), w_ffn_up[0].astype(BF16), w_ffn_down[0].astype(BF16),
           final_norm_g.reshape(1, D_MODEL))
    tm2 = SEG
    row_s2 = lambda i: 1 + i // (dl // tm2)
    y_p = _mix_ffn(xp, mod, row_p, ys_p, yf_p.reshape(nb * sl, FFT_WIDTH), gs_p, gf_p, wts, tm2)
    y_s = _mix_ffn(xs, mod, row_s2, ys_s, yf_s.reshape(db * dl, FFT_WIDTH), gs_s, gf_s, wts, tm2)
    return (y_p.reshape(nb, sl, D_MODEL), y_s.reshape(db, dl, D_MODEL), new_state)
```

```python
import functools
import math

import numpy as np
import jax
import jax.numpy as jnp
from jax import lax
from jax.experimental import pallas as pl
from jax.experimental.pallas import tpu as pltpu

F32 = jnp.float32
BF16 = jnp.bfloat16
HIGHEST = lax.Precision.HIGHEST

D_MODEL = 1024
S5_WIDTH = 512
S5_GROUPS = 32
S5_GROUP = 16
S5_STATE = 64
FFT_WIDTH = 512
FFT_GROUPS = 4
FFT_GROUP = 128
D_FF = 2816
N_MOD = 6
EPS = 1e-6

LANES = 128
CHUNK = 16
SEG_CHUNKS = 16
SEG = CHUNK * SEG_CHUNKS
GP = S5_GROUPS * S5_STATE
PAIRS = S5_GROUPS // 2
CH = CHUNK * S5_GROUP
OCT = LANES // S5_GROUP

VMEM_LIMIT = 56 * 1024 * 1024


def _cparams(sem):
    return pltpu.CompilerParams(dimension_semantics=sem, vmem_limit_bytes=VMEM_LIMIT)


def _ada_kernel(c_ref, w_ref, b_ref, o_ref):
    c = c_ref[...]
    s = c * jax.nn.sigmoid(c)
    o_ref[...] = jnp.dot(s, w_ref[...], preferred_element_type=F32, precision=HIGHEST) + b_ref[...]


def _adaln(cvec8, w_ada, b_ada):
    n_out = w_ada.shape[1]
    tn = 1024
    return pl.pallas_call(
        _ada_kernel,
        grid=(n_out // tn,),
        in_specs=[pl.BlockSpec((8, D_MODEL), lambda j: (0, 0)),
                  pl.BlockSpec((D_MODEL, tn), lambda j: (0, j)),
                  pl.BlockSpec((1, tn), lambda j: (0, j))],
        out_specs=pl.BlockSpec((8, tn), lambda j: (0, j)),
        out_shape=jax.ShapeDtypeStruct((8, n_out), F32),
        compiler_params=_cparams(("arbitrary",)),
        name="adaln",
    )(cvec8, w_ada, b_ada.reshape(1, n_out))


def _prep_kernel(lre_ref, lim_ref, lst_ref, bre_ref, bim_ref, cre_ref, cim_ref, dt_ref,
                 apre_ref, apim_ref, m_ref, w_ref, e_ref, gb_scr, ca_scr, ts_scr):
    lre = lre_ref[...]
    lim = lim_ref[...]
    step = jnp.exp(lst_ref[...])
    mag = jnp.exp(lre * step)
    are = mag * jnp.cos(lim * step)
    aim = mag * jnp.sin(lim * step)
    nr = are - 1.0
    den = lre * lre + lim * lim
    fr = (nr * lre + aim * lim) / den
    fi = (aim * lre - nr * lim) / den

    pr = [jnp.ones_like(are)]
    pi = [jnp.zeros_like(are)]
    for _ in range(CHUNK):
        r, i = pr[-1], pi[-1]
        pr.append(r * are - i * aim)
        pi.append(r * aim + i * are)
    a16r, a16i = pr[CHUNK], pi[CHUNK]
    qr, qi = jnp.ones_like(are), jnp.zeros_like(are)
    for c in range(SEG_CHUNKS + 1):
        apre_ref[c] = qr
        apim_ref[c] = qi
        qr, qi = qr * a16r - qi * a16i, qr * a16i + qi * a16r

    for d in range(2):
        bbre = fr[d:d + 1] * bre_ref[d] - fi[d:d + 1] * bim_ref[d]
        bbim = fr[d:d + 1] * bim_ref[d] + fi[d:d + 1] * bre_ref[d]
        cre, cim = cre_ref[d], cim_ref[d]
        for s in range(CHUNK):
            k = CHUNK - 1 - s if d == 0 else s
            r, i = pr[k][d:d + 1], pi[k][d:d + 1]
            gb_scr[0, d, s] = r * bbre - i * bbim
            gb_scr[1, d, s] = r * bbim + i * bbre
            f = s + 1 if d == 0 else CHUNK - s
            r, i = pr[f][d:d + 1], pi[f][d:d + 1]
            ca_scr[0, d, s] = r * cre - i * cim
            ca_scr[1, d, s] = -(r * cim + i * cre)

    dot_nt = lambda a, b: lax.dot_general(a, b, (((1,), (1,)), ((), ())), preferred_element_type=F32)
    lane_gi = lax.broadcasted_iota(jnp.int32, (CH, LANES), 1) // S5_STATE
    row_gi = lax.broadcasted_iota(jnp.int32, (LANES, CH), 0) // S5_STATE
    eye = (lax.broadcasted_iota(jnp.int32, (LANES, LANES), 0)
           == lax.broadcasted_iota(jnp.int32, (LANES, LANES), 1)).astype(BF16)
    slot = lax.broadcasted_iota(jnp.int32, (CH, LANES), 1) // S5_GROUP
    n_lag_rows = (2 * CHUNK - 1) * S5_GROUP

    def pair(q, carry):
        lanes = pl.ds(pl.multiple_of(q * LANES, LANES), LANES)
        for d in range(2):
            for ri in range(2):
                col = slice((2 * d + ri) * LANES, (2 * d + ri + 1) * LANES)
                gb = gb_scr[ri, d, :, :, lanes].reshape(CH, LANES)
                ca = ca_scr[ri, d, :, :, lanes].reshape(CH, LANES).astype(BF16)
                ca_t = dot_nt(eye, ca)
                for gi in range(2):
                    w_ref[q, gi * CH:(gi + 1) * CH, col] = jnp.where(lane_gi == gi, gb, 0.0).astype(BF16)
                    e_ref[q, col, gi * CH:(gi + 1) * CH] = jnp.where(row_gi == gi, ca_t, 0.0).astype(BF16)
        for gi in range(2):
            g = 2 * q + gi
            lag = []
            for d in range(2):
                c_re = jnp.concatenate([cre_ref[d, :, lanes]] * OCT, axis=0)
                c_imn = jnp.concatenate([-cim_ref[d, :, lanes]] * OCT, axis=0)
                keep = lax.broadcasted_iota(jnp.int32, (LANES, LANES), 1) // S5_STATE == gi
                c_re = jnp.where(keep, c_re, 0.0).astype(BF16)
                c_imn = jnp.where(keep, c_imn, 0.0).astype(BF16)
                gre = gb_scr[0, d, :, :, lanes].reshape(CH, LANES).astype(BF16)
                gim = gb_scr[1, d, :, :, lanes].reshape(CH, LANES).astype(BF16)
                lag.append(dot_nt(gre, c_re) + dot_nt(gim, c_imn))
            zl = (CHUNK - 1) * S5_GROUP
            ts_scr[0:zl, :] = lag[0][0:zl]
            ts_scr[zl:zl + S5_GROUP, :] = lag[0][zl:] + lag[1][0:S5_GROUP] + dt_ref[g]
            ts_scr[zl + S5_GROUP:n_lag_rows, :] = lag[1][S5_GROUP:]
            for hf in range(CHUNK // OCT):
                acc = None
                for s in range(OCT):
                    t = OCT * hf + s
                    win = ts_scr[(CHUNK - 1 - t) * S5_GROUP:(CHUNK - 1 - t) * S5_GROUP + CH, :]
                    acc = win if acc is None else jnp.where(slot == s, win, acc)
                m_ref[g, :, hf * LANES:(hf + 1) * LANES] = acc.astype(BF16)
        return carry

    lax.fori_loop(0, PAIRS, pair, 0)


def _s5_tables(lam_re, lam_im, log_step, b_re, b_im, c_re, c_im, s5_d):
    lre = lam_re.reshape(2, GP)
    lim = lam_im.reshape(2, GP)
    lst = jnp.repeat(log_step, S5_STATE, axis=-1)
    bre = b_re.transpose(0, 3, 1, 2).reshape(2, S5_GROUP, GP)
    bim = b_im.transpose(0, 3, 1, 2).reshape(2, S5_GROUP, GP)
    cre = c_re.transpose(0, 2, 1, 3).reshape(2, S5_GROUP, GP)
    cim = c_im.transpose(0, 2, 1, 3).reshape(2, S5_GROUP, GP)
    eye = jnp.asarray(np.eye(S5_GROUP, dtype=np.float32))
    dt = jnp.tile(eye[None] * s5_d.reshape(S5_GROUPS, 1, S5_GROUP), (1, 1, OCT))

    vm = pl.BlockSpec(memory_space=pltpu.VMEM)
    tab = (2, 2, CHUNK, S5_GROUP, GP)
    apre, apim, m, w, e = pl.pallas_call(
        _prep_kernel,
        in_specs=[vm] * 8,
        out_specs=[vm] * 5,
        out_shape=[jax.ShapeDtypeStruct((SEG_CHUNKS + 1, 2, GP), F32),
                   jax.ShapeDtypeStruct((SEG_CHUNKS + 1, 2, GP), F32),
                   jax.ShapeDtypeStruct((S5_GROUPS, CH, CH), BF16),
                   jax.ShapeDtypeStruct((PAIRS, 2 * CH, 4 * LANES), BF16),
                   jax.ShapeDtypeStruct((PAIRS, 4 * LANES, 2 * CH), BF16)],
        scratch_shapes=[pltpu.VMEM(tab, F32),
                        pltpu.VMEM(tab, F32),
                        pltpu.VMEM(((2 * CHUNK - 1) * S5_GROUP, LANES), F32)],
        compiler_params=pltpu.CompilerParams(vmem_limit_bytes=VMEM_LIMIT),
        name="s5_tables",
    )(lre, lim, lst, bre, bim, cre, cim, dt)
    return m, w, e, apre, apim


def _rms(x, g):
    return x * lax.rsqrt(jnp.mean(x * x, axis=-1, keepdims=True) + EPS) * g


def _lane_slot_masks(rows):
    slot = lax.broadcasted_iota(jnp.int32, (rows, LANES), 1) // S5_GROUP
    return [slot == s for s in range(OCT)]


def _inproj_kernel(x_ref, mod_ref, g_ref, win_ref, bdc_ref, bds_ref,
                   xg_ref, fc_ref, fs_ref, gs_ref, gf_ref, h_scr, hb_scr, zs_scr):
    tm = x_ref.shape[0]
    mod = mod_ref[0]
    h = _rms(x_ref[...], g_ref[...]) * (1.0 + mod[1:2]) + mod[0:1]
    for k in range(D_MODEL // LANES):
        h_scr[k] = h[:, k * LANES:(k + 1) * LANES]
    for j in range(tm // SEG):
        for s in range(CHUNK):
            r0 = j * SEG + s * SEG_CHUNKS
            for k in range(D_MODEL // LANES):
                hb_scr[r0:r0 + SEG_CHUNKS, k * LANES:(k + 1) * LANES] = (
                    h_scr[k, pl.ds(j * SEG + s, SEG_CHUNKS, stride=CHUNK), :].astype(BF16))
    hb = hb_scr[...]
    dot = functools.partial(jnp.dot, preferred_element_type=F32)
    zs_scr[...] = dot(hb, win_ref[:, 0:S5_WIDTH])
    masks = _lane_slot_masks(SEG_CHUNKS)
    for j in range(tm // SEG):
        for b in range(S5_WIDTH // LANES):
            z = [zs_scr[j * SEG + s * SEG_CHUNKS:j * SEG + (s + 1) * SEG_CHUNKS, b * LANES:(b + 1) * LANES]
                 for s in range(CHUNK)]
            for i in range(OCT):
                for hf in range(CHUNK // OCT):
                    acc = None
                    for s in range(OCT):
                        v = z[OCT * hf + s]
                        sh = ((s - i) % OCT) * S5_GROUP
                        if sh:
                            v = pltpu.roll(v, sh, axis=1)
                        acc = v if acc is None else jnp.where(masks[s], v, acc)
                    xg_ref[OCT * b + i, j * SEG_CHUNKS:(j + 1) * SEG_CHUNKS,
                           hf * LANES:(hf + 1) * LANES] = acc.astype(BF16)
    uf = dot(hb, win_ref[:, S5_WIDTH:2 * S5_WIDTH]).astype(BF16)
    fc_ref[...] = dot(uf, bdc_ref[...]).astype(BF16)
    fs_ref[...] = dot(uf, bds_ref[...]).astype(BF16)
    o = 2 * S5_WIDTH
    gs_ref[...] = jax.nn.sigmoid(dot(hb, win_ref[:, o:o + D_MODEL])).astype(BF16)
    gf_ref[...] = jax.nn.sigmoid(dot(hb, win_ref[:, o + D_MODEL:o + 2 * D_MODEL])).astype(BF16)


def _channel_dft_mats():
    j = np.arange(FFT_GROUP)
    ang = 2.0 * np.pi * ((j[:, None] * j[None, :]) % FFT_GROUP) / FFT_GROUP
    blk_c = np.cos(ang) / math.sqrt(FFT_GROUP)
    blk_s = np.sin(ang) / math.sqrt(FFT_GROUP)
    bdc = np.kron(np.eye(FFT_GROUPS), blk_c)
    bds = np.kron(np.eye(FFT_GROUPS), blk_s)
    return jnp.asarray(bdc, F32).astype(BF16), jnp.asarray(bds, F32).astype(BF16)


def _const_spec(shape):
    nd = len(shape)
    return pl.BlockSpec(shape, lambda i: (0,) * nd, pipeline_mode=pl.Buffered(1))


def _inproj(x2d, mod, mod_row, norm_g, w_in_b, tm):
    t = x2d.shape[0]
    bdc, bds = _channel_dft_mats()
    tok = lambda w: pl.BlockSpec((tm, w), lambda i: (i, 0))
    out = lambda w: jax.ShapeDtypeStruct((t, w), BF16)
    return pl.pallas_call(
        _inproj_kernel,
        grid=(t // tm,),
        in_specs=[tok(D_MODEL),
                  pl.BlockSpec((1, N_MOD, D_MODEL), lambda i: (mod_row(i), 0, 0)),
                  _const_spec((1, D_MODEL)),
                  _const_spec(w_in_b.shape),
                  _const_spec(bdc.shape),
                  _const_spec(bds.shape)],
        out_specs=[pl.BlockSpec((S5_GROUPS, tm // CHUNK, CH), lambda i: (0, i, 0)),
                   tok(FFT_WIDTH), tok(FFT_WIDTH), tok(D_MODEL), tok(D_MODEL)],
        out_shape=[jax.ShapeDtypeStruct((S5_GROUPS, t // CHUNK, CH), BF16),
                   out(FFT_WIDTH), out(FFT_WIDTH), out(D_MODEL), out(D_MODEL)],
        scratch_shapes=[pltpu.VMEM((D_MODEL // LANES, tm, LANES), F32),
                        pltpu.VMEM((tm, D_MODEL), BF16),
                        pltpu.VMEM((tm, S5_WIDTH), F32)],
        compiler_params=_cparams(("parallel",)),
        name="inproj",
    )(x2d, mod, norm_g.reshape(1, D_MODEL), w_in_b, bdc, bds)


N_PROMPT_SEG = 16
N_SAMPLE_SEG = 32
N_SAMPLE_SEQ = 2
N_SEG = N_PROMPT_SEG + N_SAMPLE_SEG
ROWS = SEG_CHUNKS * N_SEG
ROWS_P = SEG_CHUNKS * N_PROMPT_SEG
OCT_PAIRS = OCT // 2


def _cmul_add(ar, ai, hr, hi, sr, si):
    return ar * hr - ai * hi + sr, ar * hi + ai * hr + si


def _s5_kernel(xp_ref, xs_ref, m_ref, w_ref, e_ref, apre_ref, apim_ref, h0_ref,
               yp_ref, ys_ref, fin_ref, s_scr, hin_scr, hinp_scr, f_scr, hs_scr, y_scr):
    dot = functools.partial(jnp.dot, preferred_element_type=F32)
    parts = ((xp_ref, 0, ROWS_P), (xs_ref, ROWS_P, ROWS - ROWS_P))
    seg_rows = lambda c: pl.ds(c, N_SEG, stride=SEG_CHUNKS)
    blk = lambda c: pl.ds(c * N_SEG, N_SEG)
    seq_rows = lambda j: pl.ds(j, N_SAMPLE_SEQ, stride=SEG_CHUNKS)
    lat = lambda c: pl.ds(c * N_SEG + N_PROMPT_SEG, N_SAMPLE_SEG)
    zero = jnp.zeros((N_SEG, LANES), F32)

    for pr in range(OCT_PAIRS):
        ln = slice(pr * LANES, (pr + 1) * LANES)
        g0, g1 = 2 * pr, 2 * pr + 1
        for x_ref, r0, nr in parts:
            s = dot(x_ref[g0], w_ref[pr, 0:CH, :]) + dot(x_ref[g1], w_ref[pr, CH:2 * CH, :])
            for k in range(4):
                s_scr[k, r0:r0 + nr, :] = s[:, k * LANES:(k + 1) * LANES]

        ar, ai = apre_ref[1, 0:1, ln], apim_ref[1, 0:1, ln]
        hr, hi = zero, zero
        for c in range(SEG_CHUNKS):
            hin_scr[0, blk(c), :] = hr
            hin_scr[1, blk(c), :] = hi
            hr, hi = _cmul_add(ar, ai, hr, hi, s_scr[0, seg_rows(c), :], s_scr[1, seg_rows(c), :])
        fin_ref[pr, 0] = hr[0:N_PROMPT_SEG]
        fin_ref[pr, 1] = hi[0:N_PROMPT_SEG]
        f_scr[0] = hr[N_PROMPT_SEG:]
        f_scr[1] = hi[N_PROMPT_SEG:]
        br, bi = apre_ref[1, 1:2, ln], apim_ref[1, 1:2, ln]
        gr, gi = zero, zero
        for c in range(SEG_CHUNKS - 1, -1, -1):
            hin_scr[2, blk(c), :] = gr
            hin_scr[3, blk(c), :] = gi
            gr, gi = _cmul_add(br, bi, gr, gi, s_scr[2, seg_rows(c), :], s_scr[3, seg_rows(c), :])
        fin_ref[pr, 2] = gr[0:N_PROMPT_SEG]
        fin_ref[pr, 3] = gi[0:N_PROMPT_SEG]
        f_scr[2] = gr[N_PROMPT_SEG:]
        f_scr[3] = gi[N_PROMPT_SEG:]

        a2r, a2i = apre_ref[SEG_CHUNKS, 0:1, ln], apim_ref[SEG_CHUNKS, 0:1, ln]
        hr, hi = h0_ref[0, :, ln], h0_ref[1, :, ln]
        for j in range(SEG_CHUNKS):
            hs_scr[0, seq_rows(j), :] = hr
            hs_scr[1, seq_rows(j), :] = hi
            hr, hi = _cmul_add(a2r, a2i, hr, hi, f_scr[0, seq_rows(j), :], f_scr[1, seq_rows(j), :])
        b2r, b2i = apre_ref[SEG_CHUNKS, 1:2, ln], apim_ref[SEG_CHUNKS, 1:2, ln]
        gr, gi = h0_ref[2, :, ln], h0_ref[3, :, ln]
        for j in range(SEG_CHUNKS - 1, -1, -1):
            hs_scr[2, seq_rows(j), :] = gr
            hs_scr[3, seq_rows(j), :] = gi
            gr, gi = _cmul_add(b2r, b2i, gr, gi, f_scr[2, seq_rows(j), :], f_scr[3, seq_rows(j), :])
        for c in range(SEG_CHUNKS):
            p_r, p_i = apre_ref[c, 0:1, ln], apim_ref[c, 0:1, ln]
            hr, hi = _cmul_add(p_r, p_i, hs_scr[0], hs_scr[1], hin_scr[0, lat(c), :], hin_scr[1, lat(c), :])
            hin_scr[0, lat(c), :] = hr
            hin_scr[1, lat(c), :] = hi
            cb = SEG_CHUNKS - 1 - c
            p_r, p_i = apre_ref[cb, 1:2, ln], apim_ref[cb, 1:2, ln]
            gr, gi = _cmul_add(p_r, p_i, hs_scr[2], hs_scr[3], hin_scr[2, lat(c), :], hin_scr[3, lat(c), :])
            hin_scr[2, lat(c), :] = gr
            hin_scr[3, lat(c), :] = gi

        for k in range(4):
            for sg in range(N_SEG):
                hinp_scr[sg * SEG_CHUNKS:(sg + 1) * SEG_CHUNKS, k * LANES:(k + 1) * LANES] = (
                    hin_scr[k, pl.ds(sg, SEG_CHUNKS, stride=N_SEG), :].astype(BF16))
        for gi_, g in ((0, g0), (1, g1)):
            for x_ref, r0, nr in parts:
                y_scr[g, r0:r0 + nr, :] = (dot(x_ref[g], m_ref[g])
                                           + dot(hinp_scr[r0:r0 + nr, :], e_ref[pr, :, gi_ * CH:(gi_ + 1) * CH]))

    rb_rows = LANES
    masks = _lane_slot_masks(rb_rows)
    for rb in range(ROWS // rb_rows):
        for t in range(CHUNK):
            hf, sl = divmod(t, OCT)
            acc = None
            for i in range(OCT):
                v = y_scr[i, rb * rb_rows:(rb + 1) * rb_rows, hf * LANES:(hf + 1) * LANES]
                sh = ((i - sl) % OCT) * S5_GROUP
                if sh:
                    v = pltpu.roll(v, sh, axis=1)
                acc = v if acc is None else jnp.where(masks[i], v, acc)
            for sg in range(rb_rows // SEG_CHUNKS):
                seg = rb * (rb_rows // SEG_CHUNKS) + sg
                piece = acc[sg * SEG_CHUNKS:(sg + 1) * SEG_CHUNKS]
                if seg < N_PROMPT_SEG:
                    r0 = seg * SEG + t * SEG_CHUNKS
                    yp_ref[0, r0:r0 + SEG_CHUNKS, :] = piece
                else:
                    r0 = (seg - N_PROMPT_SEG) * SEG + t * SEG_CHUNKS
                    ys_ref[0, r0:r0 + SEG_CHUNKS, :] = piece


def _s5_conv(xg_p, xg_s, m, w, e, apre, apim, h0):
    n_oct = S5_GROUPS // OCT
    tp, ts = xg_p.shape[1] * CHUNK, xg_s.shape[1] * CHUNK
    return pl.pallas_call(
        _s5_kernel,
        grid=(n_oct,),
        in_specs=[pl.BlockSpec((OCT, ROWS_P, CH), lambda o: (o, 0, 0)),
                  pl.BlockSpec((OCT, ROWS - ROWS_P, CH), lambda o: (o, 0, 0)),
                  pl.BlockSpec((OCT, CH, CH), lambda o: (o, 0, 0)),
                  pl.BlockSpec((OCT_PAIRS, 2 * CH, 4 * LANES), lambda o: (o, 0, 0)),
                  pl.BlockSpec((OCT_PAIRS, 4 * LANES, 2 * CH), lambda o: (o, 0, 0)),
                  pl.BlockSpec((SEG_CHUNKS + 1, 2, OCT_PAIRS * LANES), lambda o: (0, 0, o)),
                  pl.BlockSpec((SEG_CHUNKS + 1, 2, OCT_PAIRS * LANES), lambda o: (0, 0, o)),
                  pl.BlockSpec((4, N_SAMPLE_SEQ, OCT_PAIRS * LANES), lambda o: (0, 0, o))],
        out_specs=[pl.BlockSpec((1, tp, LANES), lambda o: (o, 0, 0)),
                   pl.BlockSpec((1, ts, LANES), lambda o: (o, 0, 0)),
                   pl.BlockSpec((OCT_PAIRS, 4, N_PROMPT_SEG, LANES), lambda o: (o, 0, 0, 0))],
        out_shape=[jax.ShapeDtypeStruct((n_oct, tp, LANES), F32),
                   jax.ShapeDtypeStruct((n_oct, ts, LANES), F32),
                   jax.ShapeDtypeStruct((PAIRS, 4, N_PROMPT_SEG, LANES), F32)],
        scratch_shapes=[pltpu.VMEM((4, ROWS, LANES), F32),
                        pltpu.VMEM((4, ROWS, LANES), F32),
                        pltpu.VMEM((ROWS, 4 * LANES), BF16),
                        pltpu.VMEM((4, N_SAMPLE_SEG, LANES), F32),
                        pltpu.VMEM((4, N_SAMPLE_SEG, LANES), F32),
                        pltpu.VMEM((OCT, ROWS, CH), F32)],
        compiler_params=_cparams(("parallel",)),
        name="s5_conv",
    )(xg_p, xg_s, m, w, e, apre, apim, h0)


def _tile_order_positions(l):
    r = np.arange(l)
    tile, wi = r // SEG, r % SEG
    return tile * SEG + (wi % SEG_CHUNKS) * CHUNK + wi // SEG_CHUNKS


def _tile_dft_tables(scale, rows_in_tile_order=True):
    pos = _tile_order_positions(SEG)
    freq = pos if rows_in_tile_order else np.arange(SEG)
    ang = 2.0 * np.pi * ((freq[:, None] * pos[None, :]) % SEG) / SEG
    return jnp.asarray(np.cos(ang) * scale, F32), jnp.asarray(np.sin(ang) * scale, F32)


def _pos_dft_tile_kernel(cm_ref, sm_ref, fc_ref, fs_ref, o_ref):
    dot = functools.partial(jnp.dot, preferred_element_type=F32)
    o_ref[0] = (dot(cm_ref[...].astype(BF16), fc_ref[0]) - dot(sm_ref[...].astype(BF16), fs_ref[0])).astype(BF16)


def _pos_dft_tile(fc, fs):
    n, l, wdt = fc.shape
    cm, sm = _tile_dft_tables(1.0 / math.sqrt(l))
    seq = pl.BlockSpec((1, l, wdt), lambda b: (b, 0, 0))
    return pl.pallas_call(
        _pos_dft_tile_kernel,
        grid=(n,),
        in_specs=[_const_spec((l, l)), _const_spec((l, l)), seq, seq],
        out_specs=seq,
        out_shape=jax.ShapeDtypeStruct((n, l, wdt), BF16),
        compiler_params=_cparams(("parallel",)),
        name="pos_dft_tile",
    )(cm, sm, fc, fs)


N_TILES = 16


def _fft16(xr, xi):
    n = len(xr)
    rev = [int(format(i, "04b")[::-1], 2) for i in range(n)]
    ar = [xr[r] for r in rev]
    ai = [xi[r] for r in rev]
    size = 2
    while size <= n:
        half = size // 2
        for start in range(0, n, size):
            for k in range(half):
                wr = math.cos(2.0 * math.pi * k / size)
                wi = -math.sin(2.0 * math.pi * k / size)
                i0, i1 = start + k, start + k + half
                if k == 0:
                    tr, ti = ar[i1], ai[i1]
                elif 4 * k == size:
                    tr, ti = ai[i1], -ar[i1]
                else:
                    tr = ar[i1] * wr - ai[i1] * wi
                    ti = ar[i1] * wi + ai[i1] * wr
                ar[i1], ai[i1] = ar[i0] - tr, ai[i0] - ti
                ar[i0], ai[i0] = ar[i0] + tr, ai[i0] + ti
        size *= 2
    return ar, ai


def _tiles_fft_kernel(fc_ref, fs_ref, a_ref):
    rb, wdt = fc_ref.shape[1], fc_ref.shape[2]
    for r in range(rb // CHUNK):
        for b in range(wdt // LANES):
            rows, lns = slice(r * CHUNK, (r + 1) * CHUNK), slice(b * LANES, (b + 1) * LANES)
            xr = [fc_ref[j, rows, lns].astype(F32) for j in range(N_TILES)]
            xi = [-fs_ref[j, rows, lns].astype(F32) for j in range(N_TILES)]
            ar, ai = _fft16(xr, xi)
            for k in range(N_TILES):
                a_ref[k, 0, rows, lns] = ar[k].astype(BF16)
                a_ref[k, 1, rows, lns] = ai[k].astype(BF16)


def _twiddled_dft_kernel(cb_ref, sb_ref, ca_ref, sa_ref, a_ref, o_ref):
    k1 = pl.program_id(1)
    dot = functools.partial(jnp.dot, preferred_element_type=F32)
    ca, sa = ca_ref[pl.ds(k1, 1), :], sa_ref[pl.ds(k1, 1), :]
    cb, sb = cb_ref[...], sb_ref[...]
    dc = (cb * ca - sb * sa).astype(BF16)
    ds = (sb * ca + cb * sa).astype(BF16)
    out = dot(dc, a_ref[0]) + dot(ds, a_ref[1])
    o_ref[...] = out.astype(BF16).reshape(o_ref.shape)


def _pos_dft_long(fc, fs):
    n, l, wdt = fc.shape
    assert l == N_TILES * SEG
    rb = 32
    tiles = pl.BlockSpec((None, N_TILES, rb, wdt), lambda b, r: (b, 0, r, 0))
    a = pl.pallas_call(
        _tiles_fft_kernel,
        grid=(n, SEG // rb),
        in_specs=[tiles, tiles],
        out_specs=pl.BlockSpec((None, N_TILES, 2, rb, wdt), lambda b, r: (b, 0, 0, r, 0)),
        out_shape=jax.ShapeDtypeStruct((n, N_TILES, 2, SEG, wdt), BF16),
        compiler_params=_cparams(("parallel", "parallel")),
        name="tiles_fft",
    )(fc.reshape(n, N_TILES, SEG, wdt), fs.reshape(n, N_TILES, SEG, wdt))

    cb, sb = _tile_dft_tables(1.0 / math.sqrt(l), rows_in_tile_order=False)
    pos = _tile_order_positions(SEG)
    ang = 2.0 * np.pi * (np.arange(N_TILES)[:, None] * pos[None, :]) / l
    ca, sa = jnp.asarray(np.cos(ang), F32), jnp.asarray(np.sin(ang), F32)
    const2 = lambda shape: pl.BlockSpec(shape, lambda b, k: (0, 0), pipeline_mode=pl.Buffered(1))
    out = pl.pallas_call(
        _twiddled_dft_kernel,
        grid=(n, N_TILES),
        in_specs=[const2((SEG, SEG)), const2((SEG, SEG)), const2((N_TILES, SEG)), const2((N_TILES, SEG)),
                  pl.BlockSpec((None, None, 2, SEG, wdt), lambda b, k: (b, k, 0, 0, 0))],
        out_specs=pl.BlockSpec((None, N_TILES, None, SEG_CHUNKS, wdt), lambda b, k: (b, 0, k, 0, 0)),
        out_shape=jax.ShapeDtypeStruct((n, N_TILES, CHUNK, SEG_CHUNKS, wdt), BF16),
        compiler_params=_cparams(("parallel", "parallel")),
        name="twiddled_dft",
    )(cb, sb, ca, sa, a)
    return out.reshape(n, l, wdt)


def _mix_ffn_kernel(x_ref, mod_ref, ys_ref, yf_ref, gs_ref, gf_ref,
                    wglu_ref, bglu_ref, wps_ref, wpf_ref, wout_ref, n2_ref,
                    wg_ref, wu_ref, wd_ref, fn_ref, o_ref, m_scr, mb_scr):
    tm = x_ref.shape[0]
    dot = functools.partial(jnp.dot, preferred_element_type=F32)
    mod = mod_ref[0]
    y = jnp.concatenate([ys_ref[b] for b in range(S5_WIDTH // LANES)], axis=1)
    z = jax.nn.gelu(y)
    z = z * jax.nn.sigmoid(dot(z.astype(BF16), wglu_ref[...]) + bglu_ref[...])
    m = (gs_ref[...].astype(F32) * dot(z.astype(BF16), wps_ref[...])
         + gf_ref[...].astype(F32) * dot(yf_ref[...], wpf_ref[...]))
    for k in range(D_MODEL // LANES):
        m_scr[k] = m[:, k * LANES:(k + 1) * LANES]
    for j in range(tm // SEG):
        for c in range(SEG_CHUNKS):
            r0 = j * SEG + c * CHUNK
            for k in range(D_MODEL // LANES):
                mb_scr[r0:r0 + CHUNK, k * LANES:(k + 1) * LANES] = (
                    m_scr[k, pl.ds(j * SEG + c, CHUNK, stride=SEG_CHUNKS), :].astype(BF16))
    x1 = x_ref[...] + mod[2:3] * dot(mb_scr[...], wout_ref[...])
    h2 = (_rms(x1, n2_ref[...]) * (1.0 + mod[4:5]) + mod[3:4]).astype(BF16)
    gate = dot(h2, wg_ref[...])
    up = dot(h2, wu_ref[...])
    ff = dot((gate * jax.nn.sigmoid(gate) * up).astype(BF16), wd_ref[...])
    x2 = x1 + mod[5:6] * ff
    o_ref[...] = _rms(x2, fn_ref[...])


def _mix_ffn(x2d, mod, mod_row, ys, yf, gs, gf, wts, tm):
    t = x2d.shape[0]
    tok = lambda w: pl.BlockSpec((tm, w), lambda i: (i, 0))
    return pl.pallas_call(
        _mix_ffn_kernel,
        grid=(t // tm,),
        in_specs=[tok(D_MODEL),
                  pl.BlockSpec((1, N_MOD, D_MODEL), lambda i: (mod_row(i), 0, 0)),
                  pl.BlockSpec((S5_WIDTH // LANES, tm, LANES), lambda i: (0, i, 0)),
                  tok(FFT_WIDTH), tok(D_MODEL), tok(D_MODEL)]
                 + [_const_spec(w.shape) for w in wts],
        out_specs=tok(D_MODEL),
        out_shape=jax.ShapeDtypeStruct((t, D_MODEL), F32),
        scratch_shapes=[pltpu.VMEM((D_MODEL // LANES, tm, LANES), F32),
                        pltpu.VMEM((tm, D_MODEL), BF16)],
        compiler_params=_cparams(("parallel",)),
        name="mix_ffn",
    )(x2d, mod, ys, yf, gs, gf, *wts)


def kernel(x_prompt, x_sample, state_s5, c, c_ctx, norm1_g, norm2_g, w_ada, b_ada, w_in,
           s5_lambda_re, s5_lambda_im, s5_log_step, s5_b_re, s5_b_im, s5_c_re, s5_c_im,
           s5_d, w_glu, b_glu, w_proj_s5, w_proj_fft, w_out, w_ffn_gate, w_ffn_up,
           w_ffn_down, final_norm_g):
    nb, sl, _ = x_prompt.shape
    db, dl, _ = x_sample.shape
    assert w_in.shape[0] == 1 and sl == SEG and nb == N_PROMPT_SEG
    assert db == N_SAMPLE_SEQ and dl == SEG * SEG_CHUNKS

    cvec = jnp.concatenate([c_ctx[None], c, jnp.zeros((8 - 1 - db, D_MODEL), F32)], axis=0)
    mod = _adaln(cvec, w_ada[0], b_ada[0]).reshape(8, N_MOD, D_MODEL)

    m, w, e, apre, apim = _s5_tables(s5_lambda_re[0], s5_lambda_im[0], s5_log_step[0], s5_b_re[0],
                                     s5_b_im[0], s5_c_re[0], s5_c_im[0], s5_d[0])

    tm = 512
    xp = x_prompt.reshape(nb * sl, D_MODEL)
    xs = x_sample.reshape(db * dl, D_MODEL)
    row_p = lambda i: 0
    row_s = lambda i: 1 + i // (dl // tm)
    w_in_b = w_in[0].astype(BF16)
    xg_p, fc_p, fs_p, gs_p, gf_p = _inproj(xp, mod, row_p, norm1_g[0], w_in_b, tm)
    xg_s, fc_s, fs_s, gs_s, gf_s = _inproj(xs, mod, row_s, norm1_g[0], w_in_b, tm)

    h0 = state_s5[:, 0].astype(F32).transpose(1, 2, 0, 3, 4).reshape(4, db, GP)
    ys_p, ys_s, fin = _s5_conv(xg_p, xg_s, m, w, e, apre, apim, h0)
    new_state = fin.reshape(PAIRS, 2, 2, nb, 2, S5_STATE).transpose(3, 1, 2, 0, 4, 5)
    new_state = new_state.reshape(nb, 1, 2, 2, S5_GROUPS, S5_STATE)

    yf_p = _pos_dft_tile(fc_p.reshape(nb, sl, FFT_WIDTH), fs_p.reshape(nb, sl, FFT_WIDTH))
    yf_s = _pos_dft_long(fc_s.reshape(db, dl, FFT_WIDTH), fs_s.reshape(db, dl, FFT_WIDTH))

    wts = (w_glu[0].astype(BF16), b_glu[0].reshape(1, S5_WIDTH), w_proj_s5[0].astype(BF16),
           w_proj_fft[0].astype(BF16), w_out[0].astype(BF16), norm2_g[0].reshape(1, D_MODEL),
           w_ffn_gate[0].astype(BF16), w_ffn_up[0].astype(BF16), w_ffn_down[0].astype(BF16),
           final_norm_g.reshape(1, D_MODEL))
    tm2 = SEG
    row_s2 = lambda i: 1 + i // (dl // tm2)
    y_p = _mix_ffn(xp, mod, row_p, ys_p, yf_p.reshape(nb * sl, FFT_WIDTH), gs_p, gf_p, wts, tm2)
    y_s = _mix_ffn(xs, mod, row_s2, ys_s, yf_s.reshape(db * dl, FFT_WIDTH), gs_s, gf_s, wts, tm2)
    return (y_p.reshape(nb, sl, D_MODEL), y_s.reshape(db, dl, D_MODEL), new_state)
```

```python
import functools
import math

import numpy as np
import jax
import jax.numpy as jnp
from jax import lax
from jax.experimental import pallas as pl
from jax.experimental.pallas import tpu as pltpu

F32 = jnp.float32
BF16 = jnp.bfloat16
HIGHEST = lax.Precision.HIGHEST

D_MODEL = 1024
S5_WIDTH = 512
S5_GROUPS = 32
S5_GROUP = 16
S5_STATE = 64
FFT_WIDTH = 512
FFT_GROUPS = 4
FFT_GROUP = 128
D_FF = 2816
N_MOD = 6
EPS = 1e-6

LANES = 128
CHUNK = 16
SEG_CHUNKS = 16
SEG = CHUNK * SEG_CHUNKS
GP = S5_GROUPS * S5_STATE
PAIRS = S5_GROUPS // 2
CH = CHUNK * S5_GROUP
OCT = LANES // S5_GROUP

VMEM_LIMIT = 56 * 1024 * 1024


def _cparams(sem):
    return pltpu.CompilerParams(dimension_semantics=sem, vmem_limit_bytes=VMEM_LIMIT)


def _ada_kernel(c_ref, w_ref, b_ref, o_ref):
    c = c_ref[...]
    s = c * jax.nn.sigmoid(c)
    o_ref[...] = jnp.dot(s, w_ref[...], preferred_element_type=F32, precision=HIGHEST) + b_ref[...]


def _adaln(cvec8, w_ada, b_ada):
    n_out = w_ada.shape[1]
    tn = 1024
    return pl.pallas_call(
        _ada_kernel,
        grid=(n_out // tn,),
        in_specs=[pl.BlockSpec((8, D_MODEL), lambda j: (0, 0)),
                  pl.BlockSpec((D_MODEL, tn), lambda j: (0, j)),
                  pl.BlockSpec((1, tn), lambda j: (0, j))],
        out_specs=pl.BlockSpec((8, tn), lambda j: (0, j)),
        out_shape=jax.ShapeDtypeStruct((8, n_out), F32),
        compiler_params=_cparams(("arbitrary",)),
        name="adaln",
    )(cvec8, w_ada, b_ada.reshape(1, n_out))


def _prep_kernel(lre_ref, lim_ref, lst_ref, bre_ref, bim_ref, cre_ref, cim_ref, dt_ref,
                 apre_ref, apim_ref, m_ref, w_ref, e_ref, gb_scr, ca_scr, ts_scr):
    lre = lre_ref[...]
    lim = lim_ref[...]
    step = jnp.exp(lst_ref[...])
    mag = jnp.exp(lre * step)
    are = mag * jnp.cos(lim * step)
    aim = mag * jnp.sin(lim * step)
    nr = are - 1.0
    den = lre * lre + lim * lim
    fr = (nr * lre + aim * lim) / den
    fi = (aim * lre - nr * lim) / den

    pr = [jnp.ones_like(are)]
    pi = [jnp.zeros_like(are)]
    for _ in range(CHUNK):
        r, i = pr[-1], pi[-1]
        pr.append(r * are - i * aim)
        pi.append(r * aim + i * are)
    a16r, a16i = pr[CHUNK], pi[CHUNK]
    qr, qi = jnp.ones_like(are), jnp.zeros_like(are)
    for c in range(SEG_CHUNKS + 1):
        apre_ref[c] = qr
        apim_ref[c] = qi
        qr, qi = qr * a16r - qi * a16i, qr * a16i + qi * a16r

    for d in range(2):
        bbre = fr[d:d + 1] * bre_ref[d] - fi[d:d + 1] * bim_ref[d]
        bbim = fr[d:d + 1] * bim_ref[d] + fi[d:d + 1] * bre_ref[d]
        cre, cim = cre_ref[d], cim_ref[d]
        for s in range(CHUNK):
            k = CHUNK - 1 - s if d == 0 else s
            r, i = pr[k][d:d + 1], pi[k][d:d + 1]
            gb_scr[0, d, s] = r * bbre - i * bbim
            gb_scr[1, d, s] = r * bbim + i * bbre
            f = s + 1 if d == 0 else CHUNK - s
            r, i = pr[f][d:d + 1], pi[f][d:d + 1]
            ca_scr[0, d, s] = r * cre - i * cim
            ca_scr[1, d, s] = -(r * cim + i * cre)

    dot_nt = lambda a, b: lax.dot_general(a, b, (((1,), (1,)), ((), ())), preferred_element_type=F32)
    lane_gi = lax.broadcasted_iota(jnp.int32, (CH, LANES), 1) // S5_STATE
    row_gi = lax.broadcasted_iota(jnp.int32, (LANES, CH), 0) // S5_STATE
    eye = (lax.broadcasted_iota(jnp.int32, (LANES, LANES), 0)
           == lax.broadcasted_iota(jnp.int32, (LANES, LANES), 1)).astype(BF16)
    slot = lax.broadcasted_iota(jnp.int32, (CH, LANES), 1) // S5_GROUP
    n_lag_rows = (2 * CHUNK - 1) * S5_GROUP

    def pair(q, carry):
        lanes = pl.ds(pl.multiple_of(q * LANES, LANES), LANES)
        for d in range(2):
            for ri in range(2):
                col = slice((2 * d + ri) * LANES, (2 * d + ri + 1) * LANES)
                gb = gb_scr[ri, d, :, :, lanes].reshape(CH, LANES)
                ca = ca_scr[ri, d, :, :, lanes].reshape(CH, LANES).astype(BF16)
                ca_t = dot_nt(eye, ca)
                for gi in range(2):
                    w_ref[q, gi * CH:(gi + 1) * CH, col] = jnp.where(lane_gi == gi, gb, 0.0).astype(BF16)
                    e_ref[q, col, gi * CH:(gi + 1) * CH] = jnp.where(row_gi == gi, ca_t, 0.0).astype(BF16)
        for gi in range(2):
            g = 2 * q + gi
            lag = []
            for d in range(2):
                c_re = jnp.concatenate([cre_ref[d, :, lanes]] * OCT, axis=0)
                c_imn = jnp.concatenate([-cim_ref[d, :, lanes]] * OCT, axis=0)
                keep = lax.broadcasted_iota(jnp.int32, (LANES, LANES), 1) // S5_STATE == gi
                c_re = jnp.where(keep, c_re, 0.0).astype(BF16)
                c_imn = jnp.where(keep, c_imn, 0.0).astype(BF16)
                gre = gb_scr[0, d, :, :, lanes].reshape(CH, LANES).astype(BF16)
                gim = gb_scr[1, d, :, :, lanes].reshape(CH, LANES).astype(BF16)
                lag.append(dot_nt(gre, c_re) + dot_nt(gim, c_imn))
            zl = (CHUNK - 1) * S5_GROUP
            ts_scr[0:zl, :] = lag[0][0:zl]
            ts_scr[zl:zl + S5_GROUP, :] = lag[0][zl:] + lag[1][0:S5_GROUP] + dt_ref[g]
            ts_scr[zl + S5_GROUP:n_lag_rows, :] = lag[1][S5_GROUP:]
            for hf in range(CHUNK // OCT):
                acc = None
                for s in range(OCT):
                    t = OCT * hf + s
                    win = ts_scr[(CHUNK - 1 - t) * S5_GROUP:(CHUNK - 1 - t) * S5_GROUP + CH, :]
                    acc = win if acc is None else jnp.where(slot == s, win, acc)
                m_ref[g, :, hf * LANES:(hf + 1) * LANES] = acc.astype(BF16)
        return carry

    lax.fori_loop(0, PAIRS, pair, 0)


def _s5_tables(lam_re, lam_im, log_step, b_re, b_im, c_re, c_im, s5_d):
    lre = lam_re.reshape(2, GP)
    lim = lam_im.reshape(2, GP)
    lst = jnp.repeat(log_step, S5_STATE, axis=-1)
    bre = b_re.transpose(0, 3, 1, 2).reshape(2, S5_GROUP, GP)
    bim = b_im.transpose(0, 3, 1, 2).reshape(2, S5_GROUP, GP)
    cre = c_re.transpose(0, 2, 1, 3).reshape(2, S5_GROUP, GP)
    cim = c_im.transpose(0, 2, 1, 3).reshape(2, S5_GROUP, GP)
    eye = jnp.asarray(np.eye(S5_GROUP, dtype=np.float32))
    dt = jnp.tile(eye[None] * s5_d.reshape(S5_GROUPS, 1, S5_GROUP), (1, 1, OCT))

    vm = pl.BlockSpec(memory_space=pltpu.VMEM)
    tab = (2, 2, CHUNK, S5_GROUP, GP)
    apre, apim, m, w, e = pl.pallas_call(
        _prep_kernel,
        in_specs=[vm] * 8,
        out_specs=[vm] * 5,
        out_shape=[jax.ShapeDtypeStruct((SEG_CHUNKS + 1, 2, GP), F32),
                   jax.ShapeDtypeStruct((SEG_CHUNKS + 1, 2, GP), F32),
                   jax.ShapeDtypeStruct((S5_GROUPS, CH, CH), BF16),
                   jax.ShapeDtypeStruct((PAIRS, 2 * CH, 4 * LANES), BF16),
                   jax.ShapeDtypeStruct((PAIRS, 4 * LANES, 2 * CH), BF16)],
        scratch_shapes=[pltpu.VMEM(tab, F32),
                        pltpu.VMEM(tab, F32),
                        pltpu.VMEM(((2 * CHUNK - 1) * S5_GROUP, LANES), F32)],
        compiler_params=pltpu.CompilerParams(vmem_limit_bytes=VMEM_LIMIT),
        name="s5_tables",
    )(lre, lim, lst, bre, bim, cre, cim, dt)
    return m, w, e, apre, apim


def _rms(x, g):
    return x * lax.rsqrt(jnp.mean(x * x, axis=-1, keepdims=True) + EPS) * g


def _slot_transpose(v):
    v = list(v)
    slot = lax.broadcasted_iota(jnp.int32, v[0].shape, 1) // S5_GROUP
    for k in (4, 2, 1):
        low = (slot & k) == 0
        for i in range(OCT):
            if i & k:
                continue
            a, b = v[i], v[i + k]
            v[i] = jnp.where(low, a, pltpu.roll(b, k * S5_GROUP, axis=1))
            v[i + k] = jnp.where(low, pltpu.roll(a, LANES - k * S5_GROUP, axis=1), b)
    return v


def _inproj_kernel(x_ref, mod_ref, g_ref, win_ref, bdc_ref, bds_ref,
                   xg_ref, fc_ref, fs_ref, gs_ref, gf_ref, h_scr, hb_scr, zs_scr):
    tm = x_ref.shape[0]
    mod = mod_ref[0]
    h = _rms(x_ref[...], g_ref[...]) * (1.0 + mod[1:2]) + mod[0:1]
    for k in range(D_MODEL // LANES):
        h_scr[k] = h[:, k * LANES:(k + 1) * LANES]
    for j in range(tm // SEG):
        for s in range(CHUNK):
            r0 = j * SEG + s * SEG_CHUNKS
            for k in range(D_MODEL // LANES):
                hb_scr[r0:r0 + SEG_CHUNKS, k * LANES:(k + 1) * LANES] = (
                    h_scr[k, pl.ds(j * SEG + s, SEG_CHUNKS, stride=CHUNK), :].astype(BF16))
    hb = hb_scr[...]
    dot = functools.partial(jnp.dot, preferred_element_type=F32)
    zs_scr[...] = dot(hb, win_ref[:, 0:S5_WIDTH])
    for j in range(tm // SEG):
        for b in range(S5_WIDTH // LANES):
            for hf in range(CHUNK // OCT):
                z = [zs_scr[j * SEG + s * SEG_CHUNKS:j * SEG + (s + 1) * SEG_CHUNKS, b * LANES:(b + 1) * LANES]
                     for s in range(OCT * hf, OCT * (hf + 1))]
                for i, xi in enumerate(_slot_transpose(z)):
                    xg_ref[OCT * b + i, j * SEG_CHUNKS:(j + 1) * SEG_CHUNKS,
                           hf * LANES:(hf + 1) * LANES] = xi.astype(BF16)
    uf = dot(hb, win_ref[:, S5_WIDTH:2 * S5_WIDTH]).astype(BF16)
    fc_ref[...] = dot(uf, bdc_ref[...]).astype(BF16)
    fs_ref[...] = dot(uf, bds_ref[...]).astype(BF16)
    o = 2 * S5_WIDTH
    gs_ref[...] = jax.nn.sigmoid(dot(hb, win_ref[:, o:o + D_MODEL])).astype(BF16)
    gf_ref[...] = jax.nn.sigmoid(dot(hb, win_ref[:, o + D_MODEL:o + 2 * D_MODEL])).astype(BF16)


def _channel_dft_mats():
    j = np.arange(FFT_GROUP)
    ang = 2.0 * np.pi * ((j[:, None] * j[None, :]) % FFT_GROUP) / FFT_GROUP
    blk_c = np.cos(ang) / math.sqrt(FFT_GROUP)
    blk_s = np.sin(ang) / math.sqrt(FFT_GROUP)
    bdc = np.kron(np.eye(FFT_GROUPS), blk_c)
    bds = np.kron(np.eye(FFT_GROUPS), blk_s)
    return jnp.asarray(bdc, F32).astype(BF16), jnp.asarray(bds, F32).astype(BF16)


def _const_spec(shape):
    nd = len(shape)
    return pl.BlockSpec(shape, lambda i: (0,) * nd, pipeline_mode=pl.Buffered(1))


def _inproj(x2d, mod, mod_row, norm_g, w_in_b, tm):
    t = x2d.shape[0]
    bdc, bds = _channel_dft_mats()
    tok = lambda w: pl.BlockSpec((tm, w), lambda i: (i, 0))
    out = lambda w: jax.ShapeDtypeStruct((t, w), BF16)
    return pl.pallas_call(
        _inproj_kernel,
        grid=(t // tm,),
        in_specs=[tok(D_MODEL),
                  pl.BlockSpec((1, N_MOD, D_MODEL), lambda i: (mod_row(i), 0, 0)),
                  _const_spec((1, D_MODEL)),
                  _const_spec(w_in_b.shape),
                  _const_spec(bdc.shape),
                  _const_spec(bds.shape)],
        out_specs=[pl.BlockSpec((S5_GROUPS, tm // CHUNK, CH), lambda i: (0, i, 0)),
                   tok(FFT_WIDTH), tok(FFT_WIDTH), tok(D_MODEL), tok(D_MODEL)],
        out_shape=[jax.ShapeDtypeStruct((S5_GROUPS, t // CHUNK, CH), BF16),
                   out(FFT_WIDTH), out(FFT_WIDTH), out(D_MODEL), out(D_MODEL)],
        scratch_shapes=[pltpu.VMEM((D_MODEL // LANES, tm, LANES), F32),
                        pltpu.VMEM((tm, D_MODEL), BF16),
                        pltpu.VMEM((tm, S5_WIDTH), F32)],
        compiler_params=_cparams(("parallel",)),
        name="inproj",
    )(x2d, mod, norm_g.reshape(1, D_MODEL), w_in_b, bdc, bds)


N_PROMPT_SEG = 16
N_SAMPLE_SEG = 32
N_SAMPLE_SEQ = 2
N_SEG = N_PROMPT_SEG + N_SAMPLE_SEG
ROWS = SEG_CHUNKS * N_SEG
ROWS_P = SEG_CHUNKS * N_PROMPT_SEG
OCT_PAIRS = OCT // 2


def _cmul_add(ar, ai, hr, hi, sr, si):
    return ar * hr - ai * hi + sr, ar * hi + ai * hr + si


def _s5_kernel(xp_ref, xs_ref, m_ref, w_ref, e_ref, apre_ref, apim_ref, h0_ref,
               yp_ref, ys_ref, fin_ref, s_scr, hin_scr, hinp_scr, f_scr, hs_scr, y_scr):
    dot = functools.partial(jnp.dot, preferred_element_type=F32)
    parts = ((xp_ref, 0, ROWS_P), (xs_ref, ROWS_P, ROWS - ROWS_P))
    seg_rows = lambda c: pl.ds(c, N_SEG, stride=SEG_CHUNKS)
    blk = lambda c: pl.ds(c * N_SEG, N_SEG)
    seq_rows = lambda j: pl.ds(j, N_SAMPLE_SEQ, stride=SEG_CHUNKS)
    lat = lambda c: pl.ds(c * N_SEG + N_PROMPT_SEG, N_SAMPLE_SEG)
    zero = jnp.zeros((N_SEG, LANES), F32)

    for pr in range(OCT_PAIRS):
        ln = slice(pr * LANES, (pr + 1) * LANES)
        g0, g1 = 2 * pr, 2 * pr + 1
        for x_ref, r0, nr in parts:
            s = dot(x_ref[g0], w_ref[pr, 0:CH, :]) + dot(x_ref[g1], w_ref[pr, CH:2 * CH, :])
            for k in range(4):
                s_scr[k, r0:r0 + nr, :] = s[:, k * LANES:(k + 1) * LANES]

        ar, ai = apre_ref[1, 0:1, ln], apim_ref[1, 0:1, ln]
        hr, hi = zero, zero
        for c in range(SEG_CHUNKS):
            hin_scr[0, blk(c), :] = hr
            hin_scr[1, blk(c), :] = hi
            hr, hi = _cmul_add(ar, ai, hr, hi, s_scr[0, seg_rows(c), :], s_scr[1, seg_rows(c), :])
        fin_ref[pr, 0] = hr[0:N_PROMPT_SEG]
        fin_ref[pr, 1] = hi[0:N_PROMPT_SEG]
        f_scr[0] = hr[N_PROMPT_SEG:]
        f_scr[1] = hi[N_PROMPT_SEG:]
        br, bi = apre_ref[1, 1:2, ln], apim_ref[1, 1:2, ln]
        gr, gi = zero, zero
        for c in range(SEG_CHUNKS - 1, -1, -1):
            hin_scr[2, blk(c), :] = gr
            hin_scr[3, blk(c), :] = gi
            gr, gi = _cmul_add(br, bi, gr, gi, s_scr[2, seg_rows(c), :], s_scr[3, seg_rows(c), :])
        fin_ref[pr, 2] = gr[0:N_PROMPT_SEG]
        fin_ref[pr, 3] = gi[0:N_PROMPT_SEG]
        f_scr[2] = gr[N_PROMPT_SEG:]
        f_scr[3] = gi[N_PROMPT_SEG:]

        a2r, a2i = apre_ref[SEG_CHUNKS, 0:1, ln], apim_ref[SEG_CHUNKS, 0:1, ln]
        hr, hi = h0_ref[0, :, ln], h0_ref[1, :, ln]
        for j in range(SEG_CHUNKS):
            hs_scr[0, seq_rows(j), :] = hr
            hs_scr[1, seq_rows(j), :] = hi
            hr, hi = _cmul_add(a2r, a2i, hr, hi, f_scr[0, seq_rows(j), :], f_scr[1, seq_rows(j), :])
        b2r, b2i = apre_ref[SEG_CHUNKS, 1:2, ln], apim_ref[SEG_CHUNKS, 1:2, ln]
        gr, gi = h0_ref[2, :, ln], h0_ref[3, :, ln]
        for j in range(SEG_CHUNKS - 1, -1, -1):
            hs_scr[2, seq_rows(j), :] = gr
            hs_scr[3, seq_rows(j), :] = gi
            gr, gi = _cmul_add(b2r, b2i, gr, gi, f_scr[2, seq_rows(j), :], f_scr[3, seq_rows(j), :])
        for c in range(SEG_CHUNKS):
            p_r, p_i = apre_ref[c, 0:1, ln], apim_ref[c, 0:1, ln]
            hr, hi = _cmul_add(p_r, p_i, hs_scr[0], hs_scr[1], hin_scr[0, lat(c), :], hin_scr[1, lat(c), :])
            hin_scr[0, lat(c), :] = hr
            hin_scr[1, lat(c), :] = hi
            cb = SEG_CHUNKS - 1 - c
            p_r, p_i = apre_ref[cb, 1:2, ln], apim_ref[cb, 1:2, ln]
            gr, gi = _cmul_add(p_r, p_i, hs_scr[2], hs_scr[3], hin_scr[2, lat(c), :], hin_scr[3, lat(c), :])
            hin_scr[2, lat(c), :] = gr
            hin_scr[3, lat(c), :] = gi

        for k in range(4):
            for sg in range(N_SEG):
                hinp_scr[sg * SEG_CHUNKS:(sg + 1) * SEG_CHUNKS, k * LANES:(k + 1) * LANES] = (
                    hin_scr[k, pl.ds(sg, SEG_CHUNKS, stride=N_SEG), :].astype(BF16))
        for gi_, g in ((0, g0), (1, g1)):
            for x_ref, r0, nr in parts:
                y_scr[g, r0:r0 + nr, :] = (dot(x_ref[g], m_ref[g])
                                           + dot(hinp_scr[r0:r0 + nr, :], e_ref[pr, :, gi_ * CH:(gi_ + 1) * CH]))

    rb_rows = 2 * SEG_CHUNKS
    for rb in range(ROWS // rb_rows):
        for hf in range(CHUNK // OCT):
            v = [y_scr[i, rb * rb_rows:(rb + 1) * rb_rows, hf * LANES:(hf + 1) * LANES] for i in range(OCT)]
            for s, acc in enumerate(_slot_transpose(v)):
                t = OCT * hf + s
                for sg in range(rb_rows // SEG_CHUNKS):
                    seg = rb * (rb_rows // SEG_CHUNKS) + sg
                    piece = acc[sg * SEG_CHUNKS:(sg + 1) * SEG_CHUNKS]
                    if seg < N_PROMPT_SEG:
                        r0 = seg * SEG + t * SEG_CHUNKS
                        yp_ref[0, r0:r0 + SEG_CHUNKS, :] = piece
                    else:
                        r0 = (seg - N_PROMPT_SEG) * SEG + t * SEG_CHUNKS
                        ys_ref[0, r0:r0 + SEG_CHUNKS, :] = piece


def _s5_conv(xg_p, xg_s, m, w, e, apre, apim, h0):
    n_oct = S5_GROUPS // OCT
    tp, ts = xg_p.shape[1] * CHUNK, xg_s.shape[1] * CHUNK
    return pl.pallas_call(
        _s5_kernel,
        grid=(n_oct,),
        in_specs=[pl.BlockSpec((OCT, ROWS_P, CH), lambda o: (o, 0, 0)),
                  pl.BlockSpec((OCT, ROWS - ROWS_P, CH), lambda o: (o, 0, 0)),
                  pl.BlockSpec((OCT, CH, CH), lambda o: (o, 0, 0)),
                  pl.BlockSpec((OCT_PAIRS, 2 * CH, 4 * LANES), lambda o: (o, 0, 0)),
                  pl.BlockSpec((OCT_PAIRS, 4 * LANES, 2 * CH), lambda o: (o, 0, 0)),
                  pl.BlockSpec((SEG_CHUNKS + 1, 2, OCT_PAIRS * LANES), lambda o: (0, 0, o)),
                  pl.BlockSpec((SEG_CHUNKS + 1, 2, OCT_PAIRS * LANES), lambda o: (0, 0, o)),
                  pl.BlockSpec((4, N_SAMPLE_SEQ, OCT_PAIRS * LANES), lambda o: (0, 0, o))],
        out_specs=[pl.BlockSpec((1, tp, LANES), lambda o: (o, 0, 0)),
                   pl.BlockSpec((1, ts, LANES), lambda o: (o, 0, 0)),
                   pl.BlockSpec((OCT_PAIRS, 4, N_PROMPT_SEG, LANES), lambda o: (o, 0, 0, 0))],
        out_shape=[jax.ShapeDtypeStruct((n_oct, tp, LANES), F32),
                   jax.ShapeDtypeStruct((n_oct, ts, LANES), F32),
                   jax.ShapeDtypeStruct((PAIRS, 4, N_PROMPT_SEG, LANES), F32)],
        scratch_shapes=[pltpu.VMEM((4, ROWS, LANES), F32),
                        pltpu.VMEM((4, ROWS, LANES), F32),
                        pltpu.VMEM((ROWS, 4 * LANES), BF16),
                        pltpu.VMEM((4, N_SAMPLE_SEG, LANES), F32),
                        pltpu.VMEM((4, N_SAMPLE_SEG, LANES), F32),
                        pltpu.VMEM((OCT, ROWS, CH), F32)],
        compiler_params=_cparams(("parallel",)),
        name="s5_conv",
    )(xg_p, xg_s, m, w, e, apre, apim, h0)


def _tile_order_positions(l):
    r = np.arange(l)
    tile, wi = r // SEG, r % SEG
    return tile * SEG + (wi % SEG_CHUNKS) * CHUNK + wi // SEG_CHUNKS


def _tile_dft_tables(scale, rows_in_tile_order=True):
    pos = _tile_order_positions(SEG)
    freq = pos if rows_in_tile_order else np.arange(SEG)
    ang = 2.0 * np.pi * ((freq[:, None] * pos[None, :]) % SEG) / SEG
    return jnp.asarray(np.cos(ang) * scale, F32), jnp.asarray(np.sin(ang) * scale, F32)


def _pos_dft_tile_kernel(cm_ref, sm_ref, fc_ref, fs_ref, o_ref):
    dot = functools.partial(jnp.dot, preferred_element_type=F32)
    o_ref[0] = (dot(cm_ref[...].astype(BF16), fc_ref[0]) - dot(sm_ref[...].astype(BF16), fs_ref[0])).astype(BF16)


def _pos_dft_tile(fc, fs):
    n, l, wdt = fc.shape
    cm, sm = _tile_dft_tables(1.0 / math.sqrt(l))
    seq = pl.BlockSpec((1, l, wdt), lambda b: (b, 0, 0))
    return pl.pallas_call(
        _pos_dft_tile_kernel,
        grid=(n,),
        in_specs=[_const_spec((l, l)), _const_spec((l, l)), seq, seq],
        out_specs=seq,
        out_shape=jax.ShapeDtypeStruct((n, l, wdt), BF16),
        compiler_params=_cparams(("parallel",)),
        name="pos_dft_tile",
    )(cm, sm, fc, fs)


N_TILES = 16


def _fft16(xr, xi):
    n = len(xr)
    rev = [int(format(i, "04b")[::-1], 2) for i in range(n)]
    ar = [xr[r] for r in rev]
    ai = [xi[r] for r in rev]
    size = 2
    while size <= n:
        half = size // 2
        for start in range(0, n, size):
            for k in range(half):
                wr = math.cos(2.0 * math.pi * k / size)
                wi = -math.sin(2.0 * math.pi * k / size)
                i0, i1 = start + k, start + k + half
                if k == 0:
                    tr, ti = ar[i1], ai[i1]
                elif 4 * k == size:
                    tr, ti = ai[i1], -ar[i1]
                else:
                    tr = ar[i1] * wr - ai[i1] * wi
                    ti = ar[i1] * wi + ai[i1] * wr
                ar[i1], ai[i1] = ar[i0] - tr, ai[i0] - ti
                ar[i0], ai[i0] = ar[i0] + tr, ai[i0] + ti
        size *= 2
    return ar, ai


def _tiles_fft_kernel(fc_ref, fs_ref, a_ref):
    rb, wdt = fc_ref.shape[1], fc_ref.shape[2]
    for r in range(rb // CHUNK):
        for b in range(wdt // LANES):
            rows, lns = slice(r * CHUNK, (r + 1) * CHUNK), slice(b * LANES, (b + 1) * LANES)
            xr = [fc_ref[j, rows, lns].astype(F32) for j in range(N_TILES)]
            xi = [-fs_ref[j, rows, lns].astype(F32) for j in range(N_TILES)]
            ar, ai = _fft16(xr, xi)
            for k in range(N_TILES):
                a_ref[k, 0, rows, lns] = ar[k].astype(BF16)
                a_ref[k, 1, rows, lns] = ai[k].astype(BF16)


def _twiddled_dft_kernel(cb_ref, sb_ref, ca_ref, sa_ref, a_ref, o_ref):
    k1 = pl.program_id(1)
    dot = functools.partial(jnp.dot, preferred_element_type=F32)
    ca, sa = ca_ref[pl.ds(k1, 1), :], sa_ref[pl.ds(k1, 1), :]
    cb, sb = cb_ref[...], sb_ref[...]
    dc = (cb * ca - sb * sa).astype(BF16)
    ds = (sb * ca + cb * sa).astype(BF16)
    out = dot(dc, a_ref[0]) + dot(ds, a_ref[1])
    o_ref[...] = out.astype(BF16).reshape(o_ref.shape)


def _pos_dft_long(fc, fs):
    n, l, wdt = fc.shape
    assert l == N_TILES * SEG
    rb = 32
    tiles = pl.BlockSpec((None, N_TILES, rb, wdt), lambda b, r: (b, 0, r, 0))
    a = pl.pallas_call(
        _tiles_fft_kernel,
        grid=(n, SEG // rb),
        in_specs=[tiles, tiles],
        out_specs=pl.BlockSpec((None, N_TILES, 2, rb, wdt), lambda b, r: (b, 0, 0, r, 0)),
        out_shape=jax.ShapeDtypeStruct((n, N_TILES, 2, SEG, wdt), BF16),
        compiler_params=_cparams(("parallel", "parallel")),
        name="tiles_fft",
    )(fc.reshape(n, N_TILES, SEG, wdt), fs.reshape(n, N_TILES, SEG, wdt))

    cb, sb = _tile_dft_tables(1.0 / math.sqrt(l), rows_in_tile_order=False)
    pos = _tile_order_positions(SEG)
    ang = 2.0 * np.pi * (np.arange(N_TILES)[:, None] * pos[None, :]) / l
    ca, sa = jnp.asarray(np.cos(ang), F32), jnp.asarray(np.sin(ang), F32)
    const2 = lambda shape: pl.BlockSpec(shape, lambda b, k: (0, 0), pipeline_mode=pl.Buffered(1))
    out = pl.pallas_call(
        _twiddled_dft_kernel,
        grid=(n, N_TILES),
        in_specs=[const2((SEG, SEG)), const2((SEG, SEG)), const2((N_TILES, SEG)), const2((N_TILES, SEG)),
                  pl.BlockSpec((None, None, 2, SEG, wdt), lambda b, k: (b, k, 0, 0, 0))],
        out_specs=pl.BlockSpec((None, N_TILES, None, SEG_CHUNKS, wdt), lambda b, k: (b, 0, k, 0, 0)),
        out_shape=jax.ShapeDtypeStruct((n, N_TILES, CHUNK, SEG_CHUNKS, wdt), BF16),
        compiler_params=_cparams(("parallel", "parallel")),
        name="twiddled_dft",
    )(cb, sb, ca, sa, a)
    return out.reshape(n, l, wdt)


def _mix_ffn_kernel(x_ref, mod_ref, ys_ref, yf_ref, gs_ref, gf_ref,
                    wglu_ref, bglu_ref, wps_ref, wpf_ref, wout_ref, n2_ref,
                    wg_ref, wu_ref, wd_ref, fn_ref, o_ref, m_scr, mb_scr):
    tm = x_ref.shape[0]
    dot = functools.partial(jnp.dot, preferred_element_type=F32)
    mod = mod_ref[0]
    y = jnp.concatenate([ys_ref[b] for b in range(S5_WIDTH // LANES)], axis=1)
    z = jax.nn.gelu(y)
    z = z * jax.nn.sigmoid(dot(z.astype(BF16), wglu_ref[...]) + bglu_ref[...])
    m = (gs_ref[...].astype(F32) * dot(z.astype(BF16), wps_ref[...])
         + gf_ref[...].astype(F32) * dot(yf_ref[...], wpf_ref[...]))
    for k in range(D_MODEL // LANES):
        m_scr[k] = m[:, k * LANES:(k + 1) * LANES]
    for j in range(tm // SEG):
        for c in range(SEG_CHUNKS):
            r0 = j * SEG + c * CHUNK
            for k in range(D_MODEL // LANES):
                mb_scr[r0:r0 + CHUNK, k * LANES:(k + 1) * LANES] = (
                    m_scr[k, pl.ds(j * SEG + c, CHUNK, stride=SEG_CHUNKS), :].astype(BF16))
    x1 = x_ref[...] + mod[2:3] * dot(mb_scr[...], wout_ref[...])
    h2 = (_rms(x1, n2_ref[...]) * (1.0 + mod[4:5]) + mod[3:4]).astype(BF16)
    gate = dot(h2, wg_ref[...])
    up = dot(h2, wu_ref[...])
    ff = dot((gate * jax.nn.sigmoid(gate) * up).astype(BF16), wd_ref[...])
    x2 = x1 + mod[5:6] * ff
    o_ref[...] = _rms(x2, fn_ref[...])


def _mix_ffn(x2d, mod, mod_row, ys, yf, gs, gf, wts, tm):
    t = x2d.shape[0]
    tok = lambda w: pl.BlockSpec((tm, w), lambda i: (i, 0))
    return pl.pallas_call(
        _mix_ffn_kernel,
        grid=(t // tm,),
        in_specs=[tok(D_MODEL),
                  pl.BlockSpec((1, N_MOD, D_MODEL), lambda i: (mod_row(i), 0, 0)),
                  pl.BlockSpec((S5_WIDTH // LANES, tm, LANES), lambda i: (0, i, 0)),
                  tok(FFT_WIDTH), tok(D_MODEL), tok(D_MODEL)]
                 + [_const_spec(w.shape) for w in wts],
        out_specs=tok(D_MODEL),
        out_shape=jax.ShapeDtypeStruct((t, D_MODEL), F32),
        scratch_shapes=[pltpu.VMEM((D_MODEL // LANES, tm, LANES), F32),
                        pltpu.VMEM((tm, D_MODEL), BF16)],
        compiler_params=_cparams(("parallel",)),
        name="mix_ffn",
    )(x2d, mod, ys, yf, gs, gf, *wts)


def kernel(x_prompt, x_sample, state_s5, c, c_ctx, norm1_g, norm2_g, w_ada, b_ada, w_in,
           s5_lambda_re, s5_lambda_im, s5_log_step, s5_b_re, s5_b_im, s5_c_re, s5_c_im,
           s5_d, w_glu, b_glu, w_proj_s5, w_proj_fft, w_out, w_ffn_gate, w_ffn_up,
           w_ffn_down, final_norm_g):
    nb, sl, _ = x_prompt.shape
    db, dl, _ = x_sample.shape
    assert w_in.shape[0] == 1 and sl == SEG and nb == N_PROMPT_SEG
    assert db == N_SAMPLE_SEQ and dl == SEG * SEG_CHUNKS

    cvec = jnp.concatenate([c_ctx[None], c, jnp.zeros((8 - 1 - db, D_MODEL), F32)], axis=0)
    mod = _adaln(cvec, w_ada[0], b_ada[0]).reshape(8, N_MOD, D_MODEL)

    m, w, e, apre, apim = _s5_tables(s5_lambda_re[0], s5_lambda_im[0], s5_log_step[0], s5_b_re[0],
                                     s5_b_im[0], s5_c_re[0], s5_c_im[0], s5_d[0])

    tm = 512
    xp = x_prompt.reshape(nb * sl, D_MODEL)
    xs = x_sample.reshape(db * dl, D_MODEL)
    row_p = lambda i: 0
    row_s = lambda i: 1 + i // (dl // tm)
    w_in_b = w_in[0].astype(BF16)
    xg_p, fc_p, fs_p, gs_p, gf_p = _inproj(xp, mod, row_p, norm1_g[0], w_in_b, tm)
    xg_s, fc_s, fs_s, gs_s, gf_s = _inproj(xs, mod, row_s, norm1_g[0], w_in_b, tm)

    h0 = state_s5[:, 0].astype(F32).transpose(1, 2, 0, 3, 4).reshape(4, db, GP)
    ys_p, ys_s, fin = _s5_conv(xg_p, xg_s, m, w, e, apre, apim, h0)
    new_state = fin.reshape(PAIRS, 2, 2, nb, 2, S5_STATE).transpose(3, 1, 2, 0, 4, 5)
    new_state = new_state.reshape(nb, 1, 2, 2, S5_GROUPS, S5_STATE)

    yf_p = _pos_dft_tile(fc_p.reshape(nb, sl, FFT_WIDTH), fs_p.reshape(nb, sl, FFT_WIDTH))
    yf_s = _pos_dft_long(fc_s.reshape(db, dl, FFT_WIDTH), fs_s.reshape(db, dl, FFT_WIDTH))

    wts = (w_glu[0].astype(BF16), b_glu[0].reshape(1, S5_WIDTH), w_proj_s5[0].astype(BF16),
           w_proj_fft[0].astype(BF16), w_out[0].astype(BF16), norm2_g[0].reshape(1, D_MODEL),
           w_ffn_gate[0].astype(BF16), w_ffn_up[0].astype(BF16), w_ffn_down[0].astype(BF16),
           final_norm_g.reshape(1, D_MODEL))
    tm2 = SEG
    row_s2 = lambda i: 1 + i // (dl // tm2)
    y_p = _mix_ffn(xp, mod, row_p, ys_p, yf_p.reshape(nb * sl, FFT_WIDTH), gs_p, gf_p, wts, tm2)
    y_s = _mix_ffn(xs, mod, row_s2, ys_s, yf_s.reshape(db * dl, FFT_WIDTH), gs_s, gf_s, wts, tm2)
    return (y_p.reshape(nb, sl, D_MODEL), y_s.reshape(db, dl, D_MODEL), new_state)
```

```python
import functools
import math

import numpy as np
import jax
import jax.numpy as jnp
from jax import lax
from jax.experimental import pallas as pl
from jax.experimental.pallas import tpu as pltpu

F32 = jnp.float32
BF16 = jnp.bfloat16
HIGHEST = lax.Precision.HIGHEST

D_MODEL = 1024
S5_WIDTH = 512
S5_GROUPS = 32
S5_GROUP = 16
S5_STATE = 64
FFT_WIDTH = 512
FFT_GROUPS = 4
FFT_GROUP = 128
D_FF = 2816
N_MOD = 6
EPS = 1e-6

LANES = 128
CHUNK = 16
SEG_CHUNKS = 16
SEG = CHUNK * SEG_CHUNKS
GP = S5_GROUPS * S5_STATE
PAIRS = S5_GROUPS // 2
CH = CHUNK * S5_GROUP
OCT = LANES // S5_GROUP

MXU_DIM = 256
FF_SPLITS = (0, (D_FF // MXU_DIM + 1) // 2 * MXU_DIM, D_FF)
VMEM_LIMIT = 58 * 1024 * 1024


def _cparams(sem):
    return pltpu.CompilerParams(dimension_semantics=sem, vmem_limit_bytes=VMEM_LIMIT)


def _ada_kernel(c_ref, w_ref, b_ref, o_ref):
    c = c_ref[...]
    s = c * jax.nn.sigmoid(c)
    o_ref[...] = jnp.dot(s, w_ref[...], preferred_element_type=F32, precision=HIGHEST) + b_ref[...]


def _adaln(cvec8, w_ada, b_ada):
    n_out = w_ada.shape[1]
    tn = 1024
    return pl.pallas_call(
        _ada_kernel,
        grid=(n_out // tn,),
        in_specs=[pl.BlockSpec((8, D_MODEL), lambda j: (0, 0)),
                  pl.BlockSpec((D_MODEL, tn), lambda j: (0, j)),
                  pl.BlockSpec((1, tn), lambda j: (0, j))],
        out_specs=pl.BlockSpec((8, tn), lambda j: (0, j)),
        out_shape=jax.ShapeDtypeStruct((8, n_out), F32),
        compiler_params=_cparams(("arbitrary",)),
        name="adaln",
    )(cvec8, w_ada, b_ada.reshape(1, n_out))


def _prep_kernel(lre_ref, lim_ref, lst_ref, bre_ref, bim_ref, cre_ref, cim_ref, dt_ref,
                 apre_ref, apim_ref, m_ref, w_ref, e_ref, gb_scr, ca_scr, ts_scr):
    lre = lre_ref[...]
    lim = lim_ref[...]
    step = jnp.exp(lst_ref[...])
    mag = jnp.exp(lre * step)
    are = mag * jnp.cos(lim * step)
    aim = mag * jnp.sin(lim * step)
    nr = are - 1.0
    den = lre * lre + lim * lim
    fr = (nr * lre + aim * lim) / den
    fi = (aim * lre - nr * lim) / den

    pr = [jnp.ones_like(are)]
    pi = [jnp.zeros_like(are)]
    for _ in range(CHUNK):
        r, i = pr[-1], pi[-1]
        pr.append(r * are - i * aim)
        pi.append(r * aim + i * are)
    a16r, a16i = pr[CHUNK], pi[CHUNK]
    qr, qi = jnp.ones_like(are), jnp.zeros_like(are)
    for c in range(SEG_CHUNKS + 1):
        apre_ref[c] = qr
        apim_ref[c] = qi
        qr, qi = qr * a16r - qi * a16i, qr * a16i + qi * a16r

    for d in range(2):
        bbre = fr[d:d + 1] * bre_ref[d] - fi[d:d + 1] * bim_ref[d]
        bbim = fr[d:d + 1] * bim_ref[d] + fi[d:d + 1] * bre_ref[d]
        cre, cim = cre_ref[d], cim_ref[d]
        for s in range(CHUNK):
            k = CHUNK - 1 - s if d == 0 else s
            r, i = pr[k][d:d + 1], pi[k][d:d + 1]
            gb_scr[0, d, s] = r * bbre - i * bbim
            gb_scr[1, d, s] = r * bbim + i * bbre
            f = s + 1 if d == 0 else CHUNK - s
            r, i = pr[f][d:d + 1], pi[f][d:d + 1]
            ca_scr[0, d, s] = r * cre - i * cim
            ca_scr[1, d, s] = -(r * cim + i * cre)

    dot_nt = lambda a, b: lax.dot_general(a, b, (((1,), (1,)), ((), ())), preferred_element_type=F32)
    lane_gi = lax.broadcasted_iota(jnp.int32, (CH, LANES), 1) // S5_STATE
    row_gi = lax.broadcasted_iota(jnp.int32, (LANES, CH), 0) // S5_STATE
    eye = (lax.broadcasted_iota(jnp.int32, (LANES, LANES), 0)
           == lax.broadcasted_iota(jnp.int32, (LANES, LANES), 1)).astype(BF16)
    slot = lax.broadcasted_iota(jnp.int32, (CH, LANES), 1) // S5_GROUP
    n_lag_rows = (2 * CHUNK - 1) * S5_GROUP

    def pair(q, carry):
        lanes = pl.ds(pl.multiple_of(q * LANES, LANES), LANES)
        for d in range(2):
            for ri in range(2):
                col = slice((2 * d + ri) * LANES, (2 * d + ri + 1) * LANES)
                gb = gb_scr[ri, d, :, :, lanes].reshape(CH, LANES)
                ca = ca_scr[ri, d, :, :, lanes].reshape(CH, LANES).astype(BF16)
                ca_t = dot_nt(eye, ca)
                for gi in range(2):
                    w_ref[q, gi * CH:(gi + 1) * CH, col] = jnp.where(lane_gi == gi, gb, 0.0).astype(BF16)
                    e_ref[q, col, gi * CH:(gi + 1) * CH] = jnp.where(row_gi == gi, ca_t, 0.0).astype(BF16)
        for gi in range(2):
            g = 2 * q + gi
            lag = []
            for d in range(2):
                c_re = jnp.concatenate([cre_ref[d, :, lanes]] * OCT, axis=0)
                c_imn = jnp.concatenate([-cim_ref[d, :, lanes]] * OCT, axis=0)
                keep = lax.broadcasted_iota(jnp.int32, (LANES, LANES), 1) // S5_STATE == gi
                c_re = jnp.where(keep, c_re, 0.0).astype(BF16)
                c_imn = jnp.where(keep, c_imn, 0.0).astype(BF16)
                gre = gb_scr[0, d, :, :, lanes].reshape(CH, LANES).astype(BF16)
                gim = gb_scr[1, d, :, :, lanes].reshape(CH, LANES).astype(BF16)
                lag.append(dot_nt(gre, c_re) + dot_nt(gim, c_imn))
            zl = (CHUNK - 1) * S5_GROUP
            ts_scr[0:zl, :] = lag[0][0:zl]
            ts_scr[zl:zl + S5_GROUP, :] = lag[0][zl:] + lag[1][0:S5_GROUP] + dt_ref[g]
            ts_scr[zl + S5_GROUP:n_lag_rows, :] = lag[1][S5_GROUP:]
            for hf in range(CHUNK // OCT):
                acc = None
                for s in range(OCT):
                    t = OCT * hf + s
                    win = ts_scr[(CHUNK - 1 - t) * S5_GROUP:(CHUNK - 1 - t) * S5_GROUP + CH, :]
                    acc = win if acc is None else jnp.where(slot == s, win, acc)
                m_ref[g, :, hf * LANES:(hf + 1) * LANES] = acc.astype(BF16)
        return carry

    lax.fori_loop(0, PAIRS, pair, 0)


def _s5_tables(lam_re, lam_im, log_step, b_re, b_im, c_re, c_im, s5_d):
    lre = lam_re.reshape(2, GP)
    lim = lam_im.reshape(2, GP)
    lst = jnp.repeat(log_step, S5_STATE, axis=-1)
    bre = b_re.transpose(0, 3, 1, 2).reshape(2, S5_GROUP, GP)
    bim = b_im.transpose(0, 3, 1, 2).reshape(2, S5_GROUP, GP)
    cre = c_re.transpose(0, 2, 1, 3).reshape(2, S5_GROUP, GP)
    cim = c_im.transpose(0, 2, 1, 3).reshape(2, S5_GROUP, GP)
    eye = jnp.asarray(np.eye(S5_GROUP, dtype=np.float32))
    dt = jnp.tile(eye[None] * s5_d.reshape(S5_GROUPS, 1, S5_GROUP), (1, 1, OCT))

    vm = pl.BlockSpec(memory_space=pltpu.VMEM)
    tab = (2, 2, CHUNK, S5_GROUP, GP)
    apre, apim, m, w, e = pl.pallas_call(
        _prep_kernel,
        in_specs=[vm] * 8,
        out_specs=[vm] * 5,
        out_shape=[jax.ShapeDtypeStruct((SEG_CHUNKS + 1, 2, GP), F32),
                   jax.ShapeDtypeStruct((SEG_CHUNKS + 1, 2, GP), F32),
                   jax.ShapeDtypeStruct((S5_GROUPS, CH, CH), BF16),
                   jax.ShapeDtypeStruct((PAIRS, 2 * CH, 4 * LANES), BF16),
                   jax.ShapeDtypeStruct((PAIRS, 4 * LANES, 2 * CH), BF16)],
        scratch_shapes=[pltpu.VMEM(tab, F32),
                        pltpu.VMEM(tab, F32),
                        pltpu.VMEM(((2 * CHUNK - 1) * S5_GROUP, LANES), F32)],
        compiler_params=pltpu.CompilerParams(vmem_limit_bytes=VMEM_LIMIT),
        name="s5_tables",
    )(lre, lim, lst, bre, bim, cre, cim, dt)
    return m, w, e, apre, apim


def _rms(x, g):
    return x * lax.rsqrt(jnp.mean(x * x, axis=-1, keepdims=True) + EPS) * g


def _slot_transpose(v):
    v = list(v)
    slot = lax.broadcasted_iota(jnp.int32, v[0].shape, 1) // S5_GROUP
    for k in (4, 2, 1):
        low = (slot & k) == 0
        for i in range(OCT):
            if i & k:
                continue
            a, b = v[i], v[i + k]
            v[i] = jnp.where(low, a, pltpu.roll(b, k * S5_GROUP, axis=1))
            v[i + k] = jnp.where(low, pltpu.roll(a, LANES - k * S5_GROUP, axis=1), b)
    return v


def _inproj_kernel(x_ref, mod_ref, g_ref, win_ref, bdc_ref, bds_ref,
                   xg_ref, fc_ref, fs_ref, gs_ref, gf_ref, h_scr, hb_scr, zs_scr):
    tm = x_ref.shape[0]
    mod = mod_ref[0]
    h = _rms(x_ref[...], g_ref[...]) * (1.0 + mod[1:2]) + mod[0:1]
    for k in range(D_MODEL // LANES):
        h_scr[k] = h[:, k * LANES:(k + 1) * LANES]
    for j in range(tm // SEG):
        for s in range(CHUNK):
            r0 = j * SEG + s * SEG_CHUNKS
            for k in range(D_MODEL // LANES):
                hb_scr[r0:r0 + SEG_CHUNKS, k * LANES:(k + 1) * LANES] = (
                    h_scr[k, pl.ds(j * SEG + s, SEG_CHUNKS, stride=CHUNK), :].astype(BF16))
    hb = hb_scr[...]
    dot = functools.partial(jnp.dot, preferred_element_type=F32)
    zs_scr[...] = dot(hb, win_ref[:, 0:S5_WIDTH])
    for j in range(tm // SEG):
        for b in range(S5_WIDTH // LANES):
            for hf in range(CHUNK // OCT):
                z = [zs_scr[j * SEG + s * SEG_CHUNKS:j * SEG + (s + 1) * SEG_CHUNKS, b * LANES:(b + 1) * LANES]
                     for s in range(OCT * hf, OCT * (hf + 1))]
                for i, xi in enumerate(_slot_transpose(z)):
                    xg_ref[OCT * b + i, j * SEG_CHUNKS:(j + 1) * SEG_CHUNKS,
                           hf * LANES:(hf + 1) * LANES] = xi.astype(BF16)
    uf = dot(hb, win_ref[:, S5_WIDTH:2 * S5_WIDTH]).astype(BF16)
    fc_ref[...] = dot(uf, bdc_ref[...]).astype(BF16)
    fs_ref[...] = dot(uf, bds_ref[...]).astype(BF16)
    o = 2 * S5_WIDTH
    gs_ref[...] = jax.nn.sigmoid(dot(hb, win_ref[:, o:o + D_MODEL])).astype(BF16)
    gf_ref[...] = jax.nn.sigmoid(dot(hb, win_ref[:, o + D_MODEL:o + 2 * D_MODEL])).astype(BF16)


def _channel_dft_mats():
    j = np.arange(FFT_GROUP)
    ang = 2.0 * np.pi * ((j[:, None] * j[None, :]) % FFT_GROUP) / FFT_GROUP
    blk_c = np.cos(ang) / math.sqrt(FFT_GROUP)
    blk_s = np.sin(ang) / math.sqrt(FFT_GROUP)
    bdc = np.kron(np.eye(FFT_GROUPS), blk_c)
    bds = np.kron(np.eye(FFT_GROUPS), blk_s)
    return jnp.asarray(bdc, F32).astype(BF16), jnp.asarray(bds, F32).astype(BF16)


def _const_spec(shape):
    nd = len(shape)
    return pl.BlockSpec(shape, lambda i: (0,) * nd, pipeline_mode=pl.Buffered(1))


def _inproj(x2d, mod, mod_row, norm_g, w_in_b, tm):
    t = x2d.shape[0]
    bdc, bds = _channel_dft_mats()
    tok = lambda w: pl.BlockSpec((tm, w), lambda i: (i, 0))
    out = lambda w: jax.ShapeDtypeStruct((t, w), BF16)
    return pl.pallas_call(
        _inproj_kernel,
        grid=(t // tm,),
        in_specs=[tok(D_MODEL),
                  pl.BlockSpec((1, N_MOD, D_MODEL), lambda i: (mod_row(i), 0, 0)),
                  _const_spec((1, D_MODEL)),
                  _const_spec(w_in_b.shape),
                  _const_spec(bdc.shape),
                  _const_spec(bds.shape)],
        out_specs=[pl.BlockSpec((S5_GROUPS, tm // CHUNK, CH), lambda i: (0, i, 0)),
                   tok(FFT_WIDTH), tok(FFT_WIDTH), tok(D_MODEL), tok(D_MODEL)],
        out_shape=[jax.ShapeDtypeStruct((S5_GROUPS, t // CHUNK, CH), BF16),
                   out(FFT_WIDTH), out(FFT_WIDTH), out(D_MODEL), out(D_MODEL)],
        scratch_shapes=[pltpu.VMEM((D_MODEL // LANES, tm, LANES), F32),
                        pltpu.VMEM((tm, D_MODEL), BF16),
                        pltpu.VMEM((tm, S5_WIDTH), F32)],
        compiler_params=_cparams(("parallel",)),
        name="inproj",
    )(x2d, mod, norm_g.reshape(1, D_MODEL), w_in_b, bdc, bds)


N_PROMPT_SEG = 16
N_SAMPLE_SEG = 32
N_SAMPLE_SEQ = 2
N_SEG = N_PROMPT_SEG + N_SAMPLE_SEG
ROWS = SEG_CHUNKS * N_SEG
ROWS_P = SEG_CHUNKS * N_PROMPT_SEG
OCT_PAIRS = OCT // 2


def _cmul_add(ar, ai, hr, hi, sr, si):
    return ar * hr - ai * hi + sr, ar * hi + ai * hr + si


def _s5_kernel(xp_ref, xs_ref, m_ref, w_ref, e_ref, apre_ref, apim_ref, h0_ref,
               yp_ref, ys_ref, fin_ref, s_scr, hin_scr, hinp_scr, f_scr, hs_scr, y_scr):
    dot = functools.partial(jnp.dot, preferred_element_type=F32)
    parts = ((xp_ref, 0, ROWS_P), (xs_ref, ROWS_P, ROWS - ROWS_P))
    seg_rows = lambda c: pl.ds(c, N_SEG, stride=SEG_CHUNKS)
    blk = lambda c: pl.ds(c * N_SEG, N_SEG)
    seq_rows = lambda j: pl.ds(j, N_SAMPLE_SEQ, stride=SEG_CHUNKS)
    lat = lambda c: pl.ds(c * N_SEG + N_PROMPT_SEG, N_SAMPLE_SEG)
    zero = jnp.zeros((N_SEG, LANES), F32)

    for pr in range(OCT_PAIRS):
        ln = slice(pr * LANES, (pr + 1) * LANES)
        g0, g1 = 2 * pr, 2 * pr + 1
        for x_ref, r0, nr in parts:
            s = dot(x_ref[g0], w_ref[pr, 0:CH, :]) + dot(x_ref[g1], w_ref[pr, CH:2 * CH, :])
            for k in range(4):
                s_scr[k, r0:r0 + nr, :] = s[:, k * LANES:(k + 1) * LANES]

        ar, ai = apre_ref[1, 0:1, ln], apim_ref[1, 0:1, ln]
        hr, hi = zero, zero
        for c in range(SEG_CHUNKS):
            hin_scr[0, blk(c), :] = hr
            hin_scr[1, blk(c), :] = hi
            hr, hi = _cmul_add(ar, ai, hr, hi, s_scr[0, seg_rows(c), :], s_scr[1, seg_rows(c), :])
        fin_ref[pr, 0] = hr[0:N_PROMPT_SEG]
        fin_ref[pr, 1] = hi[0:N_PROMPT_SEG]
        f_scr[0] = hr[N_PROMPT_SEG:]
        f_scr[1] = hi[N_PROMPT_SEG:]
        br, bi = apre_ref[1, 1:2, ln], apim_ref[1, 1:2, ln]
        gr, gi = zero, zero
        for c in range(SEG_CHUNKS - 1, -1, -1):
            hin_scr[2, blk(c), :] = gr
            hin_scr[3, blk(c), :] = gi
            gr, gi = _cmul_add(br, bi, gr, gi, s_scr[2, seg_rows(c), :], s_scr[3, seg_rows(c), :])
        fin_ref[pr, 2] = gr[0:N_PROMPT_SEG]
        fin_ref[pr, 3] = gi[0:N_PROMPT_SEG]
        f_scr[2] = gr[N_PROMPT_SEG:]
        f_scr[3] = gi[N_PROMPT_SEG:]

        a2r, a2i = apre_ref[SEG_CHUNKS, 0:1, ln], apim_ref[SEG_CHUNKS, 0:1, ln]
        hr, hi = h0_ref[0, :, ln], h0_ref[1, :, ln]
        for j in range(SEG_CHUNKS):
            hs_scr[0, seq_rows(j), :] = hr
            hs_scr[1, seq_rows(j), :] = hi
            hr, hi = _cmul_add(a2r, a2i, hr, hi, f_scr[0, seq_rows(j), :], f_scr[1, seq_rows(j), :])
        b2r, b2i = apre_ref[SEG_CHUNKS, 1:2, ln], apim_ref[SEG_CHUNKS, 1:2, ln]
        gr, gi = h0_ref[2, :, ln], h0_ref[3, :, ln]
        for j in range(SEG_CHUNKS - 1, -1, -1):
            hs_scr[2, seq_rows(j), :] = gr
            hs_scr[3, seq_rows(j), :] = gi
            gr, gi = _cmul_add(b2r, b2i, gr, gi, f_scr[2, seq_rows(j), :], f_scr[3, seq_rows(j), :])
        for c in range(SEG_CHUNKS):
            p_r, p_i = apre_ref[c, 0:1, ln], apim_ref[c, 0:1, ln]
            hr, hi = _cmul_add(p_r, p_i, hs_scr[0], hs_scr[1], hin_scr[0, lat(c), :], hin_scr[1, lat(c), :])
            hin_scr[0, lat(c), :] = hr
            hin_scr[1, lat(c), :] = hi
            cb = SEG_CHUNKS - 1 - c
            p_r, p_i = apre_ref[cb, 1:2, ln], apim_ref[cb, 1:2, ln]
            gr, gi = _cmul_add(p_r, p_i, hs_scr[2], hs_scr[3], hin_scr[2, lat(c), :], hin_scr[3, lat(c), :])
            hin_scr[2, lat(c), :] = gr
            hin_scr[3, lat(c), :] = gi

        for k in range(4):
            for sg in range(N_SEG):
                hinp_scr[sg * SEG_CHUNKS:(sg + 1) * SEG_CHUNKS, k * LANES:(k + 1) * LANES] = (
                    hin_scr[k, pl.ds(sg, SEG_CHUNKS, stride=N_SEG), :].astype(BF16))
        for gi_, g in ((0, g0), (1, g1)):
            for x_ref, r0, nr in parts:
                y_scr[g, r0:r0 + nr, :] = (dot(x_ref[g], m_ref[g])
                                           + dot(hinp_scr[r0:r0 + nr, :], e_ref[pr, :, gi_ * CH:(gi_ + 1) * CH]))

    rb_rows = 2 * SEG_CHUNKS
    for rb in range(ROWS // rb_rows):
        for hf in range(CHUNK // OCT):
            v = [y_scr[i, rb * rb_rows:(rb + 1) * rb_rows, hf * LANES:(hf + 1) * LANES] for i in range(OCT)]
            for s, acc in enumerate(_slot_transpose(v)):
                t = OCT * hf + s
                for sg in range(rb_rows // SEG_CHUNKS):
                    seg = rb * (rb_rows // SEG_CHUNKS) + sg
                    piece = acc[sg * SEG_CHUNKS:(sg + 1) * SEG_CHUNKS]
                    if seg < N_PROMPT_SEG:
                        r0 = seg * SEG + t * SEG_CHUNKS
                        yp_ref[0, r0:r0 + SEG_CHUNKS, :] = piece
                    else:
                        r0 = (seg - N_PROMPT_SEG) * SEG + t * SEG_CHUNKS
                        ys_ref[0, r0:r0 + SEG_CHUNKS, :] = piece


def _s5_conv(xg_p, xg_s, m, w, e, apre, apim, h0):
    n_oct = S5_GROUPS // OCT
    tp, ts = xg_p.shape[1] * CHUNK, xg_s.shape[1] * CHUNK
    return pl.pallas_call(
        _s5_kernel,
        grid=(n_oct,),
        in_specs=[pl.BlockSpec((OCT, ROWS_P, CH), lambda o: (o, 0, 0)),
                  pl.BlockSpec((OCT, ROWS - ROWS_P, CH), lambda o: (o, 0, 0)),
                  pl.BlockSpec((OCT, CH, CH), lambda o: (o, 0, 0)),
                  pl.BlockSpec((OCT_PAIRS, 2 * CH, 4 * LANES), lambda o: (o, 0, 0)),
                  pl.BlockSpec((OCT_PAIRS, 4 * LANES, 2 * CH), lambda o: (o, 0, 0)),
                  pl.BlockSpec((SEG_CHUNKS + 1, 2, OCT_PAIRS * LANES), lambda o: (0, 0, o)),
                  pl.BlockSpec((SEG_CHUNKS + 1, 2, OCT_PAIRS * LANES), lambda o: (0, 0, o)),
                  pl.BlockSpec((4, N_SAMPLE_SEQ, OCT_PAIRS * LANES), lambda o: (0, 0, o))],
        out_specs=[pl.BlockSpec((1, tp, LANES), lambda o: (o, 0, 0)),
                   pl.BlockSpec((1, ts, LANES), lambda o: (o, 0, 0)),
                   pl.BlockSpec((OCT_PAIRS, 4, N_PROMPT_SEG, LANES), lambda o: (o, 0, 0, 0))],
        out_shape=[jax.ShapeDtypeStruct((n_oct, tp, LANES), F32),
                   jax.ShapeDtypeStruct((n_oct, ts, LANES), F32),
                   jax.ShapeDtypeStruct((PAIRS, 4, N_PROMPT_SEG, LANES), F32)],
        scratch_shapes=[pltpu.VMEM((4, ROWS, LANES), F32),
                        pltpu.VMEM((4, ROWS, LANES), F32),
                        pltpu.VMEM((ROWS, 4 * LANES), BF16),
                        pltpu.VMEM((4, N_SAMPLE_SEG, LANES), F32),
                        pltpu.VMEM((4, N_SAMPLE_SEG, LANES), F32),
                        pltpu.VMEM((OCT, ROWS, CH), F32)],
        compiler_params=_cparams(("parallel",)),
        name="s5_conv",
    )(xg_p, xg_s, m, w, e, apre, apim, h0)


def _tile_order_positions(l):
    r = np.arange(l)
    tile, wi = r // SEG, r % SEG
    return tile * SEG + (wi % SEG_CHUNKS) * CHUNK + wi // SEG_CHUNKS


def _tile_dft_tables(scale, rows_in_tile_order=True):
    pos = _tile_order_positions(SEG)
    freq = pos if rows_in_tile_order else np.arange(SEG)
    ang = 2.0 * np.pi * ((freq[:, None] * pos[None, :]) % SEG) / SEG
    return jnp.asarray(np.cos(ang) * scale, F32), jnp.asarray(np.sin(ang) * scale, F32)


def _pos_dft_tile_kernel(cm_ref, sm_ref, fc_ref, fs_ref, o_ref):
    dot = functools.partial(jnp.dot, preferred_element_type=F32)
    o_ref[0] = (dot(cm_ref[...].astype(BF16), fc_ref[0]) - dot(sm_ref[...].astype(BF16), fs_ref[0])).astype(BF16)


def _pos_dft_tile(fc, fs):
    n, l, wdt = fc.shape
    cm, sm = _tile_dft_tables(1.0 / math.sqrt(l))
    seq = pl.BlockSpec((1, l, wdt), lambda b: (b, 0, 0))
    return pl.pallas_call(
        _pos_dft_tile_kernel,
        grid=(n,),
        in_specs=[_const_spec((l, l)), _const_spec((l, l)), seq, seq],
        out_specs=seq,
        out_shape=jax.ShapeDtypeStruct((n, l, wdt), BF16),
        compiler_params=_cparams(("parallel",)),
        name="pos_dft_tile",
    )(cm, sm, fc, fs)


N_TILES = 16


def _fft16(xr, xi):
    n = len(xr)
    rev = [int(format(i, "04b")[::-1], 2) for i in range(n)]
    ar = [xr[r] for r in rev]
    ai = [xi[r] for r in rev]
    size = 2
    while size <= n:
        half = size // 2
        for start in range(0, n, size):
            for k in range(half):
                wr = math.cos(2.0 * math.pi * k / size)
                wi = -math.sin(2.0 * math.pi * k / size)
                i0, i1 = start + k, start + k + half
                if k == 0:
                    tr, ti = ar[i1], ai[i1]
                elif 4 * k == size:
                    tr, ti = ai[i1], -ar[i1]
                else:
                    tr = ar[i1] * wr - ai[i1] * wi
                    ti = ar[i1] * wi + ai[i1] * wr
                ar[i1], ai[i1] = ar[i0] - tr, ai[i0] - ti
                ar[i0], ai[i0] = ar[i0] + tr, ai[i0] + ti
        size *= 2
    return ar, ai


def _tiles_fft_kernel(fc_ref, fs_ref, a_ref):
    rb, wdt = fc_ref.shape[1], fc_ref.shape[2]
    for r in range(rb // CHUNK):
        for b in range(wdt // LANES):
            rows, lns = slice(r * CHUNK, (r + 1) * CHUNK), slice(b * LANES, (b + 1) * LANES)
            xr = [fc_ref[j, rows, lns].astype(F32) for j in range(N_TILES)]
            xi = [-fs_ref[j, rows, lns].astype(F32) for j in range(N_TILES)]
            ar, ai = _fft16(xr, xi)
            for k in range(N_TILES):
                a_ref[k, 0, rows, lns] = ar[k].astype(BF16)
                a_ref[k, 1, rows, lns] = ai[k].astype(BF16)


def _twiddled_dft_kernel(cb_ref, sb_ref, ca_ref, sa_ref, a_ref, o_ref):
    k1 = pl.program_id(1)
    dot = functools.partial(jnp.dot, preferred_element_type=F32)
    ca, sa = ca_ref[pl.ds(k1, 1), :], sa_ref[pl.ds(k1, 1), :]
    cb, sb = cb_ref[...], sb_ref[...]
    dc = (cb * ca - sb * sa).astype(BF16)
    ds = (sb * ca + cb * sa).astype(BF16)
    out = dot(dc, a_ref[0]) + dot(ds, a_ref[1])
    o_ref[...] = out.astype(BF16).reshape(o_ref.shape)


def _pos_dft_long(fc, fs):
    n, l, wdt = fc.shape
    assert l == N_TILES * SEG
    rb = 32
    tiles = pl.BlockSpec((None, N_TILES, rb, wdt), lambda b, r: (b, 0, r, 0))
    a = pl.pallas_call(
        _tiles_fft_kernel,
        grid=(n, SEG // rb),
        in_specs=[tiles, tiles],
        out_specs=pl.BlockSpec((None, N_TILES, 2, rb, wdt), lambda b, r: (b, 0, 0, r, 0)),
        out_shape=jax.ShapeDtypeStruct((n, N_TILES, 2, SEG, wdt), BF16),
        compiler_params=_cparams(("parallel", "parallel")),
        name="tiles_fft",
    )(fc.reshape(n, N_TILES, SEG, wdt), fs.reshape(n, N_TILES, SEG, wdt))

    cb, sb = _tile_dft_tables(1.0 / math.sqrt(l), rows_in_tile_order=False)
    pos = _tile_order_positions(SEG)
    ang = 2.0 * np.pi * (np.arange(N_TILES)[:, None] * pos[None, :]) / l
    ca, sa = jnp.asarray(np.cos(ang), F32), jnp.asarray(np.sin(ang), F32)
    const2 = lambda shape: pl.BlockSpec(shape, lambda b, k: (0, 0), pipeline_mode=pl.Buffered(1))
    out = pl.pallas_call(
        _twiddled_dft_kernel,
        grid=(n, N_TILES),
        in_specs=[const2((SEG, SEG)), const2((SEG, SEG)), const2((N_TILES, SEG)), const2((N_TILES, SEG)),
                  pl.BlockSpec((None, None, 2, SEG, wdt), lambda b, k: (b, k, 0, 0, 0))],
        out_specs=pl.BlockSpec((None, N_TILES, None, SEG_CHUNKS, wdt), lambda b, k: (b, 0, k, 0, 0)),
        out_shape=jax.ShapeDtypeStruct((n, N_TILES, CHUNK, SEG_CHUNKS, wdt), BF16),
        compiler_params=_cparams(("parallel", "parallel")),
        name="twiddled_dft",
    )(cb, sb, ca, sa, a)
    return out.reshape(n, l, wdt)


def _mix_ffn_kernel(x_ref, mod_ref, ys_ref, yf_ref, gs_ref, gf_ref,
                    wglu_ref, bglu_ref, wps_ref, wpf_ref, wout_ref, n2_ref,
                    wg_ref, wu_ref, wd_ref, fn_ref, o_ref, m_scr, mb_scr):
    tm = x_ref.shape[0]
    dot = functools.partial(jnp.dot, preferred_element_type=F32)
    mod = mod_ref[0]
    y = jnp.concatenate([ys_ref[b] for b in range(S5_WIDTH // LANES)], axis=1)
    z = jax.nn.gelu(y)
    z = z * jax.nn.sigmoid(dot(z.astype(BF16), wglu_ref[...]) + bglu_ref[...])
    m = (gs_ref[...].astype(F32) * dot(z.astype(BF16), wps_ref[...])
         + gf_ref[...].astype(F32) * dot(yf_ref[...], wpf_ref[...]))
    for k in range(D_MODEL // LANES):
        m_scr[k] = m[:, k * LANES:(k + 1) * LANES]
    for j in range(tm // SEG):
        for c in range(SEG_CHUNKS):
            r0 = j * SEG + c * CHUNK
            for k in range(D_MODEL // LANES):
                mb_scr[r0:r0 + CHUNK, k * LANES:(k + 1) * LANES] = (
                    m_scr[k, pl.ds(j * SEG + c, CHUNK, stride=SEG_CHUNKS), :].astype(BF16))
    x1 = x_ref[...] + mod[2:3] * dot(mb_scr[...], wout_ref[...])
    h2 = (_rms(x1, n2_ref[...]) * (1.0 + mod[4:5]) + mod[3:4]).astype(BF16)
    ff = None
    for n0, n1 in zip(FF_SPLITS[:-1], FF_SPLITS[1:]):
        gate = dot(h2, wg_ref[:, n0:n1])
        up = dot(h2, wu_ref[:, n0:n1])
        part = dot((gate * jax.nn.sigmoid(gate) * up).astype(BF16), wd_ref[n0:n1, :])
        ff = part if ff is None else ff + part
    x2 = x1 + mod[5:6] * ff
    o_ref[...] = _rms(x2, fn_ref[...])


def _mix_ffn(x2d, mod, mod_row, ys, yf, gs, gf, wts, tm):
    t = x2d.shape[0]
    tok = lambda w: pl.BlockSpec((tm, w), lambda i: (i, 0))
    return pl.pallas_call(
        _mix_ffn_kernel,
        grid=(t // tm,),
        in_specs=[tok(D_MODEL),
                  pl.BlockSpec((1, N_MOD, D_MODEL), lambda i: (mod_row(i), 0, 0)),
                  pl.BlockSpec((S5_WIDTH // LANES, tm, LANES), lambda i: (0, i, 0)),
                  tok(FFT_WIDTH), tok(D_MODEL), tok(D_MODEL)]
                 + [_const_spec(w.shape) for w in wts],
        out_specs=tok(D_MODEL),
        out_shape=jax.ShapeDtypeStruct((t, D_MODEL), F32),
        scratch_shapes=[pltpu.VMEM((D_MODEL // LANES, tm, LANES), F32),
                        pltpu.VMEM((tm, D_MODEL), BF16)],
        compiler_params=_cparams(("parallel",)),
        name="mix_ffn",
    )(x2d, mod, ys, yf, gs, gf, *wts)


def kernel(x_prompt, x_sample, state_s5, c, c_ctx, norm1_g, norm2_g, w_ada, b_ada, w_in,
           s5_lambda_re, s5_lambda_im, s5_log_step, s5_b_re, s5_b_im, s5_c_re, s5_c_im,
           s5_d, w_glu, b_glu, w_proj_s5, w_proj_fft, w_out, w_ffn_gate, w_ffn_up,
           w_ffn_down, final_norm_g):
    nb, sl, _ = x_prompt.shape
    db, dl, _ = x_sample.shape
    assert w_in.shape[0] == 1 and sl == SEG and nb == N_PROMPT_SEG
    assert db == N_SAMPLE_SEQ and dl == SEG * SEG_CHUNKS

    cvec = jnp.concatenate([c_ctx[None], c, jnp.zeros((8 - 1 - db, D_MODEL), F32)], axis=0)
    mod = _adaln(cvec, w_ada[0], b_ada[0]).reshape(8, N_MOD, D_MODEL)

    m, w, e, apre, apim = _s5_tables(s5_lambda_re[0], s5_lambda_im[0], s5_log_step[0], s5_b_re[0],
                                     s5_b_im[0], s5_c_re[0], s5_c_im[0], s5_d[0])

    tm = 512
    xp = x_prompt.reshape(nb * sl, D_MODEL)
    xs = x_sample.reshape(db * dl, D_MODEL)
    row_p = lambda i: 0
    row_s = lambda i: 1 + i // (dl // tm)
    w_in_b = w_in[0].astype(BF16)
    xg_p, fc_p, fs_p, gs_p, gf_p = _inproj(xp, mod, row_p, norm1_g[0], w_in_b, tm)
    xg_s, fc_s, fs_s, gs_s, gf_s = _inproj(xs, mod, row_s, norm1_g[0], w_in_b, tm)

    h0 = state_s5[:, 0].astype(F32).transpose(1, 2, 0, 3, 4).reshape(4, db, GP)
    ys_p, ys_s, fin = _s5_conv(xg_p, xg_s, m, w, e, apre, apim, h0)
    new_state = fin.reshape(PAIRS, 2, 2, nb, 2, S5_STATE).transpose(3, 1, 2, 0, 4, 5)
    new_state = new_state.reshape(nb, 1, 2, 2, S5_GROUPS, S5_STATE)

    yf_p = _pos_dft_tile(fc_p.reshape(nb, sl, FFT_WIDTH), fs_p.reshape(nb, sl, FFT_WIDTH))
    yf_s = _pos_dft_long(fc_s.reshape(db, dl, FFT_WIDTH), fs_s.reshape(db, dl, FFT_WIDTH))

    wts = (w_glu[0].astype(BF16), b_glu[0].reshape(1, S5_WIDTH), w_proj_s5[0].astype(BF16),
           w_proj_fft[0].astype(BF16), w_out[0].astype(BF16), norm2_g[0].reshape(1, D_MODEL),
           w_ffn_gate[0].astype(BF16), w_ffn_up[0].astype(BF16), w_ffn_down[0].astype(BF16),
           final_norm_g.reshape(1, D_MODEL))
    tm2 = 2 * SEG
    row_s2 = lambda i: 1 + i // (dl // tm2)
    y_p = _mix_ffn(xp, mod, row_p, ys_p, yf_p.reshape(nb * sl, FFT_WIDTH), gs_p, gf_p, wts, tm2)
    y_s = _mix_ffn(xs, mod, row_s2, ys_s, yf_s.reshape(db * dl, FFT_WIDTH), gs_s, gf_s, wts, tm2)
    return (y_p.reshape(nb, sl, D_MODEL), y_s.reshape(db, dl, D_MODEL), new_state)
```

```python
import functools
import math

import numpy as np
import jax
import jax.numpy as jnp
from jax import lax
from jax.experimental import pallas as pl
from jax.experimental.pallas import tpu as pltpu

F32 = jnp.float32
BF16 = jnp.bfloat16

D_MODEL = 1024
S5_WIDTH = 512
S5_GROUPS = 32
S5_GROUP = 16
S5_STATE = 64
FFT_WIDTH = 512
FFT_GROUPS = 4
FFT_GROUP = 128
D_FF = 2816
N_MOD = 6
EPS = 1e-6

LANES = 128
CHUNK = 16
SEG_CHUNKS = 16
SEG = CHUNK * SEG_CHUNKS
GP = S5_GROUPS * S5_STATE
PAIRS = S5_GROUPS // 2
CH = CHUNK * S5_GROUP
OCT = LANES // S5_GROUP

MXU_DIM = 256
FF_SPLITS = (0, (D_FF // MXU_DIM + 1) // 2 * MXU_DIM, D_FF)
VMEM_LIMIT = 58 * 1024 * 1024


def _cparams(sem):
    return pltpu.CompilerParams(dimension_semantics=sem, vmem_limit_bytes=VMEM_LIMIT)


def _ada_kernel(c_ref, w_ref, b_ref, o_ref):
    dot = functools.partial(jnp.dot, preferred_element_type=F32)
    c = c_ref[...]
    s = c * jax.nn.sigmoid(c)
    s_hi = s.astype(BF16)
    s_lo = (s - s_hi.astype(F32)).astype(BF16)
    w = w_ref[...]
    w_hi = w.astype(BF16)
    w_lo = (w - w_hi.astype(F32)).astype(BF16)
    r = dot(jnp.concatenate([s_hi, s_lo], axis=0), w_hi)
    n = s.shape[0]
    o_ref[...] = r[0:n] + r[n:2 * n] + dot(s_hi, w_lo) + b_ref[...]


def _adaln(cvec8, w_ada, b_ada):
    n_out = w_ada.shape[1]
    tn = 1024
    return pl.pallas_call(
        _ada_kernel,
        grid=(n_out // tn,),
        in_specs=[pl.BlockSpec((8, D_MODEL), lambda j: (0, 0)),
                  pl.BlockSpec((D_MODEL, tn), lambda j: (0, j)),
                  pl.BlockSpec((1, tn), lambda j: (0, j))],
        out_specs=pl.BlockSpec((8, tn), lambda j: (0, j)),
        out_shape=jax.ShapeDtypeStruct((8, n_out), F32),
        compiler_params=_cparams(("arbitrary",)),
        name="adaln",
    )(cvec8, w_ada, b_ada.reshape(1, n_out))


def _prep_kernel(lre_ref, lim_ref, lst_ref, bre_ref, bim_ref, cre_ref, cim_ref, dt_ref,
                 apre_ref, apim_ref, m_ref, w_ref, e_ref, gb_scr, ca_scr, ts_scr):
    lre = lre_ref[...]
    lim = lim_ref[...]
    step = jnp.exp(lst_ref[...])
    mag = jnp.exp(lre * step)
    are = mag * jnp.cos(lim * step)
    aim = mag * jnp.sin(lim * step)
    nr = are - 1.0
    den = lre * lre + lim * lim
    fr = (nr * lre + aim * lim) / den
    fi = (aim * lre - nr * lim) / den

    pr = [jnp.ones_like(are)]
    pi = [jnp.zeros_like(are)]
    for _ in range(CHUNK):
        r, i = pr[-1], pi[-1]
        pr.append(r * are - i * aim)
        pi.append(r * aim + i * are)
    a16r, a16i = pr[CHUNK], pi[CHUNK]
    qr, qi = jnp.ones_like(are), jnp.zeros_like(are)
    for c in range(SEG_CHUNKS + 1):
        apre_ref[c] = qr
        apim_ref[c] = qi
        qr, qi = qr * a16r - qi * a16i, qr * a16i + qi * a16r

    for d in range(2):
        bbre = fr[d:d + 1] * bre_ref[d] - fi[d:d + 1] * bim_ref[d]
        bbim = fr[d:d + 1] * bim_ref[d] + fi[d:d + 1] * bre_ref[d]
        cre, cim = cre_ref[d], cim_ref[d]
        for s in range(CHUNK):
            k = CHUNK - 1 - s if d == 0 else s
            r, i = pr[k][d:d + 1], pi[k][d:d + 1]
            gb_scr[0, d, s] = r * bbre - i * bbim
            gb_scr[1, d, s] = r * bbim + i * bbre
            f = s + 1 if d == 0 else CHUNK - s
            r, i = pr[f][d:d + 1], pi[f][d:d + 1]
            ca_scr[0, d, s] = r * cre - i * cim
            ca_scr[1, d, s] = -(r * cim + i * cre)

    dot_nt = lambda a, b: lax.dot_general(a, b, (((1,), (1,)), ((), ())), preferred_element_type=F32)
    lane_gi = lax.broadcasted_iota(jnp.int32, (CH, LANES), 1) // S5_STATE
    row_gi = lax.broadcasted_iota(jnp.int32, (LANES, CH), 0) // S5_STATE
    eye = (lax.broadcasted_iota(jnp.int32, (LANES, LANES), 0)
           == lax.broadcasted_iota(jnp.int32, (LANES, LANES), 1)).astype(BF16)
    slot = lax.broadcasted_iota(jnp.int32, (CH, LANES), 1) // S5_GROUP
    n_lag_rows = (2 * CHUNK - 1) * S5_GROUP

    def pair(q, carry):
        lanes = pl.ds(pl.multiple_of(q * LANES, LANES), LANES)
        for d in range(2):
            for ri in range(2):
                col = slice((2 * d + ri) * LANES, (2 * d + ri + 1) * LANES)
                gb = gb_scr[ri, d, :, :, lanes].reshape(CH, LANES)
                ca = ca_scr[ri, d, :, :, lanes].reshape(CH, LANES).astype(BF16)
                ca_t = dot_nt(eye, ca)
                for gi in range(2):
                    w_ref[q, gi * CH:(gi + 1) * CH, col] = jnp.where(lane_gi == gi, gb, 0.0).astype(BF16)
                    e_ref[q, col, gi * CH:(gi + 1) * CH] = jnp.where(row_gi == gi, ca_t, 0.0).astype(BF16)
        for gi in range(2):
            g = 2 * q + gi
            lag = []
            for d in range(2):
                c_re = jnp.concatenate([cre_ref[d, :, lanes]] * OCT, axis=0)
                c_imn = jnp.concatenate([-cim_ref[d, :, lanes]] * OCT, axis=0)
                keep = lax.broadcasted_iota(jnp.int32, (LANES, LANES), 1) // S5_STATE == gi
                c_re = jnp.where(keep, c_re, 0.0).astype(BF16)
                c_imn = jnp.where(keep, c_imn, 0.0).astype(BF16)
                gre = gb_scr[0, d, :, :, lanes].reshape(CH, LANES).astype(BF16)
                gim = gb_scr[1, d, :, :, lanes].reshape(CH, LANES).astype(BF16)
                lag.append(dot_nt(gre, c_re) + dot_nt(gim, c_imn))
            zl = (CHUNK - 1) * S5_GROUP
            ts_scr[0:zl, :] = lag[0][0:zl]
            ts_scr[zl:zl + S5_GROUP, :] = lag[0][zl:] + lag[1][0:S5_GROUP] + dt_ref[g]
            ts_scr[zl + S5_GROUP:n_lag_rows, :] = lag[1][S5_GROUP:]
            for hf in range(CHUNK // OCT):
                acc = None
                for s in range(OCT):
                    t = OCT * hf + s
                    win = ts_scr[(CHUNK - 1 - t) * S5_GROUP:(CHUNK - 1 - t) * S5_GROUP + CH, :]
                    acc = win if acc is None else jnp.where(slot == s, win, acc)
                m_ref[g, :, hf * LANES:(hf + 1) * LANES] = acc.astype(BF16)
        return carry

    lax.fori_loop(0, PAIRS, pair, 0)


def _s5_tables(lam_re, lam_im, log_step, b_re, b_im, c_re, c_im, s5_d):
    lre = lam_re.reshape(2, GP)
    lim = lam_im.reshape(2, GP)
    lst = jnp.repeat(log_step, S5_STATE, axis=-1)
    bre = b_re.transpose(0, 3, 1, 2).reshape(2, S5_GROUP, GP)
    bim = b_im.transpose(0, 3, 1, 2).reshape(2, S5_GROUP, GP)
    cre = c_re.transpose(0, 2, 1, 3).reshape(2, S5_GROUP, GP)
    cim = c_im.transpose(0, 2, 1, 3).reshape(2, S5_GROUP, GP)
    eye = jnp.asarray(np.eye(S5_GROUP, dtype=np.float32))
    dt = jnp.tile(eye[None] * s5_d.reshape(S5_GROUPS, 1, S5_GROUP), (1, 1, OCT))

    vm = pl.BlockSpec(memory_space=pltpu.VMEM)
    tab = (2, 2, CHUNK, S5_GROUP, GP)
    apre, apim, m, w, e = pl.pallas_call(
        _prep_kernel,
        in_specs=[vm] * 8,
        out_specs=[vm] * 5,
        out_shape=[jax.ShapeDtypeStruct((SEG_CHUNKS + 1, 2, GP), F32),
                   jax.ShapeDtypeStruct((SEG_CHUNKS + 1, 2, GP), F32),
                   jax.ShapeDtypeStruct((S5_GROUPS, CH, CH), BF16),
                   jax.ShapeDtypeStruct((PAIRS, 2 * CH, 4 * LANES), BF16),
                   jax.ShapeDtypeStruct((PAIRS, 4 * LANES, 2 * CH), BF16)],
        scratch_shapes=[pltpu.VMEM(tab, F32),
                        pltpu.VMEM(tab, F32),
                        pltpu.VMEM(((2 * CHUNK - 1) * S5_GROUP, LANES), F32)],
        compiler_params=pltpu.CompilerParams(vmem_limit_bytes=VMEM_LIMIT),
        name="s5_tables",
    )(lre, lim, lst, bre, bim, cre, cim, dt)
    return m, w, e, apre, apim


def _rms(x, g):
    return x * lax.rsqrt(jnp.mean(x * x, axis=-1, keepdims=True) + EPS) * g


def _slot_transpose(v):
    v = list(v)
    slot = lax.broadcasted_iota(jnp.int32, v[0].shape, 1) // S5_GROUP
    for k in (4, 2, 1):
        low = (slot & k) == 0
        for i in range(OCT):
            if i & k:
                continue
            a, b = v[i], v[i + k]
            v[i] = jnp.where(low, a, pltpu.roll(b, k * S5_GROUP, axis=1))
            v[i + k] = jnp.where(low, pltpu.roll(a, LANES - k * S5_GROUP, axis=1), b)
    return v


def _inproj_kernel(x_ref, mod_ref, g_ref, win_ref, bdc_ref, bds_ref,
                   xg_ref, fc_ref, fs_ref, gs_ref, gf_ref, h_scr, hb_scr, zs_scr):
    tm = x_ref.shape[0]
    mod = mod_ref[0]
    h = _rms(x_ref[...], g_ref[...]) * (1.0 + mod[1:2]) + mod[0:1]
    for k in range(D_MODEL // LANES):
        h_scr[k] = h[:, k * LANES:(k + 1) * LANES]
    for j in range(tm // SEG):
        for s in range(CHUNK):
            r0 = j * SEG + s * SEG_CHUNKS
            for k in range(D_MODEL // LANES):
                hb_scr[r0:r0 + SEG_CHUNKS, k * LANES:(k + 1) * LANES] = (
                    h_scr[k, pl.ds(j * SEG + s, SEG_CHUNKS, stride=CHUNK), :].astype(BF16))
    hb = hb_scr[...]
    dot = functools.partial(jnp.dot, preferred_element_type=F32)
    zs_scr[...] = dot(hb, win_ref[:, 0:S5_WIDTH])
    for j in range(tm // SEG):
        for b in range(S5_WIDTH // LANES):
            for hf in range(CHUNK // OCT):
                z = [zs_scr[j * SEG + s * SEG_CHUNKS:j * SEG + (s + 1) * SEG_CHUNKS, b * LANES:(b + 1) * LANES]
                     for s in range(OCT * hf, OCT * (hf + 1))]
                for i, xi in enumerate(_slot_transpose(z)):
                    xg_ref[OCT * b + i, j * SEG_CHUNKS:(j + 1) * SEG_CHUNKS,
                           hf * LANES:(hf + 1) * LANES] = xi.astype(BF16)
    uf = dot(hb, win_ref[:, S5_WIDTH:2 * S5_WIDTH]).astype(BF16)
    fc_ref[...] = dot(uf, bdc_ref[...]).astype(BF16)
    fs_ref[...] = dot(uf, bds_ref[...]).astype(BF16)
    o = 2 * S5_WIDTH
    gs_ref[...] = jax.nn.sigmoid(dot(hb, win_ref[:, o:o + D_MODEL])).astype(BF16)
    gf_ref[...] = jax.nn.sigmoid(dot(hb, win_ref[:, o + D_MODEL:o + 2 * D_MODEL])).astype(BF16)


def _channel_dft_mats():
    j = np.arange(FFT_GROUP)
    ang = 2.0 * np.pi * ((j[:, None] * j[None, :]) % FFT_GROUP) / FFT_GROUP
    blk_c = np.cos(ang) / math.sqrt(FFT_GROUP)
    blk_s = np.sin(ang) / math.sqrt(FFT_GROUP)
    bdc = np.kron(np.eye(FFT_GROUPS), blk_c)
    bds = np.kron(np.eye(FFT_GROUPS), blk_s)
    return jnp.asarray(bdc, F32).astype(BF16), jnp.asarray(bds, F32).astype(BF16)


def _const_spec(shape):
    nd = len(shape)
    return pl.BlockSpec(shape, lambda i: (0,) * nd, pipeline_mode=pl.Buffered(1))


def _inproj(x2d, mod, mod_row, norm_g, w_in_b, tm):
    t = x2d.shape[0]
    bdc, bds = _channel_dft_mats()
    tok = lambda w: pl.BlockSpec((tm, w), lambda i: (i, 0))
    out = lambda w: jax.ShapeDtypeStruct((t, w), BF16)
    return pl.pallas_call(
        _inproj_kernel,
        grid=(t // tm,),
        in_specs=[tok(D_MODEL),
                  pl.BlockSpec((1, N_MOD, D_MODEL), lambda i: (mod_row(i), 0, 0)),
                  _const_spec((1, D_MODEL)),
                  _const_spec(w_in_b.shape),
                  _const_spec(bdc.shape),
                  _const_spec(bds.shape)],
        out_specs=[pl.BlockSpec((S5_GROUPS, tm // CHUNK, CH), lambda i: (0, i, 0)),
                   tok(FFT_WIDTH), tok(FFT_WIDTH), tok(D_MODEL), tok(D_MODEL)],
        out_shape=[jax.ShapeDtypeStruct((S5_GROUPS, t // CHUNK, CH), BF16),
                   out(FFT_WIDTH), out(FFT_WIDTH), out(D_MODEL), out(D_MODEL)],
        scratch_shapes=[pltpu.VMEM((D_MODEL // LANES, tm, LANES), F32),
                        pltpu.VMEM((tm, D_MODEL), BF16),
                        pltpu.VMEM((tm, S5_WIDTH), F32)],
        compiler_params=_cparams(("parallel",)),
        name="inproj",
    )(x2d, mod, norm_g.reshape(1, D_MODEL), w_in_b, bdc, bds)


N_PROMPT_SEG = 16
N_SAMPLE_SEG = 32
N_SAMPLE_SEQ = 2
N_SEG = N_PROMPT_SEG + N_SAMPLE_SEG
ROWS = SEG_CHUNKS * N_SEG
ROWS_P = SEG_CHUNKS * N_PROMPT_SEG
OCT_PAIRS = OCT // 2


def _cmul_add(ar, ai, hr, hi, sr, si):
    return ar * hr - ai * hi + sr, ar * hi + ai * hr + si


def _s5_kernel(xp_ref, xs_ref, m_ref, w_ref, e_ref, apre_ref, apim_ref, h0_ref,
               yp_ref, ys_ref, fin_ref, s_scr, hin_scr, hinp_scr, f_scr, hs_scr, y_scr):
    dot = functools.partial(jnp.dot, preferred_element_type=F32)
    parts = ((xp_ref, 0, ROWS_P), (xs_ref, ROWS_P, ROWS - ROWS_P))
    seg_rows = lambda c: pl.ds(c, N_SEG, stride=SEG_CHUNKS)
    blk = lambda c: pl.ds(c * N_SEG, N_SEG)
    seq_rows = lambda j: pl.ds(j, N_SAMPLE_SEQ, stride=SEG_CHUNKS)
    lat = lambda c: pl.ds(c * N_SEG + N_PROMPT_SEG, N_SAMPLE_SEG)
    zero = jnp.zeros((N_SEG, LANES), F32)

    for pr in range(OCT_PAIRS):
        ln = slice(pr * LANES, (pr + 1) * LANES)
        g0, g1 = 2 * pr, 2 * pr + 1
        for x_ref, r0, nr in parts:
            s = dot(x_ref[g0], w_ref[pr, 0:CH, :]) + dot(x_ref[g1], w_ref[pr, CH:2 * CH, :])
            for k in range(4):
                s_scr[k, r0:r0 + nr, :] = s[:, k * LANES:(k + 1) * LANES]

        ar, ai = apre_ref[1, 0:1, ln], apim_ref[1, 0:1, ln]
        hr, hi = zero, zero
        for c in range(SEG_CHUNKS):
            hin_scr[0, blk(c), :] = hr
            hin_scr[1, blk(c), :] = hi
            hr, hi = _cmul_add(ar, ai, hr, hi, s_scr[0, seg_rows(c), :], s_scr[1, seg_rows(c), :])
        fin_ref[pr, 0] = hr[0:N_PROMPT_SEG]
        fin_ref[pr, 1] = hi[0:N_PROMPT_SEG]
        f_scr[0] = hr[N_PROMPT_SEG:]
        f_scr[1] = hi[N_PROMPT_SEG:]
        br, bi = apre_ref[1, 1:2, ln], apim_ref[1, 1:2, ln]
        gr, gi = zero, zero
        for c in range(SEG_CHUNKS - 1, -1, -1):
            hin_scr[2, blk(c), :] = gr
            hin_scr[3, blk(c), :] = gi
            gr, gi = _cmul_add(br, bi, gr, gi, s_scr[2, seg_rows(c), :], s_scr[3, seg_rows(c), :])
        fin_ref[pr, 2] = gr[0:N_PROMPT_SEG]
        fin_ref[pr, 3] = gi[0:N_PROMPT_SEG]
        f_scr[2] = gr[N_PROMPT_SEG:]
        f_scr[3] = gi[N_PROMPT_SEG:]

        a2r, a2i = apre_ref[SEG_CHUNKS, 0:1, ln], apim_ref[SEG_CHUNKS, 0:1, ln]
        hr, hi = h0_ref[0, :, ln], h0_ref[1, :, ln]
        for j in range(SEG_CHUNKS):
            hs_scr[0, seq_rows(j), :] = hr
            hs_scr[1, seq_rows(j), :] = hi
            hr, hi = _cmul_add(a2r, a2i, hr, hi, f_scr[0, seq_rows(j), :], f_scr[1, seq_rows(j), :])
        b2r, b2i = apre_ref[SEG_CHUNKS, 1:2, ln], apim_ref[SEG_CHUNKS, 1:2, ln]
        gr, gi = h0_ref[2, :, ln], h0_ref[3, :, ln]
        for j in range(SEG_CHUNKS - 1, -1, -1):
            hs_scr[2, seq_rows(j), :] = gr
            hs_scr[3, seq_rows(j), :] = gi
            gr, gi = _cmul_add(b2r, b2i, gr, gi, f_scr[2, seq_rows(j), :], f_scr[3, seq_rows(j), :])
        for c in range(SEG_CHUNKS):
            p_r, p_i = apre_ref[c, 0:1, ln], apim_ref[c, 0:1, ln]
            hr, hi = _cmul_add(p_r, p_i, hs_scr[0], hs_scr[1], hin_scr[0, lat(c), :], hin_scr[1, lat(c), :])
            hin_scr[0, lat(c), :] = hr
            hin_scr[1, lat(c), :] = hi
            cb = SEG_CHUNKS - 1 - c
            p_r, p_i = apre_ref[cb, 1:2, ln], apim_ref[cb, 1:2, ln]
            gr, gi = _cmul_add(p_r, p_i, hs_scr[2], hs_scr[3], hin_scr[2, lat(c), :], hin_scr[3, lat(c), :])
            hin_scr[2, lat(c), :] = gr
            hin_scr[3, lat(c), :] = gi

        for k in range(4):
            for sg in range(N_SEG):
                hinp_scr[sg * SEG_CHUNKS:(sg + 1) * SEG_CHUNKS, k * LANES:(k + 1) * LANES] = (
                    hin_scr[k, pl.ds(sg, SEG_CHUNKS, stride=N_SEG), :].astype(BF16))
        for gi_, g in ((0, g0), (1, g1)):
            for x_ref, r0, nr in parts:
                y_scr[g, r0:r0 + nr, :] = (dot(x_ref[g], m_ref[g])
                                           + dot(hinp_scr[r0:r0 + nr, :], e_ref[pr, :, gi_ * CH:(gi_ + 1) * CH]))

    rb_rows = 2 * SEG_CHUNKS
    for rb in range(ROWS // rb_rows):
        for hf in range(CHUNK // OCT):
            v = [y_scr[i, rb * rb_rows:(rb + 1) * rb_rows, hf * LANES:(hf + 1) * LANES] for i in range(OCT)]
            for s, acc in enumerate(_slot_transpose(v)):
                t = OCT * hf + s
                for sg in range(rb_rows // SEG_CHUNKS):
                    seg = rb * (rb_rows // SEG_CHUNKS) + sg
                    piece = acc[sg * SEG_CHUNKS:(sg + 1) * SEG_CHUNKS]
                    if seg < N_PROMPT_SEG:
                        r0 = seg * SEG + t * SEG_CHUNKS
                        yp_ref[0, r0:r0 + SEG_CHUNKS, :] = piece
                    else:
                        r0 = (seg - N_PROMPT_SEG) * SEG + t * SEG_CHUNKS
                        ys_ref[0, r0:r0 + SEG_CHUNKS, :] = piece


def _s5_conv(xg_p, xg_s, m, w, e, apre, apim, h0):
    n_oct = S5_GROUPS // OCT
    tp, ts = xg_p.shape[1] * CHUNK, xg_s.shape[1] * CHUNK
    return pl.pallas_call(
        _s5_kernel,
        grid=(n_oct,),
        in_specs=[pl.BlockSpec((OCT, ROWS_P, CH), lambda o: (o, 0, 0)),
                  pl.BlockSpec((OCT, ROWS - ROWS_P, CH), lambda o: (o, 0, 0)),
                  pl.BlockSpec((OCT, CH, CH), lambda o: (o, 0, 0)),
                  pl.BlockSpec((OCT_PAIRS, 2 * CH, 4 * LANES), lambda o: (o, 0, 0)),
                  pl.BlockSpec((OCT_PAIRS, 4 * LANES, 2 * CH), lambda o: (o, 0, 0)),
                  pl.BlockSpec((SEG_CHUNKS + 1, 2, OCT_PAIRS * LANES), lambda o: (0, 0, o)),
                  pl.BlockSpec((SEG_CHUNKS + 1, 2, OCT_PAIRS * LANES), lambda o: (0, 0, o)),
                  pl.BlockSpec((4, N_SAMPLE_SEQ, OCT_PAIRS * LANES), lambda o: (0, 0, o))],
        out_specs=[pl.BlockSpec((1, tp, LANES), lambda o: (o, 0, 0)),
                   pl.BlockSpec((1, ts, LANES), lambda o: (o, 0, 0)),
                   pl.BlockSpec((OCT_PAIRS, 4, N_PROMPT_SEG, LANES), lambda o: (o, 0, 0, 0))],
        out_shape=[jax.ShapeDtypeStruct((n_oct, tp, LANES), F32),
                   jax.ShapeDtypeStruct((n_oct, ts, LANES), F32),
                   jax.ShapeDtypeStruct((PAIRS, 4, N_PROMPT_SEG, LANES), F32)],
        scratch_shapes=[pltpu.VMEM((4, ROWS, LANES), F32),
                        pltpu.VMEM((4, ROWS, LANES), F32),
                        pltpu.VMEM((ROWS, 4 * LANES), BF16),
                        pltpu.VMEM((4, N_SAMPLE_SEG, LANES), F32),
                        pltpu.VMEM((4, N_SAMPLE_SEG, LANES), F32),
                        pltpu.VMEM((OCT, ROWS, CH), F32)],
        compiler_params=_cparams(("parallel",)),
        name="s5_conv",
    )(xg_p, xg_s, m, w, e, apre, apim, h0)


def _tile_order_positions(l):
    r = np.arange(l)
    tile, wi = r // SEG, r % SEG
    return tile * SEG + (wi % SEG_CHUNKS) * CHUNK + wi // SEG_CHUNKS


def _tile_dft_tables(scale, rows_in_tile_order=True):
    pos = _tile_order_positions(SEG)
    freq = pos if rows_in_tile_order else np.arange(SEG)
    ang = 2.0 * np.pi * ((freq[:, None] * pos[None, :]) % SEG) / SEG
    return jnp.asarray(np.cos(ang) * scale, F32), jnp.asarray(np.sin(ang) * scale, F32)


def _pos_dft_tile_kernel(cm_ref, sm_ref, fc_ref, fs_ref, o_ref):
    dot = functools.partial(jnp.dot, preferred_element_type=F32)
    cm, sm = cm_ref[...].astype(BF16), sm_ref[...].astype(BF16)
    for b in range(fc_ref.shape[0]):
        o_ref[b] = (dot(cm, fc_ref[b]) - dot(sm, fs_ref[b])).astype(BF16)


def _pos_dft_tile(fc, fs):
    n, l, wdt = fc.shape
    cm, sm = _tile_dft_tables(1.0 / math.sqrt(l))
    nb = 4
    seq = pl.BlockSpec((nb, l, wdt), lambda b: (b, 0, 0))
    return pl.pallas_call(
        _pos_dft_tile_kernel,
        grid=(n // nb,),
        in_specs=[_const_spec((l, l)), _const_spec((l, l)), seq, seq],
        out_specs=seq,
        out_shape=jax.ShapeDtypeStruct((n, l, wdt), BF16),
        compiler_params=_cparams(("parallel",)),
        name="pos_dft_tile",
    )(cm, sm, fc, fs)


N_TILES = 16


def _fft16(xr, xi):
    n = len(xr)
    rev = [int(format(i, "04b")[::-1], 2) for i in range(n)]
    ar = [xr[r] for r in rev]
    ai = [xi[r] for r in rev]
    size = 2
    while size <= n:
        half = size // 2
        for start in range(0, n, size):
            for k in range(half):
                wr = math.cos(2.0 * math.pi * k / size)
                wi = -math.sin(2.0 * math.pi * k / size)
                i0, i1 = start + k, start + k + half
                if k == 0:
                    tr, ti = ar[i1], ai[i1]
                elif 4 * k == size:
                    tr, ti = ai[i1], -ar[i1]
                else:
                    tr = ar[i1] * wr - ai[i1] * wi
                    ti = ar[i1] * wi + ai[i1] * wr
                ar[i1], ai[i1] = ar[i0] - tr, ai[i0] - ti
                ar[i0], ai[i0] = ar[i0] + tr, ai[i0] + ti
        size *= 2
    return ar, ai


def _pos_dft_long_kernel(cb_ref, sb_ref, ca_ref, sa_ref, fc_ref, fs_ref, o_ref, a_scr):
    wdt = fc_ref.shape[2]

    def rows_fft(r, carry):
        rows = pl.ds(pl.multiple_of(r * CHUNK, CHUNK), CHUNK)
        for b in range(wdt // LANES):
            lns = slice(b * LANES, (b + 1) * LANES)
            xr = [fc_ref[j, rows, lns].astype(F32) for j in range(N_TILES)]
            xi = [-fs_ref[j, rows, lns].astype(F32) for j in range(N_TILES)]
            ar, ai = _fft16(xr, xi)
            for k in range(N_TILES):
                a_scr[k, 0, rows, lns] = ar[k].astype(BF16)
                a_scr[k, 1, rows, lns] = ai[k].astype(BF16)
        return carry

    lax.fori_loop(0, SEG // CHUNK, rows_fft, 0)

    dot = functools.partial(jnp.dot, preferred_element_type=F32)
    cb, sb = cb_ref[...], sb_ref[...]
    for k1 in range(N_TILES):
        ca, sa = ca_ref[k1:k1 + 1, :], sa_ref[k1:k1 + 1, :]
        dc = (cb * ca - sb * sa).astype(BF16)
        ds = (sb * ca + cb * sa).astype(BF16)
        out = dot(dc, a_scr[k1, 0]) + dot(ds, a_scr[k1, 1])
        o_ref[:, k1 * SEG_CHUNKS:(k1 + 1) * SEG_CHUNKS, :] = out.astype(BF16).reshape(N_TILES, SEG_CHUNKS, wdt)


def _pos_dft_long(fc, fs):
    n, l, wdt = fc.shape
    assert l == N_TILES * SEG
    wb = 2 * LANES
    cb, sb = _tile_dft_tables(1.0 / math.sqrt(l), rows_in_tile_order=False)
    pos = _tile_order_positions(SEG)
    ang = 2.0 * np.pi * (np.arange(N_TILES)[:, None] * pos[None, :]) / l
    ca, sa = jnp.asarray(np.cos(ang), F32), jnp.asarray(np.sin(ang), F32)
    const2 = lambda shape: pl.BlockSpec(shape, lambda b, k: (0, 0), pipeline_mode=pl.Buffered(1))
    seq = pl.BlockSpec((None, N_TILES, SEG, wb), lambda b, k: (b, 0, 0, k))
    out = pl.pallas_call(
        _pos_dft_long_kernel,
        grid=(n, wdt // wb),
        in_specs=[const2((SEG, SEG)), const2((SEG, SEG)), const2((N_TILES, SEG)), const2((N_TILES, SEG)), seq, seq],
        out_specs=seq,
        out_shape=jax.ShapeDtypeStruct((n, N_TILES, SEG, wdt), BF16),
        scratch_shapes=[pltpu.VMEM((N_TILES, 2, SEG, wb), BF16)],
        compiler_params=_cparams(("parallel", "parallel")),
        name="pos_dft_long",
    )(cb, sb, ca, sa, fc.reshape(n, N_TILES, SEG, wdt), fs.reshape(n, N_TILES, SEG, wdt))
    return out.reshape(n, l, wdt)


def _mix_ffn_kernel(x_ref, mod_ref, ys_ref, yf_ref, gs_ref, gf_ref,
                    wglu_ref, bglu_ref, wps_ref, wpf_ref, wout_ref, n2_ref,
                    wg_ref, wu_ref, wd_ref, fn_ref, o_ref, m_scr, mb_scr):
    tm = x_ref.shape[0]
    dot = functools.partial(jnp.dot, preferred_element_type=F32)
    mod = mod_ref[0]
    y = jnp.concatenate([ys_ref[b] for b in range(S5_WIDTH // LANES)], axis=1)
    z = jax.nn.gelu(y)
    z = z * jax.nn.sigmoid(dot(z.astype(BF16), wglu_ref[...]) + bglu_ref[...])
    m = (gs_ref[...].astype(F32) * dot(z.astype(BF16), wps_ref[...])
         + gf_ref[...].astype(F32) * dot(yf_ref[...], wpf_ref[...]))
    for k in range(D_MODEL // LANES):
        m_scr[k] = m[:, k * LANES:(k + 1) * LANES]
    for j in range(tm // SEG):
        for c in range(SEG_CHUNKS):
            r0 = j * SEG + c * CHUNK
            for k in range(D_MODEL // LANES):
                mb_scr[r0:r0 + CHUNK, k * LANES:(k + 1) * LANES] = (
                    m_scr[k, pl.ds(j * SEG + c, CHUNK, stride=SEG_CHUNKS), :].astype(BF16))
    x1 = x_ref[...] + mod[2:3] * dot(mb_scr[...], wout_ref[...])
    h2 = (_rms(x1, n2_ref[...]) * (1.0 + mod[4:5]) + mod[3:4]).astype(BF16)
    ff = None
    for n0, n1 in zip(FF_SPLITS[:-1], FF_SPLITS[1:]):
        gate = dot(h2, wg_ref[:, n0:n1])
        up = dot(h2, wu_ref[:, n0:n1])
        part = dot((gate * jax.nn.sigmoid(gate) * up).astype(BF16), wd_ref[n0:n1, :])
        ff = part if ff is None else ff + part
    x2 = x1 + mod[5:6] * ff
    o_ref[...] = _rms(x2, fn_ref[...])


def _mix_ffn(x2d, mod, mod_row, ys, yf, gs, gf, wts, tm):
    t = x2d.shape[0]
    tok = lambda w: pl.BlockSpec((tm, w), lambda i: (i, 0))
    return pl.pallas_call(
        _mix_ffn_kernel,
        grid=(t // tm,),
        in_specs=[tok(D_MODEL),
                  pl.BlockSpec((1, N_MOD, D_MODEL), lambda i: (mod_row(i), 0, 0)),
                  pl.BlockSpec((S5_WIDTH // LANES, tm, LANES), lambda i: (0, i, 0)),
                  tok(FFT_WIDTH), tok(D_MODEL), tok(D_MODEL)]
                 + [_const_spec(w.shape) for w in wts],
        out_specs=tok(D_MODEL),
        out_shape=jax.ShapeDtypeStruct((t, D_MODEL), F32),
        scratch_shapes=[pltpu.VMEM((D_MODEL // LANES, tm, LANES), F32),
                        pltpu.VMEM((tm, D_MODEL), BF16)],
        compiler_params=_cparams(("parallel",)),
        name="mix_ffn",
    )(x2d, mod, ys, yf, gs, gf, *wts)


def kernel(x_prompt, x_sample, state_s5, c, c_ctx, norm1_g, norm2_g, w_ada, b_ada, w_in,
           s5_lambda_re, s5_lambda_im, s5_log_step, s5_b_re, s5_b_im, s5_c_re, s5_c_im,
           s5_d, w_glu, b_glu, w_proj_s5, w_proj_fft, w_out, w_ffn_gate, w_ffn_up,
           w_ffn_down, final_norm_g):
    nb, sl, _ = x_prompt.shape
    db, dl, _ = x_sample.shape
    assert w_in.shape[0] == 1 and sl == SEG and nb == N_PROMPT_SEG
    assert db == N_SAMPLE_SEQ and dl == SEG * SEG_CHUNKS

    cvec = jnp.concatenate([c_ctx[None], c, jnp.zeros((8 - 1 - db, D_MODEL), F32)], axis=0)
    mod = _adaln(cvec, w_ada[0], b_ada[0]).reshape(8, N_MOD, D_MODEL)

    m, w, e, apre, apim = _s5_tables(s5_lambda_re[0], s5_lambda_im[0], s5_log_step[0], s5_b_re[0],
                                     s5_b_im[0], s5_c_re[0], s5_c_im[0], s5_d[0])

    tm = 512
    xp = x_prompt.reshape(nb * sl, D_MODEL)
    xs = x_sample.reshape(db * dl, D_MODEL)
    row_p = lambda i: 0
    row_s = lambda i: 1 + i // (dl // tm)
    w_in_b = w_in[0].astype(BF16)
    xg_p, fc_p, fs_p, gs_p, gf_p = _inproj(xp, mod, row_p, norm1_g[0], w_in_b, tm)
    xg_s, fc_s, fs_s, gs_s, gf_s = _inproj(xs, mod, row_s, norm1_g[0], w_in_b, tm)

    h0 = state_s5[:, 0].astype(F32).transpose(1, 2, 0, 3, 4).reshape(4, db, GP)
    ys_p, ys_s, fin = _s5_conv(xg_p, xg_s, m, w, e, apre, apim, h0)
    new_state = fin.reshape(PAIRS, 2, 2, nb, 2, S5_STATE).transpose(3, 1, 2, 0, 4, 5)
    new_state = new_state.reshape(nb, 1, 2, 2, S5_GROUPS, S5_STATE)

    yf_p = _pos_dft_tile(fc_p.reshape(nb, sl, FFT_WIDTH), fs_p.reshape(nb, sl, FFT_WIDTH))
    yf_s = _pos_dft_long(fc_s.reshape(db, dl, FFT_WIDTH), fs_s.reshape(db, dl, FFT_WIDTH))

    wts = (w_glu[0].astype(BF16), b_glu[0].reshape(1, S5_WIDTH), w_proj_s5[0].astype(BF16),
           w_proj_fft[0].astype(BF16), w_out[0].astype(BF16), norm2_g[0].reshape(1, D_MODEL),
           w_ffn_gate[0].astype(BF16), w_ffn_up[0].astype(BF16), w_ffn_down[0].astype(BF16),
           final_norm_g.reshape(1, D_MODEL))
    tm2 = 2 * SEG
    row_s2 = lambda i: 1 + i // (dl // tm2)
    y_p = _mix_ffn(xp, mod, row_p, ys_p, yf_p.reshape(nb * sl, FFT_WIDTH), gs_p, gf_p, wts, tm2)
    y_s = _mix_ffn(xs, mod, row_s2, ys_s, yf_s.reshape(db * dl, FFT_WIDTH), gs_s, gf_s, wts, tm2)
    return (y_p.reshape(nb, sl, D_MODEL), y_s.reshape(db, dl, D_MODEL), new_state)
```

```python
import functools
import math

import numpy as np
import jax
import jax.numpy as jnp
from jax import lax
from jax.experimental import pallas as pl
from jax.experimental.pallas import tpu as pltpu

F32 = jnp.float32
BF16 = jnp.bfloat16

D_MODEL = 1024
S5_WIDTH = 512
S5_GROUPS = 32
S5_GROUP = 16
S5_STATE = 64
FFT_WIDTH = 512
FFT_GROUPS = 4
FFT_GROUP = 128
D_FF = 2816
N_MOD = 6
EPS = 1e-6

LANES = 128
CHUNK = 16
SEG_CHUNKS = 16
SEG = CHUNK * SEG_CHUNKS
GP = S5_GROUPS * S5_STATE
PAIRS = S5_GROUPS // 2
CH = CHUNK * S5_GROUP
OCT = LANES // S5_GROUP

MXU_DIM = 256
FF_SPLITS = (0, (D_FF // MXU_DIM + 1) // 2 * MXU_DIM, D_FF)
VMEM_LIMIT = 58 * 1024 * 1024


def _cparams(sem):
    return pltpu.CompilerParams(dimension_semantics=sem, vmem_limit_bytes=VMEM_LIMIT)


def _ada_kernel(c_ref, w_ref, b_ref, o_ref):
    dot = functools.partial(jnp.dot, preferred_element_type=F32)
    c = c_ref[...]
    s = c * jax.nn.sigmoid(c)
    s_hi = s.astype(BF16)
    s_lo = (s - s_hi.astype(F32)).astype(BF16)
    w = w_ref[...]
    w_hi = w.astype(BF16)
    w_lo = (w - w_hi.astype(F32)).astype(BF16)
    r = dot(jnp.concatenate([s_hi, s_lo], axis=0), w_hi)
    n = s.shape[0]
    o_ref[...] = r[0:n] + r[n:2 * n] + dot(s_hi, w_lo) + b_ref[...]


def _adaln(cvec8, w_ada, b_ada):
    n_out = w_ada.shape[1]
    tn = 1024
    return pl.pallas_call(
        _ada_kernel,
        grid=(n_out // tn,),
        in_specs=[pl.BlockSpec((8, D_MODEL), lambda j: (0, 0)),
                  pl.BlockSpec((D_MODEL, tn), lambda j: (0, j)),
                  pl.BlockSpec((1, tn), lambda j: (0, j))],
        out_specs=pl.BlockSpec((8, tn), lambda j: (0, j)),
        out_shape=jax.ShapeDtypeStruct((8, n_out), F32),
        compiler_params=_cparams(("arbitrary",)),
        name="adaln",
    )(cvec8, w_ada, b_ada.reshape(1, n_out))


def _prep_kernel(lre_ref, lim_ref, lst_ref, bre_ref, bim_ref, cre_ref, cim_ref, dt_ref,
                 apre_ref, apim_ref, m_ref, w_ref, e_ref, gb_scr, ca_scr, ts_scr):
    lre = lre_ref[...]
    lim = lim_ref[...]
    step = jnp.exp(lst_ref[...])
    mag = jnp.exp(lre * step)
    are = mag * jnp.cos(lim * step)
    aim = mag * jnp.sin(lim * step)
    nr = are - 1.0
    den = lre * lre + lim * lim
    fr = (nr * lre + aim * lim) / den
    fi = (aim * lre - nr * lim) / den

    pr = [jnp.ones_like(are)]
    pi = [jnp.zeros_like(are)]
    for _ in range(CHUNK):
        r, i = pr[-1], pi[-1]
        pr.append(r * are - i * aim)
        pi.append(r * aim + i * are)
    a16r, a16i = pr[CHUNK], pi[CHUNK]
    qr, qi = jnp.ones_like(are), jnp.zeros_like(are)
    for c in range(SEG_CHUNKS + 1):
        apre_ref[c] = qr
        apim_ref[c] = qi
        qr, qi = qr * a16r - qi * a16i, qr * a16i + qi * a16r

    for d in range(2):
        bbre = fr[d:d + 1] * bre_ref[d] - fi[d:d + 1] * bim_ref[d]
        bbim = fr[d:d + 1] * bim_ref[d] + fi[d:d + 1] * bre_ref[d]
        cre, cim = cre_ref[d], cim_ref[d]
        for s in range(CHUNK):
            k = CHUNK - 1 - s if d == 0 else s
            r, i = pr[k][d:d + 1], pi[k][d:d + 1]
            gb_scr[0, d, s] = r * bbre - i * bbim
            gb_scr[1, d, s] = r * bbim + i * bbre
            f = s + 1 if d == 0 else CHUNK - s
            r, i = pr[f][d:d + 1], pi[f][d:d + 1]
            ca_scr[0, d, s] = r * cre - i * cim
            ca_scr[1, d, s] = -(r * cim + i * cre)

    dot_nt = lambda a, b: lax.dot_general(a, b, (((1,), (1,)), ((), ())), preferred_element_type=F32)
    lane_gi = lax.broadcasted_iota(jnp.int32, (CH, LANES), 1) // S5_STATE
    row_gi = lax.broadcasted_iota(jnp.int32, (LANES, CH), 0) // S5_STATE
    eye = (lax.broadcasted_iota(jnp.int32, (LANES, LANES), 0)
           == lax.broadcasted_iota(jnp.int32, (LANES, LANES), 1)).astype(BF16)
    slot = lax.broadcasted_iota(jnp.int32, (CH, LANES), 1) // S5_GROUP
    n_lag_rows = (2 * CHUNK - 1) * S5_GROUP

    def pair(q, carry):
        lanes = pl.ds(pl.multiple_of(q * LANES, LANES), LANES)
        for d in range(2):
            for ri in range(2):
                col = slice((2 * d + ri) * LANES, (2 * d + ri + 1) * LANES)
                gb = gb_scr[ri, d, :, :, lanes].reshape(CH, LANES)
                ca = ca_scr[ri, d, :, :, lanes].reshape(CH, LANES).astype(BF16)
                ca_t = dot_nt(eye, ca)
                for gi in range(2):
                    w_ref[q, gi * CH:(gi + 1) * CH, col] = jnp.where(lane_gi == gi, gb, 0.0).astype(BF16)
                    e_ref[q, col, gi * CH:(gi + 1) * CH] = jnp.where(row_gi == gi, ca_t, 0.0).astype(BF16)
        for gi in range(2):
            g = 2 * q + gi
            lag = []
            for d in range(2):
                c_re = jnp.concatenate([cre_ref[d, :, lanes]] * OCT, axis=0)
                c_imn = jnp.concatenate([-cim_ref[d, :, lanes]] * OCT, axis=0)
                keep = lax.broadcasted_iota(jnp.int32, (LANES, LANES), 1) // S5_STATE == gi
                c_re = jnp.where(keep, c_re, 0.0).astype(BF16)
                c_imn = jnp.where(keep, c_imn, 0.0).astype(BF16)
                gre = gb_scr[0, d, :, :, lanes].reshape(CH, LANES).astype(BF16)
                gim = gb_scr[1, d, :, :, lanes].reshape(CH, LANES).astype(BF16)
                lag.append(dot_nt(gre, c_re) + dot_nt(gim, c_imn))
            zl = (CHUNK - 1) * S5_GROUP
            ts_scr[0:zl, :] = lag[0][0:zl]
            ts_scr[zl:zl + S5_GROUP, :] = lag[0][zl:] + lag[1][0:S5_GROUP] + dt_ref[g]
            ts_scr[zl + S5_GROUP:n_lag_rows, :] = lag[1][S5_GROUP:]
            for hf in range(CHUNK // OCT):
                acc = None
                for s in range(OCT):
                    t = OCT * hf + s
                    win = ts_scr[(CHUNK - 1 - t) * S5_GROUP:(CHUNK - 1 - t) * S5_GROUP + CH, :]
                    acc = win if acc is None else jnp.where(slot == s, win, acc)
                m_ref[g, :, hf * LANES:(hf + 1) * LANES] = acc.astype(BF16)
        return carry

    lax.fori_loop(0, PAIRS, pair, 0)


def _s5_tables(lam_re, lam_im, log_step, b_re, b_im, c_re, c_im, s5_d):
    lre = lam_re.reshape(2, GP)
    lim = lam_im.reshape(2, GP)
    lst = jnp.repeat(log_step, S5_STATE, axis=-1)
    bre = b_re.transpose(0, 3, 1, 2).reshape(2, S5_GROUP, GP)
    bim = b_im.transpose(0, 3, 1, 2).reshape(2, S5_GROUP, GP)
    cre = c_re.transpose(0, 2, 1, 3).reshape(2, S5_GROUP, GP)
    cim = c_im.transpose(0, 2, 1, 3).reshape(2, S5_GROUP, GP)
    eye = jnp.asarray(np.eye(S5_GROUP, dtype=np.float32))
    dt = jnp.tile(eye[None] * s5_d.reshape(S5_GROUPS, 1, S5_GROUP), (1, 1, OCT))

    vm = pl.BlockSpec(memory_space=pltpu.VMEM)
    tab = (2, 2, CHUNK, S5_GROUP, GP)
    apre, apim, m, w, e = pl.pallas_call(
        _prep_kernel,
        in_specs=[vm] * 8,
        out_specs=[vm] * 5,
        out_shape=[jax.ShapeDtypeStruct((SEG_CHUNKS + 1, 2, GP), F32),
                   jax.ShapeDtypeStruct((SEG_CHUNKS + 1, 2, GP), F32),
                   jax.ShapeDtypeStruct((S5_GROUPS, CH, CH), BF16),
                   jax.ShapeDtypeStruct((PAIRS, 2 * CH, 4 * LANES), BF16),
                   jax.ShapeDtypeStruct((PAIRS, 4 * LANES, 2 * CH), BF16)],
        scratch_shapes=[pltpu.VMEM(tab, F32),
                        pltpu.VMEM(tab, F32),
                        pltpu.VMEM(((2 * CHUNK - 1) * S5_GROUP, LANES), F32)],
        compiler_params=pltpu.CompilerParams(vmem_limit_bytes=VMEM_LIMIT),
        name="s5_tables",
    )(lre, lim, lst, bre, bim, cre, cim, dt)
    return m, w, e, apre, apim


def _rms(x, g):
    return x * lax.rsqrt(jnp.mean(x * x, axis=-1, keepdims=True) + EPS) * g


def _slot_transpose(v):
    v = list(v)
    slot = lax.broadcasted_iota(jnp.int32, v[0].shape, 1) // S5_GROUP
    for k in (4, 2, 1):
        low = (slot & k) == 0
        for i in range(OCT):
            if i & k:
                continue
            a, b = v[i], v[i + k]
            v[i] = jnp.where(low, a, pltpu.roll(b, k * S5_GROUP, axis=1))
            v[i + k] = jnp.where(low, pltpu.roll(a, LANES - k * S5_GROUP, axis=1), b)
    return v


def _inproj_kernel(n_prompt_tiles, xp_ref, xs_ref, mod_ref, g_ref, win_ref, bdc_ref, bds_ref,
                   xg_ref, fc_ref, fs_ref, gs_ref, gf_ref, h_scr, hb_scr, zs_scr):
    tm = xp_ref.shape[0]
    mod = mod_ref[0]
    x = jnp.where(pl.program_id(0) < n_prompt_tiles, xp_ref[...], xs_ref[...])
    h = _rms(x, g_ref[...]) * (1.0 + mod[1:2]) + mod[0:1]
    for k in range(D_MODEL // LANES):
        h_scr[k] = h[:, k * LANES:(k + 1) * LANES]
    for j in range(tm // SEG):
        for s in range(CHUNK):
            r0 = j * SEG + s * SEG_CHUNKS
            for k in range(D_MODEL // LANES):
                hb_scr[r0:r0 + SEG_CHUNKS, k * LANES:(k + 1) * LANES] = (
                    h_scr[k, pl.ds(j * SEG + s, SEG_CHUNKS, stride=CHUNK), :].astype(BF16))
    hb = hb_scr[...]
    dot = functools.partial(jnp.dot, preferred_element_type=F32)
    zs_scr[...] = dot(hb, win_ref[:, 0:S5_WIDTH])
    for j in range(tm // SEG):
        for b in range(S5_WIDTH // LANES):
            for hf in range(CHUNK // OCT):
                z = [zs_scr[j * SEG + s * SEG_CHUNKS:j * SEG + (s + 1) * SEG_CHUNKS, b * LANES:(b + 1) * LANES]
                     for s in range(OCT * hf, OCT * (hf + 1))]
                for i, xi in enumerate(_slot_transpose(z)):
                    xg_ref[OCT * b + i, j * SEG_CHUNKS:(j + 1) * SEG_CHUNKS,
                           hf * LANES:(hf + 1) * LANES] = xi.astype(BF16)
    uf = dot(hb, win_ref[:, S5_WIDTH:2 * S5_WIDTH]).astype(BF16)
    fc_ref[...] = dot(uf, bdc_ref[...]).astype(BF16)
    fs_ref[...] = dot(uf, bds_ref[...]).astype(BF16)
    o = 2 * S5_WIDTH
    gs_ref[...] = jax.nn.sigmoid(dot(hb, win_ref[:, o:o + D_MODEL])).astype(BF16)
    gf_ref[...] = jax.nn.sigmoid(dot(hb, win_ref[:, o + D_MODEL:o + 2 * D_MODEL])).astype(BF16)


def _channel_dft_mats():
    j = np.arange(FFT_GROUP)
    ang = 2.0 * np.pi * ((j[:, None] * j[None, :]) % FFT_GROUP) / FFT_GROUP
    blk_c = np.cos(ang) / math.sqrt(FFT_GROUP)
    blk_s = np.sin(ang) / math.sqrt(FFT_GROUP)
    bdc = np.kron(np.eye(FFT_GROUPS), blk_c)
    bds = np.kron(np.eye(FFT_GROUPS), blk_s)
    return jnp.asarray(bdc, F32).astype(BF16), jnp.asarray(bds, F32).astype(BF16)


def _const_spec(shape):
    nd = len(shape)
    return pl.BlockSpec(shape, lambda i: (0,) * nd, pipeline_mode=pl.Buffered(1))


def _two_part_specs(tm, width, n_prompt_tiles):
    return [pl.BlockSpec((tm, width), lambda i: (jnp.minimum(i, n_prompt_tiles - 1), 0)),
            pl.BlockSpec((tm, width), lambda i: (jnp.maximum(i - n_prompt_tiles, 0), 0))]


def _mod_spec(n_prompt_tiles, tiles_per_seq):
    row = lambda i: jnp.where(i < n_prompt_tiles, 0, 1 + (i - n_prompt_tiles) // tiles_per_seq)
    return pl.BlockSpec((1, N_MOD, D_MODEL), lambda i: (row(i), 0, 0))


def _inproj(xp, xs, seq_len, mod, norm_g, w_in_b, tm):
    t = xp.shape[0] + xs.shape[0]
    n_p = xp.shape[0] // tm
    bdc, bds = _channel_dft_mats()
    tok = lambda w: pl.BlockSpec((tm, w), lambda i: (i, 0))
    out = lambda w: jax.ShapeDtypeStruct((t, w), BF16)
    return pl.pallas_call(
        functools.partial(_inproj_kernel, n_p),
        grid=(t // tm,),
        in_specs=_two_part_specs(tm, D_MODEL, n_p) + [
                  _mod_spec(n_p, seq_len // tm),
                  _const_spec((1, D_MODEL)),
                  _const_spec(w_in_b.shape),
                  _const_spec(bdc.shape),
                  _const_spec(bds.shape)],
        out_specs=[pl.BlockSpec((S5_GROUPS, tm // CHUNK, CH), lambda i: (0, i, 0)),
                   tok(FFT_WIDTH), tok(FFT_WIDTH), tok(D_MODEL), tok(D_MODEL)],
        out_shape=[jax.ShapeDtypeStruct((S5_GROUPS, t // CHUNK, CH), BF16),
                   out(FFT_WIDTH), out(FFT_WIDTH), out(D_MODEL), out(D_MODEL)],
        scratch_shapes=[pltpu.VMEM((D_MODEL // LANES, tm, LANES), F32),
                        pltpu.VMEM((tm, D_MODEL), BF16),
                        pltpu.VMEM((tm, S5_WIDTH), F32)],
        compiler_params=_cparams(("parallel",)),
        name="inproj",
    )(xp, xs, mod, norm_g.reshape(1, D_MODEL), w_in_b, bdc, bds)


N_PROMPT_SEG = 16
N_SAMPLE_SEG = 32
N_SAMPLE_SEQ = 2
N_SEG = N_PROMPT_SEG + N_SAMPLE_SEG
ROWS = SEG_CHUNKS * N_SEG
ROWS_P = SEG_CHUNKS * N_PROMPT_SEG
OCT_PAIRS = OCT // 2


def _cmul_add(ar, ai, hr, hi, sr, si):
    return ar * hr - ai * hi + sr, ar * hi + ai * hr + si


def _s5_kernel(x_ref, m_ref, w_ref, e_ref, apre_ref, apim_ref, h0_ref,
               y_ref, fin_ref, s_scr, hin_scr, hinp_scr, f_scr, hs_scr, y_scr):
    dot = functools.partial(jnp.dot, preferred_element_type=F32)
    parts = ((0, ROWS_P), (ROWS_P, ROWS - ROWS_P))
    seg_rows = lambda c: pl.ds(c, N_SEG, stride=SEG_CHUNKS)
    blk = lambda c: pl.ds(c * N_SEG, N_SEG)
    seq_rows = lambda j: pl.ds(j, N_SAMPLE_SEQ, stride=SEG_CHUNKS)
    lat = lambda c: pl.ds(c * N_SEG + N_PROMPT_SEG, N_SAMPLE_SEG)
    zero = jnp.zeros((N_SEG, LANES), F32)

    for pr in range(OCT_PAIRS):
        ln = slice(pr * LANES, (pr + 1) * LANES)
        g0, g1 = 2 * pr, 2 * pr + 1
        for r0, nr in parts:
            s = (dot(x_ref[g0, r0:r0 + nr, :], w_ref[pr, 0:CH, :])
                 + dot(x_ref[g1, r0:r0 + nr, :], w_ref[pr, CH:2 * CH, :]))
            for k in range(4):
                s_scr[k, r0:r0 + nr, :] = s[:, k * LANES:(k + 1) * LANES]

        ar, ai = apre_ref[1, 0:1, ln], apim_ref[1, 0:1, ln]
        hr, hi = zero, zero
        for c in range(SEG_CHUNKS):
            hin_scr[0, blk(c), :] = hr
            hin_scr[1, blk(c), :] = hi
            hr, hi = _cmul_add(ar, ai, hr, hi, s_scr[0, seg_rows(c), :], s_scr[1, seg_rows(c), :])
        fin_ref[pr, 0] = hr[0:N_PROMPT_SEG]
        fin_ref[pr, 1] = hi[0:N_PROMPT_SEG]
        f_scr[0] = hr[N_PROMPT_SEG:]
        f_scr[1] = hi[N_PROMPT_SEG:]
        br, bi = apre_ref[1, 1:2, ln], apim_ref[1, 1:2, ln]
        gr, gi = zero, zero
        for c in range(SEG_CHUNKS - 1, -1, -1):
            hin_scr[2, blk(c), :] = gr
            hin_scr[3, blk(c), :] = gi
            gr, gi = _cmul_add(br, bi, gr, gi, s_scr[2, seg_rows(c), :], s_scr[3, seg_rows(c), :])
        fin_ref[pr, 2] = gr[0:N_PROMPT_SEG]
        fin_ref[pr, 3] = gi[0:N_PROMPT_SEG]
        f_scr[2] = gr[N_PROMPT_SEG:]
        f_scr[3] = gi[N_PROMPT_SEG:]

        a2r, a2i = apre_ref[SEG_CHUNKS, 0:1, ln], apim_ref[SEG_CHUNKS, 0:1, ln]
        hr, hi = h0_ref[0, :, ln], h0_ref[1, :, ln]
        for j in range(SEG_CHUNKS):
            hs_scr[0, seq_rows(j), :] = hr
            hs_scr[1, seq_rows(j), :] = hi
            hr, hi = _cmul_add(a2r, a2i, hr, hi, f_scr[0, seq_rows(j), :], f_scr[1, seq_rows(j), :])
        b2r, b2i = apre_ref[SEG_CHUNKS, 1:2, ln], apim_ref[SEG_CHUNKS, 1:2, ln]
        gr, gi = h0_ref[2, :, ln], h0_ref[3, :, ln]
        for j in range(SEG_CHUNKS - 1, -1, -1):
            hs_scr[2, seq_rows(j), :] = gr
            hs_scr[3, seq_rows(j), :] = gi
            gr, gi = _cmul_add(b2r, b2i, gr, gi, f_scr[2, seq_rows(j), :], f_scr[3, seq_rows(j), :])
        for c in range(SEG_CHUNKS):
            p_r, p_i = apre_ref[c, 0:1, ln], apim_ref[c, 0:1, ln]
            hr, hi = _cmul_add(p_r, p_i, hs_scr[0], hs_scr[1], hin_scr[0, lat(c), :], hin_scr[1, lat(c), :])
            hin_scr[0, lat(c), :] = hr
            hin_scr[1, lat(c), :] = hi
            cb = SEG_CHUNKS - 1 - c
            p_r, p_i = apre_ref[cb, 1:2, ln], apim_ref[cb, 1:2, ln]
            gr, gi = _cmul_add(p_r, p_i, hs_scr[2], hs_scr[3], hin_scr[2, lat(c), :], hin_scr[3, lat(c), :])
            hin_scr[2, lat(c), :] = gr
            hin_scr[3, lat(c), :] = gi

        for k in range(4):
            for sg in range(N_SEG):
                hinp_scr[sg * SEG_CHUNKS:(sg + 1) * SEG_CHUNKS, k * LANES:(k + 1) * LANES] = (
                    hin_scr[k, pl.ds(sg, SEG_CHUNKS, stride=N_SEG), :].astype(BF16))
        for gi_, g in ((0, g0), (1, g1)):
            for r0, nr in parts:
                y_scr[g, r0:r0 + nr, :] = (dot(x_ref[g, r0:r0 + nr, :], m_ref[g])
                                           + dot(hinp_scr[r0:r0 + nr, :], e_ref[pr, :, gi_ * CH:(gi_ + 1) * CH]))

    rb_rows = 2 * SEG_CHUNKS
    for rb in range(ROWS // rb_rows):
        for hf in range(CHUNK // OCT):
            v = [y_scr[i, rb * rb_rows:(rb + 1) * rb_rows, hf * LANES:(hf + 1) * LANES] for i in range(OCT)]
            for s, acc in enumerate(_slot_transpose(v)):
                t = OCT * hf + s
                for sg in range(rb_rows // SEG_CHUNKS):
                    seg = rb * (rb_rows // SEG_CHUNKS) + sg
                    r0 = seg * SEG + t * SEG_CHUNKS
                    y_ref[0, r0:r0 + SEG_CHUNKS, :] = acc[sg * SEG_CHUNKS:(sg + 1) * SEG_CHUNKS]


def _s5_conv(xg, m, w, e, apre, apim, h0):
    n_oct = S5_GROUPS // OCT
    assert xg.shape[1] == ROWS
    nt = ROWS * CHUNK
    return pl.pallas_call(
        _s5_kernel,
        grid=(n_oct,),
        in_specs=[pl.BlockSpec((OCT, ROWS, CH), lambda o: (o, 0, 0)),
                  pl.BlockSpec((OCT, CH, CH), lambda o: (o, 0, 0)),
                  pl.BlockSpec((OCT_PAIRS, 2 * CH, 4 * LANES), lambda o: (o, 0, 0)),
                  pl.BlockSpec((OCT_PAIRS, 4 * LANES, 2 * CH), lambda o: (o, 0, 0)),
                  pl.BlockSpec((SEG_CHUNKS + 1, 2, OCT_PAIRS * LANES), lambda o: (0, 0, o)),
                  pl.BlockSpec((SEG_CHUNKS + 1, 2, OCT_PAIRS * LANES), lambda o: (0, 0, o)),
                  pl.BlockSpec((4, N_SAMPLE_SEQ, OCT_PAIRS * LANES), lambda o: (0, 0, o))],
        out_specs=[pl.BlockSpec((1, nt, LANES), lambda o: (o, 0, 0)),
                   pl.BlockSpec((OCT_PAIRS, 4, N_PROMPT_SEG, LANES), lambda o: (o, 0, 0, 0))],
        out_shape=[jax.ShapeDtypeStruct((n_oct, nt, LANES), F32),
                   jax.ShapeDtypeStruct((PAIRS, 4, N_PROMPT_SEG, LANES), F32)],
        scratch_shapes=[pltpu.VMEM((4, ROWS, LANES), F32),
                        pltpu.VMEM((4, ROWS, LANES), F32),
                        pltpu.VMEM((ROWS, 4 * LANES), BF16),
                        pltpu.VMEM((4, N_SAMPLE_SEG, LANES), F32),
                        pltpu.VMEM((4, N_SAMPLE_SEG, LANES), F32),
                        pltpu.VMEM((OCT, ROWS, CH), F32)],
        compiler_params=_cparams(("parallel",)),
        name="s5_conv",
    )(xg, m, w, e, apre, apim, h0)


def _tile_order_positions(l):
    r = np.arange(l)
    tile, wi = r // SEG, r % SEG
    return tile * SEG + (wi % SEG_CHUNKS) * CHUNK + wi // SEG_CHUNKS


def _tile_dft_tables(scale, rows_in_tile_order=True):
    pos = _tile_order_positions(SEG)
    freq = pos if rows_in_tile_order else np.arange(SEG)
    ang = 2.0 * np.pi * ((freq[:, None] * pos[None, :]) % SEG) / SEG
    return jnp.asarray(np.cos(ang) * scale, F32), jnp.asarray(np.sin(ang) * scale, F32)


def _pos_dft_tile_kernel(cm_ref, sm_ref, fc_ref, fs_ref, o_ref):
    dot = functools.partial(jnp.dot, preferred_element_type=F32)
    cm, sm = cm_ref[...].astype(BF16), sm_ref[...].astype(BF16)
    for b in range(fc_ref.shape[0]):
        o_ref[b] = (dot(cm, fc_ref[b]) - dot(sm, fs_ref[b])).astype(BF16)


def _pos_dft_tile(fc, fs, n):
    _, l, wdt = fc.shape
    cm, sm = _tile_dft_tables(1.0 / math.sqrt(l))
    nb = 4
    seq = pl.BlockSpec((nb, l, wdt), lambda b: (b, 0, 0))
    return pl.pallas_call(
        _pos_dft_tile_kernel,
        grid=(n // nb,),
        in_specs=[_const_spec((l, l)), _const_spec((l, l)), seq, seq],
        out_specs=seq,
        out_shape=jax.ShapeDtypeStruct((n, l, wdt), BF16),
        compiler_params=_cparams(("parallel",)),
        name="pos_dft_tile",
    )(cm, sm, fc, fs)


N_TILES = 16


def _fft16(xr, xi):
    n = len(xr)
    rev = [int(format(i, "04b")[::-1], 2) for i in range(n)]
    ar = [xr[r] for r in rev]
    ai = [xi[r] for r in rev]
    size = 2
    while size <= n:
        half = size // 2
        for start in range(0, n, size):
            for k in range(half):
                wr = math.cos(2.0 * math.pi * k / size)
                wi = -math.sin(2.0 * math.pi * k / size)
                i0, i1 = start + k, start + k + half
                if k == 0:
                    tr, ti = ar[i1], ai[i1]
                elif 4 * k == size:
                    tr, ti = ai[i1], -ar[i1]
                else:
                    tr = ar[i1] * wr - ai[i1] * wi
                    ti = ar[i1] * wi + ai[i1] * wr
                ar[i1], ai[i1] = ar[i0] - tr, ai[i0] - ti
                ar[i0], ai[i0] = ar[i0] + tr, ai[i0] + ti
        size *= 2
    return ar, ai


def _pos_dft_long_kernel(cb_ref, sb_ref, ca_ref, sa_ref, fc_ref, fs_ref, o_ref, a_scr):
    wdt = fc_ref.shape[2]

    def rows_fft(r, carry):
        rows = pl.ds(pl.multiple_of(r * CHUNK, CHUNK), CHUNK)
        for b in range(wdt // LANES):
            lns = slice(b * LANES, (b + 1) * LANES)
            xr = [fc_ref[j, rows, lns].astype(F32) for j in range(N_TILES)]
            xi = [-fs_ref[j, rows, lns].astype(F32) for j in range(N_TILES)]
            ar, ai = _fft16(xr, xi)
            for k in range(N_TILES):
                a_scr[k, 0, rows, lns] = ar[k].astype(BF16)
                a_scr[k, 1, rows, lns] = ai[k].astype(BF16)
        return carry

    lax.fori_loop(0, SEG // CHUNK, rows_fft, 0)

    dot = functools.partial(jnp.dot, preferred_element_type=F32)
    cb, sb = cb_ref[...], sb_ref[...]
    for k1 in range(N_TILES):
        ca, sa = ca_ref[k1:k1 + 1, :], sa_ref[k1:k1 + 1, :]
        dc = (cb * ca - sb * sa).astype(BF16)
        ds = (sb * ca + cb * sa).astype(BF16)
        out = dot(dc, a_scr[k1, 0]) + dot(ds, a_scr[k1, 1])
        o_ref[:, k1 * SEG_CHUNKS:(k1 + 1) * SEG_CHUNKS, :] = out.astype(BF16).reshape(N_TILES, SEG_CHUNKS, wdt)


def _pos_dft_long(fc, fs, first, n):
    wdt = fc.shape[-1]
    l = N_TILES * SEG
    wb = 2 * LANES
    cb, sb = _tile_dft_tables(1.0 / math.sqrt(l), rows_in_tile_order=False)
    pos = _tile_order_positions(SEG)
    ang = 2.0 * np.pi * (np.arange(N_TILES)[:, None] * pos[None, :]) / l
    ca, sa = jnp.asarray(np.cos(ang), F32), jnp.asarray(np.sin(ang), F32)
    const2 = lambda shape: pl.BlockSpec(shape, lambda b, k: (0, 0), pipeline_mode=pl.Buffered(1))
    seq_in = pl.BlockSpec((None, N_TILES, SEG, wb), lambda b, k: (first + b, 0, 0, k))
    seq_out = pl.BlockSpec((None, N_TILES, SEG, wb), lambda b, k: (b, 0, 0, k))
    out = pl.pallas_call(
        _pos_dft_long_kernel,
        grid=(n, wdt // wb),
        in_specs=[const2((SEG, SEG)), const2((SEG, SEG)), const2((N_TILES, SEG)), const2((N_TILES, SEG)),
                  seq_in, seq_in],
        out_specs=seq_out,
        out_shape=jax.ShapeDtypeStruct((n, N_TILES, SEG, wdt), BF16),
        scratch_shapes=[pltpu.VMEM((N_TILES, 2, SEG, wb), BF16)],
        compiler_params=_cparams(("parallel", "parallel")),
        name="pos_dft_long",
    )(cb, sb, ca, sa, fc, fs)
    return out.reshape(n * l, wdt)


def _mix_ffn_kernel(n_prompt_tiles, xp_ref, xs_ref, mod_ref, ys_ref, yfp_ref, yfs_ref, gs_ref, gf_ref,
                    wglu_ref, bglu_ref, wps_ref, wpf_ref, wout_ref, n2_ref,
                    wg_ref, wu_ref, wd_ref, fn_ref, op_ref, os_ref, m_scr, mb_scr):
    tm = xp_ref.shape[0]
    dot = functools.partial(jnp.dot, preferred_element_type=F32)
    mod = mod_ref[0]
    is_prompt = pl.program_id(0) < n_prompt_tiles
    x = jnp.where(is_prompt, xp_ref[...], xs_ref[...])
    yf = jnp.where(is_prompt, yfp_ref[...], yfs_ref[...])
    y = jnp.concatenate([ys_ref[b] for b in range(S5_WIDTH // LANES)], axis=1)
    z = jax.nn.gelu(y)
    z = z * jax.nn.sigmoid(dot(z.astype(BF16), wglu_ref[...]) + bglu_ref[...])
    m = (gs_ref[...].astype(F32) * dot(z.astype(BF16), wps_ref[...])
         + gf_ref[...].astype(F32) * dot(yf, wpf_ref[...]))
    for k in range(D_MODEL // LANES):
        m_scr[k] = m[:, k * LANES:(k + 1) * LANES]
    for j in range(tm // SEG):
        for c in range(SEG_CHUNKS):
            r0 = j * SEG + c * CHUNK
            for k in range(D_MODEL // LANES):
                mb_scr[r0:r0 + CHUNK, k * LANES:(k + 1) * LANES] = (
                    m_scr[k, pl.ds(j * SEG + c, CHUNK, stride=SEG_CHUNKS), :].astype(BF16))
    x1 = x + mod[2:3] * dot(mb_scr[...], wout_ref[...])
    h2 = (_rms(x1, n2_ref[...]) * (1.0 + mod[4:5]) + mod[3:4]).astype(BF16)
    ff = None
    for n0, n1 in zip(FF_SPLITS[:-1], FF_SPLITS[1:]):
        gate = dot(h2, wg_ref[:, n0:n1])
        up = dot(h2, wu_ref[:, n0:n1])
        part = dot((gate * jax.nn.sigmoid(gate) * up).astype(BF16), wd_ref[n0:n1, :])
        ff = part if ff is None else ff + part
    x2 = x1 + mod[5:6] * ff
    res = _rms(x2, fn_ref[...])

    @pl.when(is_prompt)
    def _():
        op_ref[...] = res

    @pl.when(jnp.logical_not(is_prompt))
    def _():
        os_ref[...] = res


def _mix_ffn(xp, xs, seq_len, mod, ys, yf_p, yf_s, gs, gf, wts, tm):
    tp, ts = xp.shape[0], xs.shape[0]
    n_p = tp // tm
    tok = lambda w: pl.BlockSpec((tm, w), lambda i: (i, 0))
    return pl.pallas_call(
        functools.partial(_mix_ffn_kernel, n_p),
        grid=((tp + ts) // tm,),
        in_specs=_two_part_specs(tm, D_MODEL, n_p)
                 + [_mod_spec(n_p, seq_len // tm),
                    pl.BlockSpec((S5_WIDTH // LANES, tm, LANES), lambda i: (0, i, 0))]
                 + _two_part_specs(tm, FFT_WIDTH, n_p)
                 + [tok(D_MODEL), tok(D_MODEL)]
                 + [_const_spec(w.shape) for w in wts],
        out_specs=_two_part_specs(tm, D_MODEL, n_p),
        out_shape=[jax.ShapeDtypeStruct((tp, D_MODEL), F32), jax.ShapeDtypeStruct((ts, D_MODEL), F32)],
        scratch_shapes=[pltpu.VMEM((D_MODEL // LANES, tm, LANES), F32),
                        pltpu.VMEM((tm, D_MODEL), BF16)],
        compiler_params=_cparams(("arbitrary",)),
        name="mix_ffn",
    )(xp, xs, mod, ys, yf_p, yf_s, gs, gf, *wts)


def kernel(x_prompt, x_sample, state_s5, c, c_ctx, norm1_g, norm2_g, w_ada, b_ada, w_in,
           s5_lambda_re, s5_lambda_im, s5_log_step, s5_b_re, s5_b_im, s5_c_re, s5_c_im,
           s5_d, w_glu, b_glu, w_proj_s5, w_proj_fft, w_out, w_ffn_gate, w_ffn_up,
           w_ffn_down, final_norm_g):
    nb, sl, _ = x_prompt.shape
    db, dl, _ = x_sample.shape
    assert w_in.shape[0] == 1 and sl == SEG and nb == N_PROMPT_SEG
    assert db == N_SAMPLE_SEQ and dl == SEG * SEG_CHUNKS

    cvec = jnp.concatenate([c_ctx[None], c, jnp.zeros((8 - 1 - db, D_MODEL), F32)], axis=0)
    mod = _adaln(cvec, w_ada[0], b_ada[0]).reshape(8, N_MOD, D_MODEL)

    m, w, e, apre, apim = _s5_tables(s5_lambda_re[0], s5_lambda_im[0], s5_log_step[0], s5_b_re[0],
                                     s5_b_im[0], s5_c_re[0], s5_c_im[0], s5_d[0])

    tm = 512
    xp = x_prompt.reshape(nb * sl, D_MODEL)
    xs = x_sample.reshape(db * dl, D_MODEL)
    w_in_b = w_in[0].astype(BF16)
    xg, fc, fs, gs, gf = _inproj(xp, xs, dl, mod, norm1_g[0], w_in_b, tm)

    h0 = state_s5[:, 0].astype(F32).transpose(1, 2, 0, 3, 4).reshape(4, db, GP)
    ys, fin = _s5_conv(xg, m, w, e, apre, apim, h0)
    new_state = fin.reshape(PAIRS, 2, 2, nb, 2, S5_STATE).transpose(3, 1, 2, 0, 4, 5)
    new_state = new_state.reshape(nb, 1, 2, 2, S5_GROUPS, S5_STATE)

    yf_p = _pos_dft_tile(fc.reshape(-1, SEG, FFT_WIDTH), fs.reshape(-1, SEG, FFT_WIDTH), nb)
    n_prompt_units = nb // N_TILES
    yf_s = _pos_dft_long(fc.reshape(-1, N_TILES, SEG, FFT_WIDTH), fs.reshape(-1, N_TILES, SEG, FFT_WIDTH),
                         n_prompt_units, db)

    wts = (w_glu[0].astype(BF16), b_glu[0].reshape(1, S5_WIDTH), w_proj_s5[0].astype(BF16),
           w_proj_fft[0].astype(BF16), w_out[0].astype(BF16), norm2_g[0].reshape(1, D_MODEL),
           w_ffn_gate[0].astype(BF16), w_ffn_up[0].astype(BF16), w_ffn_down[0].astype(BF16),
           final_norm_g.reshape(1, D_MODEL))
    tm2 = 2 * SEG
    y_p, y_s = _mix_ffn(xp, xs, dl, mod, ys, yf_p.reshape(nb * sl, FFT_WIDTH), yf_s, gs, gf, wts, tm2)
    return (y_p.reshape(nb, sl, D_MODEL), y_s.reshape(db, dl, D_MODEL), new_state)
```

```python
import functools
import math

import numpy as np
import jax
import jax.numpy as jnp
from jax import lax
from jax.experimental import pallas as pl
from jax.experimental.pallas import tpu as pltpu

F32 = jnp.float32
BF16 = jnp.bfloat16

D_MODEL = 1024
S5_WIDTH = 512
S5_GROUPS = 32
S5_GROUP = 16
S5_STATE = 64
FFT_WIDTH = 512
FFT_GROUPS = 4
FFT_GROUP = 128
D_FF = 2816
N_MOD = 6
EPS = 1e-6

LANES = 128
CHUNK = 16
SEG_CHUNKS = 16
SEG = CHUNK * SEG_CHUNKS
GP = S5_GROUPS * S5_STATE
PAIRS = S5_GROUPS // 2
CH = CHUNK * S5_GROUP
OCT = LANES // S5_GROUP

MXU_DIM = 256
FF_SPLITS = (0, (D_FF // MXU_DIM + 1) // 2 * MXU_DIM, D_FF)
VMEM_LIMIT = 58 * 1024 * 1024


def _cparams(sem):
    return pltpu.CompilerParams(dimension_semantics=sem, vmem_limit_bytes=VMEM_LIMIT)


def _row_block_cast_specs(w, n_steps):
    rows = w.shape[0] // n_steps
    assert rows * n_steps == w.shape[0] and rows % 16 == 0, (w.shape, n_steps)
    spec = pl.BlockSpec((rows, w.shape[1]), lambda i: (jnp.minimum(i, n_steps - 1), 0))
    return spec, jax.ShapeDtypeStruct(w.shape, BF16)


def _cast_blocks(n_steps, in_refs, out_refs):
    @pl.when(pl.program_id(0) < n_steps)
    def _():
        for src, dst in zip(in_refs, out_refs):
            dst[...] = src[...].astype(BF16)


def _ada_kernel(n_cast_steps, c_ref, w_ref, b_ref, win_ref, o_ref, winb_ref):
    _cast_blocks(n_cast_steps, [win_ref], [winb_ref])
    dot = functools.partial(jnp.dot, preferred_element_type=F32)
    c = c_ref[...]
    s = c * jax.nn.sigmoid(c)
    s_hi = s.astype(BF16)
    s_lo = (s - s_hi.astype(F32)).astype(BF16)
    w = w_ref[...]
    w_hi = w.astype(BF16)
    w_lo = (w - w_hi.astype(F32)).astype(BF16)
    r = dot(jnp.concatenate([s_hi, s_lo], axis=0), w_hi)
    n = s.shape[0]
    o_ref[...] = r[0:n] + r[n:2 * n] + dot(s_hi, w_lo) + b_ref[...]


def _adaln(cvec8, w_ada, b_ada, w_in):
    n_out = w_ada.shape[1]
    tn = 768
    n_steps = n_out // tn
    cast_spec, cast_shape = _row_block_cast_specs(w_in, n_steps)
    return pl.pallas_call(
        functools.partial(_ada_kernel, n_steps),
        grid=(n_steps,),
        in_specs=[pl.BlockSpec((8, D_MODEL), lambda j: (0, 0)),
                  pl.BlockSpec((D_MODEL, tn), lambda j: (0, j)),
                  pl.BlockSpec((1, tn), lambda j: (0, j)),
                  cast_spec],
        out_specs=[pl.BlockSpec((8, tn), lambda j: (0, j)), cast_spec],
        out_shape=[jax.ShapeDtypeStruct((8, n_out), F32), cast_shape],
        compiler_params=_cparams(("arbitrary",)),
        name="adaln",
    )(cvec8, w_ada, b_ada.reshape(1, n_out), w_in)


def _prep_kernel(lre_ref, lim_ref, lst_ref, bre_ref, bim_ref, cre_ref, cim_ref, dt_ref,
                 apre_ref, apim_ref, m_ref, w_ref, e_ref, gb_scr, ca_scr, ts_scr):
    lre = lre_ref[...]
    lim = lim_ref[...]
    step = jnp.exp(lst_ref[...])
    mag = jnp.exp(lre * step)
    are = mag * jnp.cos(lim * step)
    aim = mag * jnp.sin(lim * step)
    nr = are - 1.0
    den = lre * lre + lim * lim
    fr = (nr * lre + aim * lim) / den
    fi = (aim * lre - nr * lim) / den

    pr = [jnp.ones_like(are)]
    pi = [jnp.zeros_like(are)]
    for _ in range(CHUNK):
        r, i = pr[-1], pi[-1]
        pr.append(r * are - i * aim)
        pi.append(r * aim + i * are)
    a16r, a16i = pr[CHUNK], pi[CHUNK]
    qr, qi = jnp.ones_like(are), jnp.zeros_like(are)
    for c in range(SEG_CHUNKS + 1):
        apre_ref[c] = qr
        apim_ref[c] = qi
        qr, qi = qr * a16r - qi * a16i, qr * a16i + qi * a16r

    for d in range(2):
        bbre = fr[d:d + 1] * bre_ref[d] - fi[d:d + 1] * bim_ref[d]
        bbim = fr[d:d + 1] * bim_ref[d] + fi[d:d + 1] * bre_ref[d]
        cre, cim = cre_ref[d], cim_ref[d]
        for s in range(CHUNK):
            k = CHUNK - 1 - s if d == 0 else s
            r, i = pr[k][d:d + 1], pi[k][d:d + 1]
            gb_scr[0, d, s] = r * bbre - i * bbim
            gb_scr[1, d, s] = r * bbim + i * bbre
            f = s + 1 if d == 0 else CHUNK - s
            r, i = pr[f][d:d + 1], pi[f][d:d + 1]
            ca_scr[0, d, s] = r * cre - i * cim
            ca_scr[1, d, s] = -(r * cim + i * cre)

    dot_nt = lambda a, b: lax.dot_general(a, b, (((1,), (1,)), ((), ())), preferred_element_type=F32)
    lane_gi = lax.broadcasted_iota(jnp.int32, (CH, LANES), 1) // S5_STATE
    row_gi = lax.broadcasted_iota(jnp.int32, (LANES, CH), 0) // S5_STATE
    eye = (lax.broadcasted_iota(jnp.int32, (LANES, LANES), 0)
           == lax.broadcasted_iota(jnp.int32, (LANES, LANES), 1)).astype(BF16)
    slot = lax.broadcasted_iota(jnp.int32, (CH, LANES), 1) // S5_GROUP
    n_lag_rows = (2 * CHUNK - 1) * S5_GROUP

    def pair(q, carry):
        lanes = pl.ds(pl.multiple_of(q * LANES, LANES), LANES)
        for d in range(2):
            for ri in range(2):
                col = slice((2 * d + ri) * LANES, (2 * d + ri + 1) * LANES)
                gb = gb_scr[ri, d, :, :, lanes].reshape(CH, LANES)
                ca = ca_scr[ri, d, :, :, lanes].reshape(CH, LANES).astype(BF16)
                ca_t = dot_nt(eye, ca)
                for gi in range(2):
                    w_ref[q, gi * CH:(gi + 1) * CH, col] = jnp.where(lane_gi == gi, gb, 0.0).astype(BF16)
                    e_ref[q, col, gi * CH:(gi + 1) * CH] = jnp.where(row_gi == gi, ca_t, 0.0).astype(BF16)
        for gi in range(2):
            g = 2 * q + gi
            lag = []
            for d in range(2):
                c_re = jnp.concatenate([cre_ref[d, :, lanes]] * OCT, axis=0)
                c_imn = jnp.concatenate([-cim_ref[d, :, lanes]] * OCT, axis=0)
                keep = lax.broadcasted_iota(jnp.int32, (LANES, LANES), 1) // S5_STATE == gi
                c_re = jnp.where(keep, c_re, 0.0).astype(BF16)
                c_imn = jnp.where(keep, c_imn, 0.0).astype(BF16)
                gre = gb_scr[0, d, :, :, lanes].reshape(CH, LANES).astype(BF16)
                gim = gb_scr[1, d, :, :, lanes].reshape(CH, LANES).astype(BF16)
                lag.append(dot_nt(gre, c_re) + dot_nt(gim, c_imn))
            zl = (CHUNK - 1) * S5_GROUP
            ts_scr[0:zl, :] = lag[0][0:zl]
            ts_scr[zl:zl + S5_GROUP, :] = lag[0][zl:] + lag[1][0:S5_GROUP] + dt_ref[g]
            ts_scr[zl + S5_GROUP:n_lag_rows, :] = lag[1][S5_GROUP:]
            for hf in range(CHUNK // OCT):
                acc = None
                for s in range(OCT):
                    t = OCT * hf + s
                    win = ts_scr[(CHUNK - 1 - t) * S5_GROUP:(CHUNK - 1 - t) * S5_GROUP + CH, :]
                    acc = win if acc is None else jnp.where(slot == s, win, acc)
                m_ref[g, :, hf * LANES:(hf + 1) * LANES] = acc.astype(BF16)
        return carry

    lax.fori_loop(0, PAIRS, pair, 0)


def _s5_tables(lam_re, lam_im, log_step, b_re, b_im, c_re, c_im, s5_d):
    lre = lam_re.reshape(2, GP)
    lim = lam_im.reshape(2, GP)
    lst = jnp.repeat(log_step, S5_STATE, axis=-1)
    bre = b_re.transpose(0, 3, 1, 2).reshape(2, S5_GROUP, GP)
    bim = b_im.transpose(0, 3, 1, 2).reshape(2, S5_GROUP, GP)
    cre = c_re.transpose(0, 2, 1, 3).reshape(2, S5_GROUP, GP)
    cim = c_im.transpose(0, 2, 1, 3).reshape(2, S5_GROUP, GP)
    eye = jnp.asarray(np.eye(S5_GROUP, dtype=np.float32))
    dt = jnp.tile(eye[None] * s5_d.reshape(S5_GROUPS, 1, S5_GROUP), (1, 1, OCT))

    vm = pl.BlockSpec(memory_space=pltpu.VMEM)
    tab = (2, 2, CHUNK, S5_GROUP, GP)
    apre, apim, m, w, e = pl.pallas_call(
        _prep_kernel,
        in_specs=[vm] * 8,
        out_specs=[vm] * 5,
        out_shape=[jax.ShapeDtypeStruct((SEG_CHUNKS + 1, 2, GP), F32),
                   jax.ShapeDtypeStruct((SEG_CHUNKS + 1, 2, GP), F32),
                   jax.ShapeDtypeStruct((S5_GROUPS, CH, CH), BF16),
                   jax.ShapeDtypeStruct((PAIRS, 2 * CH, 4 * LANES), BF16),
                   jax.ShapeDtypeStruct((PAIRS, 4 * LANES, 2 * CH), BF16)],
        scratch_shapes=[pltpu.VMEM(tab, F32),
                        pltpu.VMEM(tab, F32),
                        pltpu.VMEM(((2 * CHUNK - 1) * S5_GROUP, LANES), F32)],
        compiler_params=pltpu.CompilerParams(vmem_limit_bytes=VMEM_LIMIT),
        name="s5_tables",
    )(lre, lim, lst, bre, bim, cre, cim, dt)
    return m, w, e, apre, apim


def _rms(x, g):
    return x * lax.rsqrt(jnp.mean(x * x, axis=-1, keepdims=True) + EPS) * g


def _slot_transpose(v):
    v = list(v)
    slot = lax.broadcasted_iota(jnp.int32, v[0].shape, 1) // S5_GROUP
    for k in (4, 2, 1):
        low = (slot & k) == 0
        for i in range(OCT):
            if i & k:
                continue
            a, b = v[i], v[i + k]
            v[i] = jnp.where(low, a, pltpu.roll(b, k * S5_GROUP, axis=1))
            v[i + k] = jnp.where(low, pltpu.roll(a, LANES - k * S5_GROUP, axis=1), b)
    return v


def _inproj_kernel(n_prompt_tiles, n_cast_steps, n_cast, xp_ref, xs_ref, mod_ref, g_ref, win_ref, bdc_ref, bds_ref,
                   *rest):
    cast_in, rest = rest[:n_cast], rest[n_cast:]
    xg_ref, fc_ref, fs_ref, gs_ref, gf_ref = rest[:5]
    cast_out, (h_scr, hb_scr, zs_scr) = rest[5:5 + n_cast], rest[5 + n_cast:]
    _cast_blocks(n_cast_steps, cast_in, cast_out)
    tm = xp_ref.shape[0]
    mod = mod_ref[0]
    x = jnp.where(pl.program_id(0) < n_prompt_tiles, xp_ref[...], xs_ref[...])
    h = _rms(x, g_ref[...]) * (1.0 + mod[1:2]) + mod[0:1]
    for k in range(D_MODEL // LANES):
        h_scr[k] = h[:, k * LANES:(k + 1) * LANES]
    for j in range(tm // SEG):
        for s in range(CHUNK):
            r0 = j * SEG + s * SEG_CHUNKS
            for k in range(D_MODEL // LANES):
                hb_scr[r0:r0 + SEG_CHUNKS, k * LANES:(k + 1) * LANES] = (
                    h_scr[k, pl.ds(j * SEG + s, SEG_CHUNKS, stride=CHUNK), :].astype(BF16))
    hb = hb_scr[...]
    dot = functools.partial(jnp.dot, preferred_element_type=F32)
    zs_scr[...] = dot(hb, win_ref[:, 0:S5_WIDTH])
    for j in range(tm // SEG):
        for b in range(S5_WIDTH // LANES):
            for hf in range(CHUNK // OCT):
                z = [zs_scr[j * SEG + s * SEG_CHUNKS:j * SEG + (s + 1) * SEG_CHUNKS, b * LANES:(b + 1) * LANES]
                     for s in range(OCT * hf, OCT * (hf + 1))]
                for i, xi in enumerate(_slot_transpose(z)):
                    xg_ref[OCT * b + i, j * SEG_CHUNKS:(j + 1) * SEG_CHUNKS,
                           hf * LANES:(hf + 1) * LANES] = xi.astype(BF16)
    uf = dot(hb, win_ref[:, S5_WIDTH:2 * S5_WIDTH]).astype(BF16)
    fc_ref[...] = dot(uf, bdc_ref[...]).astype(BF16)
    fs_ref[...] = dot(uf, bds_ref[...]).astype(BF16)
    o = 2 * S5_WIDTH
    gs_ref[...] = jax.nn.sigmoid(dot(hb, win_ref[:, o:o + D_MODEL])).astype(BF16)
    gf_ref[...] = jax.nn.sigmoid(dot(hb, win_ref[:, o + D_MODEL:o + 2 * D_MODEL])).astype(BF16)


def _channel_dft_mats():
    j = np.arange(FFT_GROUP)
    ang = 2.0 * np.pi * ((j[:, None] * j[None, :]) % FFT_GROUP) / FFT_GROUP
    blk_c = np.cos(ang) / math.sqrt(FFT_GROUP)
    blk_s = np.sin(ang) / math.sqrt(FFT_GROUP)
    bdc = np.kron(np.eye(FFT_GROUPS), blk_c)
    bds = np.kron(np.eye(FFT_GROUPS), blk_s)
    return jnp.asarray(bdc, F32).astype(BF16), jnp.asarray(bds, F32).astype(BF16)


def _const_spec(shape):
    nd = len(shape)
    return pl.BlockSpec(shape, lambda i: (0,) * nd, pipeline_mode=pl.Buffered(1))


def _two_part_specs(tm, width, n_prompt_tiles):
    return [pl.BlockSpec((tm, width), lambda i: (jnp.minimum(i, n_prompt_tiles - 1), 0)),
            pl.BlockSpec((tm, width), lambda i: (jnp.maximum(i - n_prompt_tiles, 0), 0))]


def _mod_spec(n_prompt_tiles, tiles_per_seq):
    row = lambda i: jnp.where(i < n_prompt_tiles, 0, 1 + (i - n_prompt_tiles) // tiles_per_seq)
    return pl.BlockSpec((1, N_MOD, D_MODEL), lambda i: (row(i), 0, 0))


N_CAST_STEPS = 16


def _inproj(xp, xs, seq_len, mod, norm_g, w_in_b, later_weights, tm):
    t = xp.shape[0] + xs.shape[0]
    n_p = xp.shape[0] // tm
    assert t // tm >= N_CAST_STEPS
    bdc, bds = _channel_dft_mats()
    casts = [_row_block_cast_specs(w, N_CAST_STEPS) for w in later_weights]
    tok = lambda w: pl.BlockSpec((tm, w), lambda i: (i, 0))
    out = lambda w: jax.ShapeDtypeStruct((t, w), BF16)
    return pl.pallas_call(
        functools.partial(_inproj_kernel, n_p, N_CAST_STEPS, len(casts)),
        grid=(t // tm,),
        in_specs=_two_part_specs(tm, D_MODEL, n_p) + [
                  _mod_spec(n_p, seq_len // tm),
                  _const_spec((1, D_MODEL)),
                  _const_spec(w_in_b.shape),
                  _const_spec(bdc.shape),
                  _const_spec(bds.shape)] + [c[0] for c in casts],
        out_specs=[pl.BlockSpec((S5_GROUPS, tm // CHUNK, CH), lambda i: (0, i, 0)),
                   tok(FFT_WIDTH), tok(FFT_WIDTH), tok(D_MODEL), tok(D_MODEL)] + [c[0] for c in casts],
        out_shape=[jax.ShapeDtypeStruct((S5_GROUPS, t // CHUNK, CH), BF16),
                   out(FFT_WIDTH), out(FFT_WIDTH), out(D_MODEL), out(D_MODEL)] + [c[1] for c in casts],
        scratch_shapes=[pltpu.VMEM((D_MODEL // LANES, tm, LANES), F32),
                        pltpu.VMEM((tm, D_MODEL), BF16),
                        pltpu.VMEM((tm, S5_WIDTH), F32)],
        compiler_params=_cparams(("arbitrary",)),
        name="inproj",
    )(xp, xs, mod, norm_g.reshape(1, D_MODEL), w_in_b, bdc, bds, *later_weights)


N_PROMPT_SEG = 16
N_SAMPLE_SEG = 32
N_SAMPLE_SEQ = 2
N_SEG = N_PROMPT_SEG + N_SAMPLE_SEG
ROWS = SEG_CHUNKS * N_SEG
ROWS_P = SEG_CHUNKS * N_PROMPT_SEG
OCT_PAIRS = OCT // 2


def _cmul_add(ar, ai, hr, hi, sr, si):
    return ar * hr - ai * hi + sr, ar * hi + ai * hr + si


def _s5_kernel(x_ref, m_ref, w_ref, e_ref, apre_ref, apim_ref, h0_ref,
               y_ref, fin_ref, s_scr, hin_scr, hinp_scr, f_scr, hs_scr, y_scr):
    dot = functools.partial(jnp.dot, preferred_element_type=F32)
    parts = ((0, ROWS_P), (ROWS_P, ROWS - ROWS_P))
    seg_rows = lambda c: pl.ds(c, N_SEG, stride=SEG_CHUNKS)
    blk = lambda c: pl.ds(c * N_SEG, N_SEG)
    seq_rows = lambda j: pl.ds(j, N_SAMPLE_SEQ, stride=SEG_CHUNKS)
    lat = lambda c: pl.ds(c * N_SEG + N_PROMPT_SEG, N_SAMPLE_SEG)
    zero = jnp.zeros((N_SEG, LANES), F32)

    for pr in range(OCT_PAIRS):
        ln = slice(pr * LANES, (pr + 1) * LANES)
        g0, g1 = 2 * pr, 2 * pr + 1
        for r0, nr in parts:
            s = (dot(x_ref[g0, r0:r0 + nr, :], w_ref[pr, 0:CH, :])
                 + dot(x_ref[g1, r0:r0 + nr, :], w_ref[pr, CH:2 * CH, :]))
            for k in range(4):
                s_scr[k, r0:r0 + nr, :] = s[:, k * LANES:(k + 1) * LANES]

        ar, ai = apre_ref[1, 0:1, ln], apim_ref[1, 0:1, ln]
        hr, hi = zero, zero
        for c in range(SEG_CHUNKS):
            hin_scr[0, blk(c), :] = hr
            hin_scr[1, blk(c), :] = hi
            hr, hi = _cmul_add(ar, ai, hr, hi, s_scr[0, seg_rows(c), :], s_scr[1, seg_rows(c), :])
        fin_ref[pr, 0] = hr[0:N_PROMPT_SEG]
        fin_ref[pr, 1] = hi[0:N_PROMPT_SEG]
        f_scr[0] = hr[N_PROMPT_SEG:]
        f_scr[1] = hi[N_PROMPT_SEG:]
        br, bi = apre_ref[1, 1:2, ln], apim_ref[1, 1:2, ln]
        gr, gi = zero, zero
        for c in range(SEG_CHUNKS - 1, -1, -1):
            hin_scr[2, blk(c), :] = gr
            hin_scr[3, blk(c), :] = gi
            gr, gi = _cmul_add(br, bi, gr, gi, s_scr[2, seg_rows(c), :], s_scr[3, seg_rows(c), :])
        fin_ref[pr, 2] = gr[0:N_PROMPT_SEG]
        fin_ref[pr, 3] = gi[0:N_PROMPT_SEG]
        f_scr[2] = gr[N_PROMPT_SEG:]
        f_scr[3] = gi[N_PROMPT_SEG:]

        a2r, a2i = apre_ref[SEG_CHUNKS, 0:1, ln], apim_ref[SEG_CHUNKS, 0:1, ln]
        hr, hi = h0_ref[0, :, ln], h0_ref[1, :, ln]
        for j in range(SEG_CHUNKS):
            hs_scr[0, seq_rows(j), :] = hr
            hs_scr[1, seq_rows(j), :] = hi
            hr, hi = _cmul_add(a2r, a2i, hr, hi, f_scr[0, seq_rows(j), :], f_scr[1, seq_rows(j), :])
        b2r, b2i = apre_ref[SEG_CHUNKS, 1:2, ln], apim_ref[SEG_CHUNKS, 1:2, ln]
        gr, gi = h0_ref[2, :, ln], h0_ref[3, :, ln]
        for j in range(SEG_CHUNKS - 1, -1, -1):
            hs_scr[2, seq_rows(j), :] = gr
            hs_scr[3, seq_rows(j), :] = gi
            gr, gi = _cmul_add(b2r, b2i, gr, gi, f_scr[2, seq_rows(j), :], f_scr[3, seq_rows(j), :])
        for c in range(SEG_CHUNKS):
            p_r, p_i = apre_ref[c, 0:1, ln], apim_ref[c, 0:1, ln]
            hr, hi = _cmul_add(p_r, p_i, hs_scr[0], hs_scr[1], hin_scr[0, lat(c), :], hin_scr[1, lat(c), :])
            hin_scr[0, lat(c), :] = hr
            hin_scr[1, lat(c), :] = hi
            cb = SEG_CHUNKS - 1 - c
            p_r, p_i = apre_ref[cb, 1:2, ln], apim_ref[cb, 1:2, ln]
            gr, gi = _cmul_add(p_r, p_i, hs_scr[2], hs_scr[3], hin_scr[2, lat(c), :], hin_scr[3, lat(c), :])
            hin_scr[2, lat(c), :] = gr
            hin_scr[3, lat(c), :] = gi

        for k in range(4):
            for sg in range(N_SEG):
                hinp_scr[sg * SEG_CHUNKS:(sg + 1) * SEG_CHUNKS, k * LANES:(k + 1) * LANES] = (
                    hin_scr[k, pl.ds(sg, SEG_CHUNKS, stride=N_SEG), :].astype(BF16))
        for gi_, g in ((0, g0), (1, g1)):
            for r0, nr in parts:
                y_scr[g, r0:r0 + nr, :] = (dot(x_ref[g, r0:r0 + nr, :], m_ref[g])
                                           + dot(hinp_scr[r0:r0 + nr, :], e_ref[pr, :, gi_ * CH:(gi_ + 1) * CH]))

    rb_rows = 2 * SEG_CHUNKS
    for rb in range(ROWS // rb_rows):
        for hf in range(CHUNK // OCT):
            v = [y_scr[i, rb * rb_rows:(rb + 1) * rb_rows, hf * LANES:(hf + 1) * LANES] for i in range(OCT)]
            for s, acc in enumerate(_slot_transpose(v)):
                t = OCT * hf + s
                for sg in range(rb_rows // SEG_CHUNKS):
                    seg = rb * (rb_rows // SEG_CHUNKS) + sg
                    r0 = seg * SEG + t * SEG_CHUNKS
                    y_ref[0, r0:r0 + SEG_CHUNKS, :] = acc[sg * SEG_CHUNKS:(sg + 1) * SEG_CHUNKS]


def _s5_conv(xg, m, w, e, apre, apim, h0):
    n_oct = S5_GROUPS // OCT
    assert xg.shape[1] == ROWS
    nt = ROWS * CHUNK
    return pl.pallas_call(
        _s5_kernel,
        grid=(n_oct,),
        in_specs=[pl.BlockSpec((OCT, ROWS, CH), lambda o: (o, 0, 0)),
                  pl.BlockSpec((OCT, CH, CH), lambda o: (o, 0, 0)),
                  pl.BlockSpec((OCT_PAIRS, 2 * CH, 4 * LANES), lambda o: (o, 0, 0)),
                  pl.BlockSpec((OCT_PAIRS, 4 * LANES, 2 * CH), lambda o: (o, 0, 0)),
                  pl.BlockSpec((SEG_CHUNKS + 1, 2, OCT_PAIRS * LANES), lambda o: (0, 0, o)),
                  pl.BlockSpec((SEG_CHUNKS + 1, 2, OCT_PAIRS * LANES), lambda o: (0, 0, o)),
                  pl.BlockSpec((4, N_SAMPLE_SEQ, OCT_PAIRS * LANES), lambda o: (0, 0, o))],
        out_specs=[pl.BlockSpec((1, nt, LANES), lambda o: (o, 0, 0)),
                   pl.BlockSpec((OCT_PAIRS, 4, N_PROMPT_SEG, LANES), lambda o: (o, 0, 0, 0))],
        out_shape=[jax.ShapeDtypeStruct((n_oct, nt, LANES), F32),
                   jax.ShapeDtypeStruct((PAIRS, 4, N_PROMPT_SEG, LANES), F32)],
        scratch_shapes=[pltpu.VMEM((4, ROWS, LANES), F32),
                        pltpu.VMEM((4, ROWS, LANES), F32),
                        pltpu.VMEM((ROWS, 4 * LANES), BF16),
                        pltpu.VMEM((4, N_SAMPLE_SEG, LANES), F32),
                        pltpu.VMEM((4, N_SAMPLE_SEG, LANES), F32),
                        pltpu.VMEM((OCT, ROWS, CH), F32)],
        compiler_params=_cparams(("parallel",)),
        name="s5_conv",
    )(xg, m, w, e, apre, apim, h0)


def _tile_order_positions(l):
    r = np.arange(l)
    tile, wi = r // SEG, r % SEG
    return tile * SEG + (wi % SEG_CHUNKS) * CHUNK + wi // SEG_CHUNKS


def _tile_dft_tables(scale, rows_in_tile_order=True):
    pos = _tile_order_positions(SEG)
    freq = pos if rows_in_tile_order else np.arange(SEG)
    ang = 2.0 * np.pi * ((freq[:, None] * pos[None, :]) % SEG) / SEG
    return jnp.asarray(np.cos(ang) * scale, F32), jnp.asarray(np.sin(ang) * scale, F32)


def _pos_dft_tile_kernel(cm_ref, sm_ref, fc_ref, fs_ref, o_ref):
    dot = functools.partial(jnp.dot, preferred_element_type=F32)
    cm, sm = cm_ref[...].astype(BF16), sm_ref[...].astype(BF16)
    for b in range(fc_ref.shape[0]):
        o_ref[b] = (dot(cm, fc_ref[b]) - dot(sm, fs_ref[b])).astype(BF16)


def _pos_dft_tile(fc, fs, n):
    _, l, wdt = fc.shape
    cm, sm = _tile_dft_tables(1.0 / math.sqrt(l))
    nb = 4
    seq = pl.BlockSpec((nb, l, wdt), lambda b: (b, 0, 0))
    return pl.pallas_call(
        _pos_dft_tile_kernel,
        grid=(n // nb,),
        in_specs=[_const_spec((l, l)), _const_spec((l, l)), seq, seq],
        out_specs=seq,
        out_shape=jax.ShapeDtypeStruct((n, l, wdt), BF16),
        compiler_params=_cparams(("parallel",)),
        name="pos_dft_tile",
    )(cm, sm, fc, fs)


N_TILES = 16


def _fft16(xr, xi):
    n = len(xr)
    rev = [int(format(i, "04b")[::-1], 2) for i in range(n)]
    ar = [xr[r] for r in rev]
    ai = [xi[r] for r in rev]
    size = 2
    while size <= n:
        half = size // 2
        for start in range(0, n, size):
            for k in range(half):
                wr = math.cos(2.0 * math.pi * k / size)
                wi = -math.sin(2.0 * math.pi * k / size)
                i0, i1 = start + k, start + k + half
                if k == 0:
                    tr, ti = ar[i1], ai[i1]
                elif 4 * k == size:
                    tr, ti = ai[i1], -ar[i1]
                else:
                    tr = ar[i1] * wr - ai[i1] * wi
                    ti = ar[i1] * wi + ai[i1] * wr
                ar[i1], ai[i1] = ar[i0] - tr, ai[i0] - ti
                ar[i0], ai[i0] = ar[i0] + tr, ai[i0] + ti
        size *= 2
    return ar, ai


def _pos_dft_long_kernel(cb_ref, sb_ref, ca_ref, sa_ref, fc_ref, fs_ref, o_ref, a_scr):
    wdt = fc_ref.shape[2]

    def rows_fft(r, carry):
        rows = pl.ds(pl.multiple_of(r * CHUNK, CHUNK), CHUNK)
        for b in range(wdt // LANES):
            lns = slice(b * LANES, (b + 1) * LANES)
            xr = [fc_ref[j, rows, lns].astype(F32) for j in range(N_TILES)]
            xi = [-fs_ref[j, rows, lns].astype(F32) for j in range(N_TILES)]
            ar, ai = _fft16(xr, xi)
            for k in range(N_TILES):
                a_scr[k, 0, rows, lns] = ar[k].astype(BF16)
                a_scr[k, 1, rows, lns] = ai[k].astype(BF16)
        return carry

    lax.fori_loop(0, SEG // CHUNK, rows_fft, 0)

    dot = functools.partial(jnp.dot, preferred_element_type=F32)
    cb, sb = cb_ref[...], sb_ref[...]
    for k1 in range(N_TILES):
        ca, sa = ca_ref[k1:k1 + 1, :], sa_ref[k1:k1 + 1, :]
        dc = (cb * ca - sb * sa).astype(BF16)
        ds = (sb * ca + cb * sa).astype(BF16)
        out = dot(dc, a_scr[k1, 0]) + dot(ds, a_scr[k1, 1])
        o_ref[:, k1 * SEG_CHUNKS:(k1 + 1) * SEG_CHUNKS, :] = out.astype(BF16).reshape(N_TILES, SEG_CHUNKS, wdt)


def _pos_dft_long(fc, fs, first, n):
    wdt = fc.shape[-1]
    l = N_TILES * SEG
    wb = 2 * LANES
    cb, sb = _tile_dft_tables(1.0 / math.sqrt(l), rows_in_tile_order=False)
    pos = _tile_order_positions(SEG)
    ang = 2.0 * np.pi * (np.arange(N_TILES)[:, None] * pos[None, :]) / l
    ca, sa = jnp.asarray(np.cos(ang), F32), jnp.asarray(np.sin(ang), F32)
    const2 = lambda shape: pl.BlockSpec(shape, lambda b, k: (0, 0), pipeline_mode=pl.Buffered(1))
    seq_in = pl.BlockSpec((None, N_TILES, SEG, wb), lambda b, k: (first + b, 0, 0, k))
    seq_out = pl.BlockSpec((None, N_TILES, SEG, wb), lambda b, k: (b, 0, 0, k))
    out = pl.pallas_call(
        _pos_dft_long_kernel,
        grid=(n, wdt // wb),
        in_specs=[const2((SEG, SEG)), const2((SEG, SEG)), const2((N_TILES, SEG)), const2((N_TILES, SEG)),
                  seq_in, seq_in],
        out_specs=seq_out,
        out_shape=jax.ShapeDtypeStruct((n, N_TILES, SEG, wdt), BF16),
        scratch_shapes=[pltpu.VMEM((N_TILES, 2, SEG, wb), BF16)],
        compiler_params=_cparams(("parallel", "parallel")),
        name="pos_dft_long",
    )(cb, sb, ca, sa, fc, fs)
    return out.reshape(n * l, wdt)


def _mix_ffn_kernel(n_prompt_tiles, xp_ref, xs_ref, mod_ref, ys_ref, yfp_ref, yfs_ref, gs_ref, gf_ref,
                    wglu_ref, bglu_ref, wps_ref, wpf_ref, wout_ref, n2_ref,
                    wg_ref, wu_ref, wd_ref, fn_ref, op_ref, os_ref, m_scr, mb_scr):
    tm = xp_ref.shape[0]
    dot = functools.partial(jnp.dot, preferred_element_type=F32)
    mod = mod_ref[0]
    is_prompt = pl.program_id(0) < n_prompt_tiles
    x = jnp.where(is_prompt, xp_ref[...], xs_ref[...])
    yf = jnp.where(is_prompt, yfp_ref[...], yfs_ref[...])
    y = jnp.concatenate([ys_ref[b] for b in range(S5_WIDTH // LANES)], axis=1)
    z = jax.nn.gelu(y)
    z = z * jax.nn.sigmoid(dot(z.astype(BF16), wglu_ref[...]) + bglu_ref[...])
    m = (gs_ref[...].astype(F32) * dot(z.astype(BF16), wps_ref[...])
         + gf_ref[...].astype(F32) * dot(yf, wpf_ref[...]))
    for k in range(D_MODEL // LANES):
        m_scr[k] = m[:, k * LANES:(k + 1) * LANES]
    for j in range(tm // SEG):
        for c in range(SEG_CHUNKS):
            r0 = j * SEG + c * CHUNK
            for k in range(D_MODEL // LANES):
                mb_scr[r0:r0 + CHUNK, k * LANES:(k + 1) * LANES] = (
                    m_scr[k, pl.ds(j * SEG + c, CHUNK, stride=SEG_CHUNKS), :].astype(BF16))
    x1 = x + mod[2:3] * dot(mb_scr[...], wout_ref[...])
    h2 = (_rms(x1, n2_ref[...]) * (1.0 + mod[4:5]) + mod[3:4]).astype(BF16)
    ff = None
    for n0, n1 in zip(FF_SPLITS[:-1], FF_SPLITS[1:]):
        gate = dot(h2, wg_ref[:, n0:n1])
        up = dot(h2, wu_ref[:, n0:n1])
        part = dot((gate * jax.nn.sigmoid(gate) * up).astype(BF16), wd_ref[n0:n1, :])
        ff = part if ff is None else ff + part
    x2 = x1 + mod[5:6] * ff
    res = _rms(x2, fn_ref[...])

    @pl.when(is_prompt)
    def _():
        op_ref[...] = res

    @pl.when(jnp.logical_not(is_prompt))
    def _():
        os_ref[...] = res


def _mix_ffn(xp, xs, seq_len, mod, ys, yf_p, yf_s, gs, gf, wts, tm):
    tp, ts = xp.shape[0], xs.shape[0]
    n_p = tp // tm
    tok = lambda w: pl.BlockSpec((tm, w), lambda i: (i, 0))
    return pl.pallas_call(
        functools.partial(_mix_ffn_kernel, n_p),
        grid=((tp + ts) // tm,),
        in_specs=_two_part_specs(tm, D_MODEL, n_p)
                 + [_mod_spec(n_p, seq_len // tm),
                    pl.BlockSpec((S5_WIDTH // LANES, tm, LANES), lambda i: (0, i, 0))]
                 + _two_part_specs(tm, FFT_WIDTH, n_p)
                 + [tok(D_MODEL), tok(D_MODEL)]
                 + [_const_spec(w.shape) for w in wts],
        out_specs=_two_part_specs(tm, D_MODEL, n_p),
        out_shape=[jax.ShapeDtypeStruct((tp, D_MODEL), F32), jax.ShapeDtypeStruct((ts, D_MODEL), F32)],
        scratch_shapes=[pltpu.VMEM((D_MODEL // LANES, tm, LANES), F32),
                        pltpu.VMEM((tm, D_MODEL), BF16)],
        compiler_params=_cparams(("arbitrary",)),
        name="mix_ffn",
    )(xp, xs, mod, ys, yf_p, yf_s, gs, gf, *wts)


def kernel(x_prompt, x_sample, state_s5, c, c_ctx, norm1_g, norm2_g, w_ada, b_ada, w_in,
           s5_lambda_re, s5_lambda_im, s5_log_step, s5_b_re, s5_b_im, s5_c_re, s5_c_im,
           s5_d, w_glu, b_glu, w_proj_s5, w_proj_fft, w_out, w_ffn_gate, w_ffn_up,
           w_ffn_down, final_norm_g):
    nb, sl, _ = x_prompt.shape
    db, dl, _ = x_sample.shape
    assert w_in.shape[0] == 1 and sl == SEG and nb == N_PROMPT_SEG
    assert db == N_SAMPLE_SEQ and dl == SEG * SEG_CHUNKS

    cvec = jnp.concatenate([c_ctx[None], c, jnp.zeros((8 - 1 - db, D_MODEL), F32)], axis=0)
    mod, w_in_b = _adaln(cvec, w_ada[0], b_ada[0], w_in[0])
    mod = mod.reshape(8, N_MOD, D_MODEL)

    m, w, e, apre, apim = _s5_tables(s5_lambda_re[0], s5_lambda_im[0], s5_log_step[0], s5_b_re[0],
                                     s5_b_im[0], s5_c_re[0], s5_c_im[0], s5_d[0])

    tm = 512
    xp = x_prompt.reshape(nb * sl, D_MODEL)
    xs = x_sample.reshape(db * dl, D_MODEL)
    later = (w_glu[0], w_proj_s5[0], w_proj_fft[0], w_out[0], w_ffn_gate[0], w_ffn_up[0], w_ffn_down[0])
    xg, fc, fs, gs, gf, *later_b = _inproj(xp, xs, dl, mod, norm1_g[0], w_in_b, later, tm)
    wglu_b, wps_b, wpf_b, wout_b, wg_b, wu_b, wd_b = later_b

    h0 = state_s5[:, 0].astype(F32).transpose(1, 2, 0, 3, 4).reshape(4, db, GP)
    ys, fin = _s5_conv(xg, m, w, e, apre, apim, h0)
    new_state = fin.reshape(PAIRS, 2, 2, nb, 2, S5_STATE).transpose(3, 1, 2, 0, 4, 5)
    new_state = new_state.reshape(nb, 1, 2, 2, S5_GROUPS, S5_STATE)

    yf_p = _pos_dft_tile(fc.reshape(-1, SEG, FFT_WIDTH), fs.reshape(-1, SEG, FFT_WIDTH), nb)
    n_prompt_units = nb // N_TILES
    yf_s = _pos_dft_long(fc.reshape(-1, N_TILES, SEG, FFT_WIDTH), fs.reshape(-1, N_TILES, SEG, FFT_WIDTH),
                         n_prompt_units, db)

    wts = (wglu_b, b_glu[0].reshape(1, S5_WIDTH), wps_b, wpf_b, wout_b, norm2_g[0].reshape(1, D_MODEL),
           wg_b, wu_b, wd_b, final_norm_g.reshape(1, D_MODEL))
    tm2 = 2 * SEG
    y_p, y_s = _mix_ffn(xp, xs, dl, mod, ys, yf_p.reshape(nb * sl, FFT_WIDTH), yf_s, gs, gf, wts, tm2)
    return (y_p.reshape(nb, sl, D_MODEL), y_s.reshape(db, dl, D_MODEL), new_state)
```

```python
import functools
import math

import numpy as np
import jax
import jax.numpy as jnp
from jax import lax
from jax.experimental import pallas as pl
from jax.experimental.pallas import tpu as pltpu

F32 = jnp.float32
BF16 = jnp.bfloat16

D_MODEL = 1024
S5_WIDTH = 512
S5_GROUPS = 32
S5_GROUP = 16
S5_STATE = 64
FFT_WIDTH = 512
FFT_GROUPS = 4
FFT_GROUP = 128
D_FF = 2816
N_MOD = 6
EPS = 1e-6

LANES = 128
CHUNK = 16
SEG_CHUNKS = 16
SEG = CHUNK * SEG_CHUNKS
GP = S5_GROUPS * S5_STATE
PAIRS = S5_GROUPS // 2
CH = CHUNK * S5_GROUP
OCT = LANES // S5_GROUP
PITCH = 24

MXU_DIM = 256
FF_SPLITS = (0, (D_FF // MXU_DIM + 1) // 2 * MXU_DIM, D_FF)
VMEM_LIMIT = 58 * 1024 * 1024


def _cparams(sem):
    return pltpu.CompilerParams(dimension_semantics=sem, vmem_limit_bytes=VMEM_LIMIT)


def _row_block_cast_specs(w, n_steps):
    rows = w.shape[0] // n_steps
    assert rows * n_steps == w.shape[0] and rows % 16 == 0, (w.shape, n_steps)
    spec = pl.BlockSpec((rows, w.shape[1]), lambda i: (jnp.minimum(i, n_steps - 1), 0))
    return spec, jax.ShapeDtypeStruct(w.shape, BF16)


def _cast_blocks(n_steps, in_refs, out_refs):
    @pl.when(pl.program_id(0) < n_steps)
    def _():
        for src, dst in zip(in_refs, out_refs):
            dst[...] = src[...].astype(BF16)


def _ada_kernel(n_cast_steps, c_ref, w_ref, b_ref, win_ref, o_ref, winb_ref):
    _cast_blocks(n_cast_steps, [win_ref], [winb_ref])
    dot = functools.partial(jnp.dot, preferred_element_type=F32)
    c = c_ref[...]
    s = c * jax.nn.sigmoid(c)
    s_hi = s.astype(BF16)
    s_lo = (s - s_hi.astype(F32)).astype(BF16)
    w = w_ref[...]
    w_hi = w.astype(BF16)
    w_lo = (w - w_hi.astype(F32)).astype(BF16)
    r = dot(jnp.concatenate([s_hi, s_lo], axis=0), w_hi)
    n = s.shape[0]
    o_ref[...] = r[0:n] + r[n:2 * n] + dot(s_hi, w_lo) + b_ref[...]


def _adaln(cvec8, w_ada, b_ada, w_in):
    n_out = w_ada.shape[1]
    tn = 768
    n_steps = n_out // tn
    cast_spec, cast_shape = _row_block_cast_specs(w_in, n_steps)
    return pl.pallas_call(
        functools.partial(_ada_kernel, n_steps),
        grid=(n_steps,),
        in_specs=[pl.BlockSpec((8, D_MODEL), lambda j: (0, 0)),
                  pl.BlockSpec((D_MODEL, tn), lambda j: (0, j)),
                  pl.BlockSpec((1, tn), lambda j: (0, j)),
                  cast_spec],
        out_specs=[pl.BlockSpec((8, tn), lambda j: (0, j)), cast_spec],
        out_shape=[jax.ShapeDtypeStruct((8, n_out), F32), cast_shape],
        compiler_params=_cparams(("arbitrary",)),
        name="adaln",
    )(cvec8, w_ada, b_ada.reshape(1, n_out), w_in)


def _prep_kernel(lre_ref, lim_ref, lst_ref, bre_ref, bim_ref, cre_ref, cim_ref, dt_ref,
                 apre_ref, apim_ref, m_ref, w_ref, e_ref, gb_scr, ca_scr, ts_scr):
    lre = lre_ref[...]
    lim = lim_ref[...]
    step = jnp.exp(lst_ref[...])
    mag = jnp.exp(lre * step)
    are = mag * jnp.cos(lim * step)
    aim = mag * jnp.sin(lim * step)
    nr = are - 1.0
    den = lre * lre + lim * lim
    fr = (nr * lre + aim * lim) / den
    fi = (aim * lre - nr * lim) / den

    pr = [jnp.ones_like(are)]
    pi = [jnp.zeros_like(are)]
    for _ in range(CHUNK):
        r, i = pr[-1], pi[-1]
        pr.append(r * are - i * aim)
        pi.append(r * aim + i * are)
    a16r, a16i = pr[CHUNK], pi[CHUNK]
    qr, qi = jnp.ones_like(are), jnp.zeros_like(are)
    for c in range(SEG_CHUNKS + 1):
        apre_ref[c] = qr
        apim_ref[c] = qi
        qr, qi = qr * a16r - qi * a16i, qr * a16i + qi * a16r

    for d in range(2):
        bbre = fr[d:d + 1] * bre_ref[d] - fi[d:d + 1] * bim_ref[d]
        bbim = fr[d:d + 1] * bim_ref[d] + fi[d:d + 1] * bre_ref[d]
        cre, cim = cre_ref[d], cim_ref[d]
        for s in range(CHUNK):
            k = CHUNK - 1 - s if d == 0 else s
            r, i = pr[k][d:d + 1], pi[k][d:d + 1]
            gb_scr[0, d, s] = r * bbre - i * bbim
            gb_scr[1, d, s] = r * bbim + i * bbre
            f = s + 1 if d == 0 else CHUNK - s
            r, i = pr[f][d:d + 1], pi[f][d:d + 1]
            ca_scr[0, d, s] = r * cre - i * cim
            ca_scr[1, d, s] = -(r * cim + i * cre)

    dot_nt = lambda a, b: lax.dot_general(a, b, (((1,), (1,)), ((), ())), preferred_element_type=F32)
    lane_gi = lax.broadcasted_iota(jnp.int32, (CH, LANES), 1) // S5_STATE
    row_gi = lax.broadcasted_iota(jnp.int32, (LANES, CH), 0) // S5_STATE
    eye = (lax.broadcasted_iota(jnp.int32, (LANES, LANES), 0)
           == lax.broadcasted_iota(jnp.int32, (LANES, LANES), 1)).astype(BF16)
    slot = lax.broadcasted_iota(jnp.int32, (CH, LANES), 1) // S5_GROUP
    n_lag_rows = (2 * CHUNK - 1) * S5_GROUP

    def pair(q, carry):
        lanes = pl.ds(pl.multiple_of(q * LANES, LANES), LANES)
        for d in range(2):
            for ri in range(2):
                col = slice((2 * d + ri) * LANES, (2 * d + ri + 1) * LANES)
                gb = gb_scr[ri, d, :, :, lanes].reshape(CH, LANES)
                ca = ca_scr[ri, d, :, :, lanes].reshape(CH, LANES).astype(BF16)
                ca_t = dot_nt(eye, ca)
                for gi in range(2):
                    w_ref[q, gi * CH:(gi + 1) * CH, col] = jnp.where(lane_gi == gi, gb, 0.0).astype(BF16)
                    e_ref[q, col, gi * CH:(gi + 1) * CH] = jnp.where(row_gi == gi, ca_t, 0.0).astype(BF16)
        for gi in range(2):
            g = 2 * q + gi
            lag = []
            for d in range(2):
                c_re = jnp.concatenate([cre_ref[d, :, lanes]] * OCT, axis=0)
                c_imn = jnp.concatenate([-cim_ref[d, :, lanes]] * OCT, axis=0)
                keep = lax.broadcasted_iota(jnp.int32, (LANES, LANES), 1) // S5_STATE == gi
                c_re = jnp.where(keep, c_re, 0.0).astype(BF16)
                c_imn = jnp.where(keep, c_imn, 0.0).astype(BF16)
                gre = gb_scr[0, d, :, :, lanes].reshape(CH, LANES).astype(BF16)
                gim = gb_scr[1, d, :, :, lanes].reshape(CH, LANES).astype(BF16)
                lag.append(dot_nt(gre, c_re) + dot_nt(gim, c_imn))
            zl = (CHUNK - 1) * S5_GROUP
            ts_scr[0:zl, :] = lag[0][0:zl]
            ts_scr[zl:zl + S5_GROUP, :] = lag[0][zl:] + lag[1][0:S5_GROUP] + dt_ref[g]
            ts_scr[zl + S5_GROUP:n_lag_rows, :] = lag[1][S5_GROUP:]
            for hf in range(CHUNK // OCT):
                acc = None
                for s in range(OCT):
                    t = OCT * hf + s
                    win = ts_scr[(CHUNK - 1 - t) * S5_GROUP:(CHUNK - 1 - t) * S5_GROUP + CH, :]
                    acc = win if acc is None else jnp.where(slot == s, win, acc)
                m_ref[g, :, hf * LANES:(hf + 1) * LANES] = acc.astype(BF16)
        return carry

    lax.fori_loop(0, PAIRS, pair, 0)


def _s5_tables(lam_re, lam_im, log_step, b_re, b_im, c_re, c_im, s5_d):
    lre = lam_re.reshape(2, GP)
    lim = lam_im.reshape(2, GP)
    lst = jnp.repeat(log_step, S5_STATE, axis=-1)
    bre = b_re.transpose(0, 3, 1, 2).reshape(2, S5_GROUP, GP)
    bim = b_im.transpose(0, 3, 1, 2).reshape(2, S5_GROUP, GP)
    cre = c_re.transpose(0, 2, 1, 3).reshape(2, S5_GROUP, GP)
    cim = c_im.transpose(0, 2, 1, 3).reshape(2, S5_GROUP, GP)
    eye = jnp.asarray(np.eye(S5_GROUP, dtype=np.float32))
    dt = jnp.tile(eye[None] * s5_d.reshape(S5_GROUPS, 1, S5_GROUP), (1, 1, OCT))

    vm = pl.BlockSpec(memory_space=pltpu.VMEM)
    tab = (2, 2, CHUNK, S5_GROUP, GP)
    apre, apim, m, w, e = pl.pallas_call(
        _prep_kernel,
        in_specs=[vm] * 8,
        out_specs=[vm] * 5,
        out_shape=[jax.ShapeDtypeStruct((SEG_CHUNKS + 1, 2, GP), F32),
                   jax.ShapeDtypeStruct((SEG_CHUNKS + 1, 2, GP), F32),
                   jax.ShapeDtypeStruct((S5_GROUPS, CH, CH), BF16),
                   jax.ShapeDtypeStruct((PAIRS, 2 * CH, 4 * LANES), BF16),
                   jax.ShapeDtypeStruct((PAIRS, 4 * LANES, 2 * CH), BF16)],
        scratch_shapes=[pltpu.VMEM(tab, F32),
                        pltpu.VMEM(tab, F32),
                        pltpu.VMEM(((2 * CHUNK - 1) * S5_GROUP, LANES), F32)],
        compiler_params=pltpu.CompilerParams(vmem_limit_bytes=VMEM_LIMIT),
        name="s5_tables",
    )(lre, lim, lst, bre, bim, cre, cim, dt)
    return m, w, e, apre, apim


def _rms(x, g):
    return x * lax.rsqrt(jnp.mean(x * x, axis=-1, keepdims=True) + EPS) * g


def _slot_transpose(v):
    v = list(v)
    slot = lax.broadcasted_iota(jnp.int32, v[0].shape, 1) // S5_GROUP
    for k in (4, 2, 1):
        low = (slot & k) == 0
        for i in range(OCT):
            if i & k:
                continue
            a, b = v[i], v[i + k]
            v[i] = jnp.where(low, a, pltpu.roll(b, k * S5_GROUP, axis=1))
            v[i + k] = jnp.where(low, pltpu.roll(a, LANES - k * S5_GROUP, axis=1), b)
    return v


def _inproj_kernel(n_prompt_tiles, n_cast_steps, n_cast, xp_ref, xs_ref, mod_ref, g_ref, win_ref, bdc_ref, bds_ref,
                   *rest):
    cast_in, rest = rest[:n_cast], rest[n_cast:]
    xg_ref, fc_ref, fs_ref, gs_ref, gf_ref = rest[:5]
    cast_out, (h_scr, hb_scr, zs_scr) = rest[5:5 + n_cast], rest[5 + n_cast:]
    _cast_blocks(n_cast_steps, cast_in, cast_out)
    tm = xp_ref.shape[0]
    mod = mod_ref[0]
    x = jnp.where(pl.program_id(0) < n_prompt_tiles, xp_ref[...], xs_ref[...])
    h = _rms(x, g_ref[...]) * (1.0 + mod[1:2]) + mod[0:1]
    for k in range(D_MODEL // LANES):
        for c in range(tm // CHUNK):
            h_scr[k, c * PITCH:c * PITCH + CHUNK, :] = h[c * CHUNK:(c + 1) * CHUNK, k * LANES:(k + 1) * LANES]
    for j in range(tm // SEG):
        for s in range(CHUNK):
            r0 = j * SEG + s * SEG_CHUNKS
            for k in range(D_MODEL // LANES):
                hb_scr[r0:r0 + SEG_CHUNKS, k * LANES:(k + 1) * LANES] = (
                    h_scr[k, pl.ds(j * SEG_CHUNKS * PITCH + s, SEG_CHUNKS, stride=PITCH), :].astype(BF16))
    hb = hb_scr[...]
    dot = functools.partial(jnp.dot, preferred_element_type=F32)
    zs_scr[...] = dot(hb, win_ref[:, 0:S5_WIDTH])
    for j in range(tm // SEG):
        for b in range(S5_WIDTH // LANES):
            for hf in range(CHUNK // OCT):
                z = [zs_scr[j * SEG + s * SEG_CHUNKS:j * SEG + (s + 1) * SEG_CHUNKS, b * LANES:(b + 1) * LANES]
                     for s in range(OCT * hf, OCT * (hf + 1))]
                for i, xi in enumerate(_slot_transpose(z)):
                    xg_ref[OCT * b + i, j * SEG_CHUNKS:(j + 1) * SEG_CHUNKS,
                           hf * LANES:(hf + 1) * LANES] = xi.astype(BF16)
    uf = dot(hb, win_ref[:, S5_WIDTH:2 * S5_WIDTH]).astype(BF16)
    fc_ref[...] = dot(uf, bdc_ref[...]).astype(BF16)
    fs_ref[...] = dot(uf, bds_ref[...]).astype(BF16)
    o = 2 * S5_WIDTH
    gs_ref[...] = jax.nn.sigmoid(dot(hb, win_ref[:, o:o + D_MODEL])).astype(BF16)
    gf_ref[...] = jax.nn.sigmoid(dot(hb, win_ref[:, o + D_MODEL:o + 2 * D_MODEL])).astype(BF16)


def _channel_dft_mats():
    j = np.arange(FFT_GROUP)
    ang = 2.0 * np.pi * ((j[:, None] * j[None, :]) % FFT_GROUP) / FFT_GROUP
    blk_c = np.cos(ang) / math.sqrt(FFT_GROUP)
    blk_s = np.sin(ang) / math.sqrt(FFT_GROUP)
    bdc = np.kron(np.eye(FFT_GROUPS), blk_c)
    bds = np.kron(np.eye(FFT_GROUPS), blk_s)
    return jnp.asarray(bdc, F32).astype(BF16), jnp.asarray(bds, F32).astype(BF16)


def _const_spec(shape):
    nd = len(shape)
    return pl.BlockSpec(shape, lambda i: (0,) * nd, pipeline_mode=pl.Buffered(1))


def _two_part_specs(tm, width, n_prompt_tiles):
    return [pl.BlockSpec((tm, width), lambda i: (jnp.minimum(i, n_prompt_tiles - 1), 0)),
            pl.BlockSpec((tm, width), lambda i: (jnp.maximum(i - n_prompt_tiles, 0), 0))]


def _mod_spec(n_prompt_tiles, tiles_per_seq):
    row = lambda i: jnp.where(i < n_prompt_tiles, 0, 1 + (i - n_prompt_tiles) // tiles_per_seq)
    return pl.BlockSpec((1, N_MOD, D_MODEL), lambda i: (row(i), 0, 0))


N_CAST_STEPS = 16


def _inproj(xp, xs, seq_len, mod, norm_g, w_in_b, later_weights, tm):
    t = xp.shape[0] + xs.shape[0]
    n_p = xp.shape[0] // tm
    assert t // tm >= N_CAST_STEPS
    bdc, bds = _channel_dft_mats()
    casts = [_row_block_cast_specs(w, N_CAST_STEPS) for w in later_weights]
    tok = lambda w: pl.BlockSpec((tm, w), lambda i: (i, 0))
    out = lambda w: jax.ShapeDtypeStruct((t, w), BF16)
    return pl.pallas_call(
        functools.partial(_inproj_kernel, n_p, N_CAST_STEPS, len(casts)),
        grid=(t // tm,),
        in_specs=_two_part_specs(tm, D_MODEL, n_p) + [
                  _mod_spec(n_p, seq_len // tm),
                  _const_spec((1, D_MODEL)),
                  _const_spec(w_in_b.shape),
                  _const_spec(bdc.shape),
                  _const_spec(bds.shape)] + [c[0] for c in casts],
        out_specs=[pl.BlockSpec((S5_GROUPS, tm // CHUNK, CH), lambda i: (0, i, 0)),
                   tok(FFT_WIDTH), tok(FFT_WIDTH), tok(D_MODEL), tok(D_MODEL)] + [c[0] for c in casts],
        out_shape=[jax.ShapeDtypeStruct((S5_GROUPS, t // CHUNK, CH), BF16),
                   out(FFT_WIDTH), out(FFT_WIDTH), out(D_MODEL), out(D_MODEL)] + [c[1] for c in casts],
        scratch_shapes=[pltpu.VMEM((D_MODEL // LANES, tm // CHUNK * PITCH, LANES), F32),
                        pltpu.VMEM((tm, D_MODEL), BF16),
                        pltpu.VMEM((tm, S5_WIDTH), F32)],
        compiler_params=_cparams(("arbitrary",)),
        name="inproj",
    )(xp, xs, mod, norm_g.reshape(1, D_MODEL), w_in_b, bdc, bds, *later_weights)


N_PROMPT_SEG = 16
N_SAMPLE_SEG = 32
N_SAMPLE_SEQ = 2
N_SEG = N_PROMPT_SEG + N_SAMPLE_SEG
ROWS = SEG_CHUNKS * N_SEG
ROWS_P = SEG_CHUNKS * N_PROMPT_SEG
OCT_PAIRS = OCT // 2
SEG_PITCH = 56


def _cmul_add(ar, ai, hr, hi, sr, si):
    return ar * hr - ai * hi + sr, ar * hi + ai * hr + si


def _s5_kernel(x_ref, m_ref, w_ref, e_ref, apre_ref, apim_ref, h0_ref,
               y_ref, fin_ref, s_scr, hin_scr, hinp_scr, f_scr, hs_scr, y_scr):
    dot = functools.partial(jnp.dot, preferred_element_type=F32)
    parts = ((0, ROWS_P), (ROWS_P, ROWS - ROWS_P))
    seg_rows = lambda c: pl.ds(c, N_SEG, stride=PITCH)
    blk = lambda c: pl.ds(c * SEG_PITCH, N_SEG)
    seq_rows = lambda j: pl.ds(j, N_SAMPLE_SEQ, stride=SEG_CHUNKS)
    lat = lambda c: pl.ds(c * SEG_PITCH + N_PROMPT_SEG, N_SAMPLE_SEG)
    zero = jnp.zeros((N_SEG, LANES), F32)

    for pr in range(OCT_PAIRS):
        ln = slice(pr * LANES, (pr + 1) * LANES)
        g0, g1 = 2 * pr, 2 * pr + 1
        for r0, nr in parts:
            s = (dot(x_ref[g0, r0:r0 + nr, :], w_ref[pr, 0:CH, :])
                 + dot(x_ref[g1, r0:r0 + nr, :], w_ref[pr, CH:2 * CH, :]))
            for k in range(4):
                for sg in range(nr // SEG_CHUNKS):
                    p0 = (r0 // SEG_CHUNKS + sg) * PITCH
                    s_scr[k, p0:p0 + SEG_CHUNKS, :] = s[sg * SEG_CHUNKS:(sg + 1) * SEG_CHUNKS, k * LANES:(k + 1) * LANES]

        ar, ai = apre_ref[1, 0:1, ln], apim_ref[1, 0:1, ln]
        hr, hi = zero, zero
        for c in range(SEG_CHUNKS):
            hin_scr[0, blk(c), :] = hr
            hin_scr[1, blk(c), :] = hi
            hr, hi = _cmul_add(ar, ai, hr, hi, s_scr[0, seg_rows(c), :], s_scr[1, seg_rows(c), :])
        fin_ref[pr, 0] = hr[0:N_PROMPT_SEG]
        fin_ref[pr, 1] = hi[0:N_PROMPT_SEG]
        f_scr[0] = hr[N_PROMPT_SEG:]
        f_scr[1] = hi[N_PROMPT_SEG:]
        br, bi = apre_ref[1, 1:2, ln], apim_ref[1, 1:2, ln]
        gr, gi = zero, zero
        for c in range(SEG_CHUNKS - 1, -1, -1):
            hin_scr[2, blk(c), :] = gr
            hin_scr[3, blk(c), :] = gi
            gr, gi = _cmul_add(br, bi, gr, gi, s_scr[2, seg_rows(c), :], s_scr[3, seg_rows(c), :])
        fin_ref[pr, 2] = gr[0:N_PROMPT_SEG]
        fin_ref[pr, 3] = gi[0:N_PROMPT_SEG]
        f_scr[2] = gr[N_PROMPT_SEG:]
        f_scr[3] = gi[N_PROMPT_SEG:]

        a2r, a2i = apre_ref[SEG_CHUNKS, 0:1, ln], apim_ref[SEG_CHUNKS, 0:1, ln]
        hr, hi = h0_ref[0, :, ln], h0_ref[1, :, ln]
        for j in range(SEG_CHUNKS):
            hs_scr[0, seq_rows(j), :] = hr
            hs_scr[1, seq_rows(j), :] = hi
            hr, hi = _cmul_add(a2r, a2i, hr, hi, f_scr[0, seq_rows(j), :], f_scr[1, seq_rows(j), :])
        b2r, b2i = apre_ref[SEG_CHUNKS, 1:2, ln], apim_ref[SEG_CHUNKS, 1:2, ln]
        gr, gi = h0_ref[2, :, ln], h0_ref[3, :, ln]
        for j in range(SEG_CHUNKS - 1, -1, -1):
            hs_scr[2, seq_rows(j), :] = gr
            hs_scr[3, seq_rows(j), :] = gi
            gr, gi = _cmul_add(b2r, b2i, gr, gi, f_scr[2, seq_rows(j), :], f_scr[3, seq_rows(j), :])
        for c in range(SEG_CHUNKS):
            p_r, p_i = apre_ref[c, 0:1, ln], apim_ref[c, 0:1, ln]
            hr, hi = _cmul_add(p_r, p_i, hs_scr[0], hs_scr[1], hin_scr[0, lat(c), :], hin_scr[1, lat(c), :])
            hin_scr[0, lat(c), :] = hr
            hin_scr[1, lat(c), :] = hi
            cb = SEG_CHUNKS - 1 - c
            p_r, p_i = apre_ref[cb, 1:2, ln], apim_ref[cb, 1:2, ln]
            gr, gi = _cmul_add(p_r, p_i, hs_scr[2], hs_scr[3], hin_scr[2, lat(c), :], hin_scr[3, lat(c), :])
            hin_scr[2, lat(c), :] = gr
            hin_scr[3, lat(c), :] = gi

        for k in range(4):
            for sg in range(N_SEG):
                hinp_scr[sg * SEG_CHUNKS:(sg + 1) * SEG_CHUNKS, k * LANES:(k + 1) * LANES] = (
                    hin_scr[k, pl.ds(sg, SEG_CHUNKS, stride=SEG_PITCH), :].astype(BF16))
        for gi_, g in ((0, g0), (1, g1)):
            for r0, nr in parts:
                y_scr[g, r0:r0 + nr, :] = (dot(x_ref[g, r0:r0 + nr, :], m_ref[g])
                                           + dot(hinp_scr[r0:r0 + nr, :], e_ref[pr, :, gi_ * CH:(gi_ + 1) * CH]))

    rb_rows = 2 * SEG_CHUNKS
    for rb in range(ROWS // rb_rows):
        for hf in range(CHUNK // OCT):
            v = [y_scr[i, rb * rb_rows:(rb + 1) * rb_rows, hf * LANES:(hf + 1) * LANES] for i in range(OCT)]
            for s, acc in enumerate(_slot_transpose(v)):
                t = OCT * hf + s
                for sg in range(rb_rows // SEG_CHUNKS):
                    seg = rb * (rb_rows // SEG_CHUNKS) + sg
                    r0 = seg * SEG + t * SEG_CHUNKS
                    y_ref[0, r0:r0 + SEG_CHUNKS, :] = acc[sg * SEG_CHUNKS:(sg + 1) * SEG_CHUNKS]


def _s5_conv(xg, m, w, e, apre, apim, h0):
    n_oct = S5_GROUPS // OCT
    assert xg.shape[1] == ROWS
    nt = ROWS * CHUNK
    return pl.pallas_call(
        _s5_kernel,
        grid=(n_oct,),
        in_specs=[pl.BlockSpec((OCT, ROWS, CH), lambda o: (o, 0, 0)),
                  pl.BlockSpec((OCT, CH, CH), lambda o: (o, 0, 0)),
                  pl.BlockSpec((OCT_PAIRS, 2 * CH, 4 * LANES), lambda o: (o, 0, 0)),
                  pl.BlockSpec((OCT_PAIRS, 4 * LANES, 2 * CH), lambda o: (o, 0, 0)),
                  pl.BlockSpec((SEG_CHUNKS + 1, 2, OCT_PAIRS * LANES), lambda o: (0, 0, o)),
                  pl.BlockSpec((SEG_CHUNKS + 1, 2, OCT_PAIRS * LANES), lambda o: (0, 0, o)),
                  pl.BlockSpec((4, N_SAMPLE_SEQ, OCT_PAIRS * LANES), lambda o: (0, 0, o))],
        out_specs=[pl.BlockSpec((1, nt, LANES), lambda o: (o, 0, 0)),
                   pl.BlockSpec((OCT_PAIRS, 4, N_PROMPT_SEG, LANES), lambda o: (o, 0, 0, 0))],
        out_shape=[jax.ShapeDtypeStruct((n_oct, nt, LANES), F32),
                   jax.ShapeDtypeStruct((PAIRS, 4, N_PROMPT_SEG, LANES), F32)],
        scratch_shapes=[pltpu.VMEM((4, N_SEG * PITCH, LANES), F32),
                        pltpu.VMEM((4, SEG_CHUNKS * SEG_PITCH, LANES), F32),
                        pltpu.VMEM((ROWS, 4 * LANES), BF16),
                        pltpu.VMEM((4, N_SAMPLE_SEG, LANES), F32),
                        pltpu.VMEM((4, N_SAMPLE_SEG, LANES), F32),
                        pltpu.VMEM((OCT, ROWS, CH), F32)],
        compiler_params=_cparams(("parallel",)),
        name="s5_conv",
    )(xg, m, w, e, apre, apim, h0)


def _tile_order_positions(l):
    r = np.arange(l)
    tile, wi = r // SEG, r % SEG
    return tile * SEG + (wi % SEG_CHUNKS) * CHUNK + wi // SEG_CHUNKS


def _tile_dft_tables(scale, rows_in_tile_order=True):
    pos = _tile_order_positions(SEG)
    freq = pos if rows_in_tile_order else np.arange(SEG)
    ang = 2.0 * np.pi * ((freq[:, None] * pos[None, :]) % SEG) / SEG
    return jnp.asarray(np.cos(ang) * scale, F32), jnp.asarray(np.sin(ang) * scale, F32)


def _pos_dft_tile_kernel(cm_ref, sm_ref, fc_ref, fs_ref, o_ref):
    dot = functools.partial(jnp.dot, preferred_element_type=F32)
    cm, sm = cm_ref[...].astype(BF16), sm_ref[...].astype(BF16)
    for b in range(fc_ref.shape[0]):
        o_ref[b] = (dot(cm, fc_ref[b]) - dot(sm, fs_ref[b])).astype(BF16)


def _pos_dft_tile(fc, fs, n):
    _, l, wdt = fc.shape
    cm, sm = _tile_dft_tables(1.0 / math.sqrt(l))
    nb = 4
    seq = pl.BlockSpec((nb, l, wdt), lambda b: (b, 0, 0))
    return pl.pallas_call(
        _pos_dft_tile_kernel,
        grid=(n // nb,),
        in_specs=[_const_spec((l, l)), _const_spec((l, l)), seq, seq],
        out_specs=seq,
        out_shape=jax.ShapeDtypeStruct((n, l, wdt), BF16),
        compiler_params=_cparams(("parallel",)),
        name="pos_dft_tile",
    )(cm, sm, fc, fs)


N_TILES = 16


def _fft16(xr, xi):
    n = len(xr)
    rev = [int(format(i, "04b")[::-1], 2) for i in range(n)]
    ar = [xr[r] for r in rev]
    ai = [xi[r] for r in rev]
    size = 2
    while size <= n:
        half = size // 2
        for start in range(0, n, size):
            for k in range(half):
                wr = math.cos(2.0 * math.pi * k / size)
                wi = -math.sin(2.0 * math.pi * k / size)
                i0, i1 = start + k, start + k + half
                if k == 0:
                    tr, ti = ar[i1], ai[i1]
                elif 4 * k == size:
                    tr, ti = ai[i1], -ar[i1]
                else:
                    tr = ar[i1] * wr - ai[i1] * wi
                    ti = ar[i1] * wi + ai[i1] * wr
                ar[i1], ai[i1] = ar[i0] - tr, ai[i0] - ti
                ar[i0], ai[i0] = ar[i0] + tr, ai[i0] + ti
        size *= 2
    return ar, ai


def _pos_dft_long_kernel(cb_ref, sb_ref, ca_ref, sa_ref, fc_ref, fs_ref, o_ref, a_scr):
    wdt = fc_ref.shape[2]

    def rows_fft(r, carry):
        rows = pl.ds(pl.multiple_of(r * CHUNK, CHUNK), CHUNK)
        for b in range(wdt // LANES):
            lns = slice(b * LANES, (b + 1) * LANES)
            xr = [fc_ref[j, rows, lns].astype(F32) for j in range(N_TILES)]
            xi = [-fs_ref[j, rows, lns].astype(F32) for j in range(N_TILES)]
            ar, ai = _fft16(xr, xi)
            for k in range(N_TILES):
                a_scr[k, 0, rows, lns] = ar[k].astype(BF16)
                a_scr[k, 1, rows, lns] = ai[k].astype(BF16)
        return carry

    lax.fori_loop(0, SEG // CHUNK, rows_fft, 0)

    dot = functools.partial(jnp.dot, preferred_element_type=F32)
    cb, sb = cb_ref[...], sb_ref[...]
    for k1 in range(N_TILES):
        ca, sa = ca_ref[k1:k1 + 1, :], sa_ref[k1:k1 + 1, :]
        dc = (cb * ca - sb * sa).astype(BF16)
        ds = (sb * ca + cb * sa).astype(BF16)
        out = dot(dc, a_scr[k1, 0]) + dot(ds, a_scr[k1, 1])
        o_ref[:, k1 * SEG_CHUNKS:(k1 + 1) * SEG_CHUNKS, :] = out.astype(BF16).reshape(N_TILES, SEG_CHUNKS, wdt)


def _pos_dft_long(fc, fs, first, n):
    wdt = fc.shape[-1]
    l = N_TILES * SEG
    wb = 2 * LANES
    cb, sb = _tile_dft_tables(1.0 / math.sqrt(l), rows_in_tile_order=False)
    pos = _tile_order_positions(SEG)
    ang = 2.0 * np.pi * (np.arange(N_TILES)[:, None] * pos[None, :]) / l
    ca, sa = jnp.asarray(np.cos(ang), F32), jnp.asarray(np.sin(ang), F32)
    const2 = lambda shape: pl.BlockSpec(shape, lambda b, k: (0, 0), pipeline_mode=pl.Buffered(1))
    seq_in = pl.BlockSpec((None, N_TILES, SEG, wb), lambda b, k: (first + b, 0, 0, k))
    seq_out = pl.BlockSpec((None, N_TILES, SEG, wb), lambda b, k: (b, 0, 0, k))
    out = pl.pallas_call(
        _pos_dft_long_kernel,
        grid=(n, wdt // wb),
        in_specs=[const2((SEG, SEG)), const2((SEG, SEG)), const2((N_TILES, SEG)), const2((N_TILES, SEG)),
                  seq_in, seq_in],
        out_specs=seq_out,
        out_shape=jax.ShapeDtypeStruct((n, N_TILES, SEG, wdt), BF16),
        scratch_shapes=[pltpu.VMEM((N_TILES, 2, SEG, wb), BF16)],
        compiler_params=_cparams(("parallel", "parallel")),
        name="pos_dft_long",
    )(cb, sb, ca, sa, fc, fs)
    return out.reshape(n * l, wdt)


def _mix_ffn_kernel(n_prompt_tiles, xp_ref, xs_ref, mod_ref, ys_ref, yfp_ref, yfs_ref, gs_ref, gf_ref,
                    wglu_ref, bglu_ref, wps_ref, wpf_ref, wout_ref, n2_ref,
                    wg_ref, wu_ref, wd_ref, fn_ref, op_ref, os_ref, m_scr, mb_scr):
    tm = xp_ref.shape[0]
    dot = functools.partial(jnp.dot, preferred_element_type=F32)
    mod = mod_ref[0]
    is_prompt = pl.program_id(0) < n_prompt_tiles
    x = jnp.where(is_prompt, xp_ref[...], xs_ref[...])
    yf = jnp.where(is_prompt, yfp_ref[...], yfs_ref[...])
    y = jnp.concatenate([ys_ref[b] for b in range(S5_WIDTH // LANES)], axis=1)
    z = jax.nn.gelu(y)
    z = z * jax.nn.sigmoid(dot(z.astype(BF16), wglu_ref[...]) + bglu_ref[...])
    m = (gs_ref[...].astype(F32) * dot(z.astype(BF16), wps_ref[...])
         + gf_ref[...].astype(F32) * dot(yf, wpf_ref[...]))
    for k in range(D_MODEL // LANES):
        for s in range(tm // SEG_CHUNKS):
            m_scr[k, s * PITCH:s * PITCH + SEG_CHUNKS, :] = (
                m[s * SEG_CHUNKS:(s + 1) * SEG_CHUNKS, k * LANES:(k + 1) * LANES])
    for j in range(tm // SEG):
        for c in range(SEG_CHUNKS):
            r0 = j * SEG + c * CHUNK
            for k in range(D_MODEL // LANES):
                mb_scr[r0:r0 + CHUNK, k * LANES:(k + 1) * LANES] = (
                    m_scr[k, pl.ds(j * CHUNK * PITCH + c, CHUNK, stride=PITCH), :].astype(BF16))
    x1 = x + mod[2:3] * dot(mb_scr[...], wout_ref[...])
    h2 = (_rms(x1, n2_ref[...]) * (1.0 + mod[4:5]) + mod[3:4]).astype(BF16)
    ff = None
    for n0, n1 in zip(FF_SPLITS[:-1], FF_SPLITS[1:]):
        gate = dot(h2, wg_ref[:, n0:n1])
        up = dot(h2, wu_ref[:, n0:n1])
        part = dot((gate * jax.nn.sigmoid(gate) * up).astype(BF16), wd_ref[n0:n1, :])
        ff = part if ff is None else ff + part
    x2 = x1 + mod[5:6] * ff
    res = _rms(x2, fn_ref[...])

    @pl.when(is_prompt)
    def _():
        op_ref[...] = res

    @pl.when(jnp.logical_not(is_prompt))
    def _():
        os_ref[...] = res


def _mix_ffn(xp, xs, seq_len, mod, ys, yf_p, yf_s, gs, gf, wts, tm):
    tp, ts = xp.shape[0], xs.shape[0]
    n_p = tp // tm
    tok = lambda w: pl.BlockSpec((tm, w), lambda i: (i, 0))
    return pl.pallas_call(
        functools.partial(_mix_ffn_kernel, n_p),
        grid=((tp + ts) // tm,),
        in_specs=_two_part_specs(tm, D_MODEL, n_p)
                 + [_mod_spec(n_p, seq_len // tm),
                    pl.BlockSpec((S5_WIDTH // LANES, tm, LANES), lambda i: (0, i, 0))]
                 + _two_part_specs(tm, FFT_WIDTH, n_p)
                 + [tok(D_MODEL), tok(D_MODEL)]
                 + [_const_spec(w.shape) for w in wts],
        out_specs=_two_part_specs(tm, D_MODEL, n_p),
        out_shape=[jax.ShapeDtypeStruct((tp, D_MODEL), F32), jax.ShapeDtypeStruct((ts, D_MODEL), F32)],
        scratch_shapes=[pltpu.VMEM((D_MODEL // LANES, tm // CHUNK * PITCH, LANES), F32),
                        pltpu.VMEM((tm, D_MODEL), BF16)],
        compiler_params=_cparams(("arbitrary",)),
        name="mix_ffn",
    )(xp, xs, mod, ys, yf_p, yf_s, gs, gf, *wts)


def kernel(x_prompt, x_sample, state_s5, c, c_ctx, norm1_g, norm2_g, w_ada, b_ada, w_in,
           s5_lambda_re, s5_lambda_im, s5_log_step, s5_b_re, s5_b_im, s5_c_re, s5_c_im,
           s5_d, w_glu, b_glu, w_proj_s5, w_proj_fft, w_out, w_ffn_gate, w_ffn_up,
           w_ffn_down, final_norm_g):
    nb, sl, _ = x_prompt.shape
    db, dl, _ = x_sample.shape
    assert w_in.shape[0] == 1 and sl == SEG and nb == N_PROMPT_SEG
    assert db == N_SAMPLE_SEQ and dl == SEG * SEG_CHUNKS

    cvec = jnp.concatenate([c_ctx[None], c, jnp.zeros((8 - 1 - db, D_MODEL), F32)], axis=0)
    mod, w_in_b = _adaln(cvec, w_ada[0], b_ada[0], w_in[0])
    mod = mod.reshape(8, N_MOD, D_MODEL)

    m, w, e, apre, apim = _s5_tables(s5_lambda_re[0], s5_lambda_im[0], s5_log_step[0], s5_b_re[0],
                                     s5_b_im[0], s5_c_re[0], s5_c_im[0], s5_d[0])

    tm = 512
    xp = x_prompt.reshape(nb * sl, D_MODEL)
    xs = x_sample.reshape(db * dl, D_MODEL)
    later = (w_glu[0], w_proj_s5[0], w_proj_fft[0], w_out[0], w_ffn_gate[0], w_ffn_up[0], w_ffn_down[0])
    xg, fc, fs, gs, gf, *later_b = _inproj(xp, xs, dl, mod, norm1_g[0], w_in_b, later, tm)
    wglu_b, wps_b, wpf_b, wout_b, wg_b, wu_b, wd_b = later_b

    h0 = state_s5[:, 0].astype(F32).transpose(1, 2, 0, 3, 4).reshape(4, db, GP)
    ys, fin = _s5_conv(xg, m, w, e, apre, apim, h0)
    new_state = fin.reshape(PAIRS, 2, 2, nb, 2, S5_STATE).transpose(3, 1, 2, 0, 4, 5)
    new_state = new_state.reshape(nb, 1, 2, 2, S5_GROUPS, S5_STATE)

    yf_p = _pos_dft_tile(fc.reshape(-1, SEG, FFT_WIDTH), fs.reshape(-1, SEG, FFT_WIDTH), nb)
    n_prompt_units = nb // N_TILES
    yf_s = _pos_dft_long(fc.reshape(-1, N_TILES, SEG, FFT_WIDTH), fs.reshape(-1, N_TILES, SEG, FFT_WIDTH),
                         n_prompt_units, db)

    wts = (wglu_b, b_glu[0].reshape(1, S5_WIDTH), wps_b, wpf_b, wout_b, norm2_g[0].reshape(1, D_MODEL),
           wg_b, wu_b, wd_b, final_norm_g.reshape(1, D_MODEL))
    tm2 = 2 * SEG
    y_p, y_s = _mix_ffn(xp, xs, dl, mod, ys, yf_p.reshape(nb * sl, FFT_WIDTH), yf_s, gs, gf, wts, tm2)
    return (y_p.reshape(nb, sl, D_MODEL), y_s.reshape(db, dl, D_MODEL), new_state)
```

```python
import functools
import math

import numpy as np
import jax
import jax.numpy as jnp
from jax import lax
from jax.experimental import pallas as pl
from jax.experimental.pallas import tpu as pltpu

F32 = jnp.float32
BF16 = jnp.bfloat16

D_MODEL = 1024
S5_WIDTH = 512
S5_GROUPS = 32
S5_GROUP = 16
S5_STATE = 64
FFT_WIDTH = 512
FFT_GROUPS = 4
FFT_GROUP = 128
D_FF = 2816
N_MOD = 6
EPS = 1e-6

LANES = 128
CHUNK = 16
SEG_CHUNKS = 16
SEG = CHUNK * SEG_CHUNKS
GP = S5_GROUPS * S5_STATE
PAIRS = S5_GROUPS // 2
CH = CHUNK * S5_GROUP
OCT = LANES // S5_GROUP
PITCH = 24

MXU_DIM = 256
FF_SPLITS = (0, (D_FF // MXU_DIM + 1) // 2 * MXU_DIM, D_FF)
VMEM_LIMIT = 58 * 1024 * 1024


def _cparams(sem):
    return pltpu.CompilerParams(dimension_semantics=sem, vmem_limit_bytes=VMEM_LIMIT)


def _row_block_cast_specs(w, n_steps):
    rows = w.shape[0] // n_steps
    assert rows * n_steps == w.shape[0] and rows % 16 == 0, (w.shape, n_steps)
    spec = pl.BlockSpec((rows, w.shape[1]), lambda i: (jnp.minimum(i, n_steps - 1), 0))
    return spec, jax.ShapeDtypeStruct(w.shape, BF16)


def _cast_blocks(n_steps, in_refs, out_refs):
    @pl.when(pl.program_id(0) < n_steps)
    def _():
        for src, dst in zip(in_refs, out_refs):
            dst[...] = src[...].astype(BF16)


def _ada_kernel(n_cast_steps, c_ref, w_ref, b_ref, win_ref, o_ref, winb_ref):
    _cast_blocks(n_cast_steps, [win_ref], [winb_ref])
    dot = functools.partial(jnp.dot, preferred_element_type=F32)
    c = c_ref[...]
    s = c * jax.nn.sigmoid(c)
    s_hi = s.astype(BF16)
    s_lo = (s - s_hi.astype(F32)).astype(BF16)
    w = w_ref[...]
    w_hi = w.astype(BF16)
    w_lo = (w - w_hi.astype(F32)).astype(BF16)
    r = dot(jnp.concatenate([s_hi, s_lo], axis=0), w_hi)
    n = s.shape[0]
    o_ref[...] = r[0:n] + r[n:2 * n] + dot(s_hi, w_lo) + b_ref[...]


def _adaln(cvec8, w_ada, b_ada, w_in):
    n_out = w_ada.shape[1]
    tn = 768
    n_steps = n_out // tn
    cast_spec, cast_shape = _row_block_cast_specs(w_in, n_steps)
    return pl.pallas_call(
        functools.partial(_ada_kernel, n_steps),
        grid=(n_steps,),
        in_specs=[pl.BlockSpec((8, D_MODEL), lambda j: (0, 0)),
                  pl.BlockSpec((D_MODEL, tn), lambda j: (0, j)),
                  pl.BlockSpec((1, tn), lambda j: (0, j)),
                  cast_spec],
        out_specs=[pl.BlockSpec((8, tn), lambda j: (0, j)), cast_spec],
        out_shape=[jax.ShapeDtypeStruct((8, n_out), F32), cast_shape],
        compiler_params=_cparams(("arbitrary",)),
        name="adaln",
    )(cvec8, w_ada, b_ada.reshape(1, n_out), w_in)


def _prep_kernel(lre_ref, lim_ref, lst_ref, bre_ref, bim_ref, cre_ref, cim_ref, dt_ref,
                 apre_ref, apim_ref, m_ref, w_ref, e_ref, gb_scr, ca_scr, ts_scr):
    lre = lre_ref[...]
    lim = lim_ref[...]
    step = jnp.exp(lst_ref[...])
    mag = jnp.exp(lre * step)
    are = mag * jnp.cos(lim * step)
    aim = mag * jnp.sin(lim * step)
    nr = are - 1.0
    den = lre * lre + lim * lim
    fr = (nr * lre + aim * lim) / den
    fi = (aim * lre - nr * lim) / den

    pr = [jnp.ones_like(are)]
    pi = [jnp.zeros_like(are)]
    for _ in range(CHUNK):
        r, i = pr[-1], pi[-1]
        pr.append(r * are - i * aim)
        pi.append(r * aim + i * are)
    a16r, a16i = pr[CHUNK], pi[CHUNK]
    qr, qi = jnp.ones_like(are), jnp.zeros_like(are)
    for c in range(SEG_CHUNKS + 1):
        apre_ref[c] = qr
        apim_ref[c] = qi
        qr, qi = qr * a16r - qi * a16i, qr * a16i + qi * a16r

    for d in range(2):
        bbre = fr[d:d + 1] * bre_ref[d] - fi[d:d + 1] * bim_ref[d]
        bbim = fr[d:d + 1] * bim_ref[d] + fi[d:d + 1] * bre_ref[d]
        cre, cim = cre_ref[d], cim_ref[d]
        for s in range(CHUNK):
            k = CHUNK - 1 - s if d == 0 else s
            r, i = pr[k][d:d + 1], pi[k][d:d + 1]
            gb_scr[0, d, s] = r * bbre - i * bbim
            gb_scr[1, d, s] = r * bbim + i * bbre
            f = s + 1 if d == 0 else CHUNK - s
            r, i = pr[f][d:d + 1], pi[f][d:d + 1]
            ca_scr[0, d, s] = r * cre - i * cim
            ca_scr[1, d, s] = -(r * cim + i * cre)

    dot_nt = lambda a, b: lax.dot_general(a, b, (((1,), (1,)), ((), ())), preferred_element_type=F32)
    lane_gi = lax.broadcasted_iota(jnp.int32, (CH, LANES), 1) // S5_STATE
    row_gi = lax.broadcasted_iota(jnp.int32, (LANES, CH), 0) // S5_STATE
    eye = (lax.broadcasted_iota(jnp.int32, (LANES, LANES), 0)
           == lax.broadcasted_iota(jnp.int32, (LANES, LANES), 1)).astype(BF16)
    slot = lax.broadcasted_iota(jnp.int32, (CH, LANES), 1) // S5_GROUP
    n_lag_rows = (2 * CHUNK - 1) * S5_GROUP

    def pair(q, carry):
        lanes = pl.ds(pl.multiple_of(q * LANES, LANES), LANES)
        for d in range(2):
            for ri in range(2):
                col = slice((2 * d + ri) * LANES, (2 * d + ri + 1) * LANES)
                gb = gb_scr[ri, d, :, :, lanes].reshape(CH, LANES)
                ca = ca_scr[ri, d, :, :, lanes].reshape(CH, LANES).astype(BF16)
                ca_t = dot_nt(eye, ca)
                for gi in range(2):
                    w_ref[q, gi * CH:(gi + 1) * CH, col] = jnp.where(lane_gi == gi, gb, 0.0).astype(BF16)
                    e_ref[q, col, gi * CH:(gi + 1) * CH] = jnp.where(row_gi == gi, ca_t, 0.0).astype(BF16)
        for gi in range(2):
            g = 2 * q + gi
            lag = []
            for d in range(2):
                c_re = jnp.concatenate([cre_ref[d, :, lanes]] * OCT, axis=0)
                c_imn = jnp.concatenate([-cim_ref[d, :, lanes]] * OCT, axis=0)
                keep = lax.broadcasted_iota(jnp.int32, (LANES, LANES), 1) // S5_STATE == gi
                c_re = jnp.where(keep, c_re, 0.0).astype(BF16)
                c_imn = jnp.where(keep, c_imn, 0.0).astype(BF16)
                gre = gb_scr[0, d, :, :, lanes].reshape(CH, LANES).astype(BF16)
                gim = gb_scr[1, d, :, :, lanes].reshape(CH, LANES).astype(BF16)
                lag.append(dot_nt(gre, c_re) + dot_nt(gim, c_imn))
            zl = (CHUNK - 1) * S5_GROUP
            ts_scr[0:zl, :] = lag[0][0:zl]
            ts_scr[zl:zl + S5_GROUP, :] = lag[0][zl:] + lag[1][0:S5_GROUP] + dt_ref[g]
            ts_scr[zl + S5_GROUP:n_lag_rows, :] = lag[1][S5_GROUP:]
            for hf in range(CHUNK // OCT):
                acc = None
                for s in range(OCT):
                    t = OCT * hf + s
                    win = ts_scr[(CHUNK - 1 - t) * S5_GROUP:(CHUNK - 1 - t) * S5_GROUP + CH, :]
                    acc = win if acc is None else jnp.where(slot == s, win, acc)
                m_ref[g, :, hf * LANES:(hf + 1) * LANES] = acc.astype(BF16)
        return carry

    lax.fori_loop(0, PAIRS, pair, 0)


def _s5_tables(lam_re, lam_im, log_step, b_re, b_im, c_re, c_im, s5_d):
    lre = lam_re.reshape(2, GP)
    lim = lam_im.reshape(2, GP)
    lst = jnp.repeat(log_step, S5_STATE, axis=-1)
    bre = b_re.transpose(0, 3, 1, 2).reshape(2, S5_GROUP, GP)
    bim = b_im.transpose(0, 3, 1, 2).reshape(2, S5_GROUP, GP)
    cre = c_re.transpose(0, 2, 1, 3).reshape(2, S5_GROUP, GP)
    cim = c_im.transpose(0, 2, 1, 3).reshape(2, S5_GROUP, GP)
    eye = jnp.asarray(np.eye(S5_GROUP, dtype=np.float32))
    dt = jnp.tile(eye[None] * s5_d.reshape(S5_GROUPS, 1, S5_GROUP), (1, 1, OCT))

    vm = pl.BlockSpec(memory_space=pltpu.VMEM)
    tab = (2, 2, CHUNK, S5_GROUP, GP)
    apre, apim, m, w, e = pl.pallas_call(
        _prep_kernel,
        in_specs=[vm] * 8,
        out_specs=[vm] * 5,
        out_shape=[jax.ShapeDtypeStruct((SEG_CHUNKS + 1, 2, GP), F32),
                   jax.ShapeDtypeStruct((SEG_CHUNKS + 1, 2, GP), F32),
                   jax.ShapeDtypeStruct((S5_GROUPS, CH, CH), BF16),
                   jax.ShapeDtypeStruct((PAIRS, 2 * CH, 4 * LANES), BF16),
                   jax.ShapeDtypeStruct((PAIRS, 4 * LANES, 2 * CH), BF16)],
        scratch_shapes=[pltpu.VMEM(tab, F32),
                        pltpu.VMEM(tab, F32),
                        pltpu.VMEM(((2 * CHUNK - 1) * S5_GROUP, LANES), F32)],
        compiler_params=pltpu.CompilerParams(vmem_limit_bytes=VMEM_LIMIT),
        name="s5_tables",
    )(lre, lim, lst, bre, bim, cre, cim, dt)
    return m, w, e, apre, apim


def _rms(x, g):
    return x * lax.rsqrt(jnp.mean(x * x, axis=-1, keepdims=True) + EPS) * g


def _slot_transpose(v):
    v = list(v)
    slot = lax.broadcasted_iota(jnp.int32, v[0].shape, 1) // S5_GROUP
    for k in (4, 2, 1):
        low = (slot & k) == 0
        for i in range(OCT):
            if i & k:
                continue
            a, b = v[i], v[i + k]
            v[i] = jnp.where(low, a, pltpu.roll(b, k * S5_GROUP, axis=1))
            v[i + k] = jnp.where(low, pltpu.roll(a, LANES - k * S5_GROUP, axis=1), b)
    return v


def _inproj_kernel(n_prompt_tiles, n_cast_steps, n_cast, xp_ref, xs_ref, mod_ref, g_ref, win_ref, bdc_ref, bds_ref,
                   *rest):
    cast_in, rest = rest[:n_cast], rest[n_cast:]
    xg_ref, fc_ref, fs_ref, gs_ref, gf_ref = rest[:5]
    cast_out, (h_scr, hb_scr, zs_scr) = rest[5:5 + n_cast], rest[5 + n_cast:]
    _cast_blocks(n_cast_steps, cast_in, cast_out)
    tm = xp_ref.shape[0]
    mod = mod_ref[0]
    x = jnp.where(pl.program_id(0) < n_prompt_tiles, xp_ref[...], xs_ref[...])
    h = _rms(x, g_ref[...]) * (1.0 + mod[1:2]) + mod[0:1]
    for k in range(D_MODEL // LANES):
        for c in range(tm // CHUNK):
            h_scr[k, c * PITCH:c * PITCH + CHUNK, :] = h[c * CHUNK:(c + 1) * CHUNK, k * LANES:(k + 1) * LANES]
    for j in range(tm // SEG):
        for s in range(CHUNK):
            r0 = j * SEG + s * SEG_CHUNKS
            for k in range(D_MODEL // LANES):
                hb_scr[r0:r0 + SEG_CHUNKS, k * LANES:(k + 1) * LANES] = (
                    h_scr[k, pl.ds(j * SEG_CHUNKS * PITCH + s, SEG_CHUNKS, stride=PITCH), :].astype(BF16))

    hb = hb_scr[...]
    dot = functools.partial(jnp.dot, preferred_element_type=F32)
    zs_scr[...] = dot(hb, win_ref[:, 0:S5_WIDTH])
    o = 2 * S5_WIDTH
    gs_ref[...] = jax.nn.sigmoid(dot(hb, win_ref[:, o:o + D_MODEL])).astype(BF16)
    for j in range(tm // SEG):
        for b in range(S5_WIDTH // LANES):
            for hf in range(CHUNK // OCT):
                z = [zs_scr[j * SEG + s * SEG_CHUNKS:j * SEG + (s + 1) * SEG_CHUNKS, b * LANES:(b + 1) * LANES]
                     for s in range(OCT * hf, OCT * (hf + 1))]
                for i, xi in enumerate(_slot_transpose(z)):
                    xg_ref[OCT * b + i, j * SEG_CHUNKS:(j + 1) * SEG_CHUNKS,
                           hf * LANES:(hf + 1) * LANES] = xi.astype(BF16)
    uf = dot(hb, win_ref[:, S5_WIDTH:2 * S5_WIDTH]).astype(BF16)
    for n0 in range(0, FFT_WIDTH, MXU_DIM):
        cols = slice(n0, n0 + MXU_DIM)
        fc_ref[:, cols] = dot(uf[:, cols], bdc_ref[cols, cols]).astype(BF16)
        fs_ref[:, cols] = dot(uf[:, cols], bds_ref[cols, cols]).astype(BF16)
    gf_ref[...] = jax.nn.sigmoid(dot(hb, win_ref[:, o + D_MODEL:o + 2 * D_MODEL])).astype(BF16)


def _channel_dft_mats():
    j = np.arange(FFT_GROUP)
    ang = 2.0 * np.pi * ((j[:, None] * j[None, :]) % FFT_GROUP) / FFT_GROUP
    blk_c = np.cos(ang) / math.sqrt(FFT_GROUP)
    blk_s = np.sin(ang) / math.sqrt(FFT_GROUP)
    bdc = np.kron(np.eye(FFT_GROUPS), blk_c)
    bds = np.kron(np.eye(FFT_GROUPS), blk_s)
    return jnp.asarray(bdc, F32).astype(BF16), jnp.asarray(bds, F32).astype(BF16)


def _const_spec(shape):
    nd = len(shape)
    return pl.BlockSpec(shape, lambda i: (0,) * nd, pipeline_mode=pl.Buffered(1))


def _two_part_specs(tm, width, n_prompt_tiles):
    return [pl.BlockSpec((tm, width), lambda i: (jnp.minimum(i, n_prompt_tiles - 1), 0)),
            pl.BlockSpec((tm, width), lambda i: (jnp.maximum(i - n_prompt_tiles, 0), 0))]


def _mod_spec(n_prompt_tiles, tiles_per_seq):
    row = lambda i: jnp.where(i < n_prompt_tiles, 0, 1 + (i - n_prompt_tiles) // tiles_per_seq)
    return pl.BlockSpec((1, N_MOD, D_MODEL), lambda i: (row(i), 0, 0))


N_CAST_STEPS = 16


def _inproj(xp, xs, seq_len, mod, norm_g, w_in_b, later_weights, tm):
    t = xp.shape[0] + xs.shape[0]
    n_p = xp.shape[0] // tm
    assert t // tm >= N_CAST_STEPS
    bdc, bds = _channel_dft_mats()
    casts = [_row_block_cast_specs(w, N_CAST_STEPS) for w in later_weights]
    tok = lambda w: pl.BlockSpec((tm, w), lambda i: (i, 0))
    out = lambda w: jax.ShapeDtypeStruct((t, w), BF16)
    return pl.pallas_call(
        functools.partial(_inproj_kernel, n_p, N_CAST_STEPS, len(casts)),
        grid=(t // tm,),
        in_specs=_two_part_specs(tm, D_MODEL, n_p) + [
                  _mod_spec(n_p, seq_len // tm),
                  _const_spec((1, D_MODEL)),
                  _const_spec(w_in_b.shape),
                  _const_spec(bdc.shape),
                  _const_spec(bds.shape)] + [c[0] for c in casts],
        out_specs=[pl.BlockSpec((S5_GROUPS, tm // CHUNK, CH), lambda i: (0, i, 0)),
                   tok(FFT_WIDTH), tok(FFT_WIDTH), tok(D_MODEL), tok(D_MODEL)] + [c[0] for c in casts],
        out_shape=[jax.ShapeDtypeStruct((S5_GROUPS, t // CHUNK, CH), BF16),
                   out(FFT_WIDTH), out(FFT_WIDTH), out(D_MODEL), out(D_MODEL)] + [c[1] for c in casts],
        scratch_shapes=[pltpu.VMEM((D_MODEL // LANES, tm // CHUNK * PITCH, LANES), F32),
                        pltpu.VMEM((tm, D_MODEL), BF16),
                        pltpu.VMEM((tm, S5_WIDTH), F32)],
        compiler_params=_cparams(("arbitrary",)),
        name="inproj",
    )(xp, xs, mod, norm_g.reshape(1, D_MODEL), w_in_b, bdc, bds, *later_weights)


N_PROMPT_SEG = 16
N_SAMPLE_SEG = 32
N_SAMPLE_SEQ = 2
N_SEG = N_PROMPT_SEG + N_SAMPLE_SEG
ROWS = SEG_CHUNKS * N_SEG
ROWS_P = SEG_CHUNKS * N_PROMPT_SEG
OCT_PAIRS = OCT // 2
SEG_PITCH = 56


def _cmul_add(ar, ai, hr, hi, sr, si):
    return ar * hr - ai * hi + sr, ar * hi + ai * hr + si


def _s5_kernel(x_ref, m_ref, w_ref, e_ref, apre_ref, apim_ref, h0_ref,
               y_ref, fin_ref, s_scr, hin_scr, hinp_scr, f_scr, hs_scr, y_scr):
    dot = functools.partial(jnp.dot, preferred_element_type=F32)
    parts = ((0, ROWS_P), (ROWS_P, ROWS - ROWS_P))
    seg_rows = lambda c: pl.ds(c, N_SEG, stride=PITCH)
    blk = lambda c: pl.ds(c * SEG_PITCH, N_SEG)
    seq_rows = lambda j: pl.ds(j, N_SAMPLE_SEQ, stride=SEG_CHUNKS)
    lat = lambda c: pl.ds(c * SEG_PITCH + N_PROMPT_SEG, N_SAMPLE_SEG)
    zero = jnp.zeros((N_SEG, LANES), F32)

    for pr in range(OCT_PAIRS):
        ln = slice(pr * LANES, (pr + 1) * LANES)
        g0, g1 = 2 * pr, 2 * pr + 1
        for r0, nr in parts:
            s = (dot(x_ref[g0, r0:r0 + nr, :], w_ref[pr, 0:CH, :])
                 + dot(x_ref[g1, r0:r0 + nr, :], w_ref[pr, CH:2 * CH, :]))
            for k in range(4):
                for sg in range(nr // SEG_CHUNKS):
                    p0 = (r0 // SEG_CHUNKS + sg) * PITCH
                    s_scr[k, p0:p0 + SEG_CHUNKS, :] = s[sg * SEG_CHUNKS:(sg + 1) * SEG_CHUNKS, k * LANES:(k + 1) * LANES]

        ar, ai = apre_ref[1, 0:1, ln], apim_ref[1, 0:1, ln]
        hr, hi = zero, zero
        for c in range(SEG_CHUNKS):
            hin_scr[0, blk(c), :] = hr
            hin_scr[1, blk(c), :] = hi
            hr, hi = _cmul_add(ar, ai, hr, hi, s_scr[0, seg_rows(c), :], s_scr[1, seg_rows(c), :])
        fin_ref[pr, 0] = hr[0:N_PROMPT_SEG]
        fin_ref[pr, 1] = hi[0:N_PROMPT_SEG]
        f_scr[0] = hr[N_PROMPT_SEG:]
        f_scr[1] = hi[N_PROMPT_SEG:]
        br, bi = apre_ref[1, 1:2, ln], apim_ref[1, 1:2, ln]
        gr, gi = zero, zero
        for c in range(SEG_CHUNKS - 1, -1, -1):
            hin_scr[2, blk(c), :] = gr
            hin_scr[3, blk(c), :] = gi
            gr, gi = _cmul_add(br, bi, gr, gi, s_scr[2, seg_rows(c), :], s_scr[3, seg_rows(c), :])
        fin_ref[pr, 2] = gr[0:N_PROMPT_SEG]
        fin_ref[pr, 3] = gi[0:N_PROMPT_SEG]
        f_scr[2] = gr[N_PROMPT_SEG:]
        f_scr[3] = gi[N_PROMPT_SEG:]

        a2r, a2i = apre_ref[SEG_CHUNKS, 0:1, ln], apim_ref[SEG_CHUNKS, 0:1, ln]
        hr, hi = h0_ref[0, :, ln], h0_ref[1, :, ln]
        for j in range(SEG_CHUNKS):
            hs_scr[0, seq_rows(j), :] = hr
            hs_scr[1, seq_rows(j), :] = hi
            hr, hi = _cmul_add(a2r, a2i, hr, hi, f_scr[0, seq_rows(j), :], f_scr[1, seq_rows(j), :])
        b2r, b2i = apre_ref[SEG_CHUNKS, 1:2, ln], apim_ref[SEG_CHUNKS, 1:2, ln]
        gr, gi = h0_ref[2, :, ln], h0_ref[3, :, ln]
        for j in range(SEG_CHUNKS - 1, -1, -1):
            hs_scr[2, seq_rows(j), :] = gr
            hs_scr[3, seq_rows(j), :] = gi
            gr, gi = _cmul_add(b2r, b2i, gr, gi, f_scr[2, seq_rows(j), :], f_scr[3, seq_rows(j), :])
        for c in range(SEG_CHUNKS):
            p_r, p_i = apre_ref[c, 0:1, ln], apim_ref[c, 0:1, ln]
            hr, hi = _cmul_add(p_r, p_i, hs_scr[0], hs_scr[1], hin_scr[0, lat(c), :], hin_scr[1, lat(c), :])
            hin_scr[0, lat(c), :] = hr
            hin_scr[1, lat(c), :] = hi
            cb = SEG_CHUNKS - 1 - c
            p_r, p_i = apre_ref[cb, 1:2, ln], apim_ref[cb, 1:2, ln]
            gr, gi = _cmul_add(p_r, p_i, hs_scr[2], hs_scr[3], hin_scr[2, lat(c), :], hin_scr[3, lat(c), :])
            hin_scr[2, lat(c), :] = gr
            hin_scr[3, lat(c), :] = gi

        for k in range(4):
            for sg in range(N_SEG):
                hinp_scr[sg * SEG_CHUNKS:(sg + 1) * SEG_CHUNKS, k * LANES:(k + 1) * LANES] = (
                    hin_scr[k, pl.ds(sg, SEG_CHUNKS, stride=SEG_PITCH), :].astype(BF16))
        for gi_, g in ((0, g0), (1, g1)):
            for r0, nr in parts:
                y_scr[g, r0:r0 + nr, :] = (dot(x_ref[g, r0:r0 + nr, :], m_ref[g])
                                           + dot(hinp_scr[r0:r0 + nr, :], e_ref[pr, :, gi_ * CH:(gi_ + 1) * CH]))

    rb_rows = 2 * SEG_CHUNKS
    for rb in range(ROWS // rb_rows):
        for hf in range(CHUNK // OCT):
            v = [y_scr[i, rb * rb_rows:(rb + 1) * rb_rows, hf * LANES:(hf + 1) * LANES] for i in range(OCT)]
            for s, acc in enumerate(_slot_transpose(v)):
                t = OCT * hf + s
                for sg in range(rb_rows // SEG_CHUNKS):
                    seg = rb * (rb_rows // SEG_CHUNKS) + sg
                    r0 = seg * SEG + t * SEG_CHUNKS
                    y_ref[0, r0:r0 + SEG_CHUNKS, :] = acc[sg * SEG_CHUNKS:(sg + 1) * SEG_CHUNKS]


def _s5_conv(xg, m, w, e, apre, apim, h0):
    n_oct = S5_GROUPS // OCT
    assert xg.shape[1] == ROWS
    nt = ROWS * CHUNK
    return pl.pallas_call(
        _s5_kernel,
        grid=(n_oct,),
        in_specs=[pl.BlockSpec((OCT, ROWS, CH), lambda o: (o, 0, 0)),
                  pl.BlockSpec((OCT, CH, CH), lambda o: (o, 0, 0)),
                  pl.BlockSpec((OCT_PAIRS, 2 * CH, 4 * LANES), lambda o: (o, 0, 0)),
                  pl.BlockSpec((OCT_PAIRS, 4 * LANES, 2 * CH), lambda o: (o, 0, 0)),
                  pl.BlockSpec((SEG_CHUNKS + 1, 2, OCT_PAIRS * LANES), lambda o: (0, 0, o)),
                  pl.BlockSpec((SEG_CHUNKS + 1, 2, OCT_PAIRS * LANES), lambda o: (0, 0, o)),
                  pl.BlockSpec((4, N_SAMPLE_SEQ, OCT_PAIRS * LANES), lambda o: (0, 0, o))],
        out_specs=[pl.BlockSpec((1, nt, LANES), lambda o: (o, 0, 0)),
                   pl.BlockSpec((OCT_PAIRS, 4, N_PROMPT_SEG, LANES), lambda o: (o, 0, 0, 0))],
        out_shape=[jax.ShapeDtypeStruct((n_oct, nt, LANES), F32),
                   jax.ShapeDtypeStruct((PAIRS, 4, N_PROMPT_SEG, LANES), F32)],
        scratch_shapes=[pltpu.VMEM((4, N_SEG * PITCH, LANES), F32),
                        pltpu.VMEM((4, SEG_CHUNKS * SEG_PITCH, LANES), F32),
                        pltpu.VMEM((ROWS, 4 * LANES), BF16),
                        pltpu.VMEM((4, N_SAMPLE_SEG, LANES), F32),
                        pltpu.VMEM((4, N_SAMPLE_SEG, LANES), F32),
                        pltpu.VMEM((OCT, ROWS, CH), F32)],
        compiler_params=_cparams(("parallel",)),
        name="s5_conv",
    )(xg, m, w, e, apre, apim, h0)


def _tile_order_positions(l):
    r = np.arange(l)
    tile, wi = r // SEG, r % SEG
    return tile * SEG + (wi % SEG_CHUNKS) * CHUNK + wi // SEG_CHUNKS


def _tile_dft_tables(scale, rows_in_tile_order=True):
    pos = _tile_order_positions(SEG)
    freq = pos if rows_in_tile_order else np.arange(SEG)
    ang = 2.0 * np.pi * ((freq[:, None] * pos[None, :]) % SEG) / SEG
    return jnp.asarray(np.cos(ang) * scale, F32), jnp.asarray(np.sin(ang) * scale, F32)


def _pos_dft_tile_kernel(cm_ref, sm_ref, fc_ref, fs_ref, o_ref):
    dot = functools.partial(jnp.dot, preferred_element_type=F32)
    cm, sm = cm_ref[...].astype(BF16), sm_ref[...].astype(BF16)
    for b in range(fc_ref.shape[0]):
        o_ref[b] = (dot(cm, fc_ref[b]) - dot(sm, fs_ref[b])).astype(BF16)


def _pos_dft_tile(fc, fs, n):
    _, l, wdt = fc.shape
    cm, sm = _tile_dft_tables(1.0 / math.sqrt(l))
    nb = 4
    seq = pl.BlockSpec((nb, l, wdt), lambda b: (b, 0, 0))
    return pl.pallas_call(
        _pos_dft_tile_kernel,
        grid=(n // nb,),
        in_specs=[_const_spec((l, l)), _const_spec((l, l)), seq, seq],
        out_specs=seq,
        out_shape=jax.ShapeDtypeStruct((n, l, wdt), BF16),
        compiler_params=_cparams(("parallel",)),
        name="pos_dft_tile",
    )(cm, sm, fc, fs)


N_TILES = 16


def _fft16(xr, xi):
    n = len(xr)
    rev = [int(format(i, "04b")[::-1], 2) for i in range(n)]
    ar = [xr[r] for r in rev]
    ai = [xi[r] for r in rev]
    size = 2
    while size <= n:
        half = size // 2
        for start in range(0, n, size):
            for k in range(half):
                wr = math.cos(2.0 * math.pi * k / size)
                wi = -math.sin(2.0 * math.pi * k / size)
                i0, i1 = start + k, start + k + half
                if k == 0:
                    tr, ti = ar[i1], ai[i1]
                elif 4 * k == size:
                    tr, ti = ai[i1], -ar[i1]
                else:
                    tr = ar[i1] * wr - ai[i1] * wi
                    ti = ar[i1] * wi + ai[i1] * wr
                ar[i1], ai[i1] = ar[i0] - tr, ai[i0] - ti
                ar[i0], ai[i0] = ar[i0] + tr, ai[i0] + ti
        size *= 2
    return ar, ai


def _pos_dft_long_kernel(cb_ref, sb_ref, ca_ref, sa_ref, fc_ref, fs_ref, o_ref, a_scr):
    wdt = fc_ref.shape[2]

    def rows_fft(r, carry):
        rows = pl.ds(pl.multiple_of(r * CHUNK, CHUNK), CHUNK)
        for b in range(wdt // LANES):
            lns = slice(b * LANES, (b + 1) * LANES)
            xr = [fc_ref[j, rows, lns].astype(F32) for j in range(N_TILES)]
            xi = [-fs_ref[j, rows, lns].astype(F32) for j in range(N_TILES)]
            ar, ai = _fft16(xr, xi)
            for k in range(N_TILES):
                a_scr[k, 0, rows, lns] = ar[k].astype(BF16)
                a_scr[k, 1, rows, lns] = ai[k].astype(BF16)
        return carry

    lax.fori_loop(0, SEG // CHUNK, rows_fft, 0)

    dot = functools.partial(jnp.dot, preferred_element_type=F32)
    cb, sb = cb_ref[...], sb_ref[...]
    for k1 in range(N_TILES):
        ca, sa = ca_ref[k1:k1 + 1, :], sa_ref[k1:k1 + 1, :]
        dc = (cb * ca - sb * sa).astype(BF16)
        ds = (sb * ca + cb * sa).astype(BF16)
        out = dot(dc, a_scr[k1, 0]) + dot(ds, a_scr[k1, 1])
        o_ref[:, k1 * SEG_CHUNKS:(k1 + 1) * SEG_CHUNKS, :] = out.astype(BF16).reshape(N_TILES, SEG_CHUNKS, wdt)


def _pos_dft_long(fc, fs, first, n):
    wdt = fc.shape[-1]
    l = N_TILES * SEG
    wb = 2 * LANES
    cb, sb = _tile_dft_tables(1.0 / math.sqrt(l), rows_in_tile_order=False)
    pos = _tile_order_positions(SEG)
    ang = 2.0 * np.pi * (np.arange(N_TILES)[:, None] * pos[None, :]) / l
    ca, sa = jnp.asarray(np.cos(ang), F32), jnp.asarray(np.sin(ang), F32)
    const2 = lambda shape: pl.BlockSpec(shape, lambda b, k: (0, 0), pipeline_mode=pl.Buffered(1))
    seq_in = pl.BlockSpec((None, N_TILES, SEG, wb), lambda b, k: (first + b, 0, 0, k))
    seq_out = pl.BlockSpec((None, N_TILES, SEG, wb), lambda b, k: (b, 0, 0, k))
    out = pl.pallas_call(
        _pos_dft_long_kernel,
        grid=(n, wdt // wb),
        in_specs=[const2((SEG, SEG)), const2((SEG, SEG)), const2((N_TILES, SEG)), const2((N_TILES, SEG)),
                  seq_in, seq_in],
        out_specs=seq_out,
        out_shape=jax.ShapeDtypeStruct((n, N_TILES, SEG, wdt), BF16),
        scratch_shapes=[pltpu.VMEM((N_TILES, 2, SEG, wb), BF16)],
        compiler_params=_cparams(("parallel", "parallel")),
        name="pos_dft_long",
    )(cb, sb, ca, sa, fc, fs)
    return out.reshape(n * l, wdt)


def _mix_ffn_kernel(n_prompt_tiles, xp_ref, xs_ref, mod_ref, ys_ref, yfp_ref, yfs_ref, gs_ref, gf_ref,
                    wglu_ref, bglu_ref, wps_ref, wpf_ref, wout_ref, n2_ref,
                    wg_ref, wu_ref, wd_ref, fn_ref, op_ref, os_ref, m_scr, mb_scr):
    tm = xp_ref.shape[0]
    dot = functools.partial(jnp.dot, preferred_element_type=F32)
    mod = mod_ref[0]
    is_prompt = pl.program_id(0) < n_prompt_tiles
    x = jnp.where(is_prompt, xp_ref[...], xs_ref[...])
    yf = jnp.where(is_prompt, yfp_ref[...], yfs_ref[...])
    y = jnp.concatenate([ys_ref[b] for b in range(S5_WIDTH // LANES)], axis=1)
    z = jax.nn.gelu(y)
    z = z * jax.nn.sigmoid(dot(z.astype(BF16), wglu_ref[...]) + bglu_ref[...])
    m = (gs_ref[...].astype(F32) * dot(z.astype(BF16), wps_ref[...])
         + gf_ref[...].astype(F32) * dot(yf, wpf_ref[...]))
    for k in range(D_MODEL // LANES):
        for s in range(tm // SEG_CHUNKS):
            m_scr[k, s * PITCH:s * PITCH + SEG_CHUNKS, :] = (
                m[s * SEG_CHUNKS:(s + 1) * SEG_CHUNKS, k * LANES:(k + 1) * LANES])
    for j in range(tm // SEG):
        for c in range(SEG_CHUNKS):
            r0 = j * SEG + c * CHUNK
            for k in range(D_MODEL // LANES):
                mb_scr[r0:r0 + CHUNK, k * LANES:(k + 1) * LANES] = (
                    m_scr[k, pl.ds(j * CHUNK * PITCH + c, CHUNK, stride=PITCH), :].astype(BF16))
    x1 = x + mod[2:3] * dot(mb_scr[...], wout_ref[...])
    h2 = (_rms(x1, n2_ref[...]) * (1.0 + mod[4:5]) + mod[3:4]).astype(BF16)
    ff = None
    for n0, n1 in zip(FF_SPLITS[:-1], FF_SPLITS[1:]):
        gate = dot(h2, wg_ref[:, n0:n1])
        up = dot(h2, wu_ref[:, n0:n1])
        part = dot((gate * jax.nn.sigmoid(gate) * up).astype(BF16), wd_ref[n0:n1, :])
        ff = part if ff is None else ff + part
    x2 = x1 + mod[5:6] * ff
    res = _rms(x2, fn_ref[...])

    @pl.when(is_prompt)
    def _():
        op_ref[...] = res

    @pl.when(jnp.logical_not(is_prompt))
    def _():
        os_ref[...] = res


def _mix_ffn(xp, xs, seq_len, mod, ys, yf_p, yf_s, gs, gf, wts, tm):
    tp, ts = xp.shape[0], xs.shape[0]
    n_p = tp // tm
    tok = lambda w: pl.BlockSpec((tm, w), lambda i: (i, 0))
    return pl.pallas_call(
        functools.partial(_mix_ffn_kernel, n_p),
        grid=((tp + ts) // tm,),
        in_specs=_two_part_specs(tm, D_MODEL, n_p)
                 + [_mod_spec(n_p, seq_len // tm),
                    pl.BlockSpec((S5_WIDTH // LANES, tm, LANES), lambda i: (0, i, 0))]
                 + _two_part_specs(tm, FFT_WIDTH, n_p)
                 + [tok(D_MODEL), tok(D_MODEL)]
                 + [_const_spec(w.shape) for w in wts],
        out_specs=_two_part_specs(tm, D_MODEL, n_p),
        out_shape=[jax.ShapeDtypeStruct((tp, D_MODEL), F32), jax.ShapeDtypeStruct((ts, D_MODEL), F32)],
        scratch_shapes=[pltpu.VMEM((D_MODEL // LANES, tm // CHUNK * PITCH, LANES), F32),
                        pltpu.VMEM((tm, D_MODEL), BF16)],
        compiler_params=_cparams(("arbitrary",)),
        name="mix_ffn",
    )(xp, xs, mod, ys, yf_p, yf_s, gs, gf, *wts)


def kernel(x_prompt, x_sample, state_s5, c, c_ctx, norm1_g, norm2_g, w_ada, b_ada, w_in,
           s5_lambda_re, s5_lambda_im, s5_log_step, s5_b_re, s5_b_im, s5_c_re, s5_c_im,
           s5_d, w_glu, b_glu, w_proj_s5, w_proj_fft, w_out, w_ffn_gate, w_ffn_up,
           w_ffn_down, final_norm_g):
    nb, sl, _ = x_prompt.shape
    db, dl, _ = x_sample.shape
    assert w_in.shape[0] == 1 and sl == SEG and nb == N_PROMPT_SEG
    assert db == N_SAMPLE_SEQ and dl == SEG * SEG_CHUNKS

    cvec = jnp.concatenate([c_ctx[None], c, jnp.zeros((8 - 1 - db, D_MODEL), F32)], axis=0)
    mod, w_in_b = _adaln(cvec, w_ada[0], b_ada[0], w_in[0])
    mod = mod.reshape(8, N_MOD, D_MODEL)

    m, w, e, apre, apim = _s5_tables(s5_lambda_re[0], s5_lambda_im[0], s5_log_step[0], s5_b_re[0],
                                     s5_b_im[0], s5_c_re[0], s5_c_im[0], s5_d[0])

    tm = 512
    xp = x_prompt.reshape(nb * sl, D_MODEL)
    xs = x_sample.reshape(db * dl, D_MODEL)
    later = (w_glu[0], w_proj_s5[0], w_proj_fft[0], w_out[0], w_ffn_gate[0], w_ffn_up[0], w_ffn_down[0])
    xg, fc, fs, gs, gf, *later_b = _inproj(xp, xs, dl, mod, norm1_g[0], w_in_b, later, tm)
    wglu_b, wps_b, wpf_b, wout_b, wg_b, wu_b, wd_b = later_b

    h0 = state_s5[:, 0].astype(F32).transpose(1, 2, 0, 3, 4).reshape(4, db, GP)
    ys, fin = _s5_conv(xg, m, w, e, apre, apim, h0)
    new_state = fin.reshape(PAIRS, 2, 2, nb, 2, S5_STATE).transpose(3, 1, 2, 0, 4, 5)
    new_state = new_state.reshape(nb, 1, 2, 2, S5_GROUPS, S5_STATE)

    yf_p = _pos_dft_tile(fc.reshape(-1, SEG, FFT_WIDTH), fs.reshape(-1, SEG, FFT_WIDTH), nb)
    n_prompt_units = nb // N_TILES
    yf_s = _pos_dft_long(fc.reshape(-1, N_TILES, SEG, FFT_WIDTH), fs.reshape(-1, N_TILES, SEG, FFT_WIDTH),
                         n_prompt_units, db)

    wts = (wglu_b, b_glu[0].reshape(1, S5_WIDTH), wps_b, wpf_b, wout_b, norm2_g[0].reshape(1, D_MODEL),
           wg_b, wu_b, wd_b, final_norm_g.reshape(1, D_MODEL))
    tm2 = 2 * SEG
    y_p, y_s = _mix_ffn(xp, xs, dl, mod, ys, yf_p.reshape(nb * sl, FFT_WIDTH), yf_s, gs, gf, wts, tm2)
    return (y_p.reshape(nb, sl, D_MODEL), y_s.reshape(db, dl, D_MODEL), new_state)
```

```python
import functools
import math

import numpy as np
import jax
import jax.numpy as jnp
from jax import lax
from jax.experimental import pallas as pl
from jax.experimental.pallas import tpu as pltpu

F32 = jnp.float32
BF16 = jnp.bfloat16

D_MODEL = 1024
S5_WIDTH = 512
S5_GROUPS = 32
S5_GROUP = 16
S5_STATE = 64
FFT_WIDTH = 512
FFT_GROUPS = 4
FFT_GROUP = 128
D_FF = 2816
N_MOD = 6
EPS = 1e-6

LANES = 128
CHUNK = 16
SEG_CHUNKS = 16
SEG = CHUNK * SEG_CHUNKS
GP = S5_GROUPS * S5_STATE
PAIRS = S5_GROUPS // 2
CH = CHUNK * S5_GROUP
OCT = LANES // S5_GROUP
PITCH = 24

MXU_DIM = 256
FF_SPLITS = (0, 3 * MXU_DIM, 6 * MXU_DIM, 9 * MXU_DIM, D_FF)
VMEM_LIMIT = 58 * 1024 * 1024


def _cparams(sem):
    return pltpu.CompilerParams(dimension_semantics=sem, vmem_limit_bytes=VMEM_LIMIT)


def _row_block_cast_specs(w, n_steps):
    rows = w.shape[0] // n_steps
    assert rows * n_steps == w.shape[0] and rows % 16 == 0, (w.shape, n_steps)
    spec = pl.BlockSpec((rows, w.shape[1]), lambda i: (jnp.minimum(i, n_steps - 1), 0))
    return spec, jax.ShapeDtypeStruct(w.shape, BF16)


def _cast_blocks(in_refs, out_refs):
    for src, dst in zip(in_refs, out_refs):
        dst[...] = src[...].astype(BF16)


def _ada_kernel(c_ref, w_ref, b_ref, win_ref, o_ref, winb_ref):
    _cast_blocks([win_ref], [winb_ref])
    dot = functools.partial(jnp.dot, preferred_element_type=F32)
    c = c_ref[...]
    s = c * jax.nn.sigmoid(c)
    s_hi = s.astype(BF16)
    s_lo = (s - s_hi.astype(F32)).astype(BF16)
    w = w_ref[...]
    w_hi = w.astype(BF16)
    w_lo = (w - w_hi.astype(F32)).astype(BF16)
    r = dot(jnp.concatenate([s_hi, s_lo], axis=0), w_hi)
    n = s.shape[0]
    o_ref[...] = r[0:n] + r[n:2 * n] + dot(s_hi, w_lo) + b_ref[...]


def _adaln(cvec8, w_ada, b_ada, w_in):
    n_out = w_ada.shape[1]
    tn = 768
    n_steps = n_out // tn
    cast_spec, cast_shape = _row_block_cast_specs(w_in, n_steps)
    return pl.pallas_call(
        _ada_kernel,
        grid=(n_steps,),
        in_specs=[pl.BlockSpec((8, D_MODEL), lambda j: (0, 0)),
                  pl.BlockSpec((D_MODEL, tn), lambda j: (0, j)),
                  pl.BlockSpec((1, tn), lambda j: (0, j)),
                  cast_spec],
        out_specs=[pl.BlockSpec((8, tn), lambda j: (0, j)), cast_spec],
        out_shape=[jax.ShapeDtypeStruct((8, n_out), F32), cast_shape],
        compiler_params=_cparams(("arbitrary",)),
        name="adaln",
    )(cvec8, w_ada, b_ada.reshape(1, n_out), w_in)


def _prep_kernel(lre_ref, lim_ref, lst_ref, bre_ref, bim_ref, cre_ref, cim_ref, dt_ref,
                 apre_ref, apim_ref, m_ref, w_ref, e_ref, gb_scr, ca_scr, ts_scr):
    lre = lre_ref[...]
    lim = lim_ref[...]
    step = jnp.exp(lst_ref[...])
    mag = jnp.exp(lre * step)
    are = mag * jnp.cos(lim * step)
    aim = mag * jnp.sin(lim * step)
    nr = are - 1.0
    den = lre * lre + lim * lim
    fr = (nr * lre + aim * lim) / den
    fi = (aim * lre - nr * lim) / den

    pr = [jnp.ones_like(are)]
    pi = [jnp.zeros_like(are)]
    for _ in range(CHUNK):
        r, i = pr[-1], pi[-1]
        pr.append(r * are - i * aim)
        pi.append(r * aim + i * are)
    a16r, a16i = pr[CHUNK], pi[CHUNK]
    qr, qi = jnp.ones_like(are), jnp.zeros_like(are)
    for c in range(SEG_CHUNKS + 1):
        apre_ref[c] = qr
        apim_ref[c] = qi
        qr, qi = qr * a16r - qi * a16i, qr * a16i + qi * a16r

    for d in range(2):
        bbre = fr[d:d + 1] * bre_ref[d] - fi[d:d + 1] * bim_ref[d]
        bbim = fr[d:d + 1] * bim_ref[d] + fi[d:d + 1] * bre_ref[d]
        cre, cim = cre_ref[d], cim_ref[d]
        for s in range(CHUNK):
            k = CHUNK - 1 - s if d == 0 else s
            r, i = pr[k][d:d + 1], pi[k][d:d + 1]
            gb_scr[0, d, s] = r * bbre - i * bbim
            gb_scr[1, d, s] = r * bbim + i * bbre
            f = s + 1 if d == 0 else CHUNK - s
            r, i = pr[f][d:d + 1], pi[f][d:d + 1]
            ca_scr[0, d, s] = r * cre - i * cim
            ca_scr[1, d, s] = -(r * cim + i * cre)

    dot_nt = lambda a, b: lax.dot_general(a, b, (((1,), (1,)), ((), ())), preferred_element_type=F32)
    lane_gi = lax.broadcasted_iota(jnp.int32, (CH, LANES), 1) // S5_STATE
    row_gi = lax.broadcasted_iota(jnp.int32, (LANES, CH), 0) // S5_STATE
    eye = (lax.broadcasted_iota(jnp.int32, (LANES, LANES), 0)
           == lax.broadcasted_iota(jnp.int32, (LANES, LANES), 1)).astype(BF16)
    slot = lax.broadcasted_iota(jnp.int32, (CH, LANES), 1) // S5_GROUP
    n_lag_rows = (2 * CHUNK - 1) * S5_GROUP

    def pair(q, carry):
        lanes = pl.ds(pl.multiple_of(q * LANES, LANES), LANES)
        for d in range(2):
            for ri in range(2):
                col = slice((2 * d + ri) * LANES, (2 * d + ri + 1) * LANES)
                gb = gb_scr[ri, d, :, :, lanes].reshape(CH, LANES)
                ca = ca_scr[ri, d, :, :, lanes].reshape(CH, LANES).astype(BF16)
                ca_t = dot_nt(eye, ca)
                for gi in range(2):
                    w_ref[q, gi * CH:(gi + 1) * CH, col] = jnp.where(lane_gi == gi, gb, 0.0).astype(BF16)
                    e_ref[q, col, gi * CH:(gi + 1) * CH] = jnp.where(row_gi == gi, ca_t, 0.0).astype(BF16)
        for gi in range(2):
            g = 2 * q + gi
            lag = []
            for d in range(2):
                c_re = jnp.concatenate([cre_ref[d, :, lanes]] * OCT, axis=0)
                c_imn = jnp.concatenate([-cim_ref[d, :, lanes]] * OCT, axis=0)
                keep = lax.broadcasted_iota(jnp.int32, (LANES, LANES), 1) // S5_STATE == gi
                c_re = jnp.where(keep, c_re, 0.0).astype(BF16)
                c_imn = jnp.where(keep, c_imn, 0.0).astype(BF16)
                gre = gb_scr[0, d, :, :, lanes].reshape(CH, LANES).astype(BF16)
                gim = gb_scr[1, d, :, :, lanes].reshape(CH, LANES).astype(BF16)
                lag.append(dot_nt(gre, c_re) + dot_nt(gim, c_imn))
            zl = (CHUNK - 1) * S5_GROUP
            ts_scr[0:zl, :] = lag[0][0:zl]
            ts_scr[zl:zl + S5_GROUP, :] = lag[0][zl:] + lag[1][0:S5_GROUP] + dt_ref[g]
            ts_scr[zl + S5_GROUP:n_lag_rows, :] = lag[1][S5_GROUP:]
            for hf in range(CHUNK // OCT):
                acc = None
                for s in range(OCT):
                    t = OCT * hf + s
                    win = ts_scr[(CHUNK - 1 - t) * S5_GROUP:(CHUNK - 1 - t) * S5_GROUP + CH, :]
                    acc = win if acc is None else jnp.where(slot == s, win, acc)
                m_ref[g, :, hf * LANES:(hf + 1) * LANES] = acc.astype(BF16)
        return carry

    lax.fori_loop(0, PAIRS, pair, 0)


def _s5_tables(lam_re, lam_im, log_step, b_re, b_im, c_re, c_im, s5_d):
    lre = lam_re.reshape(2, GP)
    lim = lam_im.reshape(2, GP)
    lst = jnp.repeat(log_step, S5_STATE, axis=-1)
    bre = b_re.transpose(0, 3, 1, 2).reshape(2, S5_GROUP, GP)
    bim = b_im.transpose(0, 3, 1, 2).reshape(2, S5_GROUP, GP)
    cre = c_re.transpose(0, 2, 1, 3).reshape(2, S5_GROUP, GP)
    cim = c_im.transpose(0, 2, 1, 3).reshape(2, S5_GROUP, GP)
    eye = jnp.asarray(np.eye(S5_GROUP, dtype=np.float32))
    dt = jnp.tile(eye[None] * s5_d.reshape(S5_GROUPS, 1, S5_GROUP), (1, 1, OCT))

    vm = pl.BlockSpec(memory_space=pltpu.VMEM)
    tab = (2, 2, CHUNK, S5_GROUP, GP)
    apre, apim, m, w, e = pl.pallas_call(
        _prep_kernel,
        in_specs=[vm] * 8,
        out_specs=[vm] * 5,
        out_shape=[jax.ShapeDtypeStruct((SEG_CHUNKS + 1, 2, GP), F32),
                   jax.ShapeDtypeStruct((SEG_CHUNKS + 1, 2, GP), F32),
                   jax.ShapeDtypeStruct((S5_GROUPS, CH, CH), BF16),
                   jax.ShapeDtypeStruct((PAIRS, 2 * CH, 4 * LANES), BF16),
                   jax.ShapeDtypeStruct((PAIRS, 4 * LANES, 2 * CH), BF16)],
        scratch_shapes=[pltpu.VMEM(tab, F32),
                        pltpu.VMEM(tab, F32),
                        pltpu.VMEM(((2 * CHUNK - 1) * S5_GROUP, LANES), F32)],
        compiler_params=pltpu.CompilerParams(vmem_limit_bytes=VMEM_LIMIT),
        name="s5_tables",
    )(lre, lim, lst, bre, bim, cre, cim, dt)
    return m, w, e, apre, apim


def _rms(x, g):
    return x * lax.rsqrt(jnp.mean(x * x, axis=-1, keepdims=True) + EPS) * g


def _slot_transpose(v):
    v = list(v)
    slot = lax.broadcasted_iota(jnp.int32, v[0].shape, 1) // S5_GROUP
    for k in (4, 2, 1):
        low = (slot & k) == 0
        for i in range(OCT):
            if i & k:
                continue
            a, b = v[i], v[i + k]
            v[i] = jnp.where(low, a, pltpu.roll(b, k * S5_GROUP, axis=1))
            v[i + k] = jnp.where(low, pltpu.roll(a, LANES - k * S5_GROUP, axis=1), b)
    return v


def _inproj_kernel(n_prompt_tiles, n_cast, xp_ref, xs_ref, mod_ref, g_ref, win_ref, bdc_ref, bds_ref,
                   *rest):
    cast_in, rest = rest[:n_cast], rest[n_cast:]
    xg_ref, fc_ref, fs_ref, gs_ref, gf_ref = rest[:5]
    cast_out, (h_scr, hb_scr, zs_scr) = rest[5:5 + n_cast], rest[5 + n_cast:]
    tm = xp_ref.shape[0]
    mod = mod_ref[0]
    is_prompt = pl.program_id(0) < n_prompt_tiles
    dot = functools.partial(jnp.dot, preferred_element_type=F32)
    o = 2 * S5_WIDTH
    tiles = [slice(j * SEG, (j + 1) * SEG) for j in range(tm // SEG)]

    def norm_phase(j, r):
        x = jnp.where(is_prompt, xp_ref[r, :], xs_ref[r, :])
        h = _rms(x, g_ref[...]) * (1.0 + mod[1:2]) + mod[0:1]
        for k in range(D_MODEL // LANES):
            for c in range(SEG_CHUNKS):
                p0 = (j * SEG_CHUNKS + c) * PITCH
                h_scr[k, p0:p0 + CHUNK, :] = h[c * CHUNK:(c + 1) * CHUNK, k * LANES:(k + 1) * LANES]
        for s in range(CHUNK):
            r0 = j * SEG + s * SEG_CHUNKS
            for k in range(D_MODEL // LANES):
                hb_scr[r0:r0 + SEG_CHUNKS, k * LANES:(k + 1) * LANES] = (
                    h_scr[k, pl.ds(j * SEG_CHUNKS * PITCH + s, SEG_CHUNKS, stride=PITCH), :].astype(BF16))

    def s5_gate_phase(j, r):
        zs_scr[r, :] = dot(hb_scr[r, :], win_ref[:, 0:S5_WIDTH])
        gs_ref[r, :] = jax.nn.sigmoid(dot(hb_scr[r, :], win_ref[:, o:o + D_MODEL])).astype(BF16)

    def fold_phase(j, r):
        for b in range(S5_WIDTH // LANES):
            for hf in range(CHUNK // OCT):
                z = [zs_scr[j * SEG + s * SEG_CHUNKS:j * SEG + (s + 1) * SEG_CHUNKS, b * LANES:(b + 1) * LANES]
                     for s in range(OCT * hf, OCT * (hf + 1))]
                for i, xi in enumerate(_slot_transpose(z)):
                    xg_ref[OCT * b + i, j * SEG_CHUNKS:(j + 1) * SEG_CHUNKS,
                           hf * LANES:(hf + 1) * LANES] = xi.astype(BF16)

    def fourier_gate_phase(j, r):
        uf = dot(hb_scr[r, :], win_ref[:, S5_WIDTH:2 * S5_WIDTH]).astype(BF16)
        for n0 in range(0, FFT_WIDTH, MXU_DIM):
            cols = slice(n0, n0 + MXU_DIM)
            fc_ref[r, cols] = dot(uf[:, cols], bdc_ref[cols, cols]).astype(BF16)
            fs_ref[r, cols] = dot(uf[:, cols], bds_ref[cols, cols]).astype(BF16)
        gf_ref[r, :] = jax.nn.sigmoid(dot(hb_scr[r, :], win_ref[:, o + D_MODEL:o + 2 * D_MODEL])).astype(BF16)

    for phase in (norm_phase, s5_gate_phase, fold_phase, fourier_gate_phase):
        for j, r in enumerate(tiles):
            phase(j, r)
    _cast_blocks(cast_in, cast_out)


def _channel_dft_mats():
    j = np.arange(FFT_GROUP)
    ang = 2.0 * np.pi * ((j[:, None] * j[None, :]) % FFT_GROUP) / FFT_GROUP
    blk_c = np.cos(ang) / math.sqrt(FFT_GROUP)
    blk_s = np.sin(ang) / math.sqrt(FFT_GROUP)
    bdc = np.kron(np.eye(FFT_GROUPS), blk_c)
    bds = np.kron(np.eye(FFT_GROUPS), blk_s)
    return jnp.asarray(bdc, F32).astype(BF16), jnp.asarray(bds, F32).astype(BF16)


def _const_spec(shape):
    nd = len(shape)
    return pl.BlockSpec(shape, lambda i: (0,) * nd, pipeline_mode=pl.Buffered(1))


def _two_part_specs(tm, width, n_prompt_tiles):
    return [pl.BlockSpec((tm, width), lambda i: (jnp.minimum(i, n_prompt_tiles - 1), 0)),
            pl.BlockSpec((tm, width), lambda i: (jnp.maximum(i - n_prompt_tiles, 0), 0))]


def _mod_spec(n_prompt_tiles, tiles_per_seq):
    row = lambda i: jnp.where(i < n_prompt_tiles, 0, 1 + (i - n_prompt_tiles) // tiles_per_seq)
    return pl.BlockSpec((1, N_MOD, D_MODEL), lambda i: (row(i), 0, 0))


N_CAST_STEPS = 16


def _inproj(xp, xs, seq_len, mod, norm_g, w_in_b, later_weights, tm):
    t = xp.shape[0] + xs.shape[0]
    n_p = xp.shape[0] // tm
    assert t // tm >= N_CAST_STEPS
    bdc, bds = _channel_dft_mats()
    casts = [_row_block_cast_specs(w, N_CAST_STEPS) for w in later_weights]
    tok = lambda w: pl.BlockSpec((tm, w), lambda i: (i, 0))
    out = lambda w: jax.ShapeDtypeStruct((t, w), BF16)
    return pl.pallas_call(
        functools.partial(_inproj_kernel, n_p, len(casts)),
        grid=(t // tm,),
        in_specs=_two_part_specs(tm, D_MODEL, n_p) + [
                  _mod_spec(n_p, seq_len // tm),
                  _const_spec((1, D_MODEL)),
                  _const_spec(w_in_b.shape),
                  _const_spec(bdc.shape),
                  _const_spec(bds.shape)] + [c[0] for c in casts],
        out_specs=[pl.BlockSpec((S5_GROUPS, tm // CHUNK, CH), lambda i: (0, i, 0)),
                   tok(FFT_WIDTH), tok(FFT_WIDTH), tok(D_MODEL), tok(D_MODEL)] + [c[0] for c in casts],
        out_shape=[jax.ShapeDtypeStruct((S5_GROUPS, t // CHUNK, CH), BF16),
                   out(FFT_WIDTH), out(FFT_WIDTH), out(D_MODEL), out(D_MODEL)] + [c[1] for c in casts],
        scratch_shapes=[pltpu.VMEM((D_MODEL // LANES, tm // CHUNK * PITCH, LANES), F32),
                        pltpu.VMEM((tm, D_MODEL), BF16),
                        pltpu.VMEM((tm, S5_WIDTH), F32)],
        compiler_params=_cparams(("arbitrary",)),
        name="inproj",
    )(xp, xs, mod, norm_g.reshape(1, D_MODEL), w_in_b, bdc, bds, *later_weights)


N_PROMPT_SEG = 16
N_SAMPLE_SEG = 32
N_SAMPLE_SEQ = 2
N_SEG = N_PROMPT_SEG + N_SAMPLE_SEG
ROWS = SEG_CHUNKS * N_SEG
ROWS_P = SEG_CHUNKS * N_PROMPT_SEG
OCT_PAIRS = OCT // 2
SEG_PITCH = 56


def _cmul_add(ar, ai, hr, hi, sr, si):
    return ar * hr - ai * hi + sr, ar * hi + ai * hr + si


def _s5_kernel(x_ref, m_ref, w_ref, e_ref, apre_ref, apim_ref, h0_ref,
               y_ref, fin_ref, s_scr, hin_scr, hinp_scr, f_scr, hs_scr, y_scr):
    dot = functools.partial(jnp.dot, preferred_element_type=F32)
    parts = ((0, ROWS_P), (ROWS_P, ROWS - ROWS_P))
    seg_rows = lambda c: pl.ds(c, N_SEG, stride=PITCH)
    blk = lambda c: pl.ds(c * SEG_PITCH, N_SEG)
    seq_rows = lambda j: pl.ds(j, N_SAMPLE_SEQ, stride=SEG_CHUNKS)
    lat = lambda c: pl.ds(c * SEG_PITCH + N_PROMPT_SEG, N_SAMPLE_SEG)
    zero = jnp.zeros((N_SEG, LANES), F32)

    for pr in range(OCT_PAIRS):
        ln = slice(pr * LANES, (pr + 1) * LANES)
        g0, g1 = 2 * pr, 2 * pr + 1
        for r0, nr in parts:
            s = (dot(x_ref[g0, r0:r0 + nr, :], w_ref[pr, 0:CH, :])
                 + dot(x_ref[g1, r0:r0 + nr, :], w_ref[pr, CH:2 * CH, :]))
            for k in range(4):
                for sg in range(nr // SEG_CHUNKS):
                    p0 = (r0 // SEG_CHUNKS + sg) * PITCH
                    s_scr[k, p0:p0 + SEG_CHUNKS, :] = s[sg * SEG_CHUNKS:(sg + 1) * SEG_CHUNKS, k * LANES:(k + 1) * LANES]

        ar, ai = apre_ref[1, 0:1, ln], apim_ref[1, 0:1, ln]
        hr, hi = zero, zero
        for c in range(SEG_CHUNKS):
            hin_scr[0, blk(c), :] = hr
            hin_scr[1, blk(c), :] = hi
            hr, hi = _cmul_add(ar, ai, hr, hi, s_scr[0, seg_rows(c), :], s_scr[1, seg_rows(c), :])
        fin_ref[pr, 0] = hr[0:N_PROMPT_SEG]
        fin_ref[pr, 1] = hi[0:N_PROMPT_SEG]
        f_scr[0] = hr[N_PROMPT_SEG:]
        f_scr[1] = hi[N_PROMPT_SEG:]
        br, bi = apre_ref[1, 1:2, ln], apim_ref[1, 1:2, ln]
        gr, gi = zero, zero
        for c in range(SEG_CHUNKS - 1, -1, -1):
            hin_scr[2, blk(c), :] = gr
            hin_scr[3, blk(c), :] = gi
            gr, gi = _cmul_add(br, bi, gr, gi, s_scr[2, seg_rows(c), :], s_scr[3, seg_rows(c), :])
        fin_ref[pr, 2] = gr[0:N_PROMPT_SEG]
        fin_ref[pr, 3] = gi[0:N_PROMPT_SEG]
        f_scr[2] = gr[N_PROMPT_SEG:]
        f_scr[3] = gi[N_PROMPT_SEG:]

        a2r, a2i = apre_ref[SEG_CHUNKS, 0:1, ln], apim_ref[SEG_CHUNKS, 0:1, ln]
        hr, hi = h0_ref[0, :, ln], h0_ref[1, :, ln]
        for j in range(SEG_CHUNKS):
            hs_scr[0, seq_rows(j), :] = hr
            hs_scr[1, seq_rows(j), :] = hi
            hr, hi = _cmul_add(a2r, a2i, hr, hi, f_scr[0, seq_rows(j), :], f_scr[1, seq_rows(j), :])
        b2r, b2i = apre_ref[SEG_CHUNKS, 1:2, ln], apim_ref[SEG_CHUNKS, 1:2, ln]
        gr, gi = h0_ref[2, :, ln], h0_ref[3, :, ln]
        for j in range(SEG_CHUNKS - 1, -1, -1):
            hs_scr[2, seq_rows(j), :] = gr
            hs_scr[3, seq_rows(j), :] = gi
            gr, gi = _cmul_add(b2r, b2i, gr, gi, f_scr[2, seq_rows(j), :], f_scr[3, seq_rows(j), :])
        for c in range(SEG_CHUNKS):
            p_r, p_i = apre_ref[c, 0:1, ln], apim_ref[c, 0:1, ln]
            hr, hi = _cmul_add(p_r, p_i, hs_scr[0], hs_scr[1], hin_scr[0, lat(c), :], hin_scr[1, lat(c), :])
            hin_scr[0, lat(c), :] = hr
            hin_scr[1, lat(c), :] = hi
            cb = SEG_CHUNKS - 1 - c
            p_r, p_i = apre_ref[cb, 1:2, ln], apim_ref[cb, 1:2, ln]
            gr, gi = _cmul_add(p_r, p_i, hs_scr[2], hs_scr[3], hin_scr[2, lat(c), :], hin_scr[3, lat(c), :])
            hin_scr[2, lat(c), :] = gr
            hin_scr[3, lat(c), :] = gi

        for k in range(4):
            for sg in range(N_SEG):
                hinp_scr[sg * SEG_CHUNKS:(sg + 1) * SEG_CHUNKS, k * LANES:(k + 1) * LANES] = (
                    hin_scr[k, pl.ds(sg, SEG_CHUNKS, stride=SEG_PITCH), :].astype(BF16))
        for gi_, g in ((0, g0), (1, g1)):
            for r0, nr in parts:
                y_scr[g, r0:r0 + nr, :] = (dot(x_ref[g, r0:r0 + nr, :], m_ref[g])
                                           + dot(hinp_scr[r0:r0 + nr, :], e_ref[pr, :, gi_ * CH:(gi_ + 1) * CH]))

    rb_rows = 2 * SEG_CHUNKS
    for rb in range(ROWS // rb_rows):
        for hf in range(CHUNK // OCT):
            v = [y_scr[i, rb * rb_rows:(rb + 1) * rb_rows, hf * LANES:(hf + 1) * LANES] for i in range(OCT)]
            for s, acc in enumerate(_slot_transpose(v)):
                t = OCT * hf + s
                for sg in range(rb_rows // SEG_CHUNKS):
                    seg = rb * (rb_rows // SEG_CHUNKS) + sg
                    r0 = seg * SEG + t * SEG_CHUNKS
                    y_ref[0, r0:r0 + SEG_CHUNKS, :] = acc[sg * SEG_CHUNKS:(sg + 1) * SEG_CHUNKS]


def _s5_conv(xg, m, w, e, apre, apim, h0):
    n_oct = S5_GROUPS // OCT
    assert xg.shape[1] == ROWS
    nt = ROWS * CHUNK
    return pl.pallas_call(
        _s5_kernel,
        grid=(n_oct,),
        in_specs=[pl.BlockSpec((OCT, ROWS, CH), lambda o: (o, 0, 0)),
                  pl.BlockSpec((OCT, CH, CH), lambda o: (o, 0, 0)),
                  pl.BlockSpec((OCT_PAIRS, 2 * CH, 4 * LANES), lambda o: (o, 0, 0)),
                  pl.BlockSpec((OCT_PAIRS, 4 * LANES, 2 * CH), lambda o: (o, 0, 0)),
                  pl.BlockSpec((SEG_CHUNKS + 1, 2, OCT_PAIRS * LANES), lambda o: (0, 0, o)),
                  pl.BlockSpec((SEG_CHUNKS + 1, 2, OCT_PAIRS * LANES), lambda o: (0, 0, o)),
                  pl.BlockSpec((4, N_SAMPLE_SEQ, OCT_PAIRS * LANES), lambda o: (0, 0, o))],
        out_specs=[pl.BlockSpec((1, nt, LANES), lambda o: (o, 0, 0)),
                   pl.BlockSpec((OCT_PAIRS, 4, N_PROMPT_SEG, LANES), lambda o: (o, 0, 0, 0))],
        out_shape=[jax.ShapeDtypeStruct((n_oct, nt, LANES), F32),
                   jax.ShapeDtypeStruct((PAIRS, 4, N_PROMPT_SEG, LANES), F32)],
        scratch_shapes=[pltpu.VMEM((4, N_SEG * PITCH, LANES), F32),
                        pltpu.VMEM((4, SEG_CHUNKS * SEG_PITCH, LANES), F32),
                        pltpu.VMEM((ROWS, 4 * LANES), BF16),
                        pltpu.VMEM((4, N_SAMPLE_SEG, LANES), F32),
                        pltpu.VMEM((4, N_SAMPLE_SEG, LANES), F32),
                        pltpu.VMEM((OCT, ROWS, CH), F32)],
        compiler_params=_cparams(("parallel",)),
        name="s5_conv",
    )(xg, m, w, e, apre, apim, h0)


def _tile_order_positions(l):
    r = np.arange(l)
    tile, wi = r // SEG, r % SEG
    return tile * SEG + (wi % SEG_CHUNKS) * CHUNK + wi // SEG_CHUNKS


def _tile_dft_tables(scale, rows_in_tile_order=True):
    pos = _tile_order_positions(SEG)
    freq = pos if rows_in_tile_order else np.arange(SEG)
    ang = 2.0 * np.pi * ((freq[:, None] * pos[None, :]) % SEG) / SEG
    return jnp.asarray(np.cos(ang) * scale, F32), jnp.asarray(np.sin(ang) * scale, F32)


def _pos_dft_tile_kernel(cm_ref, sm_ref, fc_ref, fs_ref, o_ref):
    dot = functools.partial(jnp.dot, preferred_element_type=F32)
    cm, sm = cm_ref[...].astype(BF16), sm_ref[...].astype(BF16)
    for b in range(fc_ref.shape[0]):
        o_ref[b] = (dot(cm, fc_ref[b]) - dot(sm, fs_ref[b])).astype(BF16)


def _pos_dft_tile(fc, fs, n):
    _, l, wdt = fc.shape
    cm, sm = _tile_dft_tables(1.0 / math.sqrt(l))
    nb = 4
    seq = pl.BlockSpec((nb, l, wdt), lambda b: (b, 0, 0))
    return pl.pallas_call(
        _pos_dft_tile_kernel,
        grid=(n // nb,),
        in_specs=[_const_spec((l, l)), _const_spec((l, l)), seq, seq],
        out_specs=seq,
        out_shape=jax.ShapeDtypeStruct((n, l, wdt), BF16),
        compiler_params=_cparams(("parallel",)),
        name="pos_dft_tile",
    )(cm, sm, fc, fs)


N_TILES = 16


def _fft16(xr, xi):
    n = len(xr)
    rev = [int(format(i, "04b")[::-1], 2) for i in range(n)]
    ar = [xr[r] for r in rev]
    ai = [xi[r] for r in rev]
    size = 2
    while size <= n:
        half = size // 2
        for start in range(0, n, size):
            for k in range(half):
                wr = math.cos(2.0 * math.pi * k / size)
                wi = -math.sin(2.0 * math.pi * k / size)
                i0, i1 = start + k, start + k + half
                if k == 0:
                    tr, ti = ar[i1], ai[i1]
                elif 4 * k == size:
                    tr, ti = ai[i1], -ar[i1]
                else:
                    tr = ar[i1] * wr - ai[i1] * wi
                    ti = ar[i1] * wi + ai[i1] * wr
                ar[i1], ai[i1] = ar[i0] - tr, ai[i0] - ti
                ar[i0], ai[i0] = ar[i0] + tr, ai[i0] + ti
        size *= 2
    return ar, ai


def _pos_dft_long_kernel(cb_ref, sb_ref, ca_ref, sa_ref, fc_ref, fs_ref, o_ref, a_scr):
    wdt = fc_ref.shape[2]

    def rows_fft(r, carry):
        rows = pl.ds(pl.multiple_of(r * CHUNK, CHUNK), CHUNK)
        for b in range(wdt // LANES):
            lns = slice(b * LANES, (b + 1) * LANES)
            xr = [fc_ref[j, rows, lns].astype(F32) for j in range(N_TILES)]
            xi = [-fs_ref[j, rows, lns].astype(F32) for j in range(N_TILES)]
            ar, ai = _fft16(xr, xi)
            for k in range(N_TILES):
                a_scr[k, 0, rows, lns] = ar[k].astype(BF16)
                a_scr[k, 1, rows, lns] = ai[k].astype(BF16)
        return carry

    lax.fori_loop(0, SEG // CHUNK, rows_fft, 0)

    dot = functools.partial(jnp.dot, preferred_element_type=F32)
    cb, sb = cb_ref[...], sb_ref[...]
    for k1 in range(N_TILES):
        ca, sa = ca_ref[k1:k1 + 1, :], sa_ref[k1:k1 + 1, :]
        dc = (cb * ca - sb * sa).astype(BF16)
        ds = (sb * ca + cb * sa).astype(BF16)
        out = dot(dc, a_scr[k1, 0]) + dot(ds, a_scr[k1, 1])
        o_ref[:, k1 * SEG_CHUNKS:(k1 + 1) * SEG_CHUNKS, :] = out.astype(BF16).reshape(N_TILES, SEG_CHUNKS, wdt)


def _pos_dft_long(fc, fs, first, n):
    wdt = fc.shape[-1]
    l = N_TILES * SEG
    wb = 2 * LANES
    cb, sb = _tile_dft_tables(1.0 / math.sqrt(l), rows_in_tile_order=False)
    pos = _tile_order_positions(SEG)
    ang = 2.0 * np.pi * (np.arange(N_TILES)[:, None] * pos[None, :]) / l
    ca, sa = jnp.asarray(np.cos(ang), F32), jnp.asarray(np.sin(ang), F32)
    const2 = lambda shape: pl.BlockSpec(shape, lambda b, k: (0, 0), pipeline_mode=pl.Buffered(1))
    seq_in = pl.BlockSpec((None, N_TILES, SEG, wb), lambda b, k: (first + b, 0, 0, k))
    seq_out = pl.BlockSpec((None, N_TILES, SEG, wb), lambda b, k: (b, 0, 0, k))
    out = pl.pallas_call(
        _pos_dft_long_kernel,
        grid=(n, wdt // wb),
        in_specs=[const2((SEG, SEG)), const2((SEG, SEG)), const2((N_TILES, SEG)), const2((N_TILES, SEG)),
                  seq_in, seq_in],
        out_specs=seq_out,
        out_shape=jax.ShapeDtypeStruct((n, N_TILES, SEG, wdt), BF16),
        scratch_shapes=[pltpu.VMEM((N_TILES, 2, SEG, wb), BF16)],
        compiler_params=_cparams(("parallel", "parallel")),
        name="pos_dft_long",
    )(cb, sb, ca, sa, fc, fs)
    return out.reshape(n * l, wdt)


def _mix_ffn_kernel(n_prompt_tiles, xp_ref, xs_ref, mod_ref, ys_ref, yfp_ref, yfs_ref, gs_ref, gf_ref,
                    wglu_ref, bglu_ref, wps_ref, wpf_ref, wout_ref, n2_ref,
                    wg_ref, wu_ref, wd_ref, fn_ref, op_ref, os_ref, m_scr, mb_scr):
    tm = xp_ref.shape[0]
    dot = functools.partial(jnp.dot, preferred_element_type=F32)
    mod = mod_ref[0]
    is_prompt = pl.program_id(0) < n_prompt_tiles
    tiles = [slice(j * SEG, (j + 1) * SEG) for j in range(tm // SEG)]
    st = [dict() for _ in tiles]

    def gelu_phase(j, r, s):
        y = jnp.concatenate([ys_ref[b, r, :] for b in range(S5_WIDTH // LANES)], axis=1)
        s["z"] = jax.nn.gelu(y)

    def glu_phase(j, r, s):
        z = s["z"]
        s["z"] = (z * jax.nn.sigmoid(dot(z.astype(BF16), wglu_ref[...]) + bglu_ref[...])).astype(BF16)

    def proj_phase(j, r, s):
        yf = jnp.where(is_prompt, yfp_ref[r, :], yfs_ref[r, :])
        m = (gs_ref[r, :].astype(F32) * dot(s.pop("z"), wps_ref[...])
             + gf_ref[r, :].astype(F32) * dot(yf, wpf_ref[...]))
        for k in range(D_MODEL // LANES):
            for t in range(CHUNK):
                p0 = (j * CHUNK + t) * PITCH
                m_scr[k, p0:p0 + SEG_CHUNKS, :] = m[t * SEG_CHUNKS:(t + 1) * SEG_CHUNKS, k * LANES:(k + 1) * LANES]
        for c in range(SEG_CHUNKS):
            r0 = j * SEG + c * CHUNK
            for k in range(D_MODEL // LANES):
                mb_scr[r0:r0 + CHUNK, k * LANES:(k + 1) * LANES] = (
                    m_scr[k, pl.ds(j * CHUNK * PITCH + c, CHUNK, stride=PITCH), :].astype(BF16))

    def out_phase(j, r, s):
        x = jnp.where(is_prompt, xp_ref[r, :], xs_ref[r, :])
        x1 = x + mod[2:3] * dot(mb_scr[r, :], wout_ref[...])
        s["x1"] = x1
        s["h2"] = (_rms(x1, n2_ref[...]) * (1.0 + mod[4:5]) + mod[3:4]).astype(BF16)

    def ffn_phase(n0, n1, j, r, s):
        gate = dot(s["h2"], wg_ref[:, n0:n1])
        up = dot(s["h2"], wu_ref[:, n0:n1])
        part = dot((gate * jax.nn.sigmoid(gate) * up).astype(BF16), wd_ref[n0:n1, :])
        s["ff"] = part if "ff" not in s else s["ff"] + part

    def final_phase(j, r, s):
        x2 = s.pop("x1") + mod[5:6] * s.pop("ff")
        s["res"] = _rms(x2, fn_ref[...])

    phases = [gelu_phase, glu_phase, proj_phase, out_phase]
    phases += [functools.partial(ffn_phase, n0, n1) for n0, n1 in zip(FF_SPLITS[:-1], FF_SPLITS[1:])]
    phases += [final_phase]
    for phase in phases:
        for j, r in enumerate(tiles):
            phase(j, r, st[j])

    @pl.when(is_prompt)
    def _():
        for j, r in enumerate(tiles):
            op_ref[r, :] = st[j]["res"]

    @pl.when(jnp.logical_not(is_prompt))
    def _():
        for j, r in enumerate(tiles):
            os_ref[r, :] = st[j]["res"]


def _mix_ffn(xp, xs, seq_len, mod, ys, yf_p, yf_s, gs, gf, wts, tm):
    tp, ts = xp.shape[0], xs.shape[0]
    n_p = tp // tm
    tok = lambda w: pl.BlockSpec((tm, w), lambda i: (i, 0))
    return pl.pallas_call(
        functools.partial(_mix_ffn_kernel, n_p),
        grid=((tp + ts) // tm,),
        in_specs=_two_part_specs(tm, D_MODEL, n_p)
                 + [_mod_spec(n_p, seq_len // tm),
                    pl.BlockSpec((S5_WIDTH // LANES, tm, LANES), lambda i: (0, i, 0))]
                 + _two_part_specs(tm, FFT_WIDTH, n_p)
                 + [tok(D_MODEL), tok(D_MODEL)]
                 + [_const_spec(w.shape) for w in wts],
        out_specs=_two_part_specs(tm, D_MODEL, n_p),
        out_shape=[jax.ShapeDtypeStruct((tp, D_MODEL), F32), jax.ShapeDtypeStruct((ts, D_MODEL), F32)],
        scratch_shapes=[pltpu.VMEM((D_MODEL // LANES, tm // CHUNK * PITCH, LANES), F32),
                        pltpu.VMEM((tm, D_MODEL), BF16)],
        compiler_params=_cparams(("arbitrary",)),
        name="mix_ffn",
    )(xp, xs, mod, ys, yf_p, yf_s, gs, gf, *wts)


def kernel(x_prompt, x_sample, state_s5, c, c_ctx, norm1_g, norm2_g, w_ada, b_ada, w_in,
           s5_lambda_re, s5_lambda_im, s5_log_step, s5_b_re, s5_b_im, s5_c_re, s5_c_im,
           s5_d, w_glu, b_glu, w_proj_s5, w_proj_fft, w_out, w_ffn_gate, w_ffn_up,
           w_ffn_down, final_norm_g):
    nb, sl, _ = x_prompt.shape
    db, dl, _ = x_sample.shape
    assert w_in.shape[0] == 1 and sl == SEG and nb == N_PROMPT_SEG
    assert db == N_SAMPLE_SEQ and dl == SEG * SEG_CHUNKS

    cvec = jnp.concatenate([c_ctx[None], c, jnp.zeros((8 - 1 - db, D_MODEL), F32)], axis=0)
    mod, w_in_b = _adaln(cvec, w_ada[0], b_ada[0], w_in[0])
    mod = mod.reshape(8, N_MOD, D_MODEL)

    m, w, e, apre, apim = _s5_tables(s5_lambda_re[0], s5_lambda_im[0], s5_log_step[0], s5_b_re[0],
                                     s5_b_im[0], s5_c_re[0], s5_c_im[0], s5_d[0])

    tm = 512
    xp = x_prompt.reshape(nb * sl, D_MODEL)
    xs = x_sample.reshape(db * dl, D_MODEL)
    later = (w_glu[0], w_proj_s5[0], w_proj_fft[0], w_out[0], w_ffn_gate[0], w_ffn_up[0], w_ffn_down[0])
    xg, fc, fs, gs, gf, *later_b = _inproj(xp, xs, dl, mod, norm1_g[0], w_in_b, later, tm)
    wglu_b, wps_b, wpf_b, wout_b, wg_b, wu_b, wd_b = later_b

    h0 = state_s5[:, 0].astype(F32).transpose(1, 2, 0, 3, 4).reshape(4, db, GP)
    ys, fin = _s5_conv(xg, m, w, e, apre, apim, h0)
    new_state = fin.reshape(PAIRS, 2, 2, nb, 2, S5_STATE).transpose(3, 1, 2, 0, 4, 5)
    new_state = new_state.reshape(nb, 1, 2, 2, S5_GROUPS, S5_STATE)

    yf_p = _pos_dft_tile(fc.reshape(-1, SEG, FFT_WIDTH), fs.reshape(-1, SEG, FFT_WIDTH), nb)
    n_prompt_units = nb // N_TILES
    yf_s = _pos_dft_long(fc.reshape(-1, N_TILES, SEG, FFT_WIDTH), fs.reshape(-1, N_TILES, SEG, FFT_WIDTH),
                         n_prompt_units, db)

    wts = (wglu_b, b_glu[0].reshape(1, S5_WIDTH), wps_b, wpf_b, wout_b, norm2_g[0].reshape(1, D_MODEL),
           wg_b, wu_b, wd_b, final_norm_g.reshape(1, D_MODEL))
    tm2 = 2 * SEG
    y_p, y_s = _mix_ffn(xp, xs, dl, mod, ys, yf_p.reshape(nb * sl, FFT_WIDTH), yf_s, gs, gf, wts, tm2)
    return (y_p.reshape(nb, sl, D_MODEL), y_s.reshape(db, dl, D_MODEL), new_state)
```

```python
import functools
import math

import numpy as np
import jax
import jax.numpy as jnp
from jax import lax
from jax.experimental import pallas as pl
from jax.experimental.pallas import tpu as pltpu

F32 = jnp.float32
BF16 = jnp.bfloat16

D_MODEL = 1024
S5_WIDTH = 512
S5_GROUPS = 32
S5_GROUP = 16
S5_STATE = 64
FFT_WIDTH = 512
FFT_GROUPS = 4
FFT_GROUP = 128
D_FF = 2816
N_MOD = 6
EPS = 1e-6

LANES = 128
CHUNK = 16
SEG_CHUNKS = 16
SEG = CHUNK * SEG_CHUNKS
GP = S5_GROUPS * S5_STATE
PAIRS = S5_GROUPS // 2
CH = CHUNK * S5_GROUP
OCT = LANES // S5_GROUP
PITCH = 24

MXU_DIM = 256
FF_SPLITS = (0, 3 * MXU_DIM, 6 * MXU_DIM, 9 * MXU_DIM, D_FF)
VMEM_LIMIT = 58 * 1024 * 1024


def _cparams(sem):
    return pltpu.CompilerParams(dimension_semantics=sem, vmem_limit_bytes=VMEM_LIMIT)


def _row_block_cast_specs(w, n_steps):
    rows = w.shape[0] // n_steps
    assert rows * n_steps == w.shape[0] and rows % 16 == 0, (w.shape, n_steps)
    spec = pl.BlockSpec((rows, w.shape[1]), lambda i: (jnp.minimum(i, n_steps - 1), 0))
    return spec, jax.ShapeDtypeStruct(w.shape, BF16)


def _cast_blocks(in_refs, out_refs):
    for src, dst in zip(in_refs, out_refs):
        dst[...] = src[...].astype(BF16)


def _ada_tile(c_ref, w_ref, b_ref, o_ref):
    dot = functools.partial(jnp.dot, preferred_element_type=F32)
    c = c_ref[...]
    s = c * jax.nn.sigmoid(c)
    s_hi = s.astype(BF16)
    s_lo = (s - s_hi.astype(F32)).astype(BF16)
    w = w_ref[...]
    w_hi = w.astype(BF16)
    w_lo = (w - w_hi.astype(F32)).astype(BF16)
    r = dot(jnp.concatenate([s_hi, s_lo], axis=0), w_hi)
    n = s.shape[0]
    o_ref[...] = r[0:n] + r[n:2 * n] + dot(s_hi, w_lo) + b_ref[...]


def _s5_discretise(lre_ref, lim_ref, lst_ref, bre_ref, bim_ref, cre_ref, cim_ref,
                   apre_ref, apim_ref, gb_scr, ca_scr):
    lre = lre_ref[...]
    lim = lim_ref[...]
    step = jnp.exp(lst_ref[...])
    mag = jnp.exp(lre * step)
    are = mag * jnp.cos(lim * step)
    aim = mag * jnp.sin(lim * step)
    nr = are - 1.0
    den = lre * lre + lim * lim
    fr = (nr * lre + aim * lim) / den
    fi = (aim * lre - nr * lim) / den

    pr = [jnp.ones_like(are)]
    pi = [jnp.zeros_like(are)]
    for _ in range(CHUNK):
        r, i = pr[-1], pi[-1]
        pr.append(r * are - i * aim)
        pi.append(r * aim + i * are)
    a16r, a16i = pr[CHUNK], pi[CHUNK]
    qr, qi = jnp.ones_like(are), jnp.zeros_like(are)
    for c in range(SEG_CHUNKS + 1):
        apre_ref[c] = qr
        apim_ref[c] = qi
        qr, qi = qr * a16r - qi * a16i, qr * a16i + qi * a16r

    for d in range(2):
        bbre = fr[d:d + 1] * bre_ref[d] - fi[d:d + 1] * bim_ref[d]
        bbim = fr[d:d + 1] * bim_ref[d] + fi[d:d + 1] * bre_ref[d]
        cre, cim = cre_ref[d], cim_ref[d]
        for s in range(CHUNK):
            k = CHUNK - 1 - s if d == 0 else s
            r, i = pr[k][d:d + 1], pi[k][d:d + 1]
            gb_scr[0, d, s] = r * bbre - i * bbim
            gb_scr[1, d, s] = r * bbim + i * bbre
            f = s + 1 if d == 0 else CHUNK - s
            r, i = pr[f][d:d + 1], pi[f][d:d + 1]
            ca_scr[0, d, s] = r * cre - i * cim
            ca_scr[1, d, s] = -(r * cim + i * cre)


def _s5_pair_operands(q, slot_q, cre_ref, cim_ref, dt_ref, gb_scr, ca_scr, ts_scr, m_ref, w_ref, e_ref):
    dot_nt = lambda a, b: lax.dot_general(a, b, (((1,), (1,)), ((), ())), preferred_element_type=F32)
    lane_gi = lax.broadcasted_iota(jnp.int32, (CH, LANES), 1) // S5_STATE
    row_gi = lax.broadcasted_iota(jnp.int32, (LANES, CH), 0) // S5_STATE
    eye = (lax.broadcasted_iota(jnp.int32, (LANES, LANES), 0)
           == lax.broadcasted_iota(jnp.int32, (LANES, LANES), 1)).astype(BF16)
    slot = lax.broadcasted_iota(jnp.int32, (CH, LANES), 1) // S5_GROUP
    n_lag_rows = (2 * CHUNK - 1) * S5_GROUP
    if True:
        lanes = pl.ds(pl.multiple_of(q * LANES, LANES), LANES)
        for d in range(2):
            for ri in range(2):
                col = slice((2 * d + ri) * LANES, (2 * d + ri + 1) * LANES)
                gb = gb_scr[ri, d, :, :, lanes].reshape(CH, LANES)
                ca = ca_scr[ri, d, :, :, lanes].reshape(CH, LANES).astype(BF16)
                ca_t = dot_nt(eye, ca)
                for gi in range(2):
                    w_ref[slot_q, gi * CH:(gi + 1) * CH, col] = jnp.where(lane_gi == gi, gb, 0.0).astype(BF16)
                    e_ref[slot_q, col, gi * CH:(gi + 1) * CH] = jnp.where(row_gi == gi, ca_t, 0.0).astype(BF16)
        for gi in range(2):
            g = 2 * q + gi
            lag = []
            for d in range(2):
                c_re = jnp.concatenate([cre_ref[d, :, lanes]] * OCT, axis=0)
                c_imn = jnp.concatenate([-cim_ref[d, :, lanes]] * OCT, axis=0)
                keep = lax.broadcasted_iota(jnp.int32, (LANES, LANES), 1) // S5_STATE == gi
                c_re = jnp.where(keep, c_re, 0.0).astype(BF16)
                c_imn = jnp.where(keep, c_imn, 0.0).astype(BF16)
                gre = gb_scr[0, d, :, :, lanes].reshape(CH, LANES).astype(BF16)
                gim = gb_scr[1, d, :, :, lanes].reshape(CH, LANES).astype(BF16)
                lag.append(dot_nt(gre, c_re) + dot_nt(gim, c_imn))
            zl = (CHUNK - 1) * S5_GROUP
            ts_scr[0:zl, :] = lag[0][0:zl]
            ts_scr[zl:zl + S5_GROUP, :] = lag[0][zl:] + lag[1][0:S5_GROUP] + dt_ref[g]
            ts_scr[zl + S5_GROUP:n_lag_rows, :] = lag[1][S5_GROUP:]
            for hf in range(CHUNK // OCT):
                acc = None
                for s in range(OCT):
                    t = OCT * hf + s
                    win = ts_scr[(CHUNK - 1 - t) * S5_GROUP:(CHUNK - 1 - t) * S5_GROUP + CH, :]
                    acc = win if acc is None else jnp.where(slot == s, win, acc)
                m_ref[2 * slot_q + gi, :, hf * LANES:(hf + 1) * LANES] = acc.astype(BF16)


PAIRS_PER_STEP = 2


def _ada_tables_kernel(c_ref, wada_ref, bada_ref, win_ref, lre_ref, lim_ref, lst_ref, bre_ref, bim_ref,
                       cre_ref, cim_ref, dt_ref, mod_ref, winb_ref, apre_ref, apim_ref, m_ref, w_ref, e_ref,
                       gb_scr, ca_scr, ts_scr):
    j = pl.program_id(0)
    _cast_blocks([win_ref], [winb_ref])
    _ada_tile(c_ref, wada_ref, bada_ref, mod_ref)

    @pl.when(j == 0)
    def _():
        _s5_discretise(lre_ref, lim_ref, lst_ref, bre_ref, bim_ref, cre_ref, cim_ref,
                       apre_ref, apim_ref, gb_scr, ca_scr)

    for lp in range(PAIRS_PER_STEP):
        _s5_pair_operands(PAIRS_PER_STEP * j + lp, lp, cre_ref, cim_ref, dt_ref, gb_scr, ca_scr, ts_scr,
                          m_ref, w_ref, e_ref)


def _adaln_and_s5_tables(cvec8, w_ada, b_ada, w_in, lam_re, lam_im, log_step, b_re, b_im, c_re, c_im, s5_d):
    lre = lam_re.reshape(2, GP)
    lim = lam_im.reshape(2, GP)
    lst = jnp.repeat(log_step, S5_STATE, axis=-1)
    bre = b_re.transpose(0, 3, 1, 2).reshape(2, S5_GROUP, GP)
    bim = b_im.transpose(0, 3, 1, 2).reshape(2, S5_GROUP, GP)
    cre = c_re.transpose(0, 2, 1, 3).reshape(2, S5_GROUP, GP)
    cim = c_im.transpose(0, 2, 1, 3).reshape(2, S5_GROUP, GP)
    eye = jnp.asarray(np.eye(S5_GROUP, dtype=np.float32))
    dt = jnp.tile(eye[None] * s5_d.reshape(S5_GROUPS, 1, S5_GROUP), (1, 1, OCT))

    n_steps = PAIRS // PAIRS_PER_STEP
    n_out = w_ada.shape[1]
    tn = n_out // n_steps
    cast_spec, cast_shape = _row_block_cast_specs(w_in, n_steps)
    whole = lambda a: pl.BlockSpec(a.shape, lambda j, nd=a.ndim: (0,) * nd)
    powers = jax.ShapeDtypeStruct((SEG_CHUNKS + 1, 2, GP), F32)
    tab = (2, 2, CHUNK, S5_GROUP, GP)
    s5_in = (lre, lim, lst, bre, bim, cre, cim, dt)
    mod, w_in_b, apre, apim, m, w, e = pl.pallas_call(
        _ada_tables_kernel,
        grid=(n_steps,),
        in_specs=[pl.BlockSpec((8, D_MODEL), lambda j: (0, 0)),
                  pl.BlockSpec((D_MODEL, tn), lambda j: (0, j)),
                  pl.BlockSpec((1, tn), lambda j: (0, j)),
                  cast_spec] + [whole(a) for a in s5_in],
        out_specs=[pl.BlockSpec((8, tn), lambda j: (0, j)), cast_spec, whole(powers), whole(powers),
                   pl.BlockSpec((2 * PAIRS_PER_STEP, CH, CH), lambda j: (j, 0, 0)),
                   pl.BlockSpec((PAIRS_PER_STEP, 2 * CH, 4 * LANES), lambda j: (j, 0, 0)),
                   pl.BlockSpec((PAIRS_PER_STEP, 4 * LANES, 2 * CH), lambda j: (j, 0, 0))],
        out_shape=[jax.ShapeDtypeStruct((8, n_out), F32), cast_shape, powers, powers,
                   jax.ShapeDtypeStruct((S5_GROUPS, CH, CH), BF16),
                   jax.ShapeDtypeStruct((PAIRS, 2 * CH, 4 * LANES), BF16),
                   jax.ShapeDtypeStruct((PAIRS, 4 * LANES, 2 * CH), BF16)],
        scratch_shapes=[pltpu.VMEM(tab, F32),
                        pltpu.VMEM(tab, F32),
                        pltpu.VMEM(((2 * CHUNK - 1) * S5_GROUP, LANES), F32)],
        compiler_params=_cparams(("arbitrary",)),
        name="adaln_s5_tables",
    )(cvec8, w_ada, b_ada.reshape(1, n_out), w_in, *s5_in)
    return mod, w_in_b, m, w, e, apre, apim


def _rms(x, g):
    return x * lax.rsqrt(jnp.mean(x * x, axis=-1, keepdims=True) + EPS) * g


def _slot_transpose(v):
    v = list(v)
    slot = lax.broadcasted_iota(jnp.int32, v[0].shape, 1) // S5_GROUP
    for k in (4, 2, 1):
        low = (slot & k) == 0
        for i in range(OCT):
            if i & k:
                continue
            a, b = v[i], v[i + k]
            v[i] = jnp.where(low, a, pltpu.roll(b, k * S5_GROUP, axis=1))
            v[i + k] = jnp.where(low, pltpu.roll(a, LANES - k * S5_GROUP, axis=1), b)
    return v


def _inproj_kernel(n_prompt_tiles, n_cast, xp_ref, xs_ref, mod_ref, g_ref, win_ref, bdc_ref, bds_ref,
                   *rest):
    cast_in, rest = rest[:n_cast], rest[n_cast:]
    xg_ref, fc_ref, fs_ref, gs_ref, gf_ref = rest[:5]
    cast_out, (h_scr, hb_scr, zs_scr) = rest[5:5 + n_cast], rest[5 + n_cast:]
    tm = xp_ref.shape[0]
    mod = mod_ref[0]
    is_prompt = pl.program_id(0) < n_prompt_tiles
    dot = functools.partial(jnp.dot, preferred_element_type=F32)
    o = 2 * S5_WIDTH
    tiles = [slice(j * SEG, (j + 1) * SEG) for j in range(tm // SEG)]

    def norm_phase(j, r):
        x = jnp.where(is_prompt, xp_ref[r, :], xs_ref[r, :])
        h = _rms(x, g_ref[...]) * (1.0 + mod[1:2]) + mod[0:1]
        for k in range(D_MODEL // LANES):
            for c in range(SEG_CHUNKS):
                p0 = (j * SEG_CHUNKS + c) * PITCH
                h_scr[k, p0:p0 + CHUNK, :] = h[c * CHUNK:(c + 1) * CHUNK, k * LANES:(k + 1) * LANES]
        for s in range(CHUNK):
            r0 = j * SEG + s * SEG_CHUNKS
            for k in range(D_MODEL // LANES):
                hb_scr[r0:r0 + SEG_CHUNKS, k * LANES:(k + 1) * LANES] = (
                    h_scr[k, pl.ds(j * SEG_CHUNKS * PITCH + s, SEG_CHUNKS, stride=PITCH), :].astype(BF16))

    def s5_gate_phase(j, r):
        zs_scr[r, :] = dot(hb_scr[r, :], win_ref[:, 0:S5_WIDTH])
        gs_ref[r, :] = jax.nn.sigmoid(dot(hb_scr[r, :], win_ref[:, o:o + D_MODEL])).astype(BF16)

    def fold_phase(j, r):
        for b in range(S5_WIDTH // LANES):
            for hf in range(CHUNK // OCT):
                z = [zs_scr[j * SEG + s * SEG_CHUNKS:j * SEG + (s + 1) * SEG_CHUNKS, b * LANES:(b + 1) * LANES]
                     for s in range(OCT * hf, OCT * (hf + 1))]
                for i, xi in enumerate(_slot_transpose(z)):
                    xg_ref[OCT * b + i, j * SEG_CHUNKS:(j + 1) * SEG_CHUNKS,
                           hf * LANES:(hf + 1) * LANES] = xi.astype(BF16)

    def fourier_gate_phase(j, r):
        uf = dot(hb_scr[r, :], win_ref[:, S5_WIDTH:2 * S5_WIDTH]).astype(BF16)
        for n0 in range(0, FFT_WIDTH, MXU_DIM):
            cols = slice(n0, n0 + MXU_DIM)
            fc_ref[r, cols] = dot(uf[:, cols], bdc_ref[cols, cols]).astype(BF16)
            fs_ref[r, cols] = dot(uf[:, cols], bds_ref[cols, cols]).astype(BF16)
        gf_ref[r, :] = jax.nn.sigmoid(dot(hb_scr[r, :], win_ref[:, o + D_MODEL:o + 2 * D_MODEL])).astype(BF16)

    for phase in (norm_phase, s5_gate_phase, fold_phase, fourier_gate_phase):
        for j, r in enumerate(tiles):
            phase(j, r)
    _cast_blocks(cast_in, cast_out)


def _channel_dft_mats():
    j = np.arange(FFT_GROUP)
    ang = 2.0 * np.pi * ((j[:, None] * j[None, :]) % FFT_GROUP) / FFT_GROUP
    blk_c = np.cos(ang) / math.sqrt(FFT_GROUP)
    blk_s = np.sin(ang) / math.sqrt(FFT_GROUP)
    bdc = np.kron(np.eye(FFT_GROUPS), blk_c)
    bds = np.kron(np.eye(FFT_GROUPS), blk_s)
    return jnp.asarray(bdc, F32).astype(BF16), jnp.asarray(bds, F32).astype(BF16)


def _const_spec(shape):
    nd = len(shape)
    return pl.BlockSpec(shape, lambda i: (0,) * nd, pipeline_mode=pl.Buffered(1))


def _two_part_specs(tm, width, n_prompt_tiles):
    return [pl.BlockSpec((tm, width), lambda i: (jnp.minimum(i, n_prompt_tiles - 1), 0)),
            pl.BlockSpec((tm, width), lambda i: (jnp.maximum(i - n_prompt_tiles, 0), 0))]


def _mod_spec(n_prompt_tiles, tiles_per_seq):
    row = lambda i: jnp.where(i < n_prompt_tiles, 0, 1 + (i - n_prompt_tiles) // tiles_per_seq)
    return pl.BlockSpec((1, N_MOD, D_MODEL), lambda i: (row(i), 0, 0))


N_CAST_STEPS = 16


def _inproj(xp, xs, seq_len, mod, norm_g, w_in_b, later_weights, tm):
    t = xp.shape[0] + xs.shape[0]
    n_p = xp.shape[0] // tm
    assert t // tm >= N_CAST_STEPS
    bdc, bds = _channel_dft_mats()
    casts = [_row_block_cast_specs(w, N_CAST_STEPS) for w in later_weights]
    tok = lambda w: pl.BlockSpec((tm, w), lambda i: (i, 0))
    out = lambda w: jax.ShapeDtypeStruct((t, w), BF16)
    return pl.pallas_call(
        functools.partial(_inproj_kernel, n_p, len(casts)),
        grid=(t // tm,),
        in_specs=_two_part_specs(tm, D_MODEL, n_p) + [
                  _mod_spec(n_p, seq_len // tm),
                  _const_spec((1, D_MODEL)),
                  _const_spec(w_in_b.shape),
                  _const_spec(bdc.shape),
                  _const_spec(bds.shape)] + [c[0] for c in casts],
        out_specs=[pl.BlockSpec((S5_GROUPS, tm // CHUNK, CH), lambda i: (0, i, 0)),
                   tok(FFT_WIDTH), tok(FFT_WIDTH), tok(D_MODEL), tok(D_MODEL)] + [c[0] for c in casts],
        out_shape=[jax.ShapeDtypeStruct((S5_GROUPS, t // CHUNK, CH), BF16),
                   out(FFT_WIDTH), out(FFT_WIDTH), out(D_MODEL), out(D_MODEL)] + [c[1] for c in casts],
        scratch_shapes=[pltpu.VMEM((D_MODEL // LANES, tm // CHUNK * PITCH, LANES), F32),
                        pltpu.VMEM((tm, D_MODEL), BF16),
                        pltpu.VMEM((tm, S5_WIDTH), F32)],
        compiler_params=_cparams(("arbitrary",)),
        name="inproj",
    )(xp, xs, mod, norm_g.reshape(1, D_MODEL), w_in_b, bdc, bds, *later_weights)


N_PROMPT_SEG = 16
N_SAMPLE_SEG = 32
N_SAMPLE_SEQ = 2
N_SEG = N_PROMPT_SEG + N_SAMPLE_SEG
ROWS = SEG_CHUNKS * N_SEG
ROWS_P = SEG_CHUNKS * N_PROMPT_SEG
OCT_PAIRS = OCT // 2
SEG_PITCH = 56


def _cmul_add(ar, ai, hr, hi, sr, si):
    return ar * hr - ai * hi + sr, ar * hi + ai * hr + si


def _s5_kernel(x_ref, m_ref, w_ref, e_ref, apre_ref, apim_ref, h0_ref,
               y_ref, fin_ref, s_scr, hin_scr, hinp_scr, f_scr, hs_scr, y_scr):
    dot = functools.partial(jnp.dot, preferred_element_type=F32)
    parts = ((0, ROWS_P), (ROWS_P, ROWS - ROWS_P))
    seg_rows = lambda c: pl.ds(c, N_SEG, stride=PITCH)
    blk = lambda c: pl.ds(c * SEG_PITCH, N_SEG)
    seq_rows = lambda j: pl.ds(j, N_SAMPLE_SEQ, stride=SEG_CHUNKS)
    lat = lambda c: pl.ds(c * SEG_PITCH + N_PROMPT_SEG, N_SAMPLE_SEG)
    zero = jnp.zeros((N_SEG, LANES), F32)
    octet = pl.program_id(0)
    state_lanes = lambda k, pr: pl.ds(pl.multiple_of(k * GP + (octet * OCT_PAIRS + pr) * LANES, LANES), LANES)

    for pr in range(OCT_PAIRS):
        ln = slice(pr * LANES, (pr + 1) * LANES)
        g0, g1 = 2 * pr, 2 * pr + 1
        for r0, nr in parts:
            s = (dot(x_ref[g0, r0:r0 + nr, :], w_ref[pr, 0:CH, :])
                 + dot(x_ref[g1, r0:r0 + nr, :], w_ref[pr, CH:2 * CH, :]))
            for k in range(4):
                for sg in range(nr // SEG_CHUNKS):
                    p0 = (r0 // SEG_CHUNKS + sg) * PITCH
                    s_scr[k, p0:p0 + SEG_CHUNKS, :] = s[sg * SEG_CHUNKS:(sg + 1) * SEG_CHUNKS, k * LANES:(k + 1) * LANES]

        ar, ai = apre_ref[1, 0:1, ln], apim_ref[1, 0:1, ln]
        hr, hi = zero, zero
        for c in range(SEG_CHUNKS):
            hin_scr[0, blk(c), :] = hr
            hin_scr[1, blk(c), :] = hi
            hr, hi = _cmul_add(ar, ai, hr, hi, s_scr[0, seg_rows(c), :], s_scr[1, seg_rows(c), :])
        fin_ref[:, state_lanes(0, pr)] = hr[0:N_PROMPT_SEG]
        fin_ref[:, state_lanes(1, pr)] = hi[0:N_PROMPT_SEG]
        f_scr[0] = hr[N_PROMPT_SEG:]
        f_scr[1] = hi[N_PROMPT_SEG:]
        br, bi = apre_ref[1, 1:2, ln], apim_ref[1, 1:2, ln]
        gr, gi = zero, zero
        for c in range(SEG_CHUNKS - 1, -1, -1):
            hin_scr[2, blk(c), :] = gr
            hin_scr[3, blk(c), :] = gi
            gr, gi = _cmul_add(br, bi, gr, gi, s_scr[2, seg_rows(c), :], s_scr[3, seg_rows(c), :])
        fin_ref[:, state_lanes(2, pr)] = gr[0:N_PROMPT_SEG]
        fin_ref[:, state_lanes(3, pr)] = gi[0:N_PROMPT_SEG]
        f_scr[2] = gr[N_PROMPT_SEG:]
        f_scr[3] = gi[N_PROMPT_SEG:]

        a2r, a2i = apre_ref[SEG_CHUNKS, 0:1, ln], apim_ref[SEG_CHUNKS, 0:1, ln]
        hr, hi = h0_ref[:, state_lanes(0, pr)], h0_ref[:, state_lanes(1, pr)]
        for j in range(SEG_CHUNKS):
            hs_scr[0, seq_rows(j), :] = hr
            hs_scr[1, seq_rows(j), :] = hi
            hr, hi = _cmul_add(a2r, a2i, hr, hi, f_scr[0, seq_rows(j), :], f_scr[1, seq_rows(j), :])
        b2r, b2i = apre_ref[SEG_CHUNKS, 1:2, ln], apim_ref[SEG_CHUNKS, 1:2, ln]
        gr, gi = h0_ref[:, state_lanes(2, pr)], h0_ref[:, state_lanes(3, pr)]
        for j in range(SEG_CHUNKS - 1, -1, -1):
            hs_scr[2, seq_rows(j), :] = gr
            hs_scr[3, seq_rows(j), :] = gi
            gr, gi = _cmul_add(b2r, b2i, gr, gi, f_scr[2, seq_rows(j), :], f_scr[3, seq_rows(j), :])
        for c in range(SEG_CHUNKS):
            p_r, p_i = apre_ref[c, 0:1, ln], apim_ref[c, 0:1, ln]
            hr, hi = _cmul_add(p_r, p_i, hs_scr[0], hs_scr[1], hin_scr[0, lat(c), :], hin_scr[1, lat(c), :])
            hin_scr[0, lat(c), :] = hr
            hin_scr[1, lat(c), :] = hi
            cb = SEG_CHUNKS - 1 - c
            p_r, p_i = apre_ref[cb, 1:2, ln], apim_ref[cb, 1:2, ln]
            gr, gi = _cmul_add(p_r, p_i, hs_scr[2], hs_scr[3], hin_scr[2, lat(c), :], hin_scr[3, lat(c), :])
            hin_scr[2, lat(c), :] = gr
            hin_scr[3, lat(c), :] = gi

        for k in range(4):
            for sg in range(N_SEG):
                hinp_scr[sg * SEG_CHUNKS:(sg + 1) * SEG_CHUNKS, k * LANES:(k + 1) * LANES] = (
                    hin_scr[k, pl.ds(sg, SEG_CHUNKS, stride=SEG_PITCH), :].astype(BF16))
        for gi_, g in ((0, g0), (1, g1)):
            for r0, nr in parts:
                y_scr[g, r0:r0 + nr, :] = (dot(x_ref[g, r0:r0 + nr, :], m_ref[g])
                                           + dot(hinp_scr[r0:r0 + nr, :], e_ref[pr, :, gi_ * CH:(gi_ + 1) * CH]))

    rb_rows = 2 * SEG_CHUNKS
    for rb in range(ROWS // rb_rows):
        for hf in range(CHUNK // OCT):
            v = [y_scr[i, rb * rb_rows:(rb + 1) * rb_rows, hf * LANES:(hf + 1) * LANES] for i in range(OCT)]
            for s, acc in enumerate(_slot_transpose(v)):
                t = OCT * hf + s
                for sg in range(rb_rows // SEG_CHUNKS):
                    seg = rb * (rb_rows // SEG_CHUNKS) + sg
                    r0 = seg * SEG + t * SEG_CHUNKS
                    y_ref[0, r0:r0 + SEG_CHUNKS, :] = acc[sg * SEG_CHUNKS:(sg + 1) * SEG_CHUNKS]


def _s5_conv(xg, m, w, e, apre, apim, h0):
    n_oct = S5_GROUPS // OCT
    assert xg.shape[1] == ROWS
    nt = ROWS * CHUNK
    return pl.pallas_call(
        _s5_kernel,
        grid=(n_oct,),
        in_specs=[pl.BlockSpec((OCT, ROWS, CH), lambda o: (o, 0, 0)),
                  pl.BlockSpec((OCT, CH, CH), lambda o: (o, 0, 0)),
                  pl.BlockSpec((OCT_PAIRS, 2 * CH, 4 * LANES), lambda o: (o, 0, 0)),
                  pl.BlockSpec((OCT_PAIRS, 4 * LANES, 2 * CH), lambda o: (o, 0, 0)),
                  pl.BlockSpec((SEG_CHUNKS + 1, 2, OCT_PAIRS * LANES), lambda o: (0, 0, o)),
                  pl.BlockSpec((SEG_CHUNKS + 1, 2, OCT_PAIRS * LANES), lambda o: (0, 0, o)),
                  pl.BlockSpec((N_SAMPLE_SEQ, 4 * GP), lambda o: (0, 0))],
        out_specs=[pl.BlockSpec((1, nt, LANES), lambda o: (o, 0, 0)),
                   pl.BlockSpec((N_PROMPT_SEG, 4 * GP), lambda o: (0, 0))],
        out_shape=[jax.ShapeDtypeStruct((n_oct, nt, LANES), F32),
                   jax.ShapeDtypeStruct((N_PROMPT_SEG, 4 * GP), F32)],
        scratch_shapes=[pltpu.VMEM((4, N_SEG * PITCH, LANES), F32),
                        pltpu.VMEM((4, SEG_CHUNKS * SEG_PITCH, LANES), F32),
                        pltpu.VMEM((ROWS, 4 * LANES), BF16),
                        pltpu.VMEM((4, N_SAMPLE_SEG, LANES), F32),
                        pltpu.VMEM((4, N_SAMPLE_SEG, LANES), F32),
                        pltpu.VMEM((OCT, ROWS, CH), F32)],
        compiler_params=_cparams(("arbitrary",)),
        name="s5_conv",
    )(xg, m, w, e, apre, apim, h0)


def _tile_order_positions(l):
    r = np.arange(l)
    tile, wi = r // SEG, r % SEG
    return tile * SEG + (wi % SEG_CHUNKS) * CHUNK + wi // SEG_CHUNKS


def _tile_dft_tables(scale, rows_in_tile_order=True):
    pos = _tile_order_positions(SEG)
    freq = pos if rows_in_tile_order else np.arange(SEG)
    ang = 2.0 * np.pi * ((freq[:, None] * pos[None, :]) % SEG) / SEG
    return jnp.asarray(np.cos(ang) * scale, F32), jnp.asarray(np.sin(ang) * scale, F32)


def _pos_dft_tile_kernel(cm_ref, sm_ref, fc_ref, fs_ref, o_ref):
    dot = functools.partial(jnp.dot, preferred_element_type=F32)
    cm, sm = cm_ref[...].astype(BF16), sm_ref[...].astype(BF16)
    for b in range(fc_ref.shape[0]):
        o_ref[b] = (dot(cm, fc_ref[b]) - dot(sm, fs_ref[b])).astype(BF16)


def _pos_dft_tile(fc, fs, n):
    _, l, wdt = fc.shape
    cm, sm = _tile_dft_tables(1.0 / math.sqrt(l))
    nb = 4
    seq = pl.BlockSpec((nb, l, wdt), lambda b: (b, 0, 0))
    return pl.pallas_call(
        _pos_dft_tile_kernel,
        grid=(n // nb,),
        in_specs=[_const_spec((l, l)), _const_spec((l, l)), seq, seq],
        out_specs=seq,
        out_shape=jax.ShapeDtypeStruct((n, l, wdt), BF16),
        compiler_params=_cparams(("parallel",)),
        name="pos_dft_tile",
    )(cm, sm, fc, fs)


N_TILES = 16


def _fft16(xr, xi):
    n = len(xr)
    rev = [int(format(i, "04b")[::-1], 2) for i in range(n)]
    ar = [xr[r] for r in rev]
    ai = [xi[r] for r in rev]
    size = 2
    while size <= n:
        half = size // 2
        for start in range(0, n, size):
            for k in range(half):
                wr = math.cos(2.0 * math.pi * k / size)
                wi = -math.sin(2.0 * math.pi * k / size)
                i0, i1 = start + k, start + k + half
                if k == 0:
                    tr, ti = ar[i1], ai[i1]
                elif 4 * k == size:
                    tr, ti = ai[i1], -ar[i1]
                else:
                    tr = ar[i1] * wr - ai[i1] * wi
                    ti = ar[i1] * wi + ai[i1] * wr
                ar[i1], ai[i1] = ar[i0] - tr, ai[i0] - ti
                ar[i0], ai[i0] = ar[i0] + tr, ai[i0] + ti
        size *= 2
    return ar, ai


def _pos_dft_long_kernel(cb_ref, sb_ref, ca_ref, sa_ref, fc_ref, fs_ref, o_ref, a_scr):
    wdt = fc_ref.shape[2]

    def rows_fft(r, carry):
        rows = pl.ds(pl.multiple_of(r * CHUNK, CHUNK), CHUNK)
        for b in range(wdt // LANES):
            lns = slice(b * LANES, (b + 1) * LANES)
            xr = [fc_ref[j, rows, lns].astype(F32) for j in range(N_TILES)]
            xi = [-fs_ref[j, rows, lns].astype(F32) for j in range(N_TILES)]
            ar, ai = _fft16(xr, xi)
            for k in range(N_TILES):
                a_scr[k, 0, rows, lns] = ar[k].astype(BF16)
                a_scr[k, 1, rows, lns] = ai[k].astype(BF16)
        return carry

    lax.fori_loop(0, SEG // CHUNK, rows_fft, 0)

    dot = functools.partial(jnp.dot, preferred_element_type=F32)
    cb, sb = cb_ref[...], sb_ref[...]
    for k1 in range(N_TILES):
        ca, sa = ca_ref[k1:k1 + 1, :], sa_ref[k1:k1 + 1, :]
        dc = (cb * ca - sb * sa).astype(BF16)
        ds = (sb * ca + cb * sa).astype(BF16)
        out = dot(dc, a_scr[k1, 0]) + dot(ds, a_scr[k1, 1])
        o_ref[:, k1 * SEG_CHUNKS:(k1 + 1) * SEG_CHUNKS, :] = out.astype(BF16).reshape(N_TILES, SEG_CHUNKS, wdt)


def _pos_dft_long(fc, fs, first, n):
    wdt = fc.shape[-1]
    l = N_TILES * SEG
    wb = 2 * LANES
    cb, sb = _tile_dft_tables(1.0 / math.sqrt(l), rows_in_tile_order=False)
    pos = _tile_order_positions(SEG)
    ang = 2.0 * np.pi * (np.arange(N_TILES)[:, None] * pos[None, :]) / l
    ca, sa = jnp.asarray(np.cos(ang), F32), jnp.asarray(np.sin(ang), F32)
    const2 = lambda shape: pl.BlockSpec(shape, lambda b, k: (0, 0), pipeline_mode=pl.Buffered(1))
    seq_in = pl.BlockSpec((None, N_TILES, SEG, wb), lambda b, k: (first + b, 0, 0, k))
    seq_out = pl.BlockSpec((None, N_TILES, SEG, wb), lambda b, k: (b, 0, 0, k))
    out = pl.pallas_call(
        _pos_dft_long_kernel,
        grid=(n, wdt // wb),
        in_specs=[const2((SEG, SEG)), const2((SEG, SEG)), const2((N_TILES, SEG)), const2((N_TILES, SEG)),
                  seq_in, seq_in],
        out_specs=seq_out,
        out_shape=jax.ShapeDtypeStruct((n, N_TILES, SEG, wdt), BF16),
        scratch_shapes=[pltpu.VMEM((N_TILES, 2, SEG, wb), BF16)],
        compiler_params=_cparams(("parallel", "parallel")),
        name="pos_dft_long",
    )(cb, sb, ca, sa, fc, fs)
    return out.reshape(n * l, wdt)


def _mix_ffn_kernel(n_prompt_tiles, xp_ref, xs_ref, mod_ref, ys_ref, yfp_ref, yfs_ref, gs_ref, gf_ref,
                    wglu_ref, bglu_ref, wps_ref, wpf_ref, wout_ref, n2_ref,
                    wg_ref, wu_ref, wd_ref, fn_ref, op_ref, os_ref, m_scr, mb_scr):
    tm = xp_ref.shape[0]
    dot = functools.partial(jnp.dot, preferred_element_type=F32)
    mod = mod_ref[0]
    is_prompt = pl.program_id(0) < n_prompt_tiles
    tiles = [slice(j * SEG, (j + 1) * SEG) for j in range(tm // SEG)]
    st = [dict() for _ in tiles]

    def gelu_phase(j, r, s):
        y = jnp.concatenate([ys_ref[b, r, :] for b in range(S5_WIDTH // LANES)], axis=1)
        s["z"] = jax.nn.gelu(y)

    def glu_phase(j, r, s):
        z = s["z"]
        s["z"] = (z * jax.nn.sigmoid(dot(z.astype(BF16), wglu_ref[...]) + bglu_ref[...])).astype(BF16)

    def proj_phase(j, r, s):
        yf = jnp.where(is_prompt, yfp_ref[r, :], yfs_ref[r, :])
        m = (gs_ref[r, :].astype(F32) * dot(s.pop("z"), wps_ref[...])
             + gf_ref[r, :].astype(F32) * dot(yf, wpf_ref[...]))
        for k in range(D_MODEL // LANES):
            for t in range(CHUNK):
                p0 = (j * CHUNK + t) * PITCH
                m_scr[k, p0:p0 + SEG_CHUNKS, :] = m[t * SEG_CHUNKS:(t + 1) * SEG_CHUNKS, k * LANES:(k + 1) * LANES]
        for c in range(SEG_CHUNKS):
            r0 = j * SEG + c * CHUNK
            for k in range(D_MODEL // LANES):
                mb_scr[r0:r0 + CHUNK, k * LANES:(k + 1) * LANES] = (
                    m_scr[k, pl.ds(j * CHUNK * PITCH + c, CHUNK, stride=PITCH), :].astype(BF16))

    def out_phase(j, r, s):
        x = jnp.where(is_prompt, xp_ref[r, :], xs_ref[r, :])
        x1 = x + mod[2:3] * dot(mb_scr[r, :], wout_ref[...])
        s["x1"] = x1
        s["h2"] = (_rms(x1, n2_ref[...]) * (1.0 + mod[4:5]) + mod[3:4]).astype(BF16)

    def ffn_phase(n0, n1, j, r, s):
        gate = dot(s["h2"], wg_ref[:, n0:n1])
        up = dot(s["h2"], wu_ref[:, n0:n1])
        part = dot((gate * jax.nn.sigmoid(gate) * up).astype(BF16), wd_ref[n0:n1, :])
        s["ff"] = part if "ff" not in s else s["ff"] + part

    def final_phase(j, r, s):
        x2 = s.pop("x1") + mod[5:6] * s.pop("ff")
        s["res"] = _rms(x2, fn_ref[...])

    phases = [gelu_phase, glu_phase, proj_phase, out_phase]
    phases += [functools.partial(ffn_phase, n0, n1) for n0, n1 in zip(FF_SPLITS[:-1], FF_SPLITS[1:])]
    phases += [final_phase]
    for phase in phases:
        for j, r in enumerate(tiles):
            phase(j, r, st[j])

    @pl.when(is_prompt)
    def _():
        for j, r in enumerate(tiles):
            op_ref[r, :] = st[j]["res"]

    @pl.when(jnp.logical_not(is_prompt))
    def _():
        for j, r in enumerate(tiles):
            os_ref[r, :] = st[j]["res"]


def _mix_ffn(xp, xs, seq_len, mod, ys, yf_p, yf_s, gs, gf, wts, tm):
    tp, ts = xp.shape[0], xs.shape[0]
    n_p = tp // tm
    tok = lambda w: pl.BlockSpec((tm, w), lambda i: (i, 0))
    return pl.pallas_call(
        functools.partial(_mix_ffn_kernel, n_p),
        grid=((tp + ts) // tm,),
        in_specs=_two_part_specs(tm, D_MODEL, n_p)
                 + [_mod_spec(n_p, seq_len // tm),
                    pl.BlockSpec((S5_WIDTH // LANES, tm, LANES), lambda i: (0, i, 0))]
                 + _two_part_specs(tm, FFT_WIDTH, n_p)
                 + [tok(D_MODEL), tok(D_MODEL)]
                 + [_const_spec(w.shape) for w in wts],
        out_specs=_two_part_specs(tm, D_MODEL, n_p),
        out_shape=[jax.ShapeDtypeStruct((tp, D_MODEL), F32), jax.ShapeDtypeStruct((ts, D_MODEL), F32)],
        scratch_shapes=[pltpu.VMEM((D_MODEL // LANES, tm // CHUNK * PITCH, LANES), F32),
                        pltpu.VMEM((tm, D_MODEL), BF16)],
        compiler_params=_cparams(("arbitrary",)),
        name="mix_ffn",
    )(xp, xs, mod, ys, yf_p, yf_s, gs, gf, *wts)


def kernel(x_prompt, x_sample, state_s5, c, c_ctx, norm1_g, norm2_g, w_ada, b_ada, w_in,
           s5_lambda_re, s5_lambda_im, s5_log_step, s5_b_re, s5_b_im, s5_c_re, s5_c_im,
           s5_d, w_glu, b_glu, w_proj_s5, w_proj_fft, w_out, w_ffn_gate, w_ffn_up,
           w_ffn_down, final_norm_g):
    nb, sl, _ = x_prompt.shape
    db, dl, _ = x_sample.shape
    assert w_in.shape[0] == 1 and sl == SEG and nb == N_PROMPT_SEG
    assert db == N_SAMPLE_SEQ and dl == SEG * SEG_CHUNKS

    cvec = jnp.concatenate([c_ctx[None], c, jnp.zeros((8 - 1 - db, D_MODEL), F32)], axis=0)
    mod, w_in_b, m, w, e, apre, apim = _adaln_and_s5_tables(
        cvec, w_ada[0], b_ada[0], w_in[0], s5_lambda_re[0], s5_lambda_im[0], s5_log_step[0], s5_b_re[0],
        s5_b_im[0], s5_c_re[0], s5_c_im[0], s5_d[0])
    mod = mod.reshape(8, N_MOD, D_MODEL)

    tm = 512
    xp = x_prompt.reshape(nb * sl, D_MODEL)
    xs = x_sample.reshape(db * dl, D_MODEL)
    later = (w_glu[0], w_proj_s5[0], w_proj_fft[0], w_out[0], w_ffn_gate[0], w_ffn_up[0], w_ffn_down[0])
    xg, fc, fs, gs, gf, *later_b = _inproj(xp, xs, dl, mod, norm1_g[0], w_in_b, later, tm)
    wglu_b, wps_b, wpf_b, wout_b, wg_b, wu_b, wd_b = later_b

    ys, fin = _s5_conv(xg, m, w, e, apre, apim, state_s5.astype(F32).reshape(db, 4 * GP))
    new_state = fin.reshape(nb, 1, 2, 2, S5_GROUPS, S5_STATE)

    yf_p = _pos_dft_tile(fc.reshape(-1, SEG, FFT_WIDTH), fs.reshape(-1, SEG, FFT_WIDTH), nb)
    n_prompt_units = nb // N_TILES
    yf_s = _pos_dft_long(fc.reshape(-1, N_TILES, SEG, FFT_WIDTH), fs.reshape(-1, N_TILES, SEG, FFT_WIDTH),
                         n_prompt_units, db)

    wts = (wglu_b, b_glu[0].reshape(1, S5_WIDTH), wps_b, wpf_b, wout_b, norm2_g[0].reshape(1, D_MODEL),
           wg_b, wu_b, wd_b, final_norm_g.reshape(1, D_MODEL))
    tm2 = 2 * SEG
    y_p, y_s = _mix_ffn(xp, xs, dl, mod, ys, yf_p.reshape(nb * sl, FFT_WIDTH), yf_s, gs, gf, wts, tm2)
    return (y_p.reshape(nb, sl, D_MODEL), y_s.reshape(db, dl, D_MODEL), new_state)
```

```python
import functools
import math

import numpy as np
import jax
import jax.numpy as jnp
from jax import lax
from jax.experimental import pallas as pl
from jax.experimental.pallas import tpu as pltpu

F32 = jnp.float32
BF16 = jnp.bfloat16

D_MODEL = 1024
S5_WIDTH = 512
S5_GROUPS = 32
S5_GROUP = 16
S5_STATE = 64
FFT_WIDTH = 512
FFT_GROUPS = 4
FFT_GROUP = 128
D_FF = 2816
N_MOD = 6
EPS = 1e-6

LANES = 128
CHUNK = 16
SEG_CHUNKS = 16
SEG = CHUNK * SEG_CHUNKS
GP = S5_GROUPS * S5_STATE
PAIRS = S5_GROUPS // 2
CH = CHUNK * S5_GROUP
OCT = LANES // S5_GROUP
PITCH = 24

MXU_DIM = 256
FF_SPLITS = (0, 3 * MXU_DIM, 6 * MXU_DIM, 9 * MXU_DIM, D_FF)
VMEM_LIMIT = 58 * 1024 * 1024


def _cparams(sem):
    return pltpu.CompilerParams(dimension_semantics=sem, vmem_limit_bytes=VMEM_LIMIT)


def _row_block_cast_specs(w, n_steps):
    rows = w.shape[0] // n_steps
    assert rows * n_steps == w.shape[0] and rows % 16 == 0, (w.shape, n_steps)
    spec = pl.BlockSpec((rows, w.shape[1]), lambda i: (jnp.minimum(i, n_steps - 1), 0))
    return spec, jax.ShapeDtypeStruct(w.shape, BF16)


def _cast_blocks(in_refs, out_refs):
    for src, dst in zip(in_refs, out_refs):
        dst[...] = src[...].astype(BF16)


def _ada_tile(c_ref, w_ref, b_ref, o_ref):
    dot = functools.partial(jnp.dot, preferred_element_type=F32)
    c = c_ref[...]
    s = c * jax.nn.sigmoid(c)
    s_hi = s.astype(BF16)
    s_lo = (s - s_hi.astype(F32)).astype(BF16)
    w = w_ref[...]
    w_hi = w.astype(BF16)
    w_lo = (w - w_hi.astype(F32)).astype(BF16)
    r = dot(jnp.concatenate([s_hi, s_lo], axis=0), w_hi)
    n = s.shape[0]
    o_ref[...] = r[0:n] + r[n:2 * n] + dot(s_hi, w_lo) + b_ref[...]


ROW_LAM_RE, ROW_LAM_IM, ROW_STEP, ROW_B_RE = 0, 2, 4, 8
ROW_B_IM = ROW_B_RE + 2 * S5_GROUP
ROW_C_RE = ROW_B_IM + 2 * S5_GROUP
ROW_C_IM = ROW_C_RE + 2 * S5_GROUP
PARAM_ROWS = ROW_C_IM + 2 * S5_GROUP


def _param_rows(p_ref, row0, d, lanes=slice(None)):
    return p_ref[row0 + d * S5_GROUP:row0 + (d + 1) * S5_GROUP, lanes]


def _exact_bf16_terms(x):
    hi = x.astype(BF16)
    r = x - hi.astype(F32)
    mid = r.astype(BF16)
    return hi, mid, (r - mid.astype(F32)).astype(BF16)


def _s5_params_to_lanes(lam_re_ref, lam_im_ref, lstep_ref, b_re_ref, b_im_ref, c_re_ref, c_im_ref, d_ref, ind_ref,
                        p_scr, dt_scr):
    dot_nt = lambda a, b: lax.dot_general(a, b, (((1,), (1,)), ((), ())), preferred_element_type=F32)
    dot = functools.partial(jnp.dot, preferred_element_type=F32)
    eye = (lax.broadcasted_iota(jnp.int32, (S5_GROUP, S5_GROUP), 0)
           == lax.broadcasted_iota(jnp.int32, (S5_GROUP, S5_GROUP), 1)).astype(BF16)
    p_scr[ROW_STEP + 2:ROW_B_RE, :] = jnp.zeros((ROW_B_RE - ROW_STEP - 2, GP), F32)
    p_scr[ROW_STEP:ROW_STEP + 2, :] = sum(dot(t, ind_ref[...]) for t in _exact_bf16_terms(lstep_ref[...]))
    for d in range(2):
        for row0, src in ((ROW_LAM_RE, lam_re_ref), (ROW_LAM_IM, lam_im_ref)):
            for q in range(PAIRS):
                p_scr[row0 + d:row0 + d + 1, q * LANES:(q + 1) * LANES] = jnp.concatenate(
                    [src[d, 2 * q:2 * q + 1, :], src[d, 2 * q + 1:2 * q + 2, :]], axis=1)
        for row0, src in ((ROW_B_RE, b_re_ref), (ROW_B_IM, b_im_ref)):
            p_scr[row0 + d * S5_GROUP:row0 + (d + 1) * S5_GROUP, :] = sum(
                dot_nt(eye, t) for t in _exact_bf16_terms(src[d]))
        for row0, src in ((ROW_C_RE, c_re_ref), (ROW_C_IM, c_im_ref)):
            for q in range(PAIRS):
                p_scr[row0 + d * S5_GROUP:row0 + (d + 1) * S5_GROUP, q * LANES:(q + 1) * LANES] = jnp.concatenate(
                    [src[d, 2 * q], src[d, 2 * q + 1]], axis=1)
    slot_h = lax.broadcasted_iota(jnp.int32, (S5_GROUP, LANES), 1) % S5_GROUP
    spread = (slot_h == lax.broadcasted_iota(jnp.int32, (S5_GROUP, LANES), 0)).astype(BF16)
    dt_scr[...] = sum(dot(t, spread) for t in _exact_bf16_terms(d_ref[...]))


def _s5_discretise(p_ref, apre_ref, apim_ref, gb_scr, ca_scr):
    lre = p_ref[ROW_LAM_RE:ROW_LAM_RE + 2, :]
    lim = p_ref[ROW_LAM_IM:ROW_LAM_IM + 2, :]
    step = jnp.exp(p_ref[ROW_STEP:ROW_STEP + 2, :])
    mag = jnp.exp(lre * step)
    are = mag * jnp.cos(lim * step)
    aim = mag * jnp.sin(lim * step)
    nr = are - 1.0
    den = lre * lre + lim * lim
    fr = (nr * lre + aim * lim) / den
    fi = (aim * lre - nr * lim) / den

    pr = [jnp.ones_like(are)]
    pi = [jnp.zeros_like(are)]
    for _ in range(CHUNK):
        r, i = pr[-1], pi[-1]
        pr.append(r * are - i * aim)
        pi.append(r * aim + i * are)
    a16r, a16i = pr[CHUNK], pi[CHUNK]
    qr, qi = jnp.ones_like(are), jnp.zeros_like(are)
    for c in range(SEG_CHUNKS + 1):
        apre_ref[c] = qr
        apim_ref[c] = qi
        qr, qi = qr * a16r - qi * a16i, qr * a16i + qi * a16r

    for d in range(2):
        bre, bim = _param_rows(p_ref, ROW_B_RE, d), _param_rows(p_ref, ROW_B_IM, d)
        bbre = fr[d:d + 1] * bre - fi[d:d + 1] * bim
        bbim = fr[d:d + 1] * bim + fi[d:d + 1] * bre
        cre, cim = _param_rows(p_ref, ROW_C_RE, d), _param_rows(p_ref, ROW_C_IM, d)
        for s in range(CHUNK):
            k = CHUNK - 1 - s if d == 0 else s
            r, i = pr[k][d:d + 1], pi[k][d:d + 1]
            gb_scr[0, d, s] = r * bbre - i * bbim
            gb_scr[1, d, s] = r * bbim + i * bbre
            f = s + 1 if d == 0 else CHUNK - s
            r, i = pr[f][d:d + 1], pi[f][d:d + 1]
            ca_scr[0, d, s] = r * cre - i * cim
            ca_scr[1, d, s] = -(r * cim + i * cre)


def _s5_pair_operands(q, slot_q, p_ref, dt_ref, gb_scr, ca_scr, ts_scr, m_ref, w_ref, e_ref):
    dot_nt = lambda a, b: lax.dot_general(a, b, (((1,), (1,)), ((), ())), preferred_element_type=F32)
    lane_gi = lax.broadcasted_iota(jnp.int32, (CH, LANES), 1) // S5_STATE
    row_gi = lax.broadcasted_iota(jnp.int32, (LANES, CH), 0) // S5_STATE
    eye = (lax.broadcasted_iota(jnp.int32, (LANES, LANES), 0)
           == lax.broadcasted_iota(jnp.int32, (LANES, LANES), 1)).astype(BF16)
    slot = lax.broadcasted_iota(jnp.int32, (CH, LANES), 1) // S5_GROUP
    n_lag_rows = (2 * CHUNK - 1) * S5_GROUP
    lanes = pl.ds(pl.multiple_of(q * LANES, LANES), LANES)
    for d in range(2):
        for ri in range(2):
            col = slice((2 * d + ri) * LANES, (2 * d + ri + 1) * LANES)
            gb = gb_scr[ri, d, :, :, lanes].reshape(CH, LANES)
            ca = ca_scr[ri, d, :, :, lanes].reshape(CH, LANES).astype(BF16)
            ca_t = dot_nt(eye, ca)
            for gi in range(2):
                w_ref[slot_q, gi * CH:(gi + 1) * CH, col] = jnp.where(lane_gi == gi, gb, 0.0).astype(BF16)
                e_ref[slot_q, col, gi * CH:(gi + 1) * CH] = jnp.where(row_gi == gi, ca_t, 0.0).astype(BF16)
    for gi in range(2):
        g = 2 * q + gi
        lag = []
        for d in range(2):
            c_re = jnp.concatenate([_param_rows(p_ref, ROW_C_RE, d, lanes)] * OCT, axis=0)
            c_imn = jnp.concatenate([-_param_rows(p_ref, ROW_C_IM, d, lanes)] * OCT, axis=0)
            keep = lax.broadcasted_iota(jnp.int32, (LANES, LANES), 1) // S5_STATE == gi
            c_re = jnp.where(keep, c_re, 0.0).astype(BF16)
            c_imn = jnp.where(keep, c_imn, 0.0).astype(BF16)
            gre = gb_scr[0, d, :, :, lanes].reshape(CH, LANES).astype(BF16)
            gim = gb_scr[1, d, :, :, lanes].reshape(CH, LANES).astype(BF16)
            lag.append(dot_nt(gre, c_re) + dot_nt(gim, c_imn))
        zl = (CHUNK - 1) * S5_GROUP
        ts_scr[0:zl, :] = lag[0][0:zl]
        on_diag = (lax.broadcasted_iota(jnp.int32, (S5_GROUP, LANES), 1) % S5_GROUP
                   == lax.broadcasted_iota(jnp.int32, (S5_GROUP, LANES), 0))
        skip = jnp.where(on_diag, dt_ref[pl.ds(g, 1), :], 0.0)
        ts_scr[zl:zl + S5_GROUP, :] = lag[0][zl:] + lag[1][0:S5_GROUP] + skip
        ts_scr[zl + S5_GROUP:n_lag_rows, :] = lag[1][S5_GROUP:]
        for hf in range(CHUNK // OCT):
            acc = None
            for s in range(OCT):
                t = OCT * hf + s
                win = ts_scr[(CHUNK - 1 - t) * S5_GROUP:(CHUNK - 1 - t) * S5_GROUP + CH, :]
                acc = win if acc is None else jnp.where(slot == s, win, acc)
            m_ref[2 * slot_q + gi, :, hf * LANES:(hf + 1) * LANES] = acc.astype(BF16)


PAIRS_PER_STEP = 2


def _ada_tables_kernel(c_ref, wada_ref, bada_ref, win_ref, lam_re_ref, lam_im_ref, lstep_ref, b_re_ref, b_im_ref,
                       c_re_ref, c_im_ref, d_ref, ind_ref, mod_ref, winb_ref, apre_ref, apim_ref, m_ref, w_ref, e_ref,
                       p_scr, dt_scr, gb_scr, ca_scr, ts_scr):
    j = pl.program_id(0)
    _cast_blocks([win_ref], [winb_ref])
    _ada_tile(c_ref, wada_ref, bada_ref, mod_ref)

    @pl.when(j == 0)
    def _():
        _s5_params_to_lanes(lam_re_ref, lam_im_ref, lstep_ref, b_re_ref, b_im_ref, c_re_ref, c_im_ref, d_ref, ind_ref,
                            p_scr, dt_scr)
        _s5_discretise(p_scr, apre_ref, apim_ref, gb_scr, ca_scr)

    for lp in range(PAIRS_PER_STEP):
        _s5_pair_operands(PAIRS_PER_STEP * j + lp, lp, p_scr, dt_scr, gb_scr, ca_scr, ts_scr, m_ref, w_ref, e_ref)


def _adaln_and_s5_tables(cvec8, w_ada, b_ada, w_in, lam_re, lam_im, log_step, b_re, b_im, c_re, c_im, s5_d):
    ind_np = np.zeros((S5_GROUPS, GP), np.float32)
    ind_np[np.arange(GP) // S5_STATE, np.arange(GP)] = 1.0
    s5_in = (lam_re, lam_im, log_step, b_re.reshape(2, GP, S5_GROUP), b_im.reshape(2, GP, S5_GROUP), c_re, c_im,
             s5_d.reshape(S5_GROUPS, S5_GROUP), jnp.asarray(ind_np).astype(BF16))

    n_steps = PAIRS // PAIRS_PER_STEP
    n_out = w_ada.shape[1]
    tn = n_out // n_steps
    cast_spec, cast_shape = _row_block_cast_specs(w_in, n_steps)
    whole = lambda a: pl.BlockSpec(a.shape, lambda j, nd=a.ndim: (0,) * nd)
    powers = jax.ShapeDtypeStruct((SEG_CHUNKS + 1, 2, GP), F32)
    tab = (2, 2, CHUNK, S5_GROUP, GP)
    mod, w_in_b, apre, apim, m, w, e = pl.pallas_call(
        _ada_tables_kernel,
        grid=(n_steps,),
        in_specs=[pl.BlockSpec((8, D_MODEL), lambda j: (0, 0)),
                  pl.BlockSpec((D_MODEL, tn), lambda j: (0, j)),
                  pl.BlockSpec((1, tn), lambda j: (0, j)),
                  cast_spec] + [whole(a) for a in s5_in],
        out_specs=[pl.BlockSpec((8, tn), lambda j: (0, j)), cast_spec, whole(powers), whole(powers),
                   pl.BlockSpec((2 * PAIRS_PER_STEP, CH, CH), lambda j: (j, 0, 0)),
                   pl.BlockSpec((PAIRS_PER_STEP, 2 * CH, 4 * LANES), lambda j: (j, 0, 0)),
                   pl.BlockSpec((PAIRS_PER_STEP, 4 * LANES, 2 * CH), lambda j: (j, 0, 0))],
        out_shape=[jax.ShapeDtypeStruct((8, n_out), F32), cast_shape, powers, powers,
                   jax.ShapeDtypeStruct((S5_GROUPS, CH, CH), BF16),
                   jax.ShapeDtypeStruct((PAIRS, 2 * CH, 4 * LANES), BF16),
                   jax.ShapeDtypeStruct((PAIRS, 4 * LANES, 2 * CH), BF16)],
        scratch_shapes=[pltpu.VMEM((PARAM_ROWS, GP), F32),
                        pltpu.VMEM((S5_GROUPS, LANES), F32),
                        pltpu.VMEM(tab, F32),
                        pltpu.VMEM(tab, F32),
                        pltpu.VMEM(((2 * CHUNK - 1) * S5_GROUP, LANES), F32)],
        compiler_params=_cparams(("arbitrary",)),
        name="adaln_s5_tables",
    )(cvec8, w_ada, b_ada.reshape(1, n_out), w_in, *s5_in)
    return mod, w_in_b, m, w, e, apre, apim


def _rms(x, g):
    return x * lax.rsqrt(jnp.mean(x * x, axis=-1, keepdims=True) + EPS) * g


def _slot_transpose(v):
    v = list(v)
    slot = lax.broadcasted_iota(jnp.int32, v[0].shape, 1) // S5_GROUP
    for k in (4, 2, 1):
        low = (slot & k) == 0
        for i in range(OCT):
            if i & k:
                continue
            a, b = v[i], v[i + k]
            v[i] = jnp.where(low, a, pltpu.roll(b, k * S5_GROUP, axis=1))
            v[i + k] = jnp.where(low, pltpu.roll(a, LANES - k * S5_GROUP, axis=1), b)
    return v


def _inproj_kernel(n_prompt_tiles, n_cast, xp_ref, xs_ref, mod_ref, g_ref, win_ref, bdc_ref, bds_ref,
                   *rest):
    cast_in, rest = rest[:n_cast], rest[n_cast:]
    xg_ref, fc_ref, fs_ref, gs_ref, gf_ref = rest[:5]
    cast_out, (h_scr, hb_scr, zs_scr) = rest[5:5 + n_cast], rest[5 + n_cast:]
    tm = xp_ref.shape[0]
    mod = mod_ref[0]
    is_prompt = pl.program_id(0) < n_prompt_tiles
    dot = functools.partial(jnp.dot, preferred_element_type=F32)
    o = 2 * S5_WIDTH
    tiles = [slice(j * SEG, (j + 1) * SEG) for j in range(tm // SEG)]

    def norm_phase(j, r):
        x = jnp.where(is_prompt, xp_ref[r, :], xs_ref[r, :])
        h = _rms(x, g_ref[...]) * (1.0 + mod[1:2]) + mod[0:1]
        for k in range(D_MODEL // LANES):
            for c in range(SEG_CHUNKS):
                p0 = (j * SEG_CHUNKS + c) * PITCH
                h_scr[k, p0:p0 + CHUNK, :] = h[c * CHUNK:(c + 1) * CHUNK, k * LANES:(k + 1) * LANES]
        for s in range(CHUNK):
            r0 = j * SEG + s * SEG_CHUNKS
            for k in range(D_MODEL // LANES):
                hb_scr[r0:r0 + SEG_CHUNKS, k * LANES:(k + 1) * LANES] = (
                    h_scr[k, pl.ds(j * SEG_CHUNKS * PITCH + s, SEG_CHUNKS, stride=PITCH), :].astype(BF16))

    def s5_gate_phase(j, r):
        zs_scr[r, :] = dot(hb_scr[r, :], win_ref[:, 0:S5_WIDTH])
        gs_ref[r, :] = jax.nn.sigmoid(dot(hb_scr[r, :], win_ref[:, o:o + D_MODEL])).astype(BF16)

    def fold_phase(j, r):
        for b in range(S5_WIDTH // LANES):
            for hf in range(CHUNK // OCT):
                z = [zs_scr[j * SEG + s * SEG_CHUNKS:j * SEG + (s + 1) * SEG_CHUNKS, b * LANES:(b + 1) * LANES]
                     for s in range(OCT * hf, OCT * (hf + 1))]
                for i, xi in enumerate(_slot_transpose(z)):
                    xg_ref[OCT * b + i, j * SEG_CHUNKS:(j + 1) * SEG_CHUNKS,
                           hf * LANES:(hf + 1) * LANES] = xi.astype(BF16)

    def fourier_gate_phase(j, r):
        uf = dot(hb_scr[r, :], win_ref[:, S5_WIDTH:2 * S5_WIDTH]).astype(BF16)
        for n0 in range(0, FFT_WIDTH, MXU_DIM):
            cols = slice(n0, n0 + MXU_DIM)
            fc_ref[r, cols] = dot(uf[:, cols], bdc_ref[cols, cols]).astype(BF16)
            fs_ref[r, cols] = dot(uf[:, cols], bds_ref[cols, cols]).astype(BF16)
        gf_ref[r, :] = jax.nn.sigmoid(dot(hb_scr[r, :], win_ref[:, o + D_MODEL:o + 2 * D_MODEL])).astype(BF16)

    for phase in (norm_phase, s5_gate_phase, fold_phase, fourier_gate_phase):
        for j, r in enumerate(tiles):
            phase(j, r)
    _cast_blocks(cast_in, cast_out)


def _channel_dft_mats():
    j = np.arange(FFT_GROUP)
    ang = 2.0 * np.pi * ((j[:, None] * j[None, :]) % FFT_GROUP) / FFT_GROUP
    blk_c = np.cos(ang) / math.sqrt(FFT_GROUP)
    blk_s = np.sin(ang) / math.sqrt(FFT_GROUP)
    bdc = np.kron(np.eye(FFT_GROUPS), blk_c)
    bds = np.kron(np.eye(FFT_GROUPS), blk_s)
    return jnp.asarray(bdc, F32).astype(BF16), jnp.asarray(bds, F32).astype(BF16)


def _const_spec(shape):
    nd = len(shape)
    return pl.BlockSpec(shape, lambda i: (0,) * nd, pipeline_mode=pl.Buffered(1))


def _two_part_specs(tm, width, n_prompt_tiles):
    return [pl.BlockSpec((tm, width), lambda i: (jnp.minimum(i, n_prompt_tiles - 1), 0)),
            pl.BlockSpec((tm, width), lambda i: (jnp.maximum(i - n_prompt_tiles, 0), 0))]


def _mod_spec(n_prompt_tiles, tiles_per_seq):
    row = lambda i: jnp.where(i < n_prompt_tiles, 0, 1 + (i - n_prompt_tiles) // tiles_per_seq)
    return pl.BlockSpec((1, N_MOD, D_MODEL), lambda i: (row(i), 0, 0))


N_CAST_STEPS = 16


def _inproj(xp, xs, seq_len, mod, norm_g, w_in_b, later_weights, tm):
    t = xp.shape[0] + xs.shape[0]
    n_p = xp.shape[0] // tm
    assert t // tm >= N_CAST_STEPS
    bdc, bds = _channel_dft_mats()
    casts = [_row_block_cast_specs(w, N_CAST_STEPS) for w in later_weights]
    tok = lambda w: pl.BlockSpec((tm, w), lambda i: (i, 0))
    out = lambda w: jax.ShapeDtypeStruct((t, w), BF16)
    return pl.pallas_call(
        functools.partial(_inproj_kernel, n_p, len(casts)),
        grid=(t // tm,),
        in_specs=_two_part_specs(tm, D_MODEL, n_p) + [
                  _mod_spec(n_p, seq_len // tm),
                  _const_spec((1, D_MODEL)),
                  _const_spec(w_in_b.shape),
                  _const_spec(bdc.shape),
                  _const_spec(bds.shape)] + [c[0] for c in casts],
        out_specs=[pl.BlockSpec((S5_GROUPS, tm // CHUNK, CH), lambda i: (0, i, 0)),
                   tok(FFT_WIDTH), tok(FFT_WIDTH), tok(D_MODEL), tok(D_MODEL)] + [c[0] for c in casts],
        out_shape=[jax.ShapeDtypeStruct((S5_GROUPS, t // CHUNK, CH), BF16),
                   out(FFT_WIDTH), out(FFT_WIDTH), out(D_MODEL), out(D_MODEL)] + [c[1] for c in casts],
        scratch_shapes=[pltpu.VMEM((D_MODEL // LANES, tm // CHUNK * PITCH, LANES), F32),
                        pltpu.VMEM((tm, D_MODEL), BF16),
                        pltpu.VMEM((tm, S5_WIDTH), F32)],
        compiler_params=_cparams(("arbitrary",)),
        name="inproj",
    )(xp, xs, mod, norm_g.reshape(1, D_MODEL), w_in_b, bdc, bds, *later_weights)


N_PROMPT_SEG = 16
N_SAMPLE_SEG = 32
N_SAMPLE_SEQ = 2
N_SEG = N_PROMPT_SEG + N_SAMPLE_SEG
ROWS = SEG_CHUNKS * N_SEG
ROWS_P = SEG_CHUNKS * N_PROMPT_SEG
OCT_PAIRS = OCT // 2
SEG_PITCH = 56


def _cmul_add(ar, ai, hr, hi, sr, si):
    return ar * hr - ai * hi + sr, ar * hi + ai * hr + si


def _s5_kernel(x_ref, m_ref, w_ref, e_ref, apre_ref, apim_ref, h0_ref,
               y_ref, fin_ref, s_scr, hin_scr, hinp_scr, f_scr, hs_scr, y_scr):
    dot = functools.partial(jnp.dot, preferred_element_type=F32)
    parts = ((0, ROWS_P), (ROWS_P, ROWS - ROWS_P))
    seg_rows = lambda c: pl.ds(c, N_SEG, stride=PITCH)
    blk = lambda c: pl.ds(c * SEG_PITCH, N_SEG)
    seq_rows = lambda j: pl.ds(j, N_SAMPLE_SEQ, stride=SEG_CHUNKS)
    lat = lambda c: pl.ds(c * SEG_PITCH + N_PROMPT_SEG, N_SAMPLE_SEG)
    zero = jnp.zeros((N_SEG, LANES), F32)
    octet = pl.program_id(0)
    state_lanes = lambda k, pr: pl.ds(pl.multiple_of(k * GP + (octet * OCT_PAIRS + pr) * LANES, LANES), LANES)

    for pr in range(OCT_PAIRS):
        ln = slice(pr * LANES, (pr + 1) * LANES)
        g0, g1 = 2 * pr, 2 * pr + 1
        for r0, nr in parts:
            s = (dot(x_ref[g0, r0:r0 + nr, :], w_ref[pr, 0:CH, :])
                 + dot(x_ref[g1, r0:r0 + nr, :], w_ref[pr, CH:2 * CH, :]))
            for k in range(4):
                for sg in range(nr // SEG_CHUNKS):
                    p0 = (r0 // SEG_CHUNKS + sg) * PITCH
                    s_scr[k, p0:p0 + SEG_CHUNKS, :] = s[sg * SEG_CHUNKS:(sg + 1) * SEG_CHUNKS, k * LANES:(k + 1) * LANES]

        ar, ai = apre_ref[1, 0:1, ln], apim_ref[1, 0:1, ln]
        hr, hi = zero, zero
        for c in range(SEG_CHUNKS):
            hin_scr[0, blk(c), :] = hr
            hin_scr[1, blk(c), :] = hi
            hr, hi = _cmul_add(ar, ai, hr, hi, s_scr[0, seg_rows(c), :], s_scr[1, seg_rows(c), :])
        fin_ref[:, state_lanes(0, pr)] = hr[0:N_PROMPT_SEG]
        fin_ref[:, state_lanes(1, pr)] = hi[0:N_PROMPT_SEG]
        f_scr[0] = hr[N_PROMPT_SEG:]
        f_scr[1] = hi[N_PROMPT_SEG:]
        br, bi = apre_ref[1, 1:2, ln], apim_ref[1, 1:2, ln]
        gr, gi = zero, zero
        for c in range(SEG_CHUNKS - 1, -1, -1):
            hin_scr[2, blk(c), :] = gr
            hin_scr[3, blk(c), :] = gi
            gr, gi = _cmul_add(br, bi, gr, gi, s_scr[2, seg_rows(c), :], s_scr[3, seg_rows(c), :])
        fin_ref[:, state_lanes(2, pr)] = gr[0:N_PROMPT_SEG]
        fin_ref[:, state_lanes(3, pr)] = gi[0:N_PROMPT_SEG]
        f_scr[2] = gr[N_PROMPT_SEG:]
        f_scr[3] = gi[N_PROMPT_SEG:]

        a2r, a2i = apre_ref[SEG_CHUNKS, 0:1, ln], apim_ref[SEG_CHUNKS, 0:1, ln]
        hr, hi = h0_ref[:, state_lanes(0, pr)], h0_ref[:, state_lanes(1, pr)]
        for j in range(SEG_CHUNKS):
            hs_scr[0, seq_rows(j), :] = hr
            hs_scr[1, seq_rows(j), :] = hi
            hr, hi = _cmul_add(a2r, a2i, hr, hi, f_scr[0, seq_rows(j), :], f_scr[1, seq_rows(j), :])
        b2r, b2i = apre_ref[SEG_CHUNKS, 1:2, ln], apim_ref[SEG_CHUNKS, 1:2, ln]
        gr, gi = h0_ref[:, state_lanes(2, pr)], h0_ref[:, state_lanes(3, pr)]
        for j in range(SEG_CHUNKS - 1, -1, -1):
            hs_scr[2, seq_rows(j), :] = gr
            hs_scr[3, seq_rows(j), :] = gi
            gr, gi = _cmul_add(b2r, b2i, gr, gi, f_scr[2, seq_rows(j), :], f_scr[3, seq_rows(j), :])
        for c in range(SEG_CHUNKS):
            p_r, p_i = apre_ref[c, 0:1, ln], apim_ref[c, 0:1, ln]
            hr, hi = _cmul_add(p_r, p_i, hs_scr[0], hs_scr[1], hin_scr[0, lat(c), :], hin_scr[1, lat(c), :])
            hin_scr[0, lat(c), :] = hr
            hin_scr[1, lat(c), :] = hi
            cb = SEG_CHUNKS - 1 - c
            p_r, p_i = apre_ref[cb, 1:2, ln], apim_ref[cb, 1:2, ln]
            gr, gi = _cmul_add(p_r, p_i, hs_scr[2], hs_scr[3], hin_scr[2, lat(c), :], hin_scr[3, lat(c), :])
            hin_scr[2, lat(c), :] = gr
            hin_scr[3, lat(c), :] = gi

        for k in range(4):
            for sg in range(N_SEG):
                hinp_scr[sg * SEG_CHUNKS:(sg + 1) * SEG_CHUNKS, k * LANES:(k + 1) * LANES] = (
                    hin_scr[k, pl.ds(sg, SEG_CHUNKS, stride=SEG_PITCH), :].astype(BF16))
        for gi_, g in ((0, g0), (1, g1)):
            for r0, nr in parts:
                y_scr[g, r0:r0 + nr, :] = (dot(x_ref[g, r0:r0 + nr, :], m_ref[g])
                                           + dot(hinp_scr[r0:r0 + nr, :], e_ref[pr, :, gi_ * CH:(gi_ + 1) * CH]))

    rb_rows = 2 * SEG_CHUNKS
    for rb in range(ROWS // rb_rows):
        for hf in range(CHUNK // OCT):
            v = [y_scr[i, rb * rb_rows:(rb + 1) * rb_rows, hf * LANES:(hf + 1) * LANES] for i in range(OCT)]
            for s, acc in enumerate(_slot_transpose(v)):
                t = OCT * hf + s
                for sg in range(rb_rows // SEG_CHUNKS):
                    seg = rb * (rb_rows // SEG_CHUNKS) + sg
                    r0 = seg * SEG + t * SEG_CHUNKS
                    y_ref[0, r0:r0 + SEG_CHUNKS, :] = acc[sg * SEG_CHUNKS:(sg + 1) * SEG_CHUNKS]


def _s5_conv(xg, m, w, e, apre, apim, h0):
    n_oct = S5_GROUPS // OCT
    assert xg.shape[1] == ROWS
    nt = ROWS * CHUNK
    return pl.pallas_call(
        _s5_kernel,
        grid=(n_oct,),
        in_specs=[pl.BlockSpec((OCT, ROWS, CH), lambda o: (o, 0, 0)),
                  pl.BlockSpec((OCT, CH, CH), lambda o: (o, 0, 0)),
                  pl.BlockSpec((OCT_PAIRS, 2 * CH, 4 * LANES), lambda o: (o, 0, 0)),
                  pl.BlockSpec((OCT_PAIRS, 4 * LANES, 2 * CH), lambda o: (o, 0, 0)),
                  pl.BlockSpec((SEG_CHUNKS + 1, 2, OCT_PAIRS * LANES), lambda o: (0, 0, o)),
                  pl.BlockSpec((SEG_CHUNKS + 1, 2, OCT_PAIRS * LANES), lambda o: (0, 0, o)),
                  pl.BlockSpec((N_SAMPLE_SEQ, 4 * GP), lambda o: (0, 0))],
        out_specs=[pl.BlockSpec((1, nt, LANES), lambda o: (o, 0, 0)),
                   pl.BlockSpec((N_PROMPT_SEG, 4 * GP), lambda o: (0, 0))],
        out_shape=[jax.ShapeDtypeStruct((n_oct, nt, LANES), F32),
                   jax.ShapeDtypeStruct((N_PROMPT_SEG, 4 * GP), F32)],
        scratch_shapes=[pltpu.VMEM((4, N_SEG * PITCH, LANES), F32),
                        pltpu.VMEM((4, SEG_CHUNKS * SEG_PITCH, LANES), F32),
                        pltpu.VMEM((ROWS, 4 * LANES), BF16),
                        pltpu.VMEM((4, N_SAMPLE_SEG, LANES), F32),
                        pltpu.VMEM((4, N_SAMPLE_SEG, LANES), F32),
                        pltpu.VMEM((OCT, ROWS, CH), F32)],
        compiler_params=_cparams(("arbitrary",)),
        name="s5_conv",
    )(xg, m, w, e, apre, apim, h0)


def _tile_order_positions(l):
    r = np.arange(l)
    tile, wi = r // SEG, r % SEG
    return tile * SEG + (wi % SEG_CHUNKS) * CHUNK + wi // SEG_CHUNKS


def _tile_dft_tables(scale, rows_in_tile_order=True):
    pos = _tile_order_positions(SEG)
    freq = pos if rows_in_tile_order else np.arange(SEG)
    ang = 2.0 * np.pi * ((freq[:, None] * pos[None, :]) % SEG) / SEG
    return jnp.asarray(np.cos(ang) * scale, F32), jnp.asarray(np.sin(ang) * scale, F32)


def _pos_dft_tile_kernel(cm_ref, sm_ref, fc_ref, fs_ref, o_ref):
    dot = functools.partial(jnp.dot, preferred_element_type=F32)
    cm, sm = cm_ref[...].astype(BF16), sm_ref[...].astype(BF16)
    for b in range(fc_ref.shape[0]):
        o_ref[b] = (dot(cm, fc_ref[b]) - dot(sm, fs_ref[b])).astype(BF16)


def _pos_dft_tile(fc, fs, n):
    _, l, wdt = fc.shape
    cm, sm = _tile_dft_tables(1.0 / math.sqrt(l))
    nb = 4
    seq = pl.BlockSpec((nb, l, wdt), lambda b: (b, 0, 0))
    return pl.pallas_call(
        _pos_dft_tile_kernel,
        grid=(n // nb,),
        in_specs=[_const_spec((l, l)), _const_spec((l, l)), seq, seq],
        out_specs=seq,
        out_shape=jax.ShapeDtypeStruct((n, l, wdt), BF16),
        compiler_params=_cparams(("parallel",)),
        name="pos_dft_tile",
    )(cm, sm, fc, fs)


N_TILES = 16


def _fft16(xr, xi):
    n = len(xr)
    rev = [int(format(i, "04b")[::-1], 2) for i in range(n)]
    ar = [xr[r] for r in rev]
    ai = [xi[r] for r in rev]
    size = 2
    while size <= n:
        half = size // 2
        for start in range(0, n, size):
            for k in range(half):
                wr = math.cos(2.0 * math.pi * k / size)
                wi = -math.sin(2.0 * math.pi * k / size)
                i0, i1 = start + k, start + k + half
                if k == 0:
                    tr, ti = ar[i1], ai[i1]
                elif 4 * k == size:
                    tr, ti = ai[i1], -ar[i1]
                else:
                    tr = ar[i1] * wr - ai[i1] * wi
                    ti = ar[i1] * wi + ai[i1] * wr
                ar[i1], ai[i1] = ar[i0] - tr, ai[i0] - ti
                ar[i0], ai[i0] = ar[i0] + tr, ai[i0] + ti
        size *= 2
    return ar, ai


def _pos_dft_long_kernel(cb_ref, sb_ref, ca_ref, sa_ref, fc_ref, fs_ref, o_ref, a_scr):
    wdt = fc_ref.shape[2]

    def rows_fft(r, carry):
        rows = pl.ds(pl.multiple_of(r * CHUNK, CHUNK), CHUNK)
        for b in range(wdt // LANES):
            lns = slice(b * LANES, (b + 1) * LANES)
            xr = [fc_ref[j, rows, lns].astype(F32) for j in range(N_TILES)]
            xi = [-fs_ref[j, rows, lns].astype(F32) for j in range(N_TILES)]
            ar, ai = _fft16(xr, xi)
            for k in range(N_TILES):
                a_scr[k, 0, rows, lns] = ar[k].astype(BF16)
                a_scr[k, 1, rows, lns] = ai[k].astype(BF16)
        return carry

    lax.fori_loop(0, SEG // CHUNK, rows_fft, 0)

    dot = functools.partial(jnp.dot, preferred_element_type=F32)
    cb, sb = cb_ref[...], sb_ref[...]
    for k1 in range(N_TILES):
        ca, sa = ca_ref[k1:k1 + 1, :], sa_ref[k1:k1 + 1, :]
        dc = (cb * ca - sb * sa).astype(BF16)
        ds = (sb * ca + cb * sa).astype(BF16)
        out = dot(dc, a_scr[k1, 0]) + dot(ds, a_scr[k1, 1])
        o_ref[:, k1 * SEG_CHUNKS:(k1 + 1) * SEG_CHUNKS, :] = out.astype(BF16).reshape(N_TILES, SEG_CHUNKS, wdt)


def _pos_dft_long(fc, fs, first, n):
    wdt = fc.shape[-1]
    l = N_TILES * SEG
    wb = 2 * LANES
    cb, sb = _tile_dft_tables(1.0 / math.sqrt(l), rows_in_tile_order=False)
    pos = _tile_order_positions(SEG)
    ang = 2.0 * np.pi * (np.arange(N_TILES)[:, None] * pos[None, :]) / l
    ca, sa = jnp.asarray(np.cos(ang), F32), jnp.asarray(np.sin(ang), F32)
    const2 = lambda shape: pl.BlockSpec(shape, lambda b, k: (0, 0), pipeline_mode=pl.Buffered(1))
    seq_in = pl.BlockSpec((None, N_TILES, SEG, wb), lambda b, k: (first + b, 0, 0, k))
    seq_out = pl.BlockSpec((None, N_TILES, SEG, wb), lambda b, k: (b, 0, 0, k))
    out = pl.pallas_call(
        _pos_dft_long_kernel,
        grid=(n, wdt // wb),
        in_specs=[const2((SEG, SEG)), const2((SEG, SEG)), const2((N_TILES, SEG)), const2((N_TILES, SEG)),
                  seq_in, seq_in],
        out_specs=seq_out,
        out_shape=jax.ShapeDtypeStruct((n, N_TILES, SEG, wdt), BF16),
        scratch_shapes=[pltpu.VMEM((N_TILES, 2, SEG, wb), BF16)],
        compiler_params=_cparams(("parallel", "parallel")),
        name="pos_dft_long",
    )(cb, sb, ca, sa, fc, fs)
    return out.reshape(n * l, wdt)


def _mix_ffn_kernel(n_prompt_tiles, xp_ref, xs_ref, mod_ref, ys_ref, yfp_ref, yfs_ref, gs_ref, gf_ref,
                    wglu_ref, bglu_ref, wps_ref, wpf_ref, wout_ref, n2_ref,
                    wg_ref, wu_ref, wd_ref, fn_ref, op_ref, os_ref, m_scr, mb_scr):
    tm = xp_ref.shape[0]
    dot = functools.partial(jnp.dot, preferred_element_type=F32)
    mod = mod_ref[0]
    is_prompt = pl.program_id(0) < n_prompt_tiles
    tiles = [slice(j * SEG, (j + 1) * SEG) for j in range(tm // SEG)]
    st = [dict() for _ in tiles]

    def gelu_phase(j, r, s):
        y = jnp.concatenate([ys_ref[b, r, :] for b in range(S5_WIDTH // LANES)], axis=1)
        s["z"] = jax.nn.gelu(y)

    def glu_phase(j, r, s):
        z = s["z"]
        s["z"] = (z * jax.nn.sigmoid(dot(z.astype(BF16), wglu_ref[...]) + bglu_ref[...])).astype(BF16)

    def proj_phase(j, r, s):
        yf = jnp.where(is_prompt, yfp_ref[r, :], yfs_ref[r, :])
        m = (gs_ref[r, :].astype(F32) * dot(s.pop("z"), wps_ref[...])
             + gf_ref[r, :].astype(F32) * dot(yf, wpf_ref[...]))
        for k in range(D_MODEL // LANES):
            for t in range(CHUNK):
                p0 = (j * CHUNK + t) * PITCH
                m_scr[k, p0:p0 + SEG_CHUNKS, :] = m[t * SEG_CHUNKS:(t + 1) * SEG_CHUNKS, k * LANES:(k + 1) * LANES]
        for c in range(SEG_CHUNKS):
            r0 = j * SEG + c * CHUNK
            for k in range(D_MODEL // LANES):
                mb_scr[r0:r0 + CHUNK, k * LANES:(k + 1) * LANES] = (
                    m_scr[k, pl.ds(j * CHUNK * PITCH + c, CHUNK, stride=PITCH), :].astype(BF16))

    def out_phase(j, r, s):
        x = jnp.where(is_prompt, xp_ref[r, :], xs_ref[r, :])
        x1 = x + mod[2:3] * dot(mb_scr[r, :], wout_ref[...])
        s["x1"] = x1
        s["h2"] = (_rms(x1, n2_ref[...]) * (1.0 + mod[4:5]) + mod[3:4]).astype(BF16)

    def ffn_phase(n0, n1, j, r, s):
        gate = dot(s["h2"], wg_ref[:, n0:n1])
        up = dot(s["h2"], wu_ref[:, n0:n1])
        part = dot((gate * jax.nn.sigmoid(gate) * up).astype(BF16), wd_ref[n0:n1, :])
        s["ff"] = part if "ff" not in s else s["ff"] + part

    def final_phase(j, r, s):
        x2 = s.pop("x1") + mod[5:6] * s.pop("ff")
        s["res"] = _rms(x2, fn_ref[...])

    phases = [gelu_phase, glu_phase, proj_phase, out_phase]
    phases += [functools.partial(ffn_phase, n0, n1) for n0, n1 in zip(FF_SPLITS[:-1], FF_SPLITS[1:])]
    phases += [final_phase]
    for phase in phases:
        for j, r in enumerate(tiles):
            phase(j, r, st[j])

    @pl.when(is_prompt)
    def _():
        for j, r in enumerate(tiles):
            op_ref[r, :] = st[j]["res"]

    @pl.when(jnp.logical_not(is_prompt))
    def _():
        for j, r in enumerate(tiles):
            os_ref[r, :] = st[j]["res"]


def _mix_ffn(xp, xs, seq_len, mod, ys, yf_p, yf_s, gs, gf, wts, tm):
    tp, ts = xp.shape[0], xs.shape[0]
    n_p = tp // tm
    tok = lambda w: pl.BlockSpec((tm, w), lambda i: (i, 0))
    return pl.pallas_call(
        functools.partial(_mix_ffn_kernel, n_p),
        grid=((tp + ts) // tm,),
        in_specs=_two_part_specs(tm, D_MODEL, n_p)
                 + [_mod_spec(n_p, seq_len // tm),
                    pl.BlockSpec((S5_WIDTH // LANES, tm, LANES), lambda i: (0, i, 0))]
                 + _two_part_specs(tm, FFT_WIDTH, n_p)
                 + [tok(D_MODEL), tok(D_MODEL)]
                 + [_const_spec(w.shape) for w in wts],
        out_specs=_two_part_specs(tm, D_MODEL, n_p),
        out_shape=[jax.ShapeDtypeStruct((tp, D_MODEL), F32), jax.ShapeDtypeStruct((ts, D_MODEL), F32)],
        scratch_shapes=[pltpu.VMEM((D_MODEL // LANES, tm // CHUNK * PITCH, LANES), F32),
                        pltpu.VMEM((tm, D_MODEL), BF16)],
        compiler_params=_cparams(("arbitrary",)),
        name="mix_ffn",
    )(xp, xs, mod, ys, yf_p, yf_s, gs, gf, *wts)


def kernel(x_prompt, x_sample, state_s5, c, c_ctx, norm1_g, norm2_g, w_ada, b_ada, w_in,
           s5_lambda_re, s5_lambda_im, s5_log_step, s5_b_re, s5_b_im, s5_c_re, s5_c_im,
           s5_d, w_glu, b_glu, w_proj_s5, w_proj_fft, w_out, w_ffn_gate, w_ffn_up,
           w_ffn_down, final_norm_g):
    nb, sl, _ = x_prompt.shape
    db, dl, _ = x_sample.shape
    assert w_in.shape[0] == 1 and sl == SEG and nb == N_PROMPT_SEG
    assert db == N_SAMPLE_SEQ and dl == SEG * SEG_CHUNKS

    cvec = jnp.concatenate([c_ctx[None], c, jnp.zeros((8 - 1 - db, D_MODEL), F32)], axis=0)
    mod, w_in_b, m, w, e, apre, apim = _adaln_and_s5_tables(
        cvec, w_ada[0], b_ada[0], w_in[0], s5_lambda_re[0], s5_lambda_im[0], s5_log_step[0], s5_b_re[0],
        s5_b_im[0], s5_c_re[0], s5_c_im[0], s5_d[0])
    mod = mod.reshape(8, N_MOD, D_MODEL)

    tm = 512
    xp = x_prompt.reshape(nb * sl, D_MODEL)
    xs = x_sample.reshape(db * dl, D_MODEL)
    later = (w_glu[0], w_proj_s5[0], w_proj_fft[0], w_out[0], w_ffn_gate[0], w_ffn_up[0], w_ffn_down[0])
    xg, fc, fs, gs, gf, *later_b = _inproj(xp, xs, dl, mod, norm1_g[0], w_in_b, later, tm)
    wglu_b, wps_b, wpf_b, wout_b, wg_b, wu_b, wd_b = later_b

    ys, fin = _s5_conv(xg, m, w, e, apre, apim, state_s5.astype(F32).reshape(db, 4 * GP))
    new_state = fin.reshape(nb, 1, 2, 2, S5_GROUPS, S5_STATE)

    yf_p = _pos_dft_tile(fc.reshape(-1, SEG, FFT_WIDTH), fs.reshape(-1, SEG, FFT_WIDTH), nb)
    n_prompt_units = nb // N_TILES
    yf_s = _pos_dft_long(fc.reshape(-1, N_TILES, SEG, FFT_WIDTH), fs.reshape(-1, N_TILES, SEG, FFT_WIDTH),
                         n_prompt_units, db)

    wts = (wglu_b, b_glu[0].reshape(1, S5_WIDTH), wps_b, wpf_b, wout_b, norm2_g[0].reshape(1, D_MODEL),
           wg_b, wu_b, wd_b, final_norm_g.reshape(1, D_MODEL))
    tm2 = 2 * SEG
    y_p, y_s = _mix_ffn(xp, xs, dl, mod, ys, yf_p.reshape(nb * sl, FFT_WIDTH), yf_s, gs, gf, wts, tm2)
    return (y_p.reshape(nb, sl, D_MODEL), y_s.reshape(db, dl, D_MODEL), new_state)
```

```python
import functools
import math

import numpy as np
import jax
import jax.numpy as jnp
from jax import lax
from jax.experimental import pallas as pl
from jax.experimental.pallas import tpu as pltpu

F32 = jnp.float32
BF16 = jnp.bfloat16

D_MODEL = 1024
S5_WIDTH = 512
S5_GROUPS = 32
S5_GROUP = 16
S5_STATE = 64
FFT_WIDTH = 512
FFT_GROUPS = 4
FFT_GROUP = 128
D_FF = 2816
N_MOD = 6
EPS = 1e-6

LANES = 128
CHUNK = 16
SEG_CHUNKS = 16
SEG = CHUNK * SEG_CHUNKS
GP = S5_GROUPS * S5_STATE
PAIRS = S5_GROUPS // 2
CH = CHUNK * S5_GROUP
OCT = LANES // S5_GROUP
PITCH = 24

MXU_DIM = 256
FF_SPLITS = (0, 3 * MXU_DIM, 6 * MXU_DIM, 9 * MXU_DIM, D_FF)
VMEM_LIMIT = 58 * 1024 * 1024


def _cparams(sem):
    return pltpu.CompilerParams(dimension_semantics=sem, vmem_limit_bytes=VMEM_LIMIT)


def _row_block_cast_specs(w, n_steps):
    rows = w.shape[0] // n_steps
    assert rows * n_steps == w.shape[0] and rows % 16 == 0, (w.shape, n_steps)
    spec = pl.BlockSpec((rows, w.shape[1]), lambda i: (jnp.minimum(i, n_steps - 1), 0))
    return spec, jax.ShapeDtypeStruct(w.shape, BF16)


def _cast_blocks(in_refs, out_refs):
    for src, dst in zip(in_refs, out_refs):
        dst[...] = src[...].astype(BF16)


def _ada_tile(cctx_ref, c_ref, w_ref, b_ref, o_ref):
    dot = functools.partial(jnp.dot, preferred_element_type=F32)
    n_pad = o_ref.shape[0] - 1 - c_ref.shape[0]
    c = jnp.concatenate([cctx_ref[...], c_ref[...], jnp.zeros((n_pad, c_ref.shape[1]), F32)], axis=0)
    s = c * jax.nn.sigmoid(c)
    s_hi = s.astype(BF16)
    s_lo = (s - s_hi.astype(F32)).astype(BF16)
    w = w_ref[...]
    w_hi = w.astype(BF16)
    w_lo = (w - w_hi.astype(F32)).astype(BF16)
    r = dot(jnp.concatenate([s_hi, s_lo], axis=0), w_hi)
    n = s.shape[0]
    o_ref[...] = r[0:n] + r[n:2 * n] + dot(s_hi, w_lo) + b_ref[...]


ROW_LAM_RE, ROW_LAM_IM, ROW_STEP, ROW_B_RE = 0, 2, 4, 8
ROW_B_IM = ROW_B_RE + 2 * S5_GROUP
ROW_C_RE = ROW_B_IM + 2 * S5_GROUP
ROW_C_IM = ROW_C_RE + 2 * S5_GROUP
PARAM_ROWS = ROW_C_IM + 2 * S5_GROUP


def _param_rows(p_ref, row0, d, lanes=slice(None)):
    return p_ref[row0 + d * S5_GROUP:row0 + (d + 1) * S5_GROUP, lanes]


def _exact_bf16_terms(x):
    hi = x.astype(BF16)
    r = x - hi.astype(F32)
    mid = r.astype(BF16)
    return hi, mid, (r - mid.astype(F32)).astype(BF16)


def _s5_params_to_lanes(lam_re_ref, lam_im_ref, lstep_ref, b_re_ref, b_im_ref, c_re_ref, c_im_ref, d_ref, ind_ref,
                        p_scr, dt_scr):
    dot_nt = lambda a, b: lax.dot_general(a, b, (((1,), (1,)), ((), ())), preferred_element_type=F32)
    dot = functools.partial(jnp.dot, preferred_element_type=F32)
    eye = (lax.broadcasted_iota(jnp.int32, (S5_GROUP, S5_GROUP), 0)
           == lax.broadcasted_iota(jnp.int32, (S5_GROUP, S5_GROUP), 1)).astype(BF16)
    p_scr[ROW_STEP + 2:ROW_B_RE, :] = jnp.zeros((ROW_B_RE - ROW_STEP - 2, GP), F32)
    p_scr[ROW_STEP:ROW_STEP + 2, :] = sum(dot(t, ind_ref[...]) for t in _exact_bf16_terms(lstep_ref[...]))
    for d in range(2):
        for row0, src in ((ROW_LAM_RE, lam_re_ref), (ROW_LAM_IM, lam_im_ref)):
            for q in range(PAIRS):
                p_scr[row0 + d:row0 + d + 1, q * LANES:(q + 1) * LANES] = jnp.concatenate(
                    [src[d, 2 * q:2 * q + 1, :], src[d, 2 * q + 1:2 * q + 2, :]], axis=1)
        for row0, src in ((ROW_B_RE, b_re_ref), (ROW_B_IM, b_im_ref)):
            p_scr[row0 + d * S5_GROUP:row0 + (d + 1) * S5_GROUP, :] = sum(
                dot_nt(eye, t) for t in _exact_bf16_terms(src[d].reshape(GP, S5_GROUP)))
        for row0, src in ((ROW_C_RE, c_re_ref), (ROW_C_IM, c_im_ref)):
            for q in range(PAIRS):
                p_scr[row0 + d * S5_GROUP:row0 + (d + 1) * S5_GROUP, q * LANES:(q + 1) * LANES] = jnp.concatenate(
                    [src[d, 2 * q], src[d, 2 * q + 1]], axis=1)
    slot_h = lax.broadcasted_iota(jnp.int32, (S5_GROUP, LANES), 1) % S5_GROUP
    spread = (slot_h == lax.broadcasted_iota(jnp.int32, (S5_GROUP, LANES), 0)).astype(BF16)
    dt_scr[...] = sum(dot(t, spread) for t in _exact_bf16_terms(d_ref[...]))


def _s5_discretise(p_ref, apre_ref, apim_ref, gb_scr, ca_scr):
    lre = p_ref[ROW_LAM_RE:ROW_LAM_RE + 2, :]
    lim = p_ref[ROW_LAM_IM:ROW_LAM_IM + 2, :]
    step = jnp.exp(p_ref[ROW_STEP:ROW_STEP + 2, :])
    mag = jnp.exp(lre * step)
    are = mag * jnp.cos(lim * step)
    aim = mag * jnp.sin(lim * step)
    nr = are - 1.0
    den = lre * lre + lim * lim
    fr = (nr * lre + aim * lim) / den
    fi = (aim * lre - nr * lim) / den

    pr = [jnp.ones_like(are)]
    pi = [jnp.zeros_like(are)]
    for _ in range(CHUNK):
        r, i = pr[-1], pi[-1]
        pr.append(r * are - i * aim)
        pi.append(r * aim + i * are)
    a16r, a16i = pr[CHUNK], pi[CHUNK]
    qr, qi = jnp.ones_like(are), jnp.zeros_like(are)
    for c in range(SEG_CHUNKS + 1):
        apre_ref[c] = qr
        apim_ref[c] = qi
        qr, qi = qr * a16r - qi * a16i, qr * a16i + qi * a16r

    for d in range(2):
        bre, bim = _param_rows(p_ref, ROW_B_RE, d), _param_rows(p_ref, ROW_B_IM, d)
        bbre = fr[d:d + 1] * bre - fi[d:d + 1] * bim
        bbim = fr[d:d + 1] * bim + fi[d:d + 1] * bre
        cre, cim = _param_rows(p_ref, ROW_C_RE, d), _param_rows(p_ref, ROW_C_IM, d)
        for s in range(CHUNK):
            k = CHUNK - 1 - s if d == 0 else s
            r, i = pr[k][d:d + 1], pi[k][d:d + 1]
            gb_scr[0, d, s] = r * bbre - i * bbim
            gb_scr[1, d, s] = r * bbim + i * bbre
            f = s + 1 if d == 0 else CHUNK - s
            r, i = pr[f][d:d + 1], pi[f][d:d + 1]
            ca_scr[0, d, s] = r * cre - i * cim
            ca_scr[1, d, s] = -(r * cim + i * cre)


def _s5_pair_operands(q, slot_q, p_ref, dt_ref, gb_scr, ca_scr, ts_scr, m_ref, w_ref, e_ref):
    dot_nt = lambda a, b: lax.dot_general(a, b, (((1,), (1,)), ((), ())), preferred_element_type=F32)
    lane_gi = lax.broadcasted_iota(jnp.int32, (CH, LANES), 1) // S5_STATE
    row_gi = lax.broadcasted_iota(jnp.int32, (LANES, CH), 0) // S5_STATE
    eye = (lax.broadcasted_iota(jnp.int32, (LANES, LANES), 0)
           == lax.broadcasted_iota(jnp.int32, (LANES, LANES), 1)).astype(BF16)
    slot = lax.broadcasted_iota(jnp.int32, (CH, LANES), 1) // S5_GROUP
    n_lag_rows = (2 * CHUNK - 1) * S5_GROUP
    lanes = pl.ds(pl.multiple_of(q * LANES, LANES), LANES)
    for d in range(2):
        for ri in range(2):
            col = slice((2 * d + ri) * LANES, (2 * d + ri + 1) * LANES)
            gb = gb_scr[ri, d, :, :, lanes].reshape(CH, LANES)
            ca = ca_scr[ri, d, :, :, lanes].reshape(CH, LANES).astype(BF16)
            ca_t = dot_nt(eye, ca)
            for gi in range(2):
                w_ref[slot_q, gi * CH:(gi + 1) * CH, col] = jnp.where(lane_gi == gi, gb, 0.0).astype(BF16)
                e_ref[slot_q, col, gi * CH:(gi + 1) * CH] = jnp.where(row_gi == gi, ca_t, 0.0).astype(BF16)
    for gi in range(2):
        g = 2 * q + gi
        lag = []
        for d in range(2):
            c_re = jnp.concatenate([_param_rows(p_ref, ROW_C_RE, d, lanes)] * OCT, axis=0)
            c_imn = jnp.concatenate([-_param_rows(p_ref, ROW_C_IM, d, lanes)] * OCT, axis=0)
            keep = lax.broadcasted_iota(jnp.int32, (LANES, LANES), 1) // S5_STATE == gi
            c_re = jnp.where(keep, c_re, 0.0).astype(BF16)
            c_imn = jnp.where(keep, c_imn, 0.0).astype(BF16)
            gre = gb_scr[0, d, :, :, lanes].reshape(CH, LANES).astype(BF16)
            gim = gb_scr[1, d, :, :, lanes].reshape(CH, LANES).astype(BF16)
            lag.append(dot_nt(gre, c_re) + dot_nt(gim, c_imn))
        zl = (CHUNK - 1) * S5_GROUP
        ts_scr[0:zl, :] = lag[0][0:zl]
        on_diag = (lax.broadcasted_iota(jnp.int32, (S5_GROUP, LANES), 1) % S5_GROUP
                   == lax.broadcasted_iota(jnp.int32, (S5_GROUP, LANES), 0))
        skip = jnp.where(on_diag, dt_ref[pl.ds(g, 1), :], 0.0)
        ts_scr[zl:zl + S5_GROUP, :] = lag[0][zl:] + lag[1][0:S5_GROUP] + skip
        ts_scr[zl + S5_GROUP:n_lag_rows, :] = lag[1][S5_GROUP:]
        for hf in range(CHUNK // OCT):
            acc = None
            for s in range(OCT):
                t = OCT * hf + s
                win = ts_scr[(CHUNK - 1 - t) * S5_GROUP:(CHUNK - 1 - t) * S5_GROUP + CH, :]
                acc = win if acc is None else jnp.where(slot == s, win, acc)
            m_ref[2 * slot_q + gi, :, hf * LANES:(hf + 1) * LANES] = acc.astype(BF16)


PAIRS_PER_STEP = 2


def _ada_tables_kernel(cctx_ref, c_ref, wada_ref, bada_ref, win_ref, lam_re_ref, lam_im_ref, lstep_ref, b_re_ref, b_im_ref,
                       c_re_ref, c_im_ref, d_ref, ind_ref, mod_ref, winb_ref, apre_ref, apim_ref, m_ref, w_ref, e_ref,
                       p_scr, dt_scr, gb_scr, ca_scr, ts_scr):
    j = pl.program_id(0)
    _cast_blocks([win_ref], [winb_ref])
    _ada_tile(cctx_ref, c_ref, wada_ref, bada_ref, mod_ref)

    @pl.when(j == 0)
    def _():
        _s5_params_to_lanes(lam_re_ref, lam_im_ref, lstep_ref, b_re_ref, b_im_ref, c_re_ref, c_im_ref, d_ref, ind_ref,
                            p_scr, dt_scr)
        _s5_discretise(p_scr, apre_ref, apim_ref, gb_scr, ca_scr)

    for lp in range(PAIRS_PER_STEP):
        _s5_pair_operands(PAIRS_PER_STEP * j + lp, lp, p_scr, dt_scr, gb_scr, ca_scr, ts_scr, m_ref, w_ref, e_ref)


def _adaln_and_s5_tables(c_ctx, c, w_ada, b_ada, w_in, lam_re, lam_im, log_step, b_re, b_im, c_re, c_im, s5_d):
    ind_np = np.zeros((S5_GROUPS, GP), np.float32)
    ind_np[np.arange(GP) // S5_STATE, np.arange(GP)] = 1.0
    s5_in = (lam_re, lam_im, log_step, b_re, b_im, c_re, c_im,
             s5_d.reshape(S5_GROUPS, S5_GROUP), jnp.asarray(ind_np).astype(BF16))

    n_steps = PAIRS // PAIRS_PER_STEP
    n_out = w_ada.shape[1]
    tn = n_out // n_steps
    cast_spec, cast_shape = _row_block_cast_specs(w_in, n_steps)
    whole = lambda a: pl.BlockSpec(a.shape, lambda j, nd=a.ndim: (0,) * nd)
    powers = jax.ShapeDtypeStruct((SEG_CHUNKS + 1, 2, GP), F32)
    tab = (2, 2, CHUNK, S5_GROUP, GP)
    mod, w_in_b, apre, apim, m, w, e = pl.pallas_call(
        _ada_tables_kernel,
        grid=(n_steps,),
        in_specs=[whole(c_ctx), whole(c),
                  pl.BlockSpec((D_MODEL, tn), lambda j: (0, j)),
                  pl.BlockSpec((1, tn), lambda j: (0, j)),
                  cast_spec] + [whole(a) for a in s5_in],
        out_specs=[pl.BlockSpec((8, tn), lambda j: (0, j)), cast_spec, whole(powers), whole(powers),
                   pl.BlockSpec((2 * PAIRS_PER_STEP, CH, CH), lambda j: (j, 0, 0)),
                   pl.BlockSpec((PAIRS_PER_STEP, 2 * CH, 4 * LANES), lambda j: (j, 0, 0)),
                   pl.BlockSpec((PAIRS_PER_STEP, 4 * LANES, 2 * CH), lambda j: (j, 0, 0))],
        out_shape=[jax.ShapeDtypeStruct((8, n_out), F32), cast_shape, powers, powers,
                   jax.ShapeDtypeStruct((S5_GROUPS, CH, CH), BF16),
                   jax.ShapeDtypeStruct((PAIRS, 2 * CH, 4 * LANES), BF16),
                   jax.ShapeDtypeStruct((PAIRS, 4 * LANES, 2 * CH), BF16)],
        scratch_shapes=[pltpu.VMEM((PARAM_ROWS, GP), F32),
                        pltpu.VMEM((S5_GROUPS, LANES), F32),
                        pltpu.VMEM(tab, F32),
                        pltpu.VMEM(tab, F32),
                        pltpu.VMEM(((2 * CHUNK - 1) * S5_GROUP, LANES), F32)],
        compiler_params=_cparams(("arbitrary",)),
        name="adaln_s5_tables",
    )(c_ctx, c, w_ada, b_ada.reshape(1, n_out), w_in, *s5_in)
    return mod, w_in_b, m, w, e, apre, apim


def _rms(x, g):
    return x * lax.rsqrt(jnp.mean(x * x, axis=-1, keepdims=True) + EPS) * g


def _slot_transpose(v):
    v = list(v)
    slot = lax.broadcasted_iota(jnp.int32, v[0].shape, 1) // S5_GROUP
    for k in (4, 2, 1):
        low = (slot & k) == 0
        for i in range(OCT):
            if i & k:
                continue
            a, b = v[i], v[i + k]
            v[i] = jnp.where(low, a, pltpu.roll(b, k * S5_GROUP, axis=1))
            v[i + k] = jnp.where(low, pltpu.roll(a, LANES - k * S5_GROUP, axis=1), b)
    return v


def _inproj_kernel(n_prompt_tiles, n_cast, xp_ref, xs_ref, mod_ref, g_ref, win_ref, bdc_ref, bds_ref,
                   *rest):
    cast_in, rest = rest[:n_cast], rest[n_cast:]
    xg_ref, fc_ref, fs_ref, gs_ref, gf_ref = rest[:5]
    cast_out, (h_scr, hb_scr, zs_scr) = rest[5:5 + n_cast], rest[5 + n_cast:]
    tm = xp_ref.shape[0]
    mod = mod_ref[0]
    is_prompt = pl.program_id(0) < n_prompt_tiles
    dot = functools.partial(jnp.dot, preferred_element_type=F32)
    o = 2 * S5_WIDTH
    tiles = [slice(j * SEG, (j + 1) * SEG) for j in range(tm // SEG)]

    def norm_phase(j, r):
        x = jnp.where(is_prompt, xp_ref[r, :], xs_ref[r, :])
        h = _rms(x, g_ref[...]) * (1.0 + mod[1:2]) + mod[0:1]
        for k in range(D_MODEL // LANES):
            for c in range(SEG_CHUNKS):
                p0 = (j * SEG_CHUNKS + c) * PITCH
                h_scr[k, p0:p0 + CHUNK, :] = h[c * CHUNK:(c + 1) * CHUNK, k * LANES:(k + 1) * LANES]
        for s in range(CHUNK):
            r0 = j * SEG + s * SEG_CHUNKS
            for k in range(D_MODEL // LANES):
                hb_scr[r0:r0 + SEG_CHUNKS, k * LANES:(k + 1) * LANES] = (
                    h_scr[k, pl.ds(j * SEG_CHUNKS * PITCH + s, SEG_CHUNKS, stride=PITCH), :].astype(BF16))

    def s5_gate_phase(j, r):
        zs_scr[r, :] = dot(hb_scr[r, :], win_ref[:, 0:S5_WIDTH])
        gs_ref[r, :] = jax.nn.sigmoid(dot(hb_scr[r, :], win_ref[:, o:o + D_MODEL])).astype(BF16)

    def fold_phase(j, r):
        for b in range(S5_WIDTH // LANES):
            for hf in range(CHUNK // OCT):
                z = [zs_scr[j * SEG + s * SEG_CHUNKS:j * SEG + (s + 1) * SEG_CHUNKS, b * LANES:(b + 1) * LANES]
                     for s in range(OCT * hf, OCT * (hf + 1))]
                for i, xi in enumerate(_slot_transpose(z)):
                    xg_ref[OCT * b + i, j * SEG_CHUNKS:(j + 1) * SEG_CHUNKS,
                           hf * LANES:(hf + 1) * LANES] = xi.astype(BF16)

    def fourier_gate_phase(j, r):
        uf = dot(hb_scr[r, :], win_ref[:, S5_WIDTH:2 * S5_WIDTH]).astype(BF16)
        for n0 in range(0, FFT_WIDTH, MXU_DIM):
            cols = slice(n0, n0 + MXU_DIM)
            fc_ref[r, cols] = dot(uf[:, cols], bdc_ref[cols, cols]).astype(BF16)
            fs_ref[r, cols] = dot(uf[:, cols], bds_ref[cols, cols]).astype(BF16)
        gf_ref[r, :] = jax.nn.sigmoid(dot(hb_scr[r, :], win_ref[:, o + D_MODEL:o + 2 * D_MODEL])).astype(BF16)

    for phase in (norm_phase, s5_gate_phase, fold_phase, fourier_gate_phase):
        for j, r in enumerate(tiles):
            phase(j, r)
    _cast_blocks(cast_in, cast_out)


def _channel_dft_mats():
    j = np.arange(FFT_GROUP)
    ang = 2.0 * np.pi * ((j[:, None] * j[None, :]) % FFT_GROUP) / FFT_GROUP
    blk_c = np.cos(ang) / math.sqrt(FFT_GROUP)
    blk_s = np.sin(ang) / math.sqrt(FFT_GROUP)
    bdc = np.kron(np.eye(FFT_GROUPS), blk_c)
    bds = np.kron(np.eye(FFT_GROUPS), blk_s)
    return jnp.asarray(bdc, F32).astype(BF16), jnp.asarray(bds, F32).astype(BF16)


def _const_spec(shape):
    nd = len(shape)
    return pl.BlockSpec(shape, lambda i: (0,) * nd, pipeline_mode=pl.Buffered(1))


def _two_part_specs(tm, width, n_prompt_tiles):
    return [pl.BlockSpec((tm, width), lambda i: (jnp.minimum(i, n_prompt_tiles - 1), 0)),
            pl.BlockSpec((tm, width), lambda i: (jnp.maximum(i - n_prompt_tiles, 0), 0))]


def _mod_spec(n_prompt_tiles, tiles_per_seq):
    row = lambda i: jnp.where(i < n_prompt_tiles, 0, 1 + (i - n_prompt_tiles) // tiles_per_seq)
    return pl.BlockSpec((1, N_MOD, D_MODEL), lambda i: (row(i), 0, 0))


N_CAST_STEPS = 16


def _inproj(xp, xs, seq_len, mod, norm_g, w_in_b, later_weights, tm):
    t = xp.shape[0] + xs.shape[0]
    n_p = xp.shape[0] // tm
    assert t // tm >= N_CAST_STEPS
    bdc, bds = _channel_dft_mats()
    casts = [_row_block_cast_specs(w, N_CAST_STEPS) for w in later_weights]
    tok = lambda w: pl.BlockSpec((tm, w), lambda i: (i, 0))
    out = lambda w: jax.ShapeDtypeStruct((t, w), BF16)
    return pl.pallas_call(
        functools.partial(_inproj_kernel, n_p, len(casts)),
        grid=(t // tm,),
        in_specs=_two_part_specs(tm, D_MODEL, n_p) + [
                  _mod_spec(n_p, seq_len // tm),
                  _const_spec((1, D_MODEL)),
                  _const_spec(w_in_b.shape),
                  _const_spec(bdc.shape),
                  _const_spec(bds.shape)] + [c[0] for c in casts],
        out_specs=[pl.BlockSpec((S5_GROUPS, tm // CHUNK, CH), lambda i: (0, i, 0)),
                   tok(FFT_WIDTH), tok(FFT_WIDTH), tok(D_MODEL), tok(D_MODEL)] + [c[0] for c in casts],
        out_shape=[jax.ShapeDtypeStruct((S5_GROUPS, t // CHUNK, CH), BF16),
                   out(FFT_WIDTH), out(FFT_WIDTH), out(D_MODEL), out(D_MODEL)] + [c[1] for c in casts],
        scratch_shapes=[pltpu.VMEM((D_MODEL // LANES, tm // CHUNK * PITCH, LANES), F32),
                        pltpu.VMEM((tm, D_MODEL), BF16),
                        pltpu.VMEM((tm, S5_WIDTH), F32)],
        compiler_params=_cparams(("arbitrary",)),
        name="inproj",
    )(xp, xs, mod, norm_g.reshape(1, D_MODEL), w_in_b, bdc, bds, *later_weights)


N_PROMPT_SEG = 16
N_SAMPLE_SEG = 32
N_SAMPLE_SEQ = 2
N_SEG = N_PROMPT_SEG + N_SAMPLE_SEG
ROWS = SEG_CHUNKS * N_SEG
ROWS_P = SEG_CHUNKS * N_PROMPT_SEG
OCT_PAIRS = OCT // 2
SEG_PITCH = 56


def _cmul_add(ar, ai, hr, hi, sr, si):
    return ar * hr - ai * hi + sr, ar * hi + ai * hr + si


def _s5_kernel(x_ref, m_ref, w_ref, e_ref, apre_ref, apim_ref, h0_ref,
               y_ref, fin_ref, s_scr, hin_scr, hinp_scr, f_scr, hs_scr, y_scr):
    dot = functools.partial(jnp.dot, preferred_element_type=F32)
    parts = ((0, ROWS_P), (ROWS_P, ROWS - ROWS_P))
    seg_rows = lambda c: pl.ds(c, N_SEG, stride=PITCH)
    blk = lambda c: pl.ds(c * SEG_PITCH, N_SEG)
    seq_rows = lambda j: pl.ds(j, N_SAMPLE_SEQ, stride=SEG_CHUNKS)
    lat = lambda c: pl.ds(c * SEG_PITCH + N_PROMPT_SEG, N_SAMPLE_SEG)
    zero = jnp.zeros((N_SEG, LANES), F32)
    octet = pl.program_id(0)
    state_lanes = lambda k, pr: pl.ds(pl.multiple_of(k * GP + (octet * OCT_PAIRS + pr) * LANES, LANES), LANES)

    for pr in range(OCT_PAIRS):
        ln = slice(pr * LANES, (pr + 1) * LANES)
        g0, g1 = 2 * pr, 2 * pr + 1
        for r0, nr in parts:
            s = (dot(x_ref[g0, r0:r0 + nr, :], w_ref[pr, 0:CH, :])
                 + dot(x_ref[g1, r0:r0 + nr, :], w_ref[pr, CH:2 * CH, :]))
            for k in range(4):
                for sg in range(nr // SEG_CHUNKS):
                    p0 = (r0 // SEG_CHUNKS + sg) * PITCH
                    s_scr[k, p0:p0 + SEG_CHUNKS, :] = s[sg * SEG_CHUNKS:(sg + 1) * SEG_CHUNKS, k * LANES:(k + 1) * LANES]

        ar, ai = apre_ref[1, 0:1, ln], apim_ref[1, 0:1, ln]
        hr, hi = zero, zero
        for c in range(SEG_CHUNKS):
            hin_scr[0, blk(c), :] = hr
            hin_scr[1, blk(c), :] = hi
            hr, hi = _cmul_add(ar, ai, hr, hi, s_scr[0, seg_rows(c), :], s_scr[1, seg_rows(c), :])
        fin_ref[:, state_lanes(0, pr)] = hr[0:N_PROMPT_SEG]
        fin_ref[:, state_lanes(1, pr)] = hi[0:N_PROMPT_SEG]
        f_scr[0] = hr[N_PROMPT_SEG:]
        f_scr[1] = hi[N_PROMPT_SEG:]
        br, bi = apre_ref[1, 1:2, ln], apim_ref[1, 1:2, ln]
        gr, gi = zero, zero
        for c in range(SEG_CHUNKS - 1, -1, -1):
            hin_scr[2, blk(c), :] = gr
            hin_scr[3, blk(c), :] = gi
            gr, gi = _cmul_add(br, bi, gr, gi, s_scr[2, seg_rows(c), :], s_scr[3, seg_rows(c), :])
        fin_ref[:, state_lanes(2, pr)] = gr[0:N_PROMPT_SEG]
        fin_ref[:, state_lanes(3, pr)] = gi[0:N_PROMPT_SEG]
        f_scr[2] = gr[N_PROMPT_SEG:]
        f_scr[3] = gi[N_PROMPT_SEG:]

        a2r, a2i = apre_ref[SEG_CHUNKS, 0:1, ln], apim_ref[SEG_CHUNKS, 0:1, ln]
        hr, hi = h0_ref[:, state_lanes(0, pr)], h0_ref[:, state_lanes(1, pr)]
        for j in range(SEG_CHUNKS):
            hs_scr[0, seq_rows(j), :] = hr
            hs_scr[1, seq_rows(j), :] = hi
            hr, hi = _cmul_add(a2r, a2i, hr, hi, f_scr[0, seq_rows(j), :], f_scr[1, seq_rows(j), :])
        b2r, b2i = apre_ref[SEG_CHUNKS, 1:2, ln], apim_ref[SEG_CHUNKS, 1:2, ln]
        gr, gi = h0_ref[:, state_lanes(2, pr)], h0_ref[:, state_lanes(3, pr)]
        for j in range(SEG_CHUNKS - 1, -1, -1):
            hs_scr[2, seq_rows(j), :] = gr
            hs_scr[3, seq_rows(j), :] = gi
            gr, gi = _cmul_add(b2r, b2i, gr, gi, f_scr[2, seq_rows(j), :], f_scr[3, seq_rows(j), :])
        for c in range(SEG_CHUNKS):
            p_r, p_i = apre_ref[c, 0:1, ln], apim_ref[c, 0:1, ln]
            hr, hi = _cmul_add(p_r, p_i, hs_scr[0], hs_scr[1], hin_scr[0, lat(c), :], hin_scr[1, lat(c), :])
            hin_scr[0, lat(c), :] = hr
            hin_scr[1, lat(c), :] = hi
            cb = SEG_CHUNKS - 1 - c
            p_r, p_i = apre_ref[cb, 1:2, ln], apim_ref[cb, 1:2, ln]
            gr, gi = _cmul_add(p_r, p_i, hs_scr[2], hs_scr[3], hin_scr[2, lat(c), :], hin_scr[3, lat(c), :])
            hin_scr[2, lat(c), :] = gr
            hin_scr[3, lat(c), :] = gi

        for k in range(4):
            for sg in range(N_SEG):
                hinp_scr[sg * SEG_CHUNKS:(sg + 1) * SEG_CHUNKS, k * LANES:(k + 1) * LANES] = (
                    hin_scr[k, pl.ds(sg, SEG_CHUNKS, stride=SEG_PITCH), :].astype(BF16))
        for gi_, g in ((0, g0), (1, g1)):
            for r0, nr in parts:
                y_scr[g, r0:r0 + nr, :] = (dot(x_ref[g, r0:r0 + nr, :], m_ref[g])
                                           + dot(hinp_scr[r0:r0 + nr, :], e_ref[pr, :, gi_ * CH:(gi_ + 1) * CH]))

    rb_rows = 2 * SEG_CHUNKS
    for rb in range(ROWS // rb_rows):
        for hf in range(CHUNK // OCT):
            v = [y_scr[i, rb * rb_rows:(rb + 1) * rb_rows, hf * LANES:(hf + 1) * LANES] for i in range(OCT)]
            for s, acc in enumerate(_slot_transpose(v)):
                t = OCT * hf + s
                for sg in range(rb_rows // SEG_CHUNKS):
                    seg = rb * (rb_rows // SEG_CHUNKS) + sg
                    r0 = seg * SEG + t * SEG_CHUNKS
                    y_ref[0, r0:r0 + SEG_CHUNKS, :] = acc[sg * SEG_CHUNKS:(sg + 1) * SEG_CHUNKS]


def _s5_conv(xg, m, w, e, apre, apim, h0):
    n_oct = S5_GROUPS // OCT
    assert xg.shape[1] == ROWS
    nt = ROWS * CHUNK
    return pl.pallas_call(
        _s5_kernel,
        grid=(n_oct,),
        in_specs=[pl.BlockSpec((OCT, ROWS, CH), lambda o: (o, 0, 0)),
                  pl.BlockSpec((OCT, CH, CH), lambda o: (o, 0, 0)),
                  pl.BlockSpec((OCT_PAIRS, 2 * CH, 4 * LANES), lambda o: (o, 0, 0)),
                  pl.BlockSpec((OCT_PAIRS, 4 * LANES, 2 * CH), lambda o: (o, 0, 0)),
                  pl.BlockSpec((SEG_CHUNKS + 1, 2, OCT_PAIRS * LANES), lambda o: (0, 0, o)),
                  pl.BlockSpec((SEG_CHUNKS + 1, 2, OCT_PAIRS * LANES), lambda o: (0, 0, o)),
                  pl.BlockSpec((N_SAMPLE_SEQ, 4 * GP), lambda o: (0, 0))],
        out_specs=[pl.BlockSpec((1, nt, LANES), lambda o: (o, 0, 0)),
                   pl.BlockSpec((N_PROMPT_SEG, 4 * GP), lambda o: (0, 0))],
        out_shape=[jax.ShapeDtypeStruct((n_oct, nt, LANES), F32),
                   jax.ShapeDtypeStruct((N_PROMPT_SEG, 4 * GP), F32)],
        scratch_shapes=[pltpu.VMEM((4, N_SEG * PITCH, LANES), F32),
                        pltpu.VMEM((4, SEG_CHUNKS * SEG_PITCH, LANES), F32),
                        pltpu.VMEM((ROWS, 4 * LANES), BF16),
                        pltpu.VMEM((4, N_SAMPLE_SEG, LANES), F32),
                        pltpu.VMEM((4, N_SAMPLE_SEG, LANES), F32),
                        pltpu.VMEM((OCT, ROWS, CH), F32)],
        compiler_params=_cparams(("arbitrary",)),
        name="s5_conv",
    )(xg, m, w, e, apre, apim, h0)


def _tile_order_positions(l):
    r = np.arange(l)
    tile, wi = r // SEG, r % SEG
    return tile * SEG + (wi % SEG_CHUNKS) * CHUNK + wi // SEG_CHUNKS


def _tile_dft_tables(scale, rows_in_tile_order=True):
    pos = _tile_order_positions(SEG)
    freq = pos if rows_in_tile_order else np.arange(SEG)
    ang = 2.0 * np.pi * ((freq[:, None] * pos[None, :]) % SEG) / SEG
    return jnp.asarray(np.cos(ang) * scale, F32), jnp.asarray(np.sin(ang) * scale, F32)


def _pos_dft_tile_kernel(cm_ref, sm_ref, fc_ref, fs_ref, o_ref):
    dot = functools.partial(jnp.dot, preferred_element_type=F32)
    cm, sm = cm_ref[...].astype(BF16), sm_ref[...].astype(BF16)
    for b in range(fc_ref.shape[0]):
        o_ref[b] = (dot(cm, fc_ref[b]) - dot(sm, fs_ref[b])).astype(BF16)


def _pos_dft_tile(fc, fs, n):
    _, l, wdt = fc.shape
    cm, sm = _tile_dft_tables(1.0 / math.sqrt(l))
    nb = 4
    seq = pl.BlockSpec((nb, l, wdt), lambda b: (b, 0, 0))
    return pl.pallas_call(
        _pos_dft_tile_kernel,
        grid=(n // nb,),
        in_specs=[_const_spec((l, l)), _const_spec((l, l)), seq, seq],
        out_specs=seq,
        out_shape=jax.ShapeDtypeStruct((n, l, wdt), BF16),
        compiler_params=_cparams(("parallel",)),
        name="pos_dft_tile",
    )(cm, sm, fc, fs)


N_TILES = 16


def _fft16(xr, xi):
    n = len(xr)
    rev = [int(format(i, "04b")[::-1], 2) for i in range(n)]
    ar = [xr[r] for r in rev]
    ai = [xi[r] for r in rev]
    size = 2
    while size <= n:
        half = size // 2
        for start in range(0, n, size):
            for k in range(half):
                wr = math.cos(2.0 * math.pi * k / size)
                wi = -math.sin(2.0 * math.pi * k / size)
                i0, i1 = start + k, start + k + half
                if k == 0:
                    tr, ti = ar[i1], ai[i1]
                elif 4 * k == size:
                    tr, ti = ai[i1], -ar[i1]
                else:
                    tr = ar[i1] * wr - ai[i1] * wi
                    ti = ar[i1] * wi + ai[i1] * wr
                ar[i1], ai[i1] = ar[i0] - tr, ai[i0] - ti
                ar[i0], ai[i0] = ar[i0] + tr, ai[i0] + ti
        size *= 2
    return ar, ai


def _pos_dft_long_kernel(cb_ref, sb_ref, ca_ref, sa_ref, fc_ref, fs_ref, o_ref, a_scr):
    wdt = fc_ref.shape[2]

    def rows_fft(r, carry):
        rows = pl.ds(pl.multiple_of(r * CHUNK, CHUNK), CHUNK)
        for b in range(wdt // LANES):
            lns = slice(b * LANES, (b + 1) * LANES)
            xr = [fc_ref[j, rows, lns].astype(F32) for j in range(N_TILES)]
            xi = [-fs_ref[j, rows, lns].astype(F32) for j in range(N_TILES)]
            ar, ai = _fft16(xr, xi)
            for k in range(N_TILES):
                a_scr[k, 0, rows, lns] = ar[k].astype(BF16)
                a_scr[k, 1, rows, lns] = ai[k].astype(BF16)
        return carry

    lax.fori_loop(0, SEG // CHUNK, rows_fft, 0)

    dot = functools.partial(jnp.dot, preferred_element_type=F32)
    cb, sb = cb_ref[...], sb_ref[...]
    for k1 in range(N_TILES):
        ca, sa = ca_ref[k1:k1 + 1, :], sa_ref[k1:k1 + 1, :]
        dc = (cb * ca - sb * sa).astype(BF16)
        ds = (sb * ca + cb * sa).astype(BF16)
        out = dot(dc, a_scr[k1, 0]) + dot(ds, a_scr[k1, 1])
        o_ref[:, k1 * SEG_CHUNKS:(k1 + 1) * SEG_CHUNKS, :] = out.astype(BF16).reshape(N_TILES, SEG_CHUNKS, wdt)


def _pos_dft_long(fc, fs, first, n):
    wdt = fc.shape[-1]
    l = N_TILES * SEG
    wb = 2 * LANES
    cb, sb = _tile_dft_tables(1.0 / math.sqrt(l), rows_in_tile_order=False)
    pos = _tile_order_positions(SEG)
    ang = 2.0 * np.pi * (np.arange(N_TILES)[:, None] * pos[None, :]) / l
    ca, sa = jnp.asarray(np.cos(ang), F32), jnp.asarray(np.sin(ang), F32)
    const2 = lambda shape: pl.BlockSpec(shape, lambda b, k: (0, 0), pipeline_mode=pl.Buffered(1))
    seq_in = pl.BlockSpec((None, N_TILES, SEG, wb), lambda b, k: (first + b, 0, 0, k))
    seq_out = pl.BlockSpec((None, N_TILES, SEG, wb), lambda b, k: (b, 0, 0, k))
    out = pl.pallas_call(
        _pos_dft_long_kernel,
        grid=(n, wdt // wb),
        in_specs=[const2((SEG, SEG)), const2((SEG, SEG)), const2((N_TILES, SEG)), const2((N_TILES, SEG)),
                  seq_in, seq_in],
        out_specs=seq_out,
        out_shape=jax.ShapeDtypeStruct((n, N_TILES, SEG, wdt), BF16),
        scratch_shapes=[pltpu.VMEM((N_TILES, 2, SEG, wb), BF16)],
        compiler_params=_cparams(("parallel", "parallel")),
        name="pos_dft_long",
    )(cb, sb, ca, sa, fc, fs)
    return out.reshape(n * l, wdt)


def _mix_ffn_kernel(n_prompt_tiles, xp_ref, xs_ref, mod_ref, ys_ref, yfp_ref, yfs_ref, gs_ref, gf_ref,
                    wglu_ref, bglu_ref, wps_ref, wpf_ref, wout_ref, n2_ref,
                    wg_ref, wu_ref, wd_ref, fn_ref, op_ref, os_ref, m_scr, mb_scr):
    tm = xp_ref.shape[0]
    dot = functools.partial(jnp.dot, preferred_element_type=F32)
    mod = mod_ref[0]
    is_prompt = pl.program_id(0) < n_prompt_tiles
    tiles = [slice(j * SEG, (j + 1) * SEG) for j in range(tm // SEG)]
    st = [dict() for _ in tiles]

    def gelu_phase(j, r, s):
        y = jnp.concatenate([ys_ref[b, r, :] for b in range(S5_WIDTH // LANES)], axis=1)
        s["z"] = jax.nn.gelu(y)

    def glu_phase(j, r, s):
        z = s["z"]
        s["z"] = (z * jax.nn.sigmoid(dot(z.astype(BF16), wglu_ref[...]) + bglu_ref[...])).astype(BF16)

    def proj_phase(j, r, s):
        yf = jnp.where(is_prompt, yfp_ref[r, :], yfs_ref[r, :])
        m = (gs_ref[r, :].astype(F32) * dot(s.pop("z"), wps_ref[...])
             + gf_ref[r, :].astype(F32) * dot(yf, wpf_ref[...]))
        for k in range(D_MODEL // LANES):
            for t in range(CHUNK):
                p0 = (j * CHUNK + t) * PITCH
                m_scr[k, p0:p0 + SEG_CHUNKS, :] = m[t * SEG_CHUNKS:(t + 1) * SEG_CHUNKS, k * LANES:(k + 1) * LANES]
        for c in range(SEG_CHUNKS):
            r0 = j * SEG + c * CHUNK
            for k in range(D_MODEL // LANES):
                mb_scr[r0:r0 + CHUNK, k * LANES:(k + 1) * LANES] = (
                    m_scr[k, pl.ds(j * CHUNK * PITCH + c, CHUNK, stride=PITCH), :].astype(BF16))

    def out_phase(j, r, s):
        x = jnp.where(is_prompt, xp_ref[r, :], xs_ref[r, :])
        x1 = x + mod[2:3] * dot(mb_scr[r, :], wout_ref[...])
        s["x1"] = x1
        s["h2"] = (_rms(x1, n2_ref[...]) * (1.0 + mod[4:5]) + mod[3:4]).astype(BF16)

    def ffn_phase(n0, n1, j, r, s):
        gate = dot(s["h2"], wg_ref[:, n0:n1])
        up = dot(s["h2"], wu_ref[:, n0:n1])
        part = dot((gate * jax.nn.sigmoid(gate) * up).astype(BF16), wd_ref[n0:n1, :])
        s["ff"] = part if "ff" not in s else s["ff"] + part

    def final_phase(j, r, s):
        x2 = s.pop("x1") + mod[5:6] * s.pop("ff")
        s["res"] = _rms(x2, fn_ref[...])

    phases = [gelu_phase, glu_phase, proj_phase, out_phase]
    phases += [functools.partial(ffn_phase, n0, n1) for n0, n1 in zip(FF_SPLITS[:-1], FF_SPLITS[1:])]
    phases += [final_phase]
    for phase in phases:
        for j, r in enumerate(tiles):
            phase(j, r, st[j])

    @pl.when(is_prompt)
    def _():
        for j, r in enumerate(tiles):
            op_ref[r, :] = st[j]["res"]

    @pl.when(jnp.logical_not(is_prompt))
    def _():
        for j, r in enumerate(tiles):
            os_ref[r, :] = st[j]["res"]


def _mix_ffn(xp, xs, seq_len, mod, ys, yf_p, yf_s, gs, gf, wts, tm):
    tp, ts = xp.shape[0], xs.shape[0]
    n_p = tp // tm
    tok = lambda w: pl.BlockSpec((tm, w), lambda i: (i, 0))
    return pl.pallas_call(
        functools.partial(_mix_ffn_kernel, n_p),
        grid=((tp + ts) // tm,),
        in_specs=_two_part_specs(tm, D_MODEL, n_p)
                 + [_mod_spec(n_p, seq_len // tm),
                    pl.BlockSpec((S5_WIDTH // LANES, tm, LANES), lambda i: (0, i, 0))]
                 + _two_part_specs(tm, FFT_WIDTH, n_p)
                 + [tok(D_MODEL), tok(D_MODEL)]
                 + [_const_spec(w.shape) for w in wts],
        out_specs=_two_part_specs(tm, D_MODEL, n_p),
        out_shape=[jax.ShapeDtypeStruct((tp, D_MODEL), F32), jax.ShapeDtypeStruct((ts, D_MODEL), F32)],
        scratch_shapes=[pltpu.VMEM((D_MODEL // LANES, tm // CHUNK * PITCH, LANES), F32),
                        pltpu.VMEM((tm, D_MODEL), BF16)],
        compiler_params=_cparams(("arbitrary",)),
        name="mix_ffn",
    )(xp, xs, mod, ys, yf_p, yf_s, gs, gf, *wts)


def kernel(x_prompt, x_sample, state_s5, c, c_ctx, norm1_g, norm2_g, w_ada, b_ada, w_in,
           s5_lambda_re, s5_lambda_im, s5_log_step, s5_b_re, s5_b_im, s5_c_re, s5_c_im,
           s5_d, w_glu, b_glu, w_proj_s5, w_proj_fft, w_out, w_ffn_gate, w_ffn_up,
           w_ffn_down, final_norm_g):
    nb, sl, _ = x_prompt.shape
    db, dl, _ = x_sample.shape
    assert w_in.shape[0] == 1 and sl == SEG and nb == N_PROMPT_SEG
    assert db == N_SAMPLE_SEQ and dl == SEG * SEG_CHUNKS

    mod, w_in_b, m, w, e, apre, apim = _adaln_and_s5_tables(
        c_ctx[None], c, w_ada[0], b_ada[0], w_in[0], s5_lambda_re[0], s5_lambda_im[0], s5_log_step[0], s5_b_re[0],
        s5_b_im[0], s5_c_re[0], s5_c_im[0], s5_d)
    mod = mod.reshape(8, N_MOD, D_MODEL)

    tm = 512
    xp = x_prompt.reshape(nb * sl, D_MODEL)
    xs = x_sample.reshape(db * dl, D_MODEL)
    later = (w_glu[0], w_proj_s5[0], w_proj_fft[0], w_out[0], w_ffn_gate[0], w_ffn_up[0], w_ffn_down[0])
    xg, fc, fs, gs, gf, *later_b = _inproj(xp, xs, dl, mod, norm1_g[0], w_in_b, later, tm)
    wglu_b, wps_b, wpf_b, wout_b, wg_b, wu_b, wd_b = later_b

    ys, fin = _s5_conv(xg, m, w, e, apre, apim, state_s5.astype(F32).reshape(db, 4 * GP))
    new_state = fin.reshape(nb, 1, 2, 2, S5_GROUPS, S5_STATE)

    yf_p = _pos_dft_tile(fc.reshape(-1, SEG, FFT_WIDTH), fs.reshape(-1, SEG, FFT_WIDTH), nb)
    n_prompt_units = nb // N_TILES
    yf_s = _pos_dft_long(fc.reshape(-1, N_TILES, SEG, FFT_WIDTH), fs.reshape(-1, N_TILES, SEG, FFT_WIDTH),
                         n_prompt_units, db)

    wts = (wglu_b, b_glu[0].reshape(1, S5_WIDTH), wps_b, wpf_b, wout_b, norm2_g[0].reshape(1, D_MODEL),
           wg_b, wu_b, wd_b, final_norm_g.reshape(1, D_MODEL))
    tm2 = 2 * SEG
    y_p, y_s = _mix_ffn(xp, xs, dl, mod, ys, yf_p.reshape(nb * sl, FFT_WIDTH), yf_s, gs, gf, wts, tm2)
    return (y_p.reshape(nb, sl, D_MODEL), y_s.reshape(db, dl, D_MODEL), new_state)
```

```python
import functools
import math

import numpy as np
import jax
import jax.numpy as jnp
from jax import lax
from jax.experimental import pallas as pl
from jax.experimental.pallas import tpu as pltpu

F32 = jnp.float32
BF16 = jnp.bfloat16

D_MODEL = 1024
S5_WIDTH = 512
S5_GROUPS = 32
S5_GROUP = 16
S5_STATE = 64
FFT_WIDTH = 512
FFT_GROUPS = 4
FFT_GROUP = 128
D_FF = 2816
N_MOD = 6
EPS = 1e-6

LANES = 128
CHUNK = 16
SEG_CHUNKS = 16
SEG = CHUNK * SEG_CHUNKS
GP = S5_GROUPS * S5_STATE
PAIRS = S5_GROUPS // 2
CH = CHUNK * S5_GROUP
OCT = LANES // S5_GROUP
PITCH = 24

MXU_DIM = 256
FF_SPLITS = (0, 3 * MXU_DIM, 6 * MXU_DIM, 9 * MXU_DIM, D_FF)
VMEM_LIMIT = 58 * 1024 * 1024


def _cparams(sem):
    return pltpu.CompilerParams(dimension_semantics=sem, vmem_limit_bytes=VMEM_LIMIT)


def _row_block_cast_specs(w, n_steps):
    rows = w.shape[0] // n_steps
    assert rows * n_steps == w.shape[0] and rows % 16 == 0, (w.shape, n_steps)
    spec = pl.BlockSpec((rows, w.shape[1]), lambda i: (jnp.minimum(i, n_steps - 1), 0))
    return spec, jax.ShapeDtypeStruct(w.shape, BF16)


def _cast_blocks(in_refs, out_refs):
    for src, dst in zip(in_refs, out_refs):
        dst[...] = src[...].astype(BF16)


def _ada_tile(cctx_ref, c_ref, w_ref, b_ref, o_ref):
    dot = functools.partial(jnp.dot, preferred_element_type=F32)
    n_pad = o_ref.shape[0] - 1 - c_ref.shape[0]
    c = jnp.concatenate([cctx_ref[...], c_ref[...], jnp.zeros((n_pad, c_ref.shape[1]), F32)], axis=0)
    s = c * jax.nn.sigmoid(c)
    s_hi = s.astype(BF16)
    s_lo = (s - s_hi.astype(F32)).astype(BF16)
    w = w_ref[...]
    w_hi = w.astype(BF16)
    w_lo = (w - w_hi.astype(F32)).astype(BF16)
    r = dot(jnp.concatenate([s_hi, s_lo], axis=0), w_hi)
    n = s.shape[0]
    o_ref[...] = r[0:n] + r[n:2 * n] + dot(s_hi, w_lo) + b_ref[...]


ROW_LAM_RE, ROW_LAM_IM, ROW_STEP, ROW_B_RE = 0, 2, 4, 8
ROW_B_IM = ROW_B_RE + 2 * S5_GROUP
ROW_C_RE = ROW_B_IM + 2 * S5_GROUP
ROW_C_IM = ROW_C_RE + 2 * S5_GROUP
PARAM_ROWS = ROW_C_IM + 2 * S5_GROUP


def _param_rows(p_ref, row0, d, lanes=slice(None)):
    return p_ref[row0 + d * S5_GROUP:row0 + (d + 1) * S5_GROUP, lanes]


def _exact_bf16_terms(x):
    hi = x.astype(BF16)
    r = x - hi.astype(F32)
    mid = r.astype(BF16)
    return hi, mid, (r - mid.astype(F32)).astype(BF16)


def _s5_params_to_lanes(lam_re_ref, lam_im_ref, lstep_ref, b_re_ref, b_im_ref, c_re_ref, c_im_ref, d_ref, ind_ref,
                        p_scr, dt_scr):
    dot_nt = lambda a, b: lax.dot_general(a, b, (((1,), (1,)), ((), ())), preferred_element_type=F32)
    dot = functools.partial(jnp.dot, preferred_element_type=F32)
    eye = (lax.broadcasted_iota(jnp.int32, (S5_GROUP, S5_GROUP), 0)
           == lax.broadcasted_iota(jnp.int32, (S5_GROUP, S5_GROUP), 1)).astype(BF16)
    p_scr[ROW_STEP + 2:ROW_B_RE, :] = jnp.zeros((ROW_B_RE - ROW_STEP - 2, GP), F32)
    p_scr[ROW_STEP:ROW_STEP + 2, :] = sum(dot(t, ind_ref[...]) for t in _exact_bf16_terms(lstep_ref[...]))
    for d in range(2):
        for row0, src in ((ROW_LAM_RE, lam_re_ref), (ROW_LAM_IM, lam_im_ref)):
            for q in range(PAIRS):
                p_scr[row0 + d:row0 + d + 1, q * LANES:(q + 1) * LANES] = jnp.concatenate(
                    [src[d, 2 * q:2 * q + 1, :], src[d, 2 * q + 1:2 * q + 2, :]], axis=1)
        for row0, src in ((ROW_B_RE, b_re_ref), (ROW_B_IM, b_im_ref)):
            p_scr[row0 + d * S5_GROUP:row0 + (d + 1) * S5_GROUP, :] = sum(
                dot_nt(eye, t) for t in _exact_bf16_terms(src[d].reshape(GP, S5_GROUP)))
        for row0, src in ((ROW_C_RE, c_re_ref), (ROW_C_IM, c_im_ref)):
            for q in range(PAIRS):
                p_scr[row0 + d * S5_GROUP:row0 + (d + 1) * S5_GROUP, q * LANES:(q + 1) * LANES] = jnp.concatenate(
                    [src[d, 2 * q], src[d, 2 * q + 1]], axis=1)
    own = lax.broadcasted_iota(jnp.int32, (S5_GROUPS, S5_WIDTH), 1) // S5_GROUP \
        == lax.broadcasted_iota(jnp.int32, (S5_GROUPS, S5_WIDTH), 0)
    d_rows = jnp.where(own, d_ref[...], 0.0)
    spread = (lax.broadcasted_iota(jnp.int32, (S5_WIDTH, LANES), 0) % S5_GROUP
              == lax.broadcasted_iota(jnp.int32, (S5_WIDTH, LANES), 1) % S5_GROUP).astype(BF16)
    dt_scr[...] = sum(dot(t, spread) for t in _exact_bf16_terms(d_rows))


def _s5_discretise(p_ref, apre_ref, apim_ref, gb_scr, ca_scr):
    lre = p_ref[ROW_LAM_RE:ROW_LAM_RE + 2, :]
    lim = p_ref[ROW_LAM_IM:ROW_LAM_IM + 2, :]
    step = jnp.exp(p_ref[ROW_STEP:ROW_STEP + 2, :])
    mag = jnp.exp(lre * step)
    are = mag * jnp.cos(lim * step)
    aim = mag * jnp.sin(lim * step)
    nr = are - 1.0
    den = lre * lre + lim * lim
    fr = (nr * lre + aim * lim) / den
    fi = (aim * lre - nr * lim) / den

    pr = [jnp.ones_like(are)]
    pi = [jnp.zeros_like(are)]
    for _ in range(CHUNK):
        r, i = pr[-1], pi[-1]
        pr.append(r * are - i * aim)
        pi.append(r * aim + i * are)
    a16r, a16i = pr[CHUNK], pi[CHUNK]
    qr, qi = jnp.ones_like(are), jnp.zeros_like(are)
    for c in range(SEG_CHUNKS + 1):
        apre_ref[c] = qr
        apim_ref[c] = qi
        qr, qi = qr * a16r - qi * a16i, qr * a16i + qi * a16r

    for d in range(2):
        bre, bim = _param_rows(p_ref, ROW_B_RE, d), _param_rows(p_ref, ROW_B_IM, d)
        bbre = fr[d:d + 1] * bre - fi[d:d + 1] * bim
        bbim = fr[d:d + 1] * bim + fi[d:d + 1] * bre
        cre, cim = _param_rows(p_ref, ROW_C_RE, d), _param_rows(p_ref, ROW_C_IM, d)
        for s in range(CHUNK):
            k = CHUNK - 1 - s if d == 0 else s
            r, i = pr[k][d:d + 1], pi[k][d:d + 1]
            gb_scr[0, d, s] = r * bbre - i * bbim
            gb_scr[1, d, s] = r * bbim + i * bbre
            f = s + 1 if d == 0 else CHUNK - s
            r, i = pr[f][d:d + 1], pi[f][d:d + 1]
            ca_scr[0, d, s] = r * cre - i * cim
            ca_scr[1, d, s] = -(r * cim + i * cre)


def _s5_pair_operands(q, slot_q, p_ref, dt_ref, gb_scr, ca_scr, ts_scr, m_ref, w_ref, e_ref):
    dot_nt = lambda a, b: lax.dot_general(a, b, (((1,), (1,)), ((), ())), preferred_element_type=F32)
    lane_gi = lax.broadcasted_iota(jnp.int32, (CH, LANES), 1) // S5_STATE
    row_gi = lax.broadcasted_iota(jnp.int32, (LANES, CH), 0) // S5_STATE
    eye = (lax.broadcasted_iota(jnp.int32, (LANES, LANES), 0)
           == lax.broadcasted_iota(jnp.int32, (LANES, LANES), 1)).astype(BF16)
    slot = lax.broadcasted_iota(jnp.int32, (CH, LANES), 1) // S5_GROUP
    n_lag_rows = (2 * CHUNK - 1) * S5_GROUP
    lanes = pl.ds(pl.multiple_of(q * LANES, LANES), LANES)
    for d in range(2):
        for ri in range(2):
            col = slice((2 * d + ri) * LANES, (2 * d + ri + 1) * LANES)
            gb = gb_scr[ri, d, :, :, lanes].reshape(CH, LANES)
            ca = ca_scr[ri, d, :, :, lanes].reshape(CH, LANES).astype(BF16)
            ca_t = dot_nt(eye, ca)
            for gi in range(2):
                w_ref[slot_q, gi * CH:(gi + 1) * CH, col] = jnp.where(lane_gi == gi, gb, 0.0).astype(BF16)
                e_ref[slot_q, col, gi * CH:(gi + 1) * CH] = jnp.where(row_gi == gi, ca_t, 0.0).astype(BF16)
    for gi in range(2):
        g = 2 * q + gi
        lag = []
        for d in range(2):
            c_re = jnp.concatenate([_param_rows(p_ref, ROW_C_RE, d, lanes)] * OCT, axis=0)
            c_imn = jnp.concatenate([-_param_rows(p_ref, ROW_C_IM, d, lanes)] * OCT, axis=0)
            keep = lax.broadcasted_iota(jnp.int32, (LANES, LANES), 1) // S5_STATE == gi
            c_re = jnp.where(keep, c_re, 0.0).astype(BF16)
            c_imn = jnp.where(keep, c_imn, 0.0).astype(BF16)
            gre = gb_scr[0, d, :, :, lanes].reshape(CH, LANES).astype(BF16)
            gim = gb_scr[1, d, :, :, lanes].reshape(CH, LANES).astype(BF16)
            lag.append(dot_nt(gre, c_re) + dot_nt(gim, c_imn))
        zl = (CHUNK - 1) * S5_GROUP
        ts_scr[0:zl, :] = lag[0][0:zl]
        on_diag = (lax.broadcasted_iota(jnp.int32, (S5_GROUP, LANES), 1) % S5_GROUP
                   == lax.broadcasted_iota(jnp.int32, (S5_GROUP, LANES), 0))
        skip = jnp.where(on_diag, dt_ref[pl.ds(g, 1), :], 0.0)
        ts_scr[zl:zl + S5_GROUP, :] = lag[0][zl:] + lag[1][0:S5_GROUP] + skip
        ts_scr[zl + S5_GROUP:n_lag_rows, :] = lag[1][S5_GROUP:]
        for hf in range(CHUNK // OCT):
            acc = None
            for s in range(OCT):
                t = OCT * hf + s
                win = ts_scr[(CHUNK - 1 - t) * S5_GROUP:(CHUNK - 1 - t) * S5_GROUP + CH, :]
                acc = win if acc is None else jnp.where(slot == s, win, acc)
            m_ref[2 * slot_q + gi, :, hf * LANES:(hf + 1) * LANES] = acc.astype(BF16)


PAIRS_PER_STEP = 2


def _ada_tables_kernel(cctx_ref, c_ref, wada_ref, bada_ref, win_ref, lam_re_ref, lam_im_ref, lstep_ref, b_re_ref, b_im_ref,
                       c_re_ref, c_im_ref, d_ref, ind_ref, mod_ref, winb_ref, apre_ref, apim_ref, m_ref, w_ref, e_ref,
                       p_scr, dt_scr, gb_scr, ca_scr, ts_scr):
    j = pl.program_id(0)
    _cast_blocks([win_ref], [winb_ref])
    _ada_tile(cctx_ref, c_ref, wada_ref, bada_ref, mod_ref)

    @pl.when(j == 0)
    def _():
        _s5_params_to_lanes(lam_re_ref, lam_im_ref, lstep_ref, b_re_ref, b_im_ref, c_re_ref, c_im_ref, d_ref, ind_ref,
                            p_scr, dt_scr)
        _s5_discretise(p_scr, apre_ref, apim_ref, gb_scr, ca_scr)

    for lp in range(PAIRS_PER_STEP):
        _s5_pair_operands(PAIRS_PER_STEP * j + lp, lp, p_scr, dt_scr, gb_scr, ca_scr, ts_scr, m_ref, w_ref, e_ref)


def _adaln_and_s5_tables(c_ctx, c, w_ada, b_ada, w_in, lam_re, lam_im, log_step, b_re, b_im, c_re, c_im, s5_d):
    ind_np = np.zeros((S5_GROUPS, GP), np.float32)
    ind_np[np.arange(GP) // S5_STATE, np.arange(GP)] = 1.0
    s5_in = (lam_re, lam_im, log_step, b_re, b_im, c_re, c_im,
             s5_d.reshape(1, S5_WIDTH), jnp.asarray(ind_np).astype(BF16))

    n_steps = PAIRS // PAIRS_PER_STEP
    n_out = w_ada.shape[1]
    tn = n_out // n_steps
    cast_spec, cast_shape = _row_block_cast_specs(w_in, n_steps)
    whole = lambda a: pl.BlockSpec(a.shape, lambda j, nd=a.ndim: (0,) * nd)
    powers = jax.ShapeDtypeStruct((SEG_CHUNKS + 1, 2, GP), F32)
    tab = (2, 2, CHUNK, S5_GROUP, GP)
    mod, w_in_b, apre, apim, m, w, e = pl.pallas_call(
        _ada_tables_kernel,
        grid=(n_steps,),
        in_specs=[whole(c_ctx), whole(c),
                  pl.BlockSpec((D_MODEL, tn), lambda j: (0, j)),
                  pl.BlockSpec((1, tn), lambda j: (0, j)),
                  cast_spec] + [whole(a) for a in s5_in],
        out_specs=[pl.BlockSpec((8, tn), lambda j: (0, j)), cast_spec, whole(powers), whole(powers),
                   pl.BlockSpec((2 * PAIRS_PER_STEP, CH, CH), lambda j: (j, 0, 0)),
                   pl.BlockSpec((PAIRS_PER_STEP, 2 * CH, 4 * LANES), lambda j: (j, 0, 0)),
                   pl.BlockSpec((PAIRS_PER_STEP, 4 * LANES, 2 * CH), lambda j: (j, 0, 0))],
        out_shape=[jax.ShapeDtypeStruct((8, n_out), F32), cast_shape, powers, powers,
                   jax.ShapeDtypeStruct((S5_GROUPS, CH, CH), BF16),
                   jax.ShapeDtypeStruct((PAIRS, 2 * CH, 4 * LANES), BF16),
                   jax.ShapeDtypeStruct((PAIRS, 4 * LANES, 2 * CH), BF16)],
        scratch_shapes=[pltpu.VMEM((PARAM_ROWS, GP), F32),
                        pltpu.VMEM((S5_GROUPS, LANES), F32),
                        pltpu.VMEM(tab, F32),
                        pltpu.VMEM(tab, F32),
                        pltpu.VMEM(((2 * CHUNK - 1) * S5_GROUP, LANES), F32)],
        compiler_params=_cparams(("arbitrary",)),
        name="adaln_s5_tables",
    )(c_ctx, c, w_ada, b_ada.reshape(1, n_out), w_in, *s5_in)
    return mod, w_in_b, m, w, e, apre, apim


def _rms(x, g):
    return x * lax.rsqrt(jnp.mean(x * x, axis=-1, keepdims=True) + EPS) * g


def _slot_transpose(v):
    v = list(v)
    slot = lax.broadcasted_iota(jnp.int32, v[0].shape, 1) // S5_GROUP
    for k in (4, 2, 1):
        low = (slot & k) == 0
        for i in range(OCT):
            if i & k:
                continue
            a, b = v[i], v[i + k]
            v[i] = jnp.where(low, a, pltpu.roll(b, k * S5_GROUP, axis=1))
            v[i + k] = jnp.where(low, pltpu.roll(a, LANES - k * S5_GROUP, axis=1), b)
    return v


def _inproj_kernel(n_prompt_tiles, n_cast, xp_ref, xs_ref, mod_ref, g_ref, win_ref, bdc_ref, bds_ref,
                   cm_ref, sm_ref, *rest):
    cast_in, rest = rest[:n_cast], rest[n_cast:]
    xg_ref, fc_ref, fs_ref, gs_ref, gf_ref, yfp_ref = rest[:6]
    cast_out, (h_scr, hb_scr, zs_scr) = rest[6:6 + n_cast], rest[6 + n_cast:]
    tm = xp_ref.shape[0]
    mod = mod_ref[0]
    is_prompt = pl.program_id(0) < n_prompt_tiles
    dot = functools.partial(jnp.dot, preferred_element_type=F32)
    o = 2 * S5_WIDTH
    tiles = [slice(j * SEG, (j + 1) * SEG) for j in range(tm // SEG)]

    def norm_phase(j, r):
        x = jnp.where(is_prompt, xp_ref[r, :], xs_ref[r, :])
        h = _rms(x, g_ref[...]) * (1.0 + mod[1:2]) + mod[0:1]
        for k in range(D_MODEL // LANES):
            for c in range(SEG_CHUNKS):
                p0 = (j * SEG_CHUNKS + c) * PITCH
                h_scr[k, p0:p0 + CHUNK, :] = h[c * CHUNK:(c + 1) * CHUNK, k * LANES:(k + 1) * LANES]
        for s in range(CHUNK):
            r0 = j * SEG + s * SEG_CHUNKS
            for k in range(D_MODEL // LANES):
                hb_scr[r0:r0 + SEG_CHUNKS, k * LANES:(k + 1) * LANES] = (
                    h_scr[k, pl.ds(j * SEG_CHUNKS * PITCH + s, SEG_CHUNKS, stride=PITCH), :].astype(BF16))

    def s5_gate_phase(j, r):
        zs_scr[r, :] = dot(hb_scr[r, :], win_ref[:, 0:S5_WIDTH])
        gs_ref[r, :] = jax.nn.sigmoid(dot(hb_scr[r, :], win_ref[:, o:o + D_MODEL])).astype(BF16)

    def fold_phase(j, r):
        for b in range(S5_WIDTH // LANES):
            for hf in range(CHUNK // OCT):
                z = [zs_scr[j * SEG + s * SEG_CHUNKS:j * SEG + (s + 1) * SEG_CHUNKS, b * LANES:(b + 1) * LANES]
                     for s in range(OCT * hf, OCT * (hf + 1))]
                for i, xi in enumerate(_slot_transpose(z)):
                    xg_ref[OCT * b + i, j * SEG_CHUNKS:(j + 1) * SEG_CHUNKS,
                           hf * LANES:(hf + 1) * LANES] = xi.astype(BF16)

    def fourier_gate_phase(j, r):
        uf = dot(hb_scr[r, :], win_ref[:, S5_WIDTH:2 * S5_WIDTH]).astype(BF16)
        for n0 in range(0, FFT_WIDTH, MXU_DIM):
            cols = slice(n0, n0 + MXU_DIM)
            fc_ref[r, cols] = dot(uf[:, cols], bdc_ref[cols, cols]).astype(BF16)
            fs_ref[r, cols] = dot(uf[:, cols], bds_ref[cols, cols]).astype(BF16)
        gf_ref[r, :] = jax.nn.sigmoid(dot(hb_scr[r, :], win_ref[:, o + D_MODEL:o + 2 * D_MODEL])).astype(BF16)

    for phase in (norm_phase, s5_gate_phase, fold_phase, fourier_gate_phase):
        for j, r in enumerate(tiles):
            phase(j, r)
    _cast_blocks(cast_in, cast_out)

    @pl.when(is_prompt)
    def _():
        cm, sm = cm_ref[...].astype(BF16), sm_ref[...].astype(BF16)
        for r in tiles:
            yfp_ref[r, :] = (dot(cm, fc_ref[r, :]) - dot(sm, fs_ref[r, :])).astype(BF16)


def _channel_dft_mats():
    j = np.arange(FFT_GROUP)
    ang = 2.0 * np.pi * ((j[:, None] * j[None, :]) % FFT_GROUP) / FFT_GROUP
    blk_c = np.cos(ang) / math.sqrt(FFT_GROUP)
    blk_s = np.sin(ang) / math.sqrt(FFT_GROUP)
    bdc = np.kron(np.eye(FFT_GROUPS), blk_c)
    bds = np.kron(np.eye(FFT_GROUPS), blk_s)
    return jnp.asarray(bdc, F32).astype(BF16), jnp.asarray(bds, F32).astype(BF16)


def _const_spec(shape):
    nd = len(shape)
    return pl.BlockSpec(shape, lambda i: (0,) * nd, pipeline_mode=pl.Buffered(1))


def _two_part_specs(tm, width, n_prompt_tiles):
    return [pl.BlockSpec((tm, width), lambda i: (jnp.minimum(i, n_prompt_tiles - 1), 0)),
            pl.BlockSpec((tm, width), lambda i: (jnp.maximum(i - n_prompt_tiles, 0), 0))]


def _mod_spec(n_prompt_tiles, tiles_per_seq):
    row = lambda i: jnp.where(i < n_prompt_tiles, 0, 1 + (i - n_prompt_tiles) // tiles_per_seq)
    return pl.BlockSpec((1, N_MOD, D_MODEL), lambda i: (row(i), 0, 0))


N_CAST_STEPS = 16


def _inproj(xp, xs, seq_len, mod, norm_g, w_in_b, later_weights, tm):
    t = xp.shape[0] + xs.shape[0]
    n_p = xp.shape[0] // tm
    assert t // tm >= N_CAST_STEPS
    bdc, bds = _channel_dft_mats()
    cm, sm = _tile_dft_tables(1.0 / math.sqrt(SEG))
    casts = [_row_block_cast_specs(w, N_CAST_STEPS) for w in later_weights]
    tok = lambda w: pl.BlockSpec((tm, w), lambda i: (i, 0))
    out = lambda w: jax.ShapeDtypeStruct((t, w), BF16)
    return pl.pallas_call(
        functools.partial(_inproj_kernel, n_p, len(casts)),
        grid=(t // tm,),
        in_specs=_two_part_specs(tm, D_MODEL, n_p) + [
                  _mod_spec(n_p, seq_len // tm),
                  _const_spec((1, D_MODEL)),
                  _const_spec(w_in_b.shape),
                  _const_spec(bdc.shape),
                  _const_spec(bds.shape), _const_spec(cm.shape), _const_spec(sm.shape)] + [c[0] for c in casts],
        out_specs=[pl.BlockSpec((S5_GROUPS, tm // CHUNK, CH), lambda i: (0, i, 0)),
                   tok(FFT_WIDTH), tok(FFT_WIDTH), tok(D_MODEL), tok(D_MODEL),
                   _two_part_specs(tm, FFT_WIDTH, n_p)[0]] + [c[0] for c in casts],
        out_shape=[jax.ShapeDtypeStruct((S5_GROUPS, t // CHUNK, CH), BF16),
                   out(FFT_WIDTH), out(FFT_WIDTH), out(D_MODEL), out(D_MODEL),
                   jax.ShapeDtypeStruct((xp.shape[0], FFT_WIDTH), BF16)] + [c[1] for c in casts],
        scratch_shapes=[pltpu.VMEM((D_MODEL // LANES, tm // CHUNK * PITCH, LANES), F32),
                        pltpu.VMEM((tm, D_MODEL), BF16),
                        pltpu.VMEM((tm, S5_WIDTH), F32)],
        compiler_params=_cparams(("arbitrary",)),
        name="inproj",
    )(xp, xs, mod, norm_g.reshape(1, D_MODEL), w_in_b, bdc, bds, cm, sm, *later_weights)


N_PROMPT_SEG = 16
N_SAMPLE_SEG = 32
N_SAMPLE_SEQ = 2
N_SEG = N_PROMPT_SEG + N_SAMPLE_SEG
ROWS = SEG_CHUNKS * N_SEG
ROWS_P = SEG_CHUNKS * N_PROMPT_SEG
OCT_PAIRS = OCT // 2
SEG_PITCH = 56


def _cmul_add(ar, ai, hr, hi, sr, si):
    return ar * hr - ai * hi + sr, ar * hi + ai * hr + si


def _s5_kernel(x_ref, m_ref, w_ref, e_ref, apre_ref, apim_ref, h0_ref,
               y_ref, fin_ref, s_scr, hin_scr, hinp_scr, f_scr, hs_scr, y_scr):
    dot = functools.partial(jnp.dot, preferred_element_type=F32)
    parts = ((0, ROWS_P), (ROWS_P, ROWS - ROWS_P))
    seg_rows = lambda c: pl.ds(c, N_SEG, stride=PITCH)
    blk = lambda c: pl.ds(c * SEG_PITCH, N_SEG)
    seq_rows = lambda j: pl.ds(j, N_SAMPLE_SEQ, stride=SEG_CHUNKS)
    lat = lambda c: pl.ds(c * SEG_PITCH + N_PROMPT_SEG, N_SAMPLE_SEG)
    zero = jnp.zeros((N_SEG, LANES), F32)
    octet = pl.program_id(0)
    state_lanes = lambda k, pr: pl.ds(pl.multiple_of(k * GP + (octet * OCT_PAIRS + pr) * LANES, LANES), LANES)

    for pr in range(OCT_PAIRS):
        ln = slice(pr * LANES, (pr + 1) * LANES)
        g0, g1 = 2 * pr, 2 * pr + 1
        for r0, nr in parts:
            s = (dot(x_ref[g0, r0:r0 + nr, :], w_ref[pr, 0:CH, :])
                 + dot(x_ref[g1, r0:r0 + nr, :], w_ref[pr, CH:2 * CH, :]))
            for k in range(4):
                for sg in range(nr // SEG_CHUNKS):
                    p0 = (r0 // SEG_CHUNKS + sg) * PITCH
                    s_scr[k, p0:p0 + SEG_CHUNKS, :] = s[sg * SEG_CHUNKS:(sg + 1) * SEG_CHUNKS, k * LANES:(k + 1) * LANES]

        ar, ai = apre_ref[1, 0:1, ln], apim_ref[1, 0:1, ln]
        hr, hi = zero, zero
        for c in range(SEG_CHUNKS):
            hin_scr[0, blk(c), :] = hr
            hin_scr[1, blk(c), :] = hi
            hr, hi = _cmul_add(ar, ai, hr, hi, s_scr[0, seg_rows(c), :], s_scr[1, seg_rows(c), :])
        fin_ref[:, state_lanes(0, pr)] = hr[0:N_PROMPT_SEG]
        fin_ref[:, state_lanes(1, pr)] = hi[0:N_PROMPT_SEG]
        f_scr[0] = hr[N_PROMPT_SEG:]
        f_scr[1] = hi[N_PROMPT_SEG:]
        br, bi = apre_ref[1, 1:2, ln], apim_ref[1, 1:2, ln]
        gr, gi = zero, zero
        for c in range(SEG_CHUNKS - 1, -1, -1):
            hin_scr[2, blk(c), :] = gr
            hin_scr[3, blk(c), :] = gi
            gr, gi = _cmul_add(br, bi, gr, gi, s_scr[2, seg_rows(c), :], s_scr[3, seg_rows(c), :])
        fin_ref[:, state_lanes(2, pr)] = gr[0:N_PROMPT_SEG]
        fin_ref[:, state_lanes(3, pr)] = gi[0:N_PROMPT_SEG]
        f_scr[2] = gr[N_PROMPT_SEG:]
        f_scr[3] = gi[N_PROMPT_SEG:]

        a2r, a2i = apre_ref[SEG_CHUNKS, 0:1, ln], apim_ref[SEG_CHUNKS, 0:1, ln]
        hr, hi = h0_ref[:, state_lanes(0, pr)], h0_ref[:, state_lanes(1, pr)]
        for j in range(SEG_CHUNKS):
            hs_scr[0, seq_rows(j), :] = hr
            hs_scr[1, seq_rows(j), :] = hi
            hr, hi = _cmul_add(a2r, a2i, hr, hi, f_scr[0, seq_rows(j), :], f_scr[1, seq_rows(j), :])
        b2r, b2i = apre_ref[SEG_CHUNKS, 1:2, ln], apim_ref[SEG_CHUNKS, 1:2, ln]
        gr, gi = h0_ref[:, state_lanes(2, pr)], h0_ref[:, state_lanes(3, pr)]
        for j in range(SEG_CHUNKS - 1, -1, -1):
            hs_scr[2, seq_rows(j), :] = gr
            hs_scr[3, seq_rows(j), :] = gi
            gr, gi = _cmul_add(b2r, b2i, gr, gi, f_scr[2, seq_rows(j), :], f_scr[3, seq_rows(j), :])
        for c in range(SEG_CHUNKS):
            p_r, p_i = apre_ref[c, 0:1, ln], apim_ref[c, 0:1, ln]
            hr, hi = _cmul_add(p_r, p_i, hs_scr[0], hs_scr[1], hin_scr[0, lat(c), :], hin_scr[1, lat(c), :])
            hin_scr[0, lat(c), :] = hr
            hin_scr[1, lat(c), :] = hi
            cb = SEG_CHUNKS - 1 - c
            p_r, p_i = apre_ref[cb, 1:2, ln], apim_ref[cb, 1:2, ln]
            gr, gi = _cmul_add(p_r, p_i, hs_scr[2], hs_scr[3], hin_scr[2, lat(c), :], hin_scr[3, lat(c), :])
            hin_scr[2, lat(c), :] = gr
            hin_scr[3, lat(c), :] = gi

        for k in range(4):
            for sg in range(N_SEG):
                hinp_scr[sg * SEG_CHUNKS:(sg + 1) * SEG_CHUNKS, k * LANES:(k + 1) * LANES] = (
                    hin_scr[k, pl.ds(sg, SEG_CHUNKS, stride=SEG_PITCH), :].astype(BF16))
        for gi_, g in ((0, g0), (1, g1)):
            for r0, nr in parts:
                y_scr[g, r0:r0 + nr, :] = (dot(x_ref[g, r0:r0 + nr, :], m_ref[g])
                                           + dot(hinp_scr[r0:r0 + nr, :], e_ref[pr, :, gi_ * CH:(gi_ + 1) * CH]))

    rb_rows = 2 * SEG_CHUNKS
    for rb in range(ROWS // rb_rows):
        for hf in range(CHUNK // OCT):
            v = [y_scr[i, rb * rb_rows:(rb + 1) * rb_rows, hf * LANES:(hf + 1) * LANES] for i in range(OCT)]
            for s, acc in enumerate(_slot_transpose(v)):
                t = OCT * hf + s
                for sg in range(rb_rows // SEG_CHUNKS):
                    seg = rb * (rb_rows // SEG_CHUNKS) + sg
                    r0 = seg * SEG + t * SEG_CHUNKS
                    y_ref[0, r0:r0 + SEG_CHUNKS, :] = acc[sg * SEG_CHUNKS:(sg + 1) * SEG_CHUNKS]


def _s5_conv(xg, m, w, e, apre, apim, h0):
    n_oct = S5_GROUPS // OCT
    assert xg.shape[1] == ROWS
    nt = ROWS * CHUNK
    return pl.pallas_call(
        _s5_kernel,
        grid=(n_oct,),
        in_specs=[pl.BlockSpec((OCT, ROWS, CH), lambda o: (o, 0, 0)),
                  pl.BlockSpec((OCT, CH, CH), lambda o: (o, 0, 0)),
                  pl.BlockSpec((OCT_PAIRS, 2 * CH, 4 * LANES), lambda o: (o, 0, 0)),
                  pl.BlockSpec((OCT_PAIRS, 4 * LANES, 2 * CH), lambda o: (o, 0, 0)),
                  pl.BlockSpec((SEG_CHUNKS + 1, 2, OCT_PAIRS * LANES), lambda o: (0, 0, o)),
                  pl.BlockSpec((SEG_CHUNKS + 1, 2, OCT_PAIRS * LANES), lambda o: (0, 0, o)),
                  pl.BlockSpec((N_SAMPLE_SEQ, 4 * GP), lambda o: (0, 0))],
        out_specs=[pl.BlockSpec((1, nt, LANES), lambda o: (o, 0, 0)),
                   pl.BlockSpec((N_PROMPT_SEG, 4 * GP), lambda o: (0, 0))],
        out_shape=[jax.ShapeDtypeStruct((n_oct, nt, LANES), F32),
                   jax.ShapeDtypeStruct((N_PROMPT_SEG, 4 * GP), F32)],
        scratch_shapes=[pltpu.VMEM((4, N_SEG * PITCH, LANES), F32),
                        pltpu.VMEM((4, SEG_CHUNKS * SEG_PITCH, LANES), F32),
                        pltpu.VMEM((ROWS, 4 * LANES), BF16),
                        pltpu.VMEM((4, N_SAMPLE_SEG, LANES), F32),
                        pltpu.VMEM((4, N_SAMPLE_SEG, LANES), F32),
                        pltpu.VMEM((OCT, ROWS, CH), F32)],
        compiler_params=_cparams(("arbitrary",)),
        name="s5_conv",
    )(xg, m, w, e, apre, apim, h0)


def _tile_order_positions(l):
    r = np.arange(l)
    tile, wi = r // SEG, r % SEG
    return tile * SEG + (wi % SEG_CHUNKS) * CHUNK + wi // SEG_CHUNKS


def _tile_dft_tables(scale, rows_in_tile_order=True):
    pos = _tile_order_positions(SEG)
    freq = pos if rows_in_tile_order else np.arange(SEG)
    ang = 2.0 * np.pi * ((freq[:, None] * pos[None, :]) % SEG) / SEG
    return jnp.asarray(np.cos(ang) * scale, F32), jnp.asarray(np.sin(ang) * scale, F32)


N_TILES = 16


def _fft16(xr, xi):
    n = len(xr)
    rev = [int(format(i, "04b")[::-1], 2) for i in range(n)]
    ar = [xr[r] for r in rev]
    ai = [xi[r] for r in rev]
    size = 2
    while size <= n:
        half = size // 2
        for start in range(0, n, size):
            for k in range(half):
                wr = math.cos(2.0 * math.pi * k / size)
                wi = -math.sin(2.0 * math.pi * k / size)
                i0, i1 = start + k, start + k + half
                if k == 0:
                    tr, ti = ar[i1], ai[i1]
                elif 4 * k == size:
                    tr, ti = ai[i1], -ar[i1]
                else:
                    tr = ar[i1] * wr - ai[i1] * wi
                    ti = ar[i1] * wi + ai[i1] * wr
                ar[i1], ai[i1] = ar[i0] - tr, ai[i0] - ti
                ar[i0], ai[i0] = ar[i0] + tr, ai[i0] + ti
        size *= 2
    return ar, ai


def _pos_dft_long_kernel(cb_ref, sb_ref, ca_ref, sa_ref, fc_ref, fs_ref, o_ref, a_scr):
    wdt = fc_ref.shape[2]

    def rows_fft(r, carry):
        rows = pl.ds(pl.multiple_of(r * CHUNK, CHUNK), CHUNK)
        for b in range(wdt // LANES):
            lns = slice(b * LANES, (b + 1) * LANES)
            xr = [fc_ref[j, rows, lns].astype(F32) for j in range(N_TILES)]
            xi = [-fs_ref[j, rows, lns].astype(F32) for j in range(N_TILES)]
            ar, ai = _fft16(xr, xi)
            for k in range(N_TILES):
                a_scr[k, 0, rows, lns] = ar[k].astype(BF16)
                a_scr[k, 1, rows, lns] = ai[k].astype(BF16)
        return carry

    lax.fori_loop(0, SEG // CHUNK, rows_fft, 0)

    dot = functools.partial(jnp.dot, preferred_element_type=F32)
    cb, sb = cb_ref[...], sb_ref[...]
    for k1 in range(N_TILES):
        ca, sa = ca_ref[k1:k1 + 1, :], sa_ref[k1:k1 + 1, :]
        dc = (cb * ca - sb * sa).astype(BF16)
        ds = (sb * ca + cb * sa).astype(BF16)
        out = dot(dc, a_scr[k1, 0]) + dot(ds, a_scr[k1, 1])
        o_ref[:, k1 * SEG_CHUNKS:(k1 + 1) * SEG_CHUNKS, :] = out.astype(BF16).reshape(N_TILES, SEG_CHUNKS, wdt)


def _pos_dft_long(fc, fs, first, n):
    wdt = fc.shape[-1]
    l = N_TILES * SEG
    wb = 2 * LANES
    cb, sb = _tile_dft_tables(1.0 / math.sqrt(l), rows_in_tile_order=False)
    pos = _tile_order_positions(SEG)
    ang = 2.0 * np.pi * (np.arange(N_TILES)[:, None] * pos[None, :]) / l
    ca, sa = jnp.asarray(np.cos(ang), F32), jnp.asarray(np.sin(ang), F32)
    const2 = lambda shape: pl.BlockSpec(shape, lambda b, k: (0, 0), pipeline_mode=pl.Buffered(1))
    seq_in = pl.BlockSpec((None, N_TILES, SEG, wb), lambda b, k: (first + b, 0, 0, k))
    seq_out = pl.BlockSpec((None, N_TILES, SEG, wb), lambda b, k: (b, 0, 0, k))
    out = pl.pallas_call(
        _pos_dft_long_kernel,
        grid=(n, wdt // wb),
        in_specs=[const2((SEG, SEG)), const2((SEG, SEG)), const2((N_TILES, SEG)), const2((N_TILES, SEG)),
                  seq_in, seq_in],
        out_specs=seq_out,
        out_shape=jax.ShapeDtypeStruct((n, N_TILES, SEG, wdt), BF16),
        scratch_shapes=[pltpu.VMEM((N_TILES, 2, SEG, wb), BF16)],
        compiler_params=_cparams(("parallel", "parallel")),
        name="pos_dft_long",
    )(cb, sb, ca, sa, fc, fs)
    return out.reshape(n * l, wdt)


def _mix_ffn_kernel(n_prompt_tiles, xp_ref, xs_ref, mod_ref, ys_ref, yfp_ref, yfs_ref, gs_ref, gf_ref,
                    wglu_ref, bglu_ref, wps_ref, wpf_ref, wout_ref, n2_ref,
                    wg_ref, wu_ref, wd_ref, fn_ref, op_ref, os_ref, m_scr, mb_scr):
    tm = xp_ref.shape[0]
    dot = functools.partial(jnp.dot, preferred_element_type=F32)
    mod = mod_ref[0]
    is_prompt = pl.program_id(0) < n_prompt_tiles
    tiles = [slice(j * SEG, (j + 1) * SEG) for j in range(tm // SEG)]
    st = [dict() for _ in tiles]

    def gelu_phase(j, r, s):
        y = jnp.concatenate([ys_ref[b, r, :] for b in range(S5_WIDTH // LANES)], axis=1)
        s["z"] = jax.nn.gelu(y)

    def glu_phase(j, r, s):
        z = s["z"]
        s["z"] = (z * jax.nn.sigmoid(dot(z.astype(BF16), wglu_ref[...]) + bglu_ref[...])).astype(BF16)

    def proj_phase(j, r, s):
        yf = jnp.where(is_prompt, yfp_ref[r, :], yfs_ref[r, :])
        m = (gs_ref[r, :].astype(F32) * dot(s.pop("z"), wps_ref[...])
             + gf_ref[r, :].astype(F32) * dot(yf, wpf_ref[...]))
        for k in range(D_MODEL // LANES):
            for t in range(CHUNK):
                p0 = (j * CHUNK + t) * PITCH
                m_scr[k, p0:p0 + SEG_CHUNKS, :] = m[t * SEG_CHUNKS:(t + 1) * SEG_CHUNKS, k * LANES:(k + 1) * LANES]
        for c in range(SEG_CHUNKS):
            r0 = j * SEG + c * CHUNK
            for k in range(D_MODEL // LANES):
                mb_scr[r0:r0 + CHUNK, k * LANES:(k + 1) * LANES] = (
                    m_scr[k, pl.ds(j * CHUNK * PITCH + c, CHUNK, stride=PITCH), :].astype(BF16))

    def out_phase(j, r, s):
        x = jnp.where(is_prompt, xp_ref[r, :], xs_ref[r, :])
        x1 = x + mod[2:3] * dot(mb_scr[r, :], wout_ref[...])
        s["x1"] = x1
        s["h2"] = (_rms(x1, n2_ref[...]) * (1.0 + mod[4:5]) + mod[3:4]).astype(BF16)

    def ffn_phase(n0, n1, j, r, s):
        gate = dot(s["h2"], wg_ref[:, n0:n1])
        up = dot(s["h2"], wu_ref[:, n0:n1])
        part = dot((gate * jax.nn.sigmoid(gate) * up).astype(BF16), wd_ref[n0:n1, :])
        s["ff"] = part if "ff" not in s else s["ff"] + part

    def final_phase(j, r, s):
        x2 = s.pop("x1") + mod[5:6] * s.pop("ff")
        s["res"] = _rms(x2, fn_ref[...])

    phases = [gelu_phase, glu_phase, proj_phase, out_phase]
    phases += [functools.partial(ffn_phase, n0, n1) for n0, n1 in zip(FF_SPLITS[:-1], FF_SPLITS[1:])]
    phases += [final_phase]
    for phase in phases:
        for j, r in enumerate(tiles):
            phase(j, r, st[j])

    @pl.when(is_prompt)
    def _():
        for j, r in enumerate(tiles):
            op_ref[r, :] = st[j]["res"]

    @pl.when(jnp.logical_not(is_prompt))
    def _():
        for j, r in enumerate(tiles):
            os_ref[r, :] = st[j]["res"]


def _mix_ffn(xp, xs, seq_len, mod, ys, yf_p, yf_s, gs, gf, wts, tm):
    tp, ts = xp.shape[0], xs.shape[0]
    n_p = tp // tm
    tok = lambda w: pl.BlockSpec((tm, w), lambda i: (i, 0))
    return pl.pallas_call(
        functools.partial(_mix_ffn_kernel, n_p),
        grid=((tp + ts) // tm,),
        in_specs=_two_part_specs(tm, D_MODEL, n_p)
                 + [_mod_spec(n_p, seq_len // tm),
                    pl.BlockSpec((S5_WIDTH // LANES, tm, LANES), lambda i: (0, i, 0))]
                 + _two_part_specs(tm, FFT_WIDTH, n_p)
                 + [tok(D_MODEL), tok(D_MODEL)]
                 + [_const_spec(w.shape) for w in wts],
        out_specs=_two_part_specs(tm, D_MODEL, n_p),
        out_shape=[jax.ShapeDtypeStruct((tp, D_MODEL), F32), jax.ShapeDtypeStruct((ts, D_MODEL), F32)],
        scratch_shapes=[pltpu.VMEM((D_MODEL // LANES, tm // CHUNK * PITCH, LANES), F32),
                        pltpu.VMEM((tm, D_MODEL), BF16)],
        compiler_params=_cparams(("arbitrary",)),
        name="mix_ffn",
    )(xp, xs, mod, ys, yf_p, yf_s, gs, gf, *wts)


def kernel(x_prompt, x_sample, state_s5, c, c_ctx, norm1_g, norm2_g, w_ada, b_ada, w_in,
           s5_lambda_re, s5_lambda_im, s5_log_step, s5_b_re, s5_b_im, s5_c_re, s5_c_im,
           s5_d, w_glu, b_glu, w_proj_s5, w_proj_fft, w_out, w_ffn_gate, w_ffn_up,
           w_ffn_down, final_norm_g):
    nb, sl, _ = x_prompt.shape
    db, dl, _ = x_sample.shape
    assert w_in.shape[0] == 1 and sl == SEG and nb == N_PROMPT_SEG
    assert db == N_SAMPLE_SEQ and dl == SEG * SEG_CHUNKS

    mod, w_in_b, m, w, e, apre, apim = _adaln_and_s5_tables(
        c_ctx[None], c, w_ada[0], b_ada[0], w_in[0], s5_lambda_re[0], s5_lambda_im[0], s5_log_step[0], s5_b_re[0],
        s5_b_im[0], s5_c_re[0], s5_c_im[0], s5_d)
    mod = mod.reshape(8, N_MOD, D_MODEL)

    tm = 512
    xp = x_prompt.reshape(nb * sl, D_MODEL)
    xs = x_sample.reshape(db * dl, D_MODEL)
    later = (w_glu[0], w_proj_s5[0], w_proj_fft[0], w_out[0], w_ffn_gate[0], w_ffn_up[0], w_ffn_down[0])
    xg, fc, fs, gs, gf, yf_p, *later_b = _inproj(xp, xs, dl, mod, norm1_g[0], w_in_b, later, tm)
    wglu_b, wps_b, wpf_b, wout_b, wg_b, wu_b, wd_b = later_b

    ys, fin = _s5_conv(xg, m, w, e, apre, apim, state_s5.astype(F32).reshape(db, 4 * GP))
    new_state = fin.reshape(nb, 1, 2, 2, S5_GROUPS, S5_STATE)

    n_prompt_units = nb // N_TILES
    yf_s = _pos_dft_long(fc.reshape(-1, N_TILES, SEG, FFT_WIDTH), fs.reshape(-1, N_TILES, SEG, FFT_WIDTH),
                         n_prompt_units, db)

    wts = (wglu_b, b_glu[0].reshape(1, S5_WIDTH), wps_b, wpf_b, wout_b, norm2_g[0].reshape(1, D_MODEL),
           wg_b, wu_b, wd_b, final_norm_g.reshape(1, D_MODEL))
    tm2 = 2 * SEG
    y_p, y_s = _mix_ffn(xp, xs, dl, mod, ys, yf_p.reshape(nb * sl, FFT_WIDTH), yf_s, gs, gf, wts, tm2)
    return (y_p.reshape(nb, sl, D_MODEL), y_s.reshape(db, dl, D_MODEL), new_state)
```

```python
import functools
import math

import numpy as np
import jax
import jax.numpy as jnp
from jax import lax
from jax.experimental import pallas as pl
from jax.experimental.pallas import tpu as pltpu

F32 = jnp.float32
BF16 = jnp.bfloat16

D_MODEL = 1024
S5_WIDTH = 512
S5_GROUPS = 32
S5_GROUP = 16
S5_STATE = 64
FFT_WIDTH = 512
FFT_GROUPS = 4
FFT_GROUP = 128
D_FF = 2816
N_MOD = 6
EPS = 1e-6

LANES = 128
CHUNK = 16
SEG_CHUNKS = 16
SEG = CHUNK * SEG_CHUNKS
GP = S5_GROUPS * S5_STATE
PAIRS = S5_GROUPS // 2
CH = CHUNK * S5_GROUP
OCT = LANES // S5_GROUP
PITCH = 24

MXU_DIM = 256
FF_SPLITS = (0, 3 * MXU_DIM, 6 * MXU_DIM, 9 * MXU_DIM, D_FF)
VMEM_LIMIT = 58 * 1024 * 1024


def _cparams(sem):
    return pltpu.CompilerParams(dimension_semantics=sem, vmem_limit_bytes=VMEM_LIMIT)


def _row_block_cast_specs(w, n_steps):
    rows = w.shape[0] // n_steps
    assert rows * n_steps == w.shape[0] and rows % 16 == 0, (w.shape, n_steps)
    spec = pl.BlockSpec((rows, w.shape[1]), lambda i: (jnp.minimum(i, n_steps - 1), 0))
    return spec, jax.ShapeDtypeStruct(w.shape, BF16)


def _cast_blocks(in_refs, out_refs):
    for src, dst in zip(in_refs, out_refs):
        dst[...] = src[...].astype(BF16)


def _ada_tile(cctx_ref, c_ref, w_ref, b_ref, o_ref):
    dot = functools.partial(jnp.dot, preferred_element_type=F32)
    n_pad = o_ref.shape[0] - 1 - c_ref.shape[0]
    c = jnp.concatenate([cctx_ref[...], c_ref[...], jnp.zeros((n_pad, c_ref.shape[1]), F32)], axis=0)
    s = c * jax.nn.sigmoid(c)
    s_hi = s.astype(BF16)
    s_lo = (s - s_hi.astype(F32)).astype(BF16)
    w = w_ref[...]
    w_hi = w.astype(BF16)
    w_lo = (w - w_hi.astype(F32)).astype(BF16)
    r = dot(jnp.concatenate([s_hi, s_lo], axis=0), w_hi)
    n = s.shape[0]
    o_ref[...] = r[0:n] + r[n:2 * n] + dot(s_hi, w_lo) + b_ref[...]


ROW_LAM_RE, ROW_LAM_IM, ROW_STEP, ROW_B_RE = 0, 2, 4, 8
ROW_B_IM = ROW_B_RE + 2 * S5_GROUP
ROW_C_RE = ROW_B_IM + 2 * S5_GROUP
ROW_C_IM = ROW_C_RE + 2 * S5_GROUP
PARAM_ROWS = ROW_C_IM + 2 * S5_GROUP


def _param_rows(p_ref, row0, d, lanes=slice(None)):
    return p_ref[row0 + d * S5_GROUP:row0 + (d + 1) * S5_GROUP, lanes]


def _exact_bf16_terms(x):
    hi = x.astype(BF16)
    r = x - hi.astype(F32)
    mid = r.astype(BF16)
    return hi, mid, (r - mid.astype(F32)).astype(BF16)


def _s5_params_to_lanes(lam_re_ref, lam_im_ref, lstep_ref, b_re_ref, b_im_ref, c_re_ref, c_im_ref, d_ref, ind_ref,
                        p_scr, dt_scr):
    dot = functools.partial(jnp.dot, preferred_element_type=F32)
    p_scr[ROW_STEP + 2:ROW_B_RE, :] = jnp.zeros((ROW_B_RE - ROW_STEP - 2, GP), F32)
    p_scr[ROW_STEP:ROW_STEP + 2, :] = sum(dot(t, ind_ref[...]) for t in _exact_bf16_terms(lstep_ref[...]))
    for d in range(2):
        for row0, src in ((ROW_LAM_RE, lam_re_ref), (ROW_LAM_IM, lam_im_ref)):
            for q in range(PAIRS):
                p_scr[row0 + d:row0 + d + 1, q * LANES:(q + 1) * LANES] = jnp.concatenate(
                    [src[d, 2 * q:2 * q + 1, :], src[d, 2 * q + 1:2 * q + 2, :]], axis=1)
        for row0, src in ((ROW_B_RE, b_re_ref), (ROW_B_IM, b_im_ref), (ROW_C_RE, c_re_ref), (ROW_C_IM, c_im_ref)):
            for q in range(PAIRS):
                p_scr[row0 + d * S5_GROUP:row0 + (d + 1) * S5_GROUP, q * LANES:(q + 1) * LANES] = jnp.concatenate(
                    [src[d, 2 * q], src[d, 2 * q + 1]], axis=1)
    own = lax.broadcasted_iota(jnp.int32, (S5_GROUPS, S5_WIDTH), 1) // S5_GROUP \
        == lax.broadcasted_iota(jnp.int32, (S5_GROUPS, S5_WIDTH), 0)
    d_rows = jnp.where(own, d_ref[...], 0.0)
    spread = (lax.broadcasted_iota(jnp.int32, (S5_WIDTH, LANES), 0) % S5_GROUP
              == lax.broadcasted_iota(jnp.int32, (S5_WIDTH, LANES), 1) % S5_GROUP).astype(BF16)
    dt_scr[...] = sum(dot(t, spread) for t in _exact_bf16_terms(d_rows))


def _s5_discretise(p_ref, apre_ref, apim_ref, gb_scr, ca_scr):
    lre = p_ref[ROW_LAM_RE:ROW_LAM_RE + 2, :]
    lim = p_ref[ROW_LAM_IM:ROW_LAM_IM + 2, :]
    step = jnp.exp(p_ref[ROW_STEP:ROW_STEP + 2, :])
    mag = jnp.exp(lre * step)
    are = mag * jnp.cos(lim * step)
    aim = mag * jnp.sin(lim * step)
    nr = are - 1.0
    den = lre * lre + lim * lim
    fr = (nr * lre + aim * lim) / den
    fi = (aim * lre - nr * lim) / den

    pr = [jnp.ones_like(are)]
    pi = [jnp.zeros_like(are)]
    for _ in range(CHUNK):
        r, i = pr[-1], pi[-1]
        pr.append(r * are - i * aim)
        pi.append(r * aim + i * are)
    a16r, a16i = pr[CHUNK], pi[CHUNK]
    qr, qi = jnp.ones_like(are), jnp.zeros_like(are)
    for c in range(SEG_CHUNKS + 1):
        apre_ref[c] = qr
        apim_ref[c] = qi
        qr, qi = qr * a16r - qi * a16i, qr * a16i + qi * a16r

    for d in range(2):
        bre, bim = _param_rows(p_ref, ROW_B_RE, d), _param_rows(p_ref, ROW_B_IM, d)
        bbre = fr[d:d + 1] * bre - fi[d:d + 1] * bim
        bbim = fr[d:d + 1] * bim + fi[d:d + 1] * bre
        cre, cim = _param_rows(p_ref, ROW_C_RE, d), _param_rows(p_ref, ROW_C_IM, d)
        for s in range(CHUNK):
            k = CHUNK - 1 - s if d == 0 else s
            r, i = pr[k][d:d + 1], pi[k][d:d + 1]
            gb_scr[0, d, s] = r * bbre - i * bbim
            gb_scr[1, d, s] = r * bbim + i * bbre
            f = s + 1 if d == 0 else CHUNK - s
            r, i = pr[f][d:d + 1], pi[f][d:d + 1]
            ca_scr[0, d, s] = r * cre - i * cim
            ca_scr[1, d, s] = -(r * cim + i * cre)


def _s5_pair_operands(q, slot_q, p_ref, dt_ref, gb_scr, ca_scr, ts_scr, m_ref, w_ref, e_ref):
    dot_nt = lambda a, b: lax.dot_general(a, b, (((1,), (1,)), ((), ())), preferred_element_type=F32)
    lane_gi = lax.broadcasted_iota(jnp.int32, (CH, LANES), 1) // S5_STATE
    row_gi = lax.broadcasted_iota(jnp.int32, (LANES, CH), 0) // S5_STATE
    eye = (lax.broadcasted_iota(jnp.int32, (LANES, LANES), 0)
           == lax.broadcasted_iota(jnp.int32, (LANES, LANES), 1)).astype(BF16)
    slot = lax.broadcasted_iota(jnp.int32, (CH, LANES), 1) // S5_GROUP
    n_lag_rows = (2 * CHUNK - 1) * S5_GROUP
    lanes = pl.ds(pl.multiple_of(q * LANES, LANES), LANES)
    for d in range(2):
        for ri in range(2):
            col = slice((2 * d + ri) * LANES, (2 * d + ri + 1) * LANES)
            gb = gb_scr[ri, d, :, :, lanes].reshape(CH, LANES)
            ca = ca_scr[ri, d, :, :, lanes].reshape(CH, LANES).astype(BF16)
            ca_t = dot_nt(eye, ca)
            for gi in range(2):
                w_ref[slot_q, gi * CH:(gi + 1) * CH, col] = jnp.where(lane_gi == gi, gb, 0.0).astype(BF16)
                e_ref[slot_q, col, gi * CH:(gi + 1) * CH] = jnp.where(row_gi == gi, ca_t, 0.0).astype(BF16)
    for gi in range(2):
        g = 2 * q + gi
        lag = []
        for d in range(2):
            c_re = jnp.concatenate([_param_rows(p_ref, ROW_C_RE, d, lanes)] * OCT, axis=0)
            c_imn = jnp.concatenate([-_param_rows(p_ref, ROW_C_IM, d, lanes)] * OCT, axis=0)
            keep = lax.broadcasted_iota(jnp.int32, (LANES, LANES), 1) // S5_STATE == gi
            c_re = jnp.where(keep, c_re, 0.0).astype(BF16)
            c_imn = jnp.where(keep, c_imn, 0.0).astype(BF16)
            gre = gb_scr[0, d, :, :, lanes].reshape(CH, LANES).astype(BF16)
            gim = gb_scr[1, d, :, :, lanes].reshape(CH, LANES).astype(BF16)
            lag.append(dot_nt(gre, c_re) + dot_nt(gim, c_imn))
        zl = (CHUNK - 1) * S5_GROUP
        ts_scr[0:zl, :] = lag[0][0:zl]
        on_diag = (lax.broadcasted_iota(jnp.int32, (S5_GROUP, LANES), 1) % S5_GROUP
                   == lax.broadcasted_iota(jnp.int32, (S5_GROUP, LANES), 0))
        skip = jnp.where(on_diag, dt_ref[pl.ds(g, 1), :], 0.0)
        ts_scr[zl:zl + S5_GROUP, :] = lag[0][zl:] + lag[1][0:S5_GROUP] + skip
        ts_scr[zl + S5_GROUP:n_lag_rows, :] = lag[1][S5_GROUP:]
        for hf in range(CHUNK // OCT):
            acc = None
            for s in range(OCT):
                t = OCT * hf + s
                win = ts_scr[(CHUNK - 1 - t) * S5_GROUP:(CHUNK - 1 - t) * S5_GROUP + CH, :]
                acc = win if acc is None else jnp.where(slot == s, win, acc)
            m_ref[2 * slot_q + gi, :, hf * LANES:(hf + 1) * LANES] = acc.astype(BF16)


PAIRS_PER_STEP = 2


def _ada_tables_kernel(cctx_ref, c_ref, wada_ref, bada_ref, win_ref, lam_re_ref, lam_im_ref, lstep_ref, b_re_ref, b_im_ref,
                       c_re_ref, c_im_ref, d_ref, ind_ref, mod_ref, winb_ref, apre_ref, apim_ref, m_ref, w_ref, e_ref,
                       p_scr, dt_scr, gb_scr, ca_scr, ts_scr):
    j = pl.program_id(0)
    _cast_blocks([win_ref], [winb_ref])
    _ada_tile(cctx_ref, c_ref, wada_ref, bada_ref, mod_ref)

    @pl.when(j == 0)
    def _():
        _s5_params_to_lanes(lam_re_ref, lam_im_ref, lstep_ref, b_re_ref, b_im_ref, c_re_ref, c_im_ref, d_ref, ind_ref,
                            p_scr, dt_scr)
        _s5_discretise(p_scr, apre_ref, apim_ref, gb_scr, ca_scr)

    for lp in range(PAIRS_PER_STEP):
        _s5_pair_operands(PAIRS_PER_STEP * j + lp, lp, p_scr, dt_scr, gb_scr, ca_scr, ts_scr, m_ref, w_ref, e_ref)


def _adaln_and_s5_tables(c_ctx, c, w_ada, b_ada, w_in, lam_re, lam_im, log_step, b_re, b_im, c_re, c_im, s5_d):
    ind_np = np.zeros((S5_GROUPS, GP), np.float32)
    ind_np[np.arange(GP) // S5_STATE, np.arange(GP)] = 1.0
    s5_in = (lam_re, lam_im, log_step, jnp.swapaxes(b_re, -1, -2), jnp.swapaxes(b_im, -1, -2), c_re, c_im,
             s5_d.reshape(1, S5_WIDTH), jnp.asarray(ind_np).astype(BF16))

    n_steps = PAIRS // PAIRS_PER_STEP
    n_out = w_ada.shape[1]
    tn = n_out // n_steps
    cast_spec, cast_shape = _row_block_cast_specs(w_in, n_steps)
    whole = lambda a: pl.BlockSpec(a.shape, lambda j, nd=a.ndim: (0,) * nd)
    powers = jax.ShapeDtypeStruct((SEG_CHUNKS + 1, 2, GP), F32)
    tab = (2, 2, CHUNK, S5_GROUP, GP)
    mod, w_in_b, apre, apim, m, w, e = pl.pallas_call(
        _ada_tables_kernel,
        grid=(n_steps,),
        in_specs=[whole(c_ctx), whole(c),
                  pl.BlockSpec((D_MODEL, tn), lambda j: (0, j)),
                  pl.BlockSpec((1, tn), lambda j: (0, j)),
                  cast_spec] + [whole(a) for a in s5_in],
        out_specs=[pl.BlockSpec((8, tn), lambda j: (0, j)), cast_spec, whole(powers), whole(powers),
                   pl.BlockSpec((2 * PAIRS_PER_STEP, CH, CH), lambda j: (j, 0, 0)),
                   pl.BlockSpec((PAIRS_PER_STEP, 2 * CH, 4 * LANES), lambda j: (j, 0, 0)),
                   pl.BlockSpec((PAIRS_PER_STEP, 4 * LANES, 2 * CH), lambda j: (j, 0, 0))],
        out_shape=[jax.ShapeDtypeStruct((8, n_out), F32), cast_shape, powers, powers,
                   jax.ShapeDtypeStruct((S5_GROUPS, CH, CH), BF16),
                   jax.ShapeDtypeStruct((PAIRS, 2 * CH, 4 * LANES), BF16),
                   jax.ShapeDtypeStruct((PAIRS, 4 * LANES, 2 * CH), BF16)],
        scratch_shapes=[pltpu.VMEM((PARAM_ROWS, GP), F32),
                        pltpu.VMEM((S5_GROUPS, LANES), F32),
                        pltpu.VMEM(tab, F32),
                        pltpu.VMEM(tab, F32),
                        pltpu.VMEM(((2 * CHUNK - 1) * S5_GROUP, LANES), F32)],
        compiler_params=_cparams(("arbitrary",)),
        name="adaln_s5_tables",
    )(c_ctx, c, w_ada, b_ada.reshape(1, n_out), w_in, *s5_in)
    return mod, w_in_b, m, w, e, apre, apim


def _rms(x, g):
    return x * lax.rsqrt(jnp.mean(x * x, axis=-1, keepdims=True) + EPS) * g


def _slot_transpose(v):
    v = list(v)
    slot = lax.broadcasted_iota(jnp.int32, v[0].shape, 1) // S5_GROUP
    for k in (4, 2, 1):
        low = (slot & k) == 0
        for i in range(OCT):
            if i & k:
                continue
            a, b = v[i], v[i + k]
            v[i] = jnp.where(low, a, pltpu.roll(b, k * S5_GROUP, axis=1))
            v[i + k] = jnp.where(low, pltpu.roll(a, LANES - k * S5_GROUP, axis=1), b)
    return v


def _inproj_kernel(n_prompt_tiles, tiles_per_seq, n_cast, xp_ref, xs_ref, mod_ref, g_ref, win_ref, bdc_ref, bds_ref,
                   cm_ref, sm_ref, *rest):
    cast_in, rest = rest[:n_cast], rest[n_cast:]
    xg_ref, fc_ref, fs_ref, gs_ref, gf_ref, yfp_ref = rest[:6]
    cast_out, (h_scr, hb_scr, zs_scr) = rest[6:6 + n_cast], rest[6 + n_cast:]
    tm = xp_ref.shape[0]
    mod = _mod_vectors(mod_ref, n_prompt_tiles, tiles_per_seq)
    is_prompt = pl.program_id(0) < n_prompt_tiles
    dot = functools.partial(jnp.dot, preferred_element_type=F32)
    o = 2 * S5_WIDTH
    tiles = [slice(j * SEG, (j + 1) * SEG) for j in range(tm // SEG)]

    def norm_phase(j, r):
        x = jnp.where(is_prompt, xp_ref[r, :], xs_ref[r, :])
        h = _rms(x, g_ref[...]) * (1.0 + mod[1]) + mod[0]
        for k in range(D_MODEL // LANES):
            for c in range(SEG_CHUNKS):
                p0 = (j * SEG_CHUNKS + c) * PITCH
                h_scr[k, p0:p0 + CHUNK, :] = h[c * CHUNK:(c + 1) * CHUNK, k * LANES:(k + 1) * LANES]
        for s in range(CHUNK):
            r0 = j * SEG + s * SEG_CHUNKS
            for k in range(D_MODEL // LANES):
                hb_scr[r0:r0 + SEG_CHUNKS, k * LANES:(k + 1) * LANES] = (
                    h_scr[k, pl.ds(j * SEG_CHUNKS * PITCH + s, SEG_CHUNKS, stride=PITCH), :].astype(BF16))

    def s5_gate_phase(j, r):
        zs_scr[r, :] = dot(hb_scr[r, :], win_ref[:, 0:S5_WIDTH])
        gs_ref[r, :] = jax.nn.sigmoid(dot(hb_scr[r, :], win_ref[:, o:o + D_MODEL])).astype(BF16)

    def fold_phase(j, r):
        for b in range(S5_WIDTH // LANES):
            for hf in range(CHUNK // OCT):
                z = [zs_scr[j * SEG + s * SEG_CHUNKS:j * SEG + (s + 1) * SEG_CHUNKS, b * LANES:(b + 1) * LANES]
                     for s in range(OCT * hf, OCT * (hf + 1))]
                for i, xi in enumerate(_slot_transpose(z)):
                    xg_ref[OCT * b + i, j * SEG_CHUNKS:(j + 1) * SEG_CHUNKS,
                           hf * LANES:(hf + 1) * LANES] = xi.astype(BF16)

    def fourier_gate_phase(j, r):
        uf = dot(hb_scr[r, :], win_ref[:, S5_WIDTH:2 * S5_WIDTH]).astype(BF16)
        for n0 in range(0, FFT_WIDTH, MXU_DIM):
            cols = slice(n0, n0 + MXU_DIM)
            fc_ref[r, cols] = dot(uf[:, cols], bdc_ref[cols, cols]).astype(BF16)
            fs_ref[r, cols] = dot(uf[:, cols], bds_ref[cols, cols]).astype(BF16)
        gf_ref[r, :] = jax.nn.sigmoid(dot(hb_scr[r, :], win_ref[:, o + D_MODEL:o + 2 * D_MODEL])).astype(BF16)

    for phase in (norm_phase, s5_gate_phase, fold_phase, fourier_gate_phase):
        for j, r in enumerate(tiles):
            phase(j, r)
    _cast_blocks(cast_in, cast_out)

    @pl.when(is_prompt)
    def _():
        cm, sm = cm_ref[...].astype(BF16), sm_ref[...].astype(BF16)
        for r in tiles:
            yfp_ref[r, :] = (dot(cm, fc_ref[r, :]) - dot(sm, fs_ref[r, :])).astype(BF16)


def _channel_dft_mats():
    j = np.arange(FFT_GROUP)
    ang = 2.0 * np.pi * ((j[:, None] * j[None, :]) % FFT_GROUP) / FFT_GROUP
    blk_c = np.cos(ang) / math.sqrt(FFT_GROUP)
    blk_s = np.sin(ang) / math.sqrt(FFT_GROUP)
    bdc = np.kron(np.eye(FFT_GROUPS), blk_c)
    bds = np.kron(np.eye(FFT_GROUPS), blk_s)
    return jnp.asarray(bdc, F32).astype(BF16), jnp.asarray(bds, F32).astype(BF16)


def _const_spec(shape):
    nd = len(shape)
    return pl.BlockSpec(shape, lambda i: (0,) * nd, pipeline_mode=pl.Buffered(1))


def _two_part_specs(tm, width, n_prompt_tiles):
    return [pl.BlockSpec((tm, width), lambda i: (jnp.minimum(i, n_prompt_tiles - 1), 0)),
            pl.BlockSpec((tm, width), lambda i: (jnp.maximum(i - n_prompt_tiles, 0), 0))]


MOD_SPEC = pl.BlockSpec((8, N_MOD * D_MODEL), lambda i: (0, 0))


def _mod_vectors(mod_ref, n_prompt_tiles, tiles_per_seq):
    i = pl.program_id(0)
    row = jnp.where(i < n_prompt_tiles, 0, 1 + (i - n_prompt_tiles) // tiles_per_seq)
    full = mod_ref[pl.ds(row, 1), :]
    return [full[:, k * D_MODEL:(k + 1) * D_MODEL] for k in range(N_MOD)]


N_CAST_STEPS = 16


def _inproj(xp, xs, seq_len, mod, norm_g, w_in_b, later_weights, tm):
    t = xp.shape[0] + xs.shape[0]
    n_p = xp.shape[0] // tm
    assert t // tm >= N_CAST_STEPS
    bdc, bds = _channel_dft_mats()
    cm, sm = _tile_dft_tables(1.0 / math.sqrt(SEG))
    casts = [_row_block_cast_specs(w, N_CAST_STEPS) for w in later_weights]
    tok = lambda w: pl.BlockSpec((tm, w), lambda i: (i, 0))
    out = lambda w: jax.ShapeDtypeStruct((t, w), BF16)
    return pl.pallas_call(
        functools.partial(_inproj_kernel, n_p, seq_len // tm, len(casts)),
        grid=(t // tm,),
        in_specs=_two_part_specs(tm, D_MODEL, n_p) + [
                  MOD_SPEC,
                  _const_spec((1, D_MODEL)),
                  _const_spec(w_in_b.shape),
                  _const_spec(bdc.shape),
                  _const_spec(bds.shape), _const_spec(cm.shape), _const_spec(sm.shape)] + [c[0] for c in casts],
        out_specs=[pl.BlockSpec((S5_GROUPS, tm // CHUNK, CH), lambda i: (0, i, 0)),
                   tok(FFT_WIDTH), tok(FFT_WIDTH), tok(D_MODEL), tok(D_MODEL),
                   _two_part_specs(tm, FFT_WIDTH, n_p)[0]] + [c[0] for c in casts],
        out_shape=[jax.ShapeDtypeStruct((S5_GROUPS, t // CHUNK, CH), BF16),
                   out(FFT_WIDTH), out(FFT_WIDTH), out(D_MODEL), out(D_MODEL),
                   jax.ShapeDtypeStruct((xp.shape[0], FFT_WIDTH), BF16)] + [c[1] for c in casts],
        scratch_shapes=[pltpu.VMEM((D_MODEL // LANES, tm // CHUNK * PITCH, LANES), F32),
                        pltpu.VMEM((tm, D_MODEL), BF16),
                        pltpu.VMEM((tm, S5_WIDTH), F32)],
        compiler_params=_cparams(("arbitrary",)),
        name="inproj",
    )(xp, xs, mod, norm_g.reshape(1, D_MODEL), w_in_b, bdc, bds, cm, sm, *later_weights)


N_PROMPT_SEG = 16
N_SAMPLE_SEG = 32
N_SAMPLE_SEQ = 2
N_SEG = N_PROMPT_SEG + N_SAMPLE_SEG
ROWS = SEG_CHUNKS * N_SEG
ROWS_P = SEG_CHUNKS * N_PROMPT_SEG
OCT_PAIRS = OCT // 2
SEG_PITCH = 56


def _cmul_add(ar, ai, hr, hi, sr, si):
    return ar * hr - ai * hi + sr, ar * hi + ai * hr + si


def _s5_kernel(x_ref, m_ref, w_ref, e_ref, apre_ref, apim_ref, h0_ref,
               y_ref, fin_ref, s_scr, hin_scr, hinp_scr, f_scr, hs_scr, y_scr):
    dot = functools.partial(jnp.dot, preferred_element_type=F32)
    parts = ((0, ROWS_P), (ROWS_P, ROWS - ROWS_P))
    seg_rows = lambda c: pl.ds(c, N_SEG, stride=PITCH)
    blk = lambda c: pl.ds(c * SEG_PITCH, N_SEG)
    seq_rows = lambda j: pl.ds(j, N_SAMPLE_SEQ, stride=SEG_CHUNKS)
    lat = lambda c: pl.ds(c * SEG_PITCH + N_PROMPT_SEG, N_SAMPLE_SEG)
    zero = jnp.zeros((N_SEG, LANES), F32)
    octet = pl.program_id(0)
    state_lanes = lambda k, pr: pl.ds(pl.multiple_of(k * GP + (octet * OCT_PAIRS + pr) * LANES, LANES), LANES)

    for pr in range(OCT_PAIRS):
        ln = slice(pr * LANES, (pr + 1) * LANES)
        g0, g1 = 2 * pr, 2 * pr + 1
        for r0, nr in parts:
            s = (dot(x_ref[g0, r0:r0 + nr, :], w_ref[pr, 0:CH, :])
                 + dot(x_ref[g1, r0:r0 + nr, :], w_ref[pr, CH:2 * CH, :]))
            for k in range(4):
                for sg in range(nr // SEG_CHUNKS):
                    p0 = (r0 // SEG_CHUNKS + sg) * PITCH
                    s_scr[k, p0:p0 + SEG_CHUNKS, :] = s[sg * SEG_CHUNKS:(sg + 1) * SEG_CHUNKS, k * LANES:(k + 1) * LANES]

        ar, ai = apre_ref[1, 0:1, ln], apim_ref[1, 0:1, ln]
        hr, hi = zero, zero
        for c in range(SEG_CHUNKS):
            hin_scr[0, blk(c), :] = hr
            hin_scr[1, blk(c), :] = hi
            hr, hi = _cmul_add(ar, ai, hr, hi, s_scr[0, seg_rows(c), :], s_scr[1, seg_rows(c), :])
        fin_ref[:, state_lanes(0, pr)] = hr[0:N_PROMPT_SEG]
        fin_ref[:, state_lanes(1, pr)] = hi[0:N_PROMPT_SEG]
        f_scr[0] = hr[N_PROMPT_SEG:]
        f_scr[1] = hi[N_PROMPT_SEG:]
        br, bi = apre_ref[1, 1:2, ln], apim_ref[1, 1:2, ln]
        gr, gi = zero, zero
        for c in range(SEG_CHUNKS - 1, -1, -1):
            hin_scr[2, blk(c), :] = gr
            hin_scr[3, blk(c), :] = gi
            gr, gi = _cmul_add(br, bi, gr, gi, s_scr[2, seg_rows(c), :], s_scr[3, seg_rows(c), :])
        fin_ref[:, state_lanes(2, pr)] = gr[0:N_PROMPT_SEG]
        fin_ref[:, state_lanes(3, pr)] = gi[0:N_PROMPT_SEG]
        f_scr[2] = gr[N_PROMPT_SEG:]
        f_scr[3] = gi[N_PROMPT_SEG:]

        a2r, a2i = apre_ref[SEG_CHUNKS, 0:1, ln], apim_ref[SEG_CHUNKS, 0:1, ln]
        hr, hi = h0_ref[:, state_lanes(0, pr)], h0_ref[:, state_lanes(1, pr)]
        for j in range(SEG_CHUNKS):
            hs_scr[0, seq_rows(j), :] = hr
            hs_scr[1, seq_rows(j), :] = hi
            hr, hi = _cmul_add(a2r, a2i, hr, hi, f_scr[0, seq_rows(j), :], f_scr[1, seq_rows(j), :])
        b2r, b2i = apre_ref[SEG_CHUNKS, 1:2, ln], apim_ref[SEG_CHUNKS, 1:2, ln]
        gr, gi = h0_ref[:, state_lanes(2, pr)], h0_ref[:, state_lanes(3, pr)]
        for j in range(SEG_CHUNKS - 1, -1, -1):
            hs_scr[2, seq_rows(j), :] = gr
            hs_scr[3, seq_rows(j), :] = gi
            gr, gi = _cmul_add(b2r, b2i, gr, gi, f_scr[2, seq_rows(j), :], f_scr[3, seq_rows(j), :])
        for c in range(SEG_CHUNKS):
            p_r, p_i = apre_ref[c, 0:1, ln], apim_ref[c, 0:1, ln]
            hr, hi = _cmul_add(p_r, p_i, hs_scr[0], hs_scr[1], hin_scr[0, lat(c), :], hin_scr[1, lat(c), :])
            hin_scr[0, lat(c), :] = hr
            hin_scr[1, lat(c), :] = hi
            cb = SEG_CHUNKS - 1 - c
            p_r, p_i = apre_ref[cb, 1:2, ln], apim_ref[cb, 1:2, ln]
            gr, gi = _cmul_add(p_r, p_i, hs_scr[2], hs_scr[3], hin_scr[2, lat(c), :], hin_scr[3, lat(c), :])
            hin_scr[2, lat(c), :] = gr
            hin_scr[3, lat(c), :] = gi

        for k in range(4):
            for sg in range(N_SEG):
                hinp_scr[sg * SEG_CHUNKS:(sg + 1) * SEG_CHUNKS, k * LANES:(k + 1) * LANES] = (
                    hin_scr[k, pl.ds(sg, SEG_CHUNKS, stride=SEG_PITCH), :].astype(BF16))
        for gi_, g in ((0, g0), (1, g1)):
            for r0, nr in parts:
                y_scr[g, r0:r0 + nr, :] = (dot(x_ref[g, r0:r0 + nr, :], m_ref[g])
                                           + dot(hinp_scr[r0:r0 + nr, :], e_ref[pr, :, gi_ * CH:(gi_ + 1) * CH]))

    rb_rows = 2 * SEG_CHUNKS
    for rb in range(ROWS // rb_rows):
        for hf in range(CHUNK // OCT):
            v = [y_scr[i, rb * rb_rows:(rb + 1) * rb_rows, hf * LANES:(hf + 1) * LANES] for i in range(OCT)]
            for s, acc in enumerate(_slot_transpose(v)):
                t = OCT * hf + s
                for sg in range(rb_rows // SEG_CHUNKS):
                    seg = rb * (rb_rows // SEG_CHUNKS) + sg
                    r0 = seg * SEG + t * SEG_CHUNKS
                    y_ref[0, r0:r0 + SEG_CHUNKS, :] = acc[sg * SEG_CHUNKS:(sg + 1) * SEG_CHUNKS]


def _s5_conv(xg, m, w, e, apre, apim, h0):
    n_oct = S5_GROUPS // OCT
    assert xg.shape[1] == ROWS
    nt = ROWS * CHUNK
    return pl.pallas_call(
        _s5_kernel,
        grid=(n_oct,),
        in_specs=[pl.BlockSpec((OCT, ROWS, CH), lambda o: (o, 0, 0)),
                  pl.BlockSpec((OCT, CH, CH), lambda o: (o, 0, 0)),
                  pl.BlockSpec((OCT_PAIRS, 2 * CH, 4 * LANES), lambda o: (o, 0, 0)),
                  pl.BlockSpec((OCT_PAIRS, 4 * LANES, 2 * CH), lambda o: (o, 0, 0)),
                  pl.BlockSpec((SEG_CHUNKS + 1, 2, OCT_PAIRS * LANES), lambda o: (0, 0, o)),
                  pl.BlockSpec((SEG_CHUNKS + 1, 2, OCT_PAIRS * LANES), lambda o: (0, 0, o)),
                  pl.BlockSpec((N_SAMPLE_SEQ, 4 * GP), lambda o: (0, 0))],
        out_specs=[pl.BlockSpec((1, nt, LANES), lambda o: (o, 0, 0)),
                   pl.BlockSpec((N_PROMPT_SEG, 4 * GP), lambda o: (0, 0))],
        out_shape=[jax.ShapeDtypeStruct((n_oct, nt, LANES), F32),
                   jax.ShapeDtypeStruct((N_PROMPT_SEG, 4 * GP), F32)],
        scratch_shapes=[pltpu.VMEM((4, N_SEG * PITCH, LANES), F32),
                        pltpu.VMEM((4, SEG_CHUNKS * SEG_PITCH, LANES), F32),
                        pltpu.VMEM((ROWS, 4 * LANES), BF16),
                        pltpu.VMEM((4, N_SAMPLE_SEG, LANES), F32),
                        pltpu.VMEM((4, N_SAMPLE_SEG, LANES), F32),
                        pltpu.VMEM((OCT, ROWS, CH), F32)],
        compiler_params=_cparams(("arbitrary",)),
        name="s5_conv",
    )(xg, m, w, e, apre, apim, h0)


def _tile_order_positions(l):
    r = np.arange(l)
    tile, wi = r // SEG, r % SEG
    return tile * SEG + (wi % SEG_CHUNKS) * CHUNK + wi // SEG_CHUNKS


def _tile_dft_tables(scale, rows_in_tile_order=True):
    pos = _tile_order_positions(SEG)
    freq = pos if rows_in_tile_order else np.arange(SEG)
    ang = 2.0 * np.pi * ((freq[:, None] * pos[None, :]) % SEG) / SEG
    return jnp.asarray(np.cos(ang) * scale, F32), jnp.asarray(np.sin(ang) * scale, F32)


N_TILES = 16


def _fft16(xr, xi):
    n = len(xr)
    rev = [int(format(i, "04b")[::-1], 2) for i in range(n)]
    ar = [xr[r] for r in rev]
    ai = [xi[r] for r in rev]
    size = 2
    while size <= n:
        half = size // 2
        for start in range(0, n, size):
            for k in range(half):
                wr = math.cos(2.0 * math.pi * k / size)
                wi = -math.sin(2.0 * math.pi * k / size)
                i0, i1 = start + k, start + k + half
                if k == 0:
                    tr, ti = ar[i1], ai[i1]
                elif 4 * k == size:
                    tr, ti = ai[i1], -ar[i1]
                else:
                    tr = ar[i1] * wr - ai[i1] * wi
                    ti = ar[i1] * wi + ai[i1] * wr
                ar[i1], ai[i1] = ar[i0] - tr, ai[i0] - ti
                ar[i0], ai[i0] = ar[i0] + tr, ai[i0] + ti
        size *= 2
    return ar, ai


def _pos_dft_long_kernel(cb_ref, sb_ref, ca_ref, sa_ref, fc_ref, fs_ref, o_ref, a_scr):
    wdt = fc_ref.shape[2]

    def rows_fft(r, carry):
        rows = pl.ds(pl.multiple_of(r * CHUNK, CHUNK), CHUNK)
        for b in range(wdt // LANES):
            lns = slice(b * LANES, (b + 1) * LANES)
            xr = [fc_ref[j, rows, lns].astype(F32) for j in range(N_TILES)]
            xi = [-fs_ref[j, rows, lns].astype(F32) for j in range(N_TILES)]
            ar, ai = _fft16(xr, xi)
            for k in range(N_TILES):
                a_scr[k, 0, rows, lns] = ar[k].astype(BF16)
                a_scr[k, 1, rows, lns] = ai[k].astype(BF16)
        return carry

    lax.fori_loop(0, SEG // CHUNK, rows_fft, 0)

    dot = functools.partial(jnp.dot, preferred_element_type=F32)
    cb, sb = cb_ref[...], sb_ref[...]
    for k1 in range(N_TILES):
        ca, sa = ca_ref[k1:k1 + 1, :], sa_ref[k1:k1 + 1, :]
        dc = (cb * ca - sb * sa).astype(BF16)
        ds = (sb * ca + cb * sa).astype(BF16)
        out = dot(dc, a_scr[k1, 0]) + dot(ds, a_scr[k1, 1])
        o_ref[:, k1 * SEG_CHUNKS:(k1 + 1) * SEG_CHUNKS, :] = out.astype(BF16).reshape(N_TILES, SEG_CHUNKS, wdt)


def _pos_dft_long(fc, fs, first, n):
    wdt = fc.shape[-1]
    l = N_TILES * SEG
    wb = 2 * LANES
    cb, sb = _tile_dft_tables(1.0 / math.sqrt(l), rows_in_tile_order=False)
    pos = _tile_order_positions(SEG)
    ang = 2.0 * np.pi * (np.arange(N_TILES)[:, None] * pos[None, :]) / l
    ca, sa = jnp.asarray(np.cos(ang), F32), jnp.asarray(np.sin(ang), F32)
    const2 = lambda shape: pl.BlockSpec(shape, lambda b, k: (0, 0), pipeline_mode=pl.Buffered(1))
    seq_in = pl.BlockSpec((None, N_TILES, SEG, wb), lambda b, k: (first + b, 0, 0, k))
    seq_out = pl.BlockSpec((None, N_TILES, SEG, wb), lambda b, k: (b, 0, 0, k))
    out = pl.pallas_call(
        _pos_dft_long_kernel,
        grid=(n, wdt // wb),
        in_specs=[const2((SEG, SEG)), const2((SEG, SEG)), const2((N_TILES, SEG)), const2((N_TILES, SEG)),
                  seq_in, seq_in],
        out_specs=seq_out,
        out_shape=jax.ShapeDtypeStruct((n, N_TILES, SEG, wdt), BF16),
        scratch_shapes=[pltpu.VMEM((N_TILES, 2, SEG, wb), BF16)],
        compiler_params=_cparams(("parallel", "parallel")),
        name="pos_dft_long",
    )(cb, sb, ca, sa, fc, fs)
    return out.reshape(n * l, wdt)


def _mix_ffn_kernel(n_prompt_tiles, tiles_per_seq, xp_ref, xs_ref, mod_ref, ys_ref, yfp_ref, yfs_ref, gs_ref, gf_ref,
                    wglu_ref, bglu_ref, wps_ref, wpf_ref, wout_ref, n2_ref,
                    wg_ref, wu_ref, wd_ref, fn_ref, op_ref, os_ref, m_scr, mb_scr):
    tm = xp_ref.shape[0]
    dot = functools.partial(jnp.dot, preferred_element_type=F32)
    mod = _mod_vectors(mod_ref, n_prompt_tiles, tiles_per_seq)
    is_prompt = pl.program_id(0) < n_prompt_tiles
    tiles = [slice(j * SEG, (j + 1) * SEG) for j in range(tm // SEG)]
    st = [dict() for _ in tiles]

    def gelu_phase(j, r, s):
        y = jnp.concatenate([ys_ref[b, r, :] for b in range(S5_WIDTH // LANES)], axis=1)
        s["z"] = jax.nn.gelu(y)

    def glu_phase(j, r, s):
        z = s["z"]
        s["z"] = (z * jax.nn.sigmoid(dot(z.astype(BF16), wglu_ref[...]) + bglu_ref[...])).astype(BF16)

    def proj_phase(j, r, s):
        yf = jnp.where(is_prompt, yfp_ref[r, :], yfs_ref[r, :])
        m = (gs_ref[r, :].astype(F32) * dot(s.pop("z"), wps_ref[...])
             + gf_ref[r, :].astype(F32) * dot(yf, wpf_ref[...]))
        for k in range(D_MODEL // LANES):
            for t in range(CHUNK):
                p0 = (j * CHUNK + t) * PITCH
                m_scr[k, p0:p0 + SEG_CHUNKS, :] = m[t * SEG_CHUNKS:(t + 1) * SEG_CHUNKS, k * LANES:(k + 1) * LANES]
        for c in range(SEG_CHUNKS):
            r0 = j * SEG + c * CHUNK
            for k in range(D_MODEL // LANES):
                mb_scr[r0:r0 + CHUNK, k * LANES:(k + 1) * LANES] = (
                    m_scr[k, pl.ds(j * CHUNK * PITCH + c, CHUNK, stride=PITCH), :].astype(BF16))

    def out_phase(j, r, s):
        x = jnp.where(is_prompt, xp_ref[r, :], xs_ref[r, :])
        x1 = x + mod[2] * dot(mb_scr[r, :], wout_ref[...])
        s["x1"] = x1
        s["h2"] = (_rms(x1, n2_ref[...]) * (1.0 + mod[4]) + mod[3]).astype(BF16)

    def ffn_phase(n0, n1, j, r, s):
        gate = dot(s["h2"], wg_ref[:, n0:n1])
        up = dot(s["h2"], wu_ref[:, n0:n1])
        part = dot((gate * jax.nn.sigmoid(gate) * up).astype(BF16), wd_ref[n0:n1, :])
        s["ff"] = part if "ff" not in s else s["ff"] + part

    def final_phase(j, r, s):
        x2 = s.pop("x1") + mod[5] * s.pop("ff")
        s["res"] = _rms(x2, fn_ref[...])

    phases = [gelu_phase, glu_phase, proj_phase, out_phase]
    phases += [functools.partial(ffn_phase, n0, n1) for n0, n1 in zip(FF_SPLITS[:-1], FF_SPLITS[1:])]
    phases += [final_phase]
    for phase in phases:
        for j, r in enumerate(tiles):
            phase(j, r, st[j])

    @pl.when(is_prompt)
    def _():
        for j, r in enumerate(tiles):
            op_ref[r, :] = st[j]["res"]

    @pl.when(jnp.logical_not(is_prompt))
    def _():
        for j, r in enumerate(tiles):
            os_ref[r, :] = st[j]["res"]


def _mix_ffn(xp, xs, seq_len, mod, ys, yf_p, yf_s, gs, gf, wts, tm):
    tp, ts = xp.shape[0], xs.shape[0]
    n_p = tp // tm
    tok = lambda w: pl.BlockSpec((tm, w), lambda i: (i, 0))
    return pl.pallas_call(
        functools.partial(_mix_ffn_kernel, n_p, seq_len // tm),
        grid=((tp + ts) // tm,),
        in_specs=_two_part_specs(tm, D_MODEL, n_p)
                 + [MOD_SPEC,
                    pl.BlockSpec((S5_WIDTH // LANES, tm, LANES), lambda i: (0, i, 0))]
                 + _two_part_specs(tm, FFT_WIDTH, n_p)
                 + [tok(D_MODEL), tok(D_MODEL)]
                 + [_const_spec(w.shape) for w in wts],
        out_specs=_two_part_specs(tm, D_MODEL, n_p),
        out_shape=[jax.ShapeDtypeStruct((tp, D_MODEL), F32), jax.ShapeDtypeStruct((ts, D_MODEL), F32)],
        scratch_shapes=[pltpu.VMEM((D_MODEL // LANES, tm // CHUNK * PITCH, LANES), F32),
                        pltpu.VMEM((tm, D_MODEL), BF16)],
        compiler_params=_cparams(("arbitrary",)),
        name="mix_ffn",
    )(xp, xs, mod, ys, yf_p, yf_s, gs, gf, *wts)


def kernel(x_prompt, x_sample, state_s5, c, c_ctx, norm1_g, norm2_g, w_ada, b_ada, w_in,
           s5_lambda_re, s5_lambda_im, s5_log_step, s5_b_re, s5_b_im, s5_c_re, s5_c_im,
           s5_d, w_glu, b_glu, w_proj_s5, w_proj_fft, w_out, w_ffn_gate, w_ffn_up,
           w_ffn_down, final_norm_g):
    nb, sl, _ = x_prompt.shape
    db, dl, _ = x_sample.shape
    assert w_in.shape[0] == 1 and sl == SEG and nb == N_PROMPT_SEG
    assert db == N_SAMPLE_SEQ and dl == SEG * SEG_CHUNKS

    mod, w_in_b, m, w, e, apre, apim = _adaln_and_s5_tables(
        c_ctx[None], c, w_ada[0], b_ada[0], w_in[0], s5_lambda_re[0], s5_lambda_im[0], s5_log_step[0], s5_b_re[0],
        s5_b_im[0], s5_c_re[0], s5_c_im[0], s5_d)

    tm = 512
    xp = x_prompt.reshape(nb * sl, D_MODEL)
    xs = x_sample.reshape(db * dl, D_MODEL)
    later = (w_glu[0], w_proj_s5[0], w_proj_fft[0], w_out[0], w_ffn_gate[0], w_ffn_up[0], w_ffn_down[0])
    xg, fc, fs, gs, gf, yf_p, *later_b = _inproj(xp, xs, dl, mod, norm1_g[0], w_in_b, later, tm)
    wglu_b, wps_b, wpf_b, wout_b, wg_b, wu_b, wd_b = later_b

    ys, fin = _s5_conv(xg, m, w, e, apre, apim, state_s5.astype(F32).reshape(db, 4 * GP))
    new_state = fin.reshape(nb, 1, 2, 2, S5_GROUPS, S5_STATE)

    n_prompt_units = nb // N_TILES
    yf_s = _pos_dft_long(fc.reshape(-1, N_TILES, SEG, FFT_WIDTH), fs.reshape(-1, N_TILES, SEG, FFT_WIDTH),
                         n_prompt_units, db)

    wts = (wglu_b, b_glu[0].reshape(1, S5_WIDTH), wps_b, wpf_b, wout_b, norm2_g[0].reshape(1, D_MODEL),
           wg_b, wu_b, wd_b, final_norm_g.reshape(1, D_MODEL))
    tm2 = 2 * SEG
    y_p, y_s = _mix_ffn(xp, xs, dl, mod, ys, yf_p.reshape(nb * sl, FFT_WIDTH), yf_s, gs, gf, wts, tm2)
    return (y_p.reshape(nb, sl, D_MODEL), y_s.reshape(db, dl, D_MODEL), new_state)
```

```python
import functools
import math

import numpy as np
import jax
import jax.numpy as jnp
from jax import lax
from jax.experimental import pallas as pl
from jax.experimental.pallas import tpu as pltpu

F32 = jnp.float32
BF16 = jnp.bfloat16

D_MODEL = 1024
S5_WIDTH = 512
S5_GROUPS = 32
S5_GROUP = 16
S5_STATE = 64
FFT_WIDTH = 512
FFT_GROUPS = 4
FFT_GROUP = 128
D_FF = 2816
N_MOD = 6
EPS = 1e-6

LANES = 128
CHUNK = 16
SEG_CHUNKS = 16
SEG = CHUNK * SEG_CHUNKS
GP = S5_GROUPS * S5_STATE
PAIRS = S5_GROUPS // 2
CH = CHUNK * S5_GROUP
OCT = LANES // S5_GROUP
PITCH = 24

MXU_DIM = 256
FF_SPLITS = (0, 3 * MXU_DIM, 6 * MXU_DIM, 9 * MXU_DIM, D_FF)
VMEM_LIMIT = 58 * 1024 * 1024


def _cparams(sem):
    return pltpu.CompilerParams(dimension_semantics=sem, vmem_limit_bytes=VMEM_LIMIT)


def _row_block_cast_specs(w, n_steps):
    rows = w.shape[0] // n_steps
    assert rows * n_steps == w.shape[0] and rows % 16 == 0, (w.shape, n_steps)
    spec = pl.BlockSpec((rows, w.shape[1]), lambda i: (jnp.minimum(i, n_steps - 1), 0))
    return spec, jax.ShapeDtypeStruct(w.shape, BF16)


def _cast_blocks(in_refs, out_refs):
    for src, dst in zip(in_refs, out_refs):
        dst[...] = src[...].astype(BF16)


def _ada_tile(cctx_ref, c_ref, w_ref, b_ref, o_ref):
    dot = functools.partial(jnp.dot, preferred_element_type=F32)
    n_pad = o_ref.shape[0] - 1 - c_ref.shape[0]
    c = jnp.concatenate([cctx_ref[...], c_ref[...], jnp.zeros((n_pad, c_ref.shape[1]), F32)], axis=0)
    s = c * jax.nn.sigmoid(c)
    s_hi = s.astype(BF16)
    s_lo = (s - s_hi.astype(F32)).astype(BF16)
    w = w_ref[...]
    w_hi = w.astype(BF16)
    w_lo = (w - w_hi.astype(F32)).astype(BF16)
    r = dot(jnp.concatenate([s_hi, s_lo], axis=0), w_hi)
    n = s.shape[0]
    o_ref[...] = r[0:n] + r[n:2 * n] + dot(s_hi, w_lo) + b_ref[...]


ROW_LAM_RE, ROW_LAM_IM, ROW_STEP, ROW_B_RE = 0, 2, 4, 8
ROW_B_IM = ROW_B_RE + 2 * S5_GROUP
ROW_C_RE = ROW_B_IM + 2 * S5_GROUP
ROW_C_IM = ROW_C_RE + 2 * S5_GROUP
PARAM_ROWS = ROW_C_IM + 2 * S5_GROUP


def _param_rows(p_ref, row0, d, lanes=slice(None)):
    return p_ref[row0 + d * S5_GROUP:row0 + (d + 1) * S5_GROUP, lanes]


def _exact_bf16_terms(x):
    hi = x.astype(BF16)
    r = x - hi.astype(F32)
    mid = r.astype(BF16)
    return hi, mid, (r - mid.astype(F32)).astype(BF16)


def _s5_params_to_lanes(lam_re_ref, lam_im_ref, lstep_ref, b_re_ref, b_im_ref, c_re_ref, c_im_ref, d_ref, ind_ref,
                        p_scr, dt_scr):
    dot = functools.partial(jnp.dot, preferred_element_type=F32)
    p_scr[ROW_STEP + 2:ROW_B_RE, :] = jnp.zeros((ROW_B_RE - ROW_STEP - 2, GP), F32)
    p_scr[ROW_STEP:ROW_STEP + 2, :] = sum(dot(t, ind_ref[...]) for t in _exact_bf16_terms(lstep_ref[...]))
    for d in range(2):
        for row0, src in ((ROW_LAM_RE, lam_re_ref), (ROW_LAM_IM, lam_im_ref)):
            for q in range(PAIRS):
                p_scr[row0 + d:row0 + d + 1, q * LANES:(q + 1) * LANES] = jnp.concatenate(
                    [src[d, 2 * q:2 * q + 1, :], src[d, 2 * q + 1:2 * q + 2, :]], axis=1)
        for row0, src in ((ROW_B_RE, b_re_ref), (ROW_B_IM, b_im_ref), (ROW_C_RE, c_re_ref), (ROW_C_IM, c_im_ref)):
            for q in range(PAIRS):
                p_scr[row0 + d * S5_GROUP:row0 + (d + 1) * S5_GROUP, q * LANES:(q + 1) * LANES] = jnp.concatenate(
                    [src[d, 2 * q], src[d, 2 * q + 1]], axis=1)
    own = lax.broadcasted_iota(jnp.int32, (S5_GROUPS, S5_WIDTH), 1) // S5_GROUP \
        == lax.broadcasted_iota(jnp.int32, (S5_GROUPS, S5_WIDTH), 0)
    d_rows = jnp.where(own, d_ref[...], 0.0)
    spread = (lax.broadcasted_iota(jnp.int32, (S5_WIDTH, LANES), 0) % S5_GROUP
              == lax.broadcasted_iota(jnp.int32, (S5_WIDTH, LANES), 1) % S5_GROUP).astype(BF16)
    dt_scr[...] = sum(dot(t, spread) for t in _exact_bf16_terms(d_rows))


def _s5_discretise(p_ref, apre_ref, apim_ref, gb_scr, ca_scr):
    lre = p_ref[ROW_LAM_RE:ROW_LAM_RE + 2, :]
    lim = p_ref[ROW_LAM_IM:ROW_LAM_IM + 2, :]
    step = jnp.exp(p_ref[ROW_STEP:ROW_STEP + 2, :])
    mag = jnp.exp(lre * step)
    are = mag * jnp.cos(lim * step)
    aim = mag * jnp.sin(lim * step)
    nr = are - 1.0
    den = lre * lre + lim * lim
    fr = (nr * lre + aim * lim) / den
    fi = (aim * lre - nr * lim) / den

    pr = [jnp.ones_like(are)]
    pi = [jnp.zeros_like(are)]
    for _ in range(CHUNK):
        r, i = pr[-1], pi[-1]
        pr.append(r * are - i * aim)
        pi.append(r * aim + i * are)
    a16r, a16i = pr[CHUNK], pi[CHUNK]
    qr, qi = jnp.ones_like(are), jnp.zeros_like(are)
    for c in range(SEG_CHUNKS + 1):
        apre_ref[c] = qr
        apim_ref[c] = qi
        qr, qi = qr * a16r - qi * a16i, qr * a16i + qi * a16r

    for d in range(2):
        bre, bim = _param_rows(p_ref, ROW_B_RE, d), _param_rows(p_ref, ROW_B_IM, d)
        bbre = fr[d:d + 1] * bre - fi[d:d + 1] * bim
        bbim = fr[d:d + 1] * bim + fi[d:d + 1] * bre
        cre, cim = _param_rows(p_ref, ROW_C_RE, d), _param_rows(p_ref, ROW_C_IM, d)
        for s in range(CHUNK):
            k = CHUNK - 1 - s if d == 0 else s
            r, i = pr[k][d:d + 1], pi[k][d:d + 1]
            gb_scr[0, d, s] = r * bbre - i * bbim
            gb_scr[1, d, s] = r * bbim + i * bbre
            f = s + 1 if d == 0 else CHUNK - s
            r, i = pr[f][d:d + 1], pi[f][d:d + 1]
            ca_scr[0, d, s] = r * cre - i * cim
            ca_scr[1, d, s] = -(r * cim + i * cre)


def _s5_pair_operands(q, slot_q, p_ref, dt_ref, gb_scr, ca_scr, ts_scr, m_ref, w_ref, e_ref):
    dot_nt = lambda a, b: lax.dot_general(a, b, (((1,), (1,)), ((), ())), preferred_element_type=F32)
    lane_gi = lax.broadcasted_iota(jnp.int32, (CH, LANES), 1) // S5_STATE
    row_gi = lax.broadcasted_iota(jnp.int32, (LANES, CH), 0) // S5_STATE
    eye = (lax.broadcasted_iota(jnp.int32, (LANES, LANES), 0)
           == lax.broadcasted_iota(jnp.int32, (LANES, LANES), 1)).astype(BF16)
    slot = lax.broadcasted_iota(jnp.int32, (CH, LANES), 1) // S5_GROUP
    n_lag_rows = (2 * CHUNK - 1) * S5_GROUP
    lanes = pl.ds(pl.multiple_of(q * LANES, LANES), LANES)
    for d in range(2):
        for ri in range(2):
            col = slice((2 * d + ri) * LANES, (2 * d + ri + 1) * LANES)
            gb = gb_scr[ri, d, :, :, lanes].reshape(CH, LANES)
            ca = ca_scr[ri, d, :, :, lanes].reshape(CH, LANES).astype(BF16)
            ca_t = dot_nt(eye, ca)
            for gi in range(2):
                w_ref[slot_q, gi * CH:(gi + 1) * CH, col] = jnp.where(lane_gi == gi, gb, 0.0).astype(BF16)
                e_ref[slot_q, col, gi * CH:(gi + 1) * CH] = jnp.where(row_gi == gi, ca_t, 0.0).astype(BF16)
    for gi in range(2):
        g = 2 * q + gi
        lag = []
        for d in range(2):
            c_re = jnp.concatenate([_param_rows(p_ref, ROW_C_RE, d, lanes)] * OCT, axis=0)
            c_imn = jnp.concatenate([-_param_rows(p_ref, ROW_C_IM, d, lanes)] * OCT, axis=0)
            keep = lax.broadcasted_iota(jnp.int32, (LANES, LANES), 1) // S5_STATE == gi
            c_re = jnp.where(keep, c_re, 0.0).astype(BF16)
            c_imn = jnp.where(keep, c_imn, 0.0).astype(BF16)
            gre = gb_scr[0, d, :, :, lanes].reshape(CH, LANES).astype(BF16)
            gim = gb_scr[1, d, :, :, lanes].reshape(CH, LANES).astype(BF16)
            lag.append(dot_nt(gre, c_re) + dot_nt(gim, c_imn))
        zl = (CHUNK - 1) * S5_GROUP
        ts_scr[0:zl, :] = lag[0][0:zl]
        on_diag = (lax.broadcasted_iota(jnp.int32, (S5_GROUP, LANES), 1) % S5_GROUP
                   == lax.broadcasted_iota(jnp.int32, (S5_GROUP, LANES), 0))
        skip = jnp.where(on_diag, dt_ref[pl.ds(g, 1), :], 0.0)
        ts_scr[zl:zl + S5_GROUP, :] = lag[0][zl:] + lag[1][0:S5_GROUP] + skip
        ts_scr[zl + S5_GROUP:n_lag_rows, :] = lag[1][S5_GROUP:]
        for hf in range(CHUNK // OCT):
            acc = None
            for s in range(OCT):
                t = OCT * hf + s
                win = ts_scr[(CHUNK - 1 - t) * S5_GROUP:(CHUNK - 1 - t) * S5_GROUP + CH, :]
                acc = win if acc is None else jnp.where(slot == s, win, acc)
            m_ref[2 * slot_q + gi, :, hf * LANES:(hf + 1) * LANES] = acc.astype(BF16)


PAIRS_PER_STEP = 2


def _ada_tables_kernel(cctx_ref, c_ref, wada_ref, bada_ref, win_ref, lam_re_ref, lam_im_ref, lstep_ref, b_re_ref, b_im_ref,
                       c_re_ref, c_im_ref, d_ref, ind_ref, mod_ref, winb_ref, apre_ref, apim_ref, m_ref, w_ref, e_ref,
                       p_scr, dt_scr, gb_scr, ca_scr, ts_scr):
    j = pl.program_id(0)
    _cast_blocks([win_ref], [winb_ref])
    _ada_tile(cctx_ref, c_ref, wada_ref, bada_ref, mod_ref)

    @pl.when(j == 0)
    def _():
        _s5_params_to_lanes(lam_re_ref, lam_im_ref, lstep_ref, b_re_ref, b_im_ref, c_re_ref, c_im_ref, d_ref, ind_ref,
                            p_scr, dt_scr)
        _s5_discretise(p_scr, apre_ref, apim_ref, gb_scr, ca_scr)

    for lp in range(PAIRS_PER_STEP):
        _s5_pair_operands(PAIRS_PER_STEP * j + lp, lp, p_scr, dt_scr, gb_scr, ca_scr, ts_scr, m_ref, w_ref, e_ref)


def _adaln_and_s5_tables(c_ctx, c, w_ada, b_ada, w_in, lam_re, lam_im, log_step, b_re, b_im, c_re, c_im, s5_d):
    ind_np = np.zeros((S5_GROUPS, GP), np.float32)
    ind_np[np.arange(GP) // S5_STATE, np.arange(GP)] = 1.0
    s5_in = (lam_re, lam_im, log_step, jnp.swapaxes(b_re, -1, -2), jnp.swapaxes(b_im, -1, -2), c_re, c_im,
             s5_d.reshape(1, S5_WIDTH), jnp.asarray(ind_np).astype(BF16))

    n_steps = PAIRS // PAIRS_PER_STEP
    n_out = w_ada.shape[1]
    tn = n_out // n_steps
    cast_spec, cast_shape = _row_block_cast_specs(w_in, n_steps)
    whole = lambda a: pl.BlockSpec(a.shape, lambda j, nd=a.ndim: (0,) * nd)
    powers = jax.ShapeDtypeStruct((SEG_CHUNKS + 1, 2, GP), F32)
    tab = (2, 2, CHUNK, S5_GROUP, GP)
    mod, w_in_b, apre, apim, m, w, e = pl.pallas_call(
        _ada_tables_kernel,
        grid=(n_steps,),
        in_specs=[whole(c_ctx), whole(c),
                  pl.BlockSpec((D_MODEL, tn), lambda j: (0, j)),
                  pl.BlockSpec((1, tn), lambda j: (0, j)),
                  cast_spec] + [whole(a) for a in s5_in],
        out_specs=[pl.BlockSpec((8, tn), lambda j: (0, j)), cast_spec, whole(powers), whole(powers),
                   pl.BlockSpec((2 * PAIRS_PER_STEP, CH, CH), lambda j: (j, 0, 0)),
                   pl.BlockSpec((PAIRS_PER_STEP, 2 * CH, 4 * LANES), lambda j: (j, 0, 0)),
                   pl.BlockSpec((PAIRS_PER_STEP, 4 * LANES, 2 * CH), lambda j: (j, 0, 0))],
        out_shape=[jax.ShapeDtypeStruct((8, n_out), F32), cast_shape, powers, powers,
                   jax.ShapeDtypeStruct((S5_GROUPS, CH, CH), BF16),
                   jax.ShapeDtypeStruct((PAIRS, 2 * CH, 4 * LANES), BF16),
                   jax.ShapeDtypeStruct((PAIRS, 4 * LANES, 2 * CH), BF16)],
        scratch_shapes=[pltpu.VMEM((PARAM_ROWS, GP), F32),
                        pltpu.VMEM((S5_GROUPS, LANES), F32),
                        pltpu.VMEM(tab, F32),
                        pltpu.VMEM(tab, F32),
                        pltpu.VMEM(((2 * CHUNK - 1) * S5_GROUP, LANES), F32)],
        compiler_params=_cparams(("arbitrary",)),
        name="adaln_s5_tables",
    )(c_ctx, c, w_ada, b_ada.reshape(1, n_out), w_in, *s5_in)
    return mod, w_in_b, m, w, e, apre, apim


def _rms(x, g):
    return x * lax.rsqrt(jnp.mean(x * x, axis=-1, keepdims=True) + EPS) * g


def _slot_transpose(v):
    v = list(v)
    slot = lax.broadcasted_iota(jnp.int32, v[0].shape, 1) // S5_GROUP
    for k in (4, 2, 1):
        low = (slot & k) == 0
        for i in range(OCT):
            if i & k:
                continue
            a, b = v[i], v[i + k]
            v[i] = jnp.where(low, a, pltpu.roll(b, k * S5_GROUP, axis=1))
            v[i + k] = jnp.where(low, pltpu.roll(a, LANES - k * S5_GROUP, axis=1), b)
    return v


def _inproj_kernel(n_prompt_tiles, tiles_per_seq, n_cast, xp_ref, xs_ref, mod_ref, g_ref, win_ref, bdc_ref, bds_ref,
                   cm_ref, sm_ref, *rest):
    cast_in, rest = rest[:n_cast], rest[n_cast:]
    xg_ref, fc_ref, fs_ref, gs_ref, gf_ref, yfp_ref = rest[:6]
    cast_out, (h_scr, hb_scr, zs_scr) = rest[6:6 + n_cast], rest[6 + n_cast:]
    tm = xp_ref.shape[0]
    mod = _mod_vectors(mod_ref, n_prompt_tiles, tiles_per_seq)
    is_prompt = pl.program_id(0) < n_prompt_tiles
    dot = functools.partial(jnp.dot, preferred_element_type=F32)
    o = 2 * S5_WIDTH
    tiles = [slice(j * SEG, (j + 1) * SEG) for j in range(tm // SEG)]

    def norm_phase(j, r):
        x = jnp.where(is_prompt, xp_ref[r, :], xs_ref[r, :])
        h = _rms(x, g_ref[...]) * (1.0 + mod[1]) + mod[0]
        for k in range(D_MODEL // LANES):
            for c in range(SEG_CHUNKS):
                p0 = (j * SEG_CHUNKS + c) * PITCH
                h_scr[k, p0:p0 + CHUNK, :] = h[c * CHUNK:(c + 1) * CHUNK, k * LANES:(k + 1) * LANES]
        for s in range(CHUNK):
            r0 = j * SEG + s * SEG_CHUNKS
            for k in range(D_MODEL // LANES):
                hb_scr[r0:r0 + SEG_CHUNKS, k * LANES:(k + 1) * LANES] = (
                    h_scr[k, pl.ds(j * SEG_CHUNKS * PITCH + s, SEG_CHUNKS, stride=PITCH), :].astype(BF16))

    def s5_gate_phase(j, r):
        zs_scr[r, :] = dot(hb_scr[r, :], win_ref[:, 0:S5_WIDTH])
        gs_ref[r, :] = jax.nn.sigmoid(dot(hb_scr[r, :], win_ref[:, o:o + D_MODEL])).astype(BF16)

    def fold_phase(j, r):
        for b in range(S5_WIDTH // LANES):
            for hf in range(CHUNK // OCT):
                z = [zs_scr[j * SEG + s * SEG_CHUNKS:j * SEG + (s + 1) * SEG_CHUNKS, b * LANES:(b + 1) * LANES]
                     for s in range(OCT * hf, OCT * (hf + 1))]
                for i, xi in enumerate(_slot_transpose(z)):
                    xg_ref[OCT * b + i, j * SEG_CHUNKS:(j + 1) * SEG_CHUNKS,
                           hf * LANES:(hf + 1) * LANES] = xi.astype(BF16)

    def fourier_gate_phase(j, r):
        uf = dot(hb_scr[r, :], win_ref[:, S5_WIDTH:2 * S5_WIDTH]).astype(BF16)
        for n0 in range(0, FFT_WIDTH, MXU_DIM):
            cols = slice(n0, n0 + MXU_DIM)
            fc_ref[r, cols] = dot(uf[:, cols], bdc_ref[cols, cols]).astype(BF16)
            fs_ref[r, cols] = dot(uf[:, cols], bds_ref[cols, cols]).astype(BF16)
        gf_ref[r, :] = jax.nn.sigmoid(dot(hb_scr[r, :], win_ref[:, o + D_MODEL:o + 2 * D_MODEL])).astype(BF16)

    for phase in (norm_phase, s5_gate_phase, fold_phase, fourier_gate_phase):
        for j, r in enumerate(tiles):
            phase(j, r)
    _cast_blocks(cast_in, cast_out)

    @pl.when(is_prompt)
    def _():
        cm, sm = cm_ref[...].astype(BF16), sm_ref[...].astype(BF16)
        for r in tiles:
            yfp_ref[r, :] = (dot(cm, fc_ref[r, :]) - dot(sm, fs_ref[r, :])).astype(BF16)


def _channel_dft_mats():
    j = np.arange(FFT_GROUP)
    ang = 2.0 * np.pi * ((j[:, None] * j[None, :]) % FFT_GROUP) / FFT_GROUP
    blk_c = np.cos(ang) / math.sqrt(FFT_GROUP)
    blk_s = np.sin(ang) / math.sqrt(FFT_GROUP)
    bdc = np.kron(np.eye(FFT_GROUPS), blk_c)
    bds = np.kron(np.eye(FFT_GROUPS), blk_s)
    return jnp.asarray(bdc, F32).astype(BF16), jnp.asarray(bds, F32).astype(BF16)


def _const_spec(shape):
    nd = len(shape)
    return pl.BlockSpec(shape, lambda i: (0,) * nd, pipeline_mode=pl.Buffered(1))


def _two_part_specs(tm, width, n_prompt_tiles):
    return [pl.BlockSpec((tm, width), lambda i: (jnp.minimum(i, n_prompt_tiles - 1), 0)),
            pl.BlockSpec((tm, width), lambda i: (jnp.maximum(i - n_prompt_tiles, 0), 0))]


MOD_SPEC = pl.BlockSpec((8, N_MOD * D_MODEL), lambda i: (0, 0))


def _mod_vectors(mod_ref, n_prompt_tiles, tiles_per_seq):
    i = pl.program_id(0)
    row = jnp.where(i < n_prompt_tiles, 0, 1 + (i - n_prompt_tiles) // tiles_per_seq)
    full = mod_ref[pl.ds(row, 1), :]
    return [full[:, k * D_MODEL:(k + 1) * D_MODEL] for k in range(N_MOD)]


N_CAST_STEPS = 16


def _inproj(xp, xs, seq_len, mod, norm_g, w_in_b, later_weights, tm):
    t = xp.shape[0] + xs.shape[0]
    n_p = xp.shape[0] // tm
    assert t // tm >= N_CAST_STEPS
    bdc, bds = _channel_dft_mats()
    cm, sm = _tile_dft_tables(1.0 / math.sqrt(SEG))
    casts = [_row_block_cast_specs(w, N_CAST_STEPS) for w in later_weights]
    tok = lambda w: pl.BlockSpec((tm, w), lambda i: (i, 0))
    out = lambda w: jax.ShapeDtypeStruct((t, w), BF16)
    return pl.pallas_call(
        functools.partial(_inproj_kernel, n_p, seq_len // tm, len(casts)),
        grid=(t // tm,),
        in_specs=_two_part_specs(tm, D_MODEL, n_p) + [
                  MOD_SPEC,
                  _const_spec((1, D_MODEL)),
                  _const_spec(w_in_b.shape),
                  _const_spec(bdc.shape),
                  _const_spec(bds.shape), _const_spec(cm.shape), _const_spec(sm.shape)] + [c[0] for c in casts],
        out_specs=[pl.BlockSpec((S5_GROUPS, tm // CHUNK, CH), lambda i: (0, i, 0)),
                   tok(FFT_WIDTH), tok(FFT_WIDTH), tok(D_MODEL), tok(D_MODEL),
                   _two_part_specs(tm, FFT_WIDTH, n_p)[0]] + [c[0] for c in casts],
        out_shape=[jax.ShapeDtypeStruct((S5_GROUPS, t // CHUNK, CH), BF16),
                   out(FFT_WIDTH), out(FFT_WIDTH), out(D_MODEL), out(D_MODEL),
                   jax.ShapeDtypeStruct((xp.shape[0], FFT_WIDTH), BF16)] + [c[1] for c in casts],
        scratch_shapes=[pltpu.VMEM((D_MODEL // LANES, tm // CHUNK * PITCH, LANES), F32),
                        pltpu.VMEM((tm, D_MODEL), BF16),
                        pltpu.VMEM((tm, S5_WIDTH), F32)],
        compiler_params=_cparams(("arbitrary",)),
        name="inproj",
    )(xp, xs, mod, norm_g.reshape(1, D_MODEL), w_in_b, bdc, bds, cm, sm, *later_weights)


N_PROMPT_SEG = 16
N_SAMPLE_SEG = 32
N_SAMPLE_SEQ = 2
N_SEG = N_PROMPT_SEG + N_SAMPLE_SEG
ROWS = SEG_CHUNKS * N_SEG
ROWS_P = SEG_CHUNKS * N_PROMPT_SEG
OCT_PAIRS = OCT // 2
SEG_PITCH = 56


def _cmul_add(ar, ai, hr, hi, sr, si):
    return ar * hr - ai * hi + sr, ar * hi + ai * hr + si


def _s5_kernel(x_ref, m_ref, w_ref, e_ref, apre_ref, apim_ref, h0_ref,
               y_ref, fin_ref, s_scr, hin_scr, hinp_scr, f_scr, hs_scr, y_scr):
    dot = functools.partial(jnp.dot, preferred_element_type=F32)
    parts = ((0, ROWS_P), (ROWS_P, ROWS - ROWS_P))
    seg_rows = lambda c: pl.ds(c, N_SEG, stride=PITCH)
    blk = lambda c: pl.ds(c * SEG_PITCH, N_SEG)
    seq_rows = lambda j: pl.ds(j, N_SAMPLE_SEQ, stride=SEG_CHUNKS)
    lat = lambda c: pl.ds(c * SEG_PITCH + N_PROMPT_SEG, N_SAMPLE_SEG)
    zero = jnp.zeros((N_SEG, LANES), F32)
    octet = pl.program_id(0)

    def entry_state(k, pr):
        g0 = 2 * (octet * OCT_PAIRS + pr)
        rows = [jnp.concatenate([h0_ref[pl.ds(n * 4 * S5_GROUPS + k * S5_GROUPS + g0 + gi, 1), :] for gi in range(2)],
                                axis=1) for n in range(N_SAMPLE_SEQ)]
        return jnp.concatenate(rows, axis=0)

    def store_final(k, pr, v):
        for gi in range(2):
            fin_ref[:, k, 2 * pr + gi, :] = v[0:N_PROMPT_SEG, gi * S5_STATE:(gi + 1) * S5_STATE]

    for pr in range(OCT_PAIRS):
        ln = slice(pr * LANES, (pr + 1) * LANES)
        g0, g1 = 2 * pr, 2 * pr + 1
        for r0, nr in parts:
            s = (dot(x_ref[g0, r0:r0 + nr, :], w_ref[pr, 0:CH, :])
                 + dot(x_ref[g1, r0:r0 + nr, :], w_ref[pr, CH:2 * CH, :]))
            for k in range(4):
                for sg in range(nr // SEG_CHUNKS):
                    p0 = (r0 // SEG_CHUNKS + sg) * PITCH
                    s_scr[k, p0:p0 + SEG_CHUNKS, :] = s[sg * SEG_CHUNKS:(sg + 1) * SEG_CHUNKS, k * LANES:(k + 1) * LANES]

        ar, ai = apre_ref[1, 0:1, ln], apim_ref[1, 0:1, ln]
        hr, hi = zero, zero
        for c in range(SEG_CHUNKS):
            hin_scr[0, blk(c), :] = hr
            hin_scr[1, blk(c), :] = hi
            hr, hi = _cmul_add(ar, ai, hr, hi, s_scr[0, seg_rows(c), :], s_scr[1, seg_rows(c), :])
        store_final(0, pr, hr)
        store_final(1, pr, hi)
        f_scr[0] = hr[N_PROMPT_SEG:]
        f_scr[1] = hi[N_PROMPT_SEG:]
        br, bi = apre_ref[1, 1:2, ln], apim_ref[1, 1:2, ln]
        gr, gi = zero, zero
        for c in range(SEG_CHUNKS - 1, -1, -1):
            hin_scr[2, blk(c), :] = gr
            hin_scr[3, blk(c), :] = gi
            gr, gi = _cmul_add(br, bi, gr, gi, s_scr[2, seg_rows(c), :], s_scr[3, seg_rows(c), :])
        store_final(2, pr, gr)
        store_final(3, pr, gi)
        f_scr[2] = gr[N_PROMPT_SEG:]
        f_scr[3] = gi[N_PROMPT_SEG:]

        a2r, a2i = apre_ref[SEG_CHUNKS, 0:1, ln], apim_ref[SEG_CHUNKS, 0:1, ln]
        hr, hi = entry_state(0, pr), entry_state(1, pr)
        for j in range(SEG_CHUNKS):
            hs_scr[0, seq_rows(j), :] = hr
            hs_scr[1, seq_rows(j), :] = hi
            hr, hi = _cmul_add(a2r, a2i, hr, hi, f_scr[0, seq_rows(j), :], f_scr[1, seq_rows(j), :])
        b2r, b2i = apre_ref[SEG_CHUNKS, 1:2, ln], apim_ref[SEG_CHUNKS, 1:2, ln]
        gr, gi = entry_state(2, pr), entry_state(3, pr)
        for j in range(SEG_CHUNKS - 1, -1, -1):
            hs_scr[2, seq_rows(j), :] = gr
            hs_scr[3, seq_rows(j), :] = gi
            gr, gi = _cmul_add(b2r, b2i, gr, gi, f_scr[2, seq_rows(j), :], f_scr[3, seq_rows(j), :])
        for c in range(SEG_CHUNKS):
            p_r, p_i = apre_ref[c, 0:1, ln], apim_ref[c, 0:1, ln]
            hr, hi = _cmul_add(p_r, p_i, hs_scr[0], hs_scr[1], hin_scr[0, lat(c), :], hin_scr[1, lat(c), :])
            hin_scr[0, lat(c), :] = hr
            hin_scr[1, lat(c), :] = hi
            cb = SEG_CHUNKS - 1 - c
            p_r, p_i = apre_ref[cb, 1:2, ln], apim_ref[cb, 1:2, ln]
            gr, gi = _cmul_add(p_r, p_i, hs_scr[2], hs_scr[3], hin_scr[2, lat(c), :], hin_scr[3, lat(c), :])
            hin_scr[2, lat(c), :] = gr
            hin_scr[3, lat(c), :] = gi

        for k in range(4):
            for sg in range(N_SEG):
                hinp_scr[sg * SEG_CHUNKS:(sg + 1) * SEG_CHUNKS, k * LANES:(k + 1) * LANES] = (
                    hin_scr[k, pl.ds(sg, SEG_CHUNKS, stride=SEG_PITCH), :].astype(BF16))
        for gi_, g in ((0, g0), (1, g1)):
            for r0, nr in parts:
                y_scr[g, r0:r0 + nr, :] = (dot(x_ref[g, r0:r0 + nr, :], m_ref[g])
                                           + dot(hinp_scr[r0:r0 + nr, :], e_ref[pr, :, gi_ * CH:(gi_ + 1) * CH]))

    rb_rows = 2 * SEG_CHUNKS
    for rb in range(ROWS // rb_rows):
        for hf in range(CHUNK // OCT):
            v = [y_scr[i, rb * rb_rows:(rb + 1) * rb_rows, hf * LANES:(hf + 1) * LANES] for i in range(OCT)]
            for s, acc in enumerate(_slot_transpose(v)):
                t = OCT * hf + s
                for sg in range(rb_rows // SEG_CHUNKS):
                    seg = rb * (rb_rows // SEG_CHUNKS) + sg
                    r0 = seg * SEG + t * SEG_CHUNKS
                    y_ref[0, r0:r0 + SEG_CHUNKS, :] = acc[sg * SEG_CHUNKS:(sg + 1) * SEG_CHUNKS]


def _s5_conv(xg, m, w, e, apre, apim, h0):
    n_oct = S5_GROUPS // OCT
    assert xg.shape[1] == ROWS
    nt = ROWS * CHUNK
    return pl.pallas_call(
        _s5_kernel,
        grid=(n_oct,),
        in_specs=[pl.BlockSpec((OCT, ROWS, CH), lambda o: (o, 0, 0)),
                  pl.BlockSpec((OCT, CH, CH), lambda o: (o, 0, 0)),
                  pl.BlockSpec((OCT_PAIRS, 2 * CH, 4 * LANES), lambda o: (o, 0, 0)),
                  pl.BlockSpec((OCT_PAIRS, 4 * LANES, 2 * CH), lambda o: (o, 0, 0)),
                  pl.BlockSpec((SEG_CHUNKS + 1, 2, OCT_PAIRS * LANES), lambda o: (0, 0, o)),
                  pl.BlockSpec((SEG_CHUNKS + 1, 2, OCT_PAIRS * LANES), lambda o: (0, 0, o)),
                  pl.BlockSpec((N_SAMPLE_SEQ * 4 * S5_GROUPS, S5_STATE), lambda o: (0, 0))],
        out_specs=[pl.BlockSpec((1, nt, LANES), lambda o: (o, 0, 0)),
                   pl.BlockSpec((N_PROMPT_SEG, 4, OCT, S5_STATE), lambda o: (0, 0, o, 0))],
        out_shape=[jax.ShapeDtypeStruct((n_oct, nt, LANES), F32),
                   jax.ShapeDtypeStruct((N_PROMPT_SEG, 4, S5_GROUPS, S5_STATE), F32)],
        scratch_shapes=[pltpu.VMEM((4, N_SEG * PITCH, LANES), F32),
                        pltpu.VMEM((4, SEG_CHUNKS * SEG_PITCH, LANES), F32),
                        pltpu.VMEM((ROWS, 4 * LANES), BF16),
                        pltpu.VMEM((4, N_SAMPLE_SEG, LANES), F32),
                        pltpu.VMEM((4, N_SAMPLE_SEG, LANES), F32),
                        pltpu.VMEM((OCT, ROWS, CH), F32)],
        compiler_params=_cparams(("parallel",)),
        name="s5_conv",
    )(xg, m, w, e, apre, apim, h0)


def _tile_order_positions(l):
    r = np.arange(l)
    tile, wi = r // SEG, r % SEG
    return tile * SEG + (wi % SEG_CHUNKS) * CHUNK + wi // SEG_CHUNKS


def _tile_dft_tables(scale, rows_in_tile_order=True):
    pos = _tile_order_positions(SEG)
    freq = pos if rows_in_tile_order else np.arange(SEG)
    ang = 2.0 * np.pi * ((freq[:, None] * pos[None, :]) % SEG) / SEG
    return jnp.asarray(np.cos(ang) * scale, F32), jnp.asarray(np.sin(ang) * scale, F32)


N_TILES = 16


def _fft16(xr, xi):
    n = len(xr)
    rev = [int(format(i, "04b")[::-1], 2) for i in range(n)]
    ar = [xr[r] for r in rev]
    ai = [xi[r] for r in rev]
    size = 2
    while size <= n:
        half = size // 2
        for start in range(0, n, size):
            for k in range(half):
                wr = math.cos(2.0 * math.pi * k / size)
                wi = -math.sin(2.0 * math.pi * k / size)
                i0, i1 = start + k, start + k + half
                if k == 0:
                    tr, ti = ar[i1], ai[i1]
                elif 4 * k == size:
                    tr, ti = ai[i1], -ar[i1]
                else:
                    tr = ar[i1] * wr - ai[i1] * wi
                    ti = ar[i1] * wi + ai[i1] * wr
                ar[i1], ai[i1] = ar[i0] - tr, ai[i0] - ti
                ar[i0], ai[i0] = ar[i0] + tr, ai[i0] + ti
        size *= 2
    return ar, ai


def _pos_dft_long_kernel(cb_ref, sb_ref, ca_ref, sa_ref, fc_ref, fs_ref, o_ref, a_scr):
    wdt = fc_ref.shape[2]

    def rows_fft(r, carry):
        rows = pl.ds(pl.multiple_of(r * CHUNK, CHUNK), CHUNK)
        for b in range(wdt // LANES):
            lns = slice(b * LANES, (b + 1) * LANES)
            xr = [fc_ref[j, rows, lns].astype(F32) for j in range(N_TILES)]
            xi = [-fs_ref[j, rows, lns].astype(F32) for j in range(N_TILES)]
            ar, ai = _fft16(xr, xi)
            for k in range(N_TILES):
                a_scr[k, 0, rows, lns] = ar[k].astype(BF16)
                a_scr[k, 1, rows, lns] = ai[k].astype(BF16)
        return carry

    lax.fori_loop(0, SEG // CHUNK, rows_fft, 0)

    dot = functools.partial(jnp.dot, preferred_element_type=F32)
    cb, sb = cb_ref[...], sb_ref[...]
    for k1 in range(N_TILES):
        ca, sa = ca_ref[k1:k1 + 1, :], sa_ref[k1:k1 + 1, :]
        dc = (cb * ca - sb * sa).astype(BF16)
        ds = (sb * ca + cb * sa).astype(BF16)
        out = dot(dc, a_scr[k1, 0]) + dot(ds, a_scr[k1, 1])
        o_ref[:, k1 * SEG_CHUNKS:(k1 + 1) * SEG_CHUNKS, :] = out.astype(BF16).reshape(N_TILES, SEG_CHUNKS, wdt)


def _pos_dft_long(fc, fs, first, n):
    wdt = fc.shape[-1]
    l = N_TILES * SEG
    wb = 2 * LANES
    cb, sb = _tile_dft_tables(1.0 / math.sqrt(l), rows_in_tile_order=False)
    pos = _tile_order_positions(SEG)
    ang = 2.0 * np.pi * (np.arange(N_TILES)[:, None] * pos[None, :]) / l
    ca, sa = jnp.asarray(np.cos(ang), F32), jnp.asarray(np.sin(ang), F32)
    const2 = lambda shape: pl.BlockSpec(shape, lambda b, k: (0, 0), pipeline_mode=pl.Buffered(1))
    seq_in = pl.BlockSpec((None, N_TILES, SEG, wb), lambda b, k: (first + b, 0, 0, k))
    seq_out = pl.BlockSpec((None, N_TILES, SEG, wb), lambda b, k: (b, 0, 0, k))
    out = pl.pallas_call(
        _pos_dft_long_kernel,
        grid=(n, wdt // wb),
        in_specs=[const2((SEG, SEG)), const2((SEG, SEG)), const2((N_TILES, SEG)), const2((N_TILES, SEG)),
                  seq_in, seq_in],
        out_specs=seq_out,
        out_shape=jax.ShapeDtypeStruct((n, N_TILES, SEG, wdt), BF16),
        scratch_shapes=[pltpu.VMEM((N_TILES, 2, SEG, wb), BF16)],
        compiler_params=_cparams(("parallel", "parallel")),
        name="pos_dft_long",
    )(cb, sb, ca, sa, fc, fs)
    return out.reshape(n * l, wdt)


def _mix_ffn_kernel(n_prompt_tiles, tiles_per_seq, xp_ref, xs_ref, mod_ref, ys_ref, yfp_ref, yfs_ref, gs_ref, gf_ref,
                    wglu_ref, bglu_ref, wps_ref, wpf_ref, wout_ref, n2_ref,
                    wg_ref, wu_ref, wd_ref, fn_ref, op_ref, os_ref, m_scr, mb_scr):
    tm = xp_ref.shape[0]
    dot = functools.partial(jnp.dot, preferred_element_type=F32)
    mod = _mod_vectors(mod_ref, n_prompt_tiles, tiles_per_seq)
    is_prompt = pl.program_id(0) < n_prompt_tiles
    tiles = [slice(j * SEG, (j + 1) * SEG) for j in range(tm // SEG)]
    st = [dict() for _ in tiles]

    def gelu_phase(j, r, s):
        y = jnp.concatenate([ys_ref[b, r, :] for b in range(S5_WIDTH // LANES)], axis=1)
        s["z"] = jax.nn.gelu(y)

    def glu_phase(j, r, s):
        z = s["z"]
        s["z"] = (z * jax.nn.sigmoid(dot(z.astype(BF16), wglu_ref[...]) + bglu_ref[...])).astype(BF16)

    def proj_phase(j, r, s):
        yf = jnp.where(is_prompt, yfp_ref[r, :], yfs_ref[r, :])
        m = (gs_ref[r, :].astype(F32) * dot(s.pop("z"), wps_ref[...])
             + gf_ref[r, :].astype(F32) * dot(yf, wpf_ref[...]))
        for k in range(D_MODEL // LANES):
            for t in range(CHUNK):
                p0 = (j * CHUNK + t) * PITCH
                m_scr[k, p0:p0 + SEG_CHUNKS, :] = m[t * SEG_CHUNKS:(t + 1) * SEG_CHUNKS, k * LANES:(k + 1) * LANES]
        for c in range(SEG_CHUNKS):
            r0 = j * SEG + c * CHUNK
            for k in range(D_MODEL // LANES):
                mb_scr[r0:r0 + CHUNK, k * LANES:(k + 1) * LANES] = (
                    m_scr[k, pl.ds(j * CHUNK * PITCH + c, CHUNK, stride=PITCH), :].astype(BF16))

    def out_phase(j, r, s):
        x = jnp.where(is_prompt, xp_ref[r, :], xs_ref[r, :])
        x1 = x + mod[2] * dot(mb_scr[r, :], wout_ref[...])
        s["x1"] = x1
        s["h2"] = (_rms(x1, n2_ref[...]) * (1.0 + mod[4]) + mod[3]).astype(BF16)

    def ffn_phase(n0, n1, j, r, s):
        gate = dot(s["h2"], wg_ref[:, n0:n1])
        up = dot(s["h2"], wu_ref[:, n0:n1])
        part = dot((gate * jax.nn.sigmoid(gate) * up).astype(BF16), wd_ref[n0:n1, :])
        s["ff"] = part if "ff" not in s else s["ff"] + part

    def final_phase(j, r, s):
        x2 = s.pop("x1") + mod[5] * s.pop("ff")
        s["res"] = _rms(x2, fn_ref[...])

    phases = [gelu_phase, glu_phase, proj_phase, out_phase]
    phases += [functools.partial(ffn_phase, n0, n1) for n0, n1 in zip(FF_SPLITS[:-1], FF_SPLITS[1:])]
    phases += [final_phase]
    for phase in phases:
        for j, r in enumerate(tiles):
            phase(j, r, st[j])

    @pl.when(is_prompt)
    def _():
        for j, r in enumerate(tiles):
            op_ref[r, :] = st[j]["res"]

    @pl.when(jnp.logical_not(is_prompt))
    def _():
        for j, r in enumerate(tiles):
            os_ref[r, :] = st[j]["res"]


def _mix_ffn(xp, xs, seq_len, mod, ys, yf_p, yf_s, gs, gf, wts, tm):
    tp, ts = xp.shape[0], xs.shape[0]
    n_p = tp // tm
    tok = lambda w: pl.BlockSpec((tm, w), lambda i: (i, 0))
    return pl.pallas_call(
        functools.partial(_mix_ffn_kernel, n_p, seq_len // tm),
        grid=((tp + ts) // tm,),
        in_specs=_two_part_specs(tm, D_MODEL, n_p)
                 + [MOD_SPEC,
                    pl.BlockSpec((S5_WIDTH // LANES, tm, LANES), lambda i: (0, i, 0))]
                 + _two_part_specs(tm, FFT_WIDTH, n_p)
                 + [tok(D_MODEL), tok(D_MODEL)]
                 + [_const_spec(w.shape) for w in wts],
        out_specs=_two_part_specs(tm, D_MODEL, n_p),
        out_shape=[jax.ShapeDtypeStruct((tp, D_MODEL), F32), jax.ShapeDtypeStruct((ts, D_MODEL), F32)],
        scratch_shapes=[pltpu.VMEM((D_MODEL // LANES, tm // CHUNK * PITCH, LANES), F32),
                        pltpu.VMEM((tm, D_MODEL), BF16)],
        compiler_params=_cparams(("arbitrary",)),
        name="mix_ffn",
    )(xp, xs, mod, ys, yf_p, yf_s, gs, gf, *wts)


def kernel(x_prompt, x_sample, state_s5, c, c_ctx, norm1_g, norm2_g, w_ada, b_ada, w_in,
           s5_lambda_re, s5_lambda_im, s5_log_step, s5_b_re, s5_b_im, s5_c_re, s5_c_im,
           s5_d, w_glu, b_glu, w_proj_s5, w_proj_fft, w_out, w_ffn_gate, w_ffn_up,
           w_ffn_down, final_norm_g):
    nb, sl, _ = x_prompt.shape
    db, dl, _ = x_sample.shape
    assert w_in.shape[0] == 1 and sl == SEG and nb == N_PROMPT_SEG
    assert db == N_SAMPLE_SEQ and dl == SEG * SEG_CHUNKS

    mod, w_in_b, m, w, e, apre, apim = _adaln_and_s5_tables(
        c_ctx[None], c, w_ada[0], b_ada[0], w_in[0], s5_lambda_re[0], s5_lambda_im[0], s5_log_step[0], s5_b_re[0],
        s5_b_im[0], s5_c_re[0], s5_c_im[0], s5_d)

    tm = 512
    xp = x_prompt.reshape(nb * sl, D_MODEL)
    xs = x_sample.reshape(db * dl, D_MODEL)
    later = (w_glu[0], w_proj_s5[0], w_proj_fft[0], w_out[0], w_ffn_gate[0], w_ffn_up[0], w_ffn_down[0])
    xg, fc, fs, gs, gf, yf_p, *later_b = _inproj(xp, xs, dl, mod, norm1_g[0], w_in_b, later, tm)
    wglu_b, wps_b, wpf_b, wout_b, wg_b, wu_b, wd_b = later_b

    ys, fin = _s5_conv(xg, m, w, e, apre, apim, state_s5.astype(F32).reshape(db * 4 * S5_GROUPS, S5_STATE))
    new_state = fin.reshape(nb, 1, 2, 2, S5_GROUPS, S5_STATE)

    n_prompt_units = nb // N_TILES
    yf_s = _pos_dft_long(fc.reshape(-1, N_TILES, SEG, FFT_WIDTH), fs.reshape(-1, N_TILES, SEG, FFT_WIDTH),
                         n_prompt_units, db)

    wts = (wglu_b, b_glu[0].reshape(1, S5_WIDTH), wps_b, wpf_b, wout_b, norm2_g[0].reshape(1, D_MODEL),
           wg_b, wu_b, wd_b, final_norm_g.reshape(1, D_MODEL))
    tm2 = 2 * SEG
    y_p, y_s = _mix_ffn(xp, xs, dl, mod, ys, yf_p.reshape(nb * sl, FFT_WIDTH), yf_s, gs, gf, wts, tm2)
    return (y_p.reshape(nb, sl, D_MODEL), y_s.reshape(db, dl, D_MODEL), new_state)
```

```python
import functools
import math

import numpy as np
import jax
import jax.numpy as jnp
from jax import lax
from jax.experimental import pallas as pl
from jax.experimental.pallas import tpu as pltpu

F32 = jnp.float32
BF16 = jnp.bfloat16

D_MODEL = 1024
S5_WIDTH = 512
S5_GROUPS = 32
S5_GROUP = 16
S5_STATE = 64
FFT_WIDTH = 512
FFT_GROUPS = 4
FFT_GROUP = 128
D_FF = 2816
N_MOD = 6
EPS = 1e-6

LANES = 128
CHUNK = 16
SEG_CHUNKS = 16
SEG = CHUNK * SEG_CHUNKS
GP = S5_GROUPS * S5_STATE
PAIRS = S5_GROUPS // 2
CH = CHUNK * S5_GROUP
OCT = LANES // S5_GROUP
PITCH = 24

MXU_DIM = 256
FF_SPLITS = (0, 3 * MXU_DIM, 6 * MXU_DIM, 9 * MXU_DIM, D_FF)
VMEM_LIMIT = 58 * 1024 * 1024


def _cparams(sem):
    return pltpu.CompilerParams(dimension_semantics=sem, vmem_limit_bytes=VMEM_LIMIT)


def _row_block_cast_specs(w, n_steps, first_step=0):
    rows = w.shape[0] // n_steps
    assert rows * n_steps == w.shape[0] and rows % 16 == 0, (w.shape, n_steps)
    spec = pl.BlockSpec((rows, w.shape[1]), lambda i: (jnp.clip(i - first_step, 0, n_steps - 1), 0))
    return spec, jax.ShapeDtypeStruct(w.shape, BF16)


def _cast_blocks(in_refs, out_refs):
    for src, dst in zip(in_refs, out_refs):
        dst[...] = src[...].astype(BF16)


def _ada_tile(cctx_ref, c_ref, w_ref, b_ref, o_ref):
    dot = functools.partial(jnp.dot, preferred_element_type=F32)
    n_pad = o_ref.shape[0] - 1 - c_ref.shape[0]
    c = jnp.concatenate([cctx_ref[...], c_ref[...], jnp.zeros((n_pad, c_ref.shape[1]), F32)], axis=0)
    s = c * jax.nn.sigmoid(c)
    s_hi = s.astype(BF16)
    s_lo = (s - s_hi.astype(F32)).astype(BF16)
    w = w_ref[...]
    w_hi = w.astype(BF16)
    w_lo = (w - w_hi.astype(F32)).astype(BF16)
    r = dot(jnp.concatenate([s_hi, s_lo], axis=0), w_hi)
    n = s.shape[0]
    o_ref[...] = r[0:n] + r[n:2 * n] + dot(s_hi, w_lo) + b_ref[...]


ROW_LAM_RE, ROW_LAM_IM, ROW_STEP, ROW_B_RE = 0, 2, 4, 8
ROW_B_IM = ROW_B_RE + 2 * S5_GROUP
ROW_C_RE = ROW_B_IM + 2 * S5_GROUP
ROW_C_IM = ROW_C_RE + 2 * S5_GROUP
PARAM_ROWS = ROW_C_IM + 2 * S5_GROUP


def _param_rows(p_ref, row0, d, lanes=slice(None)):
    return p_ref[row0 + d * S5_GROUP:row0 + (d + 1) * S5_GROUP, lanes]


def _exact_bf16_terms(x):
    hi = x.astype(BF16)
    r = x - hi.astype(F32)
    mid = r.astype(BF16)
    return hi, mid, (r - mid.astype(F32)).astype(BF16)


def _s5_params_to_lanes(lam_re_ref, lam_im_ref, lstep_ref, b_re_ref, b_im_ref, c_re_ref, c_im_ref, d_ref, ind_ref,
                        p_scr, dt_scr):
    dot = functools.partial(jnp.dot, preferred_element_type=F32)
    p_scr[ROW_STEP + 2:ROW_B_RE, :] = jnp.zeros((ROW_B_RE - ROW_STEP - 2, GP), F32)
    p_scr[ROW_STEP:ROW_STEP + 2, :] = sum(dot(t, ind_ref[...]) for t in _exact_bf16_terms(lstep_ref[...]))
    for d in range(2):
        for row0, src in ((ROW_LAM_RE, lam_re_ref), (ROW_LAM_IM, lam_im_ref)):
            for q in range(PAIRS):
                p_scr[row0 + d:row0 + d + 1, q * LANES:(q + 1) * LANES] = jnp.concatenate(
                    [src[d, 2 * q:2 * q + 1, :], src[d, 2 * q + 1:2 * q + 2, :]], axis=1)
        for row0, src in ((ROW_B_RE, b_re_ref), (ROW_B_IM, b_im_ref), (ROW_C_RE, c_re_ref), (ROW_C_IM, c_im_ref)):
            for q in range(PAIRS):
                p_scr[row0 + d * S5_GROUP:row0 + (d + 1) * S5_GROUP, q * LANES:(q + 1) * LANES] = jnp.concatenate(
                    [src[d, 2 * q], src[d, 2 * q + 1]], axis=1)
    own = lax.broadcasted_iota(jnp.int32, (S5_GROUPS, S5_WIDTH), 1) // S5_GROUP \
        == lax.broadcasted_iota(jnp.int32, (S5_GROUPS, S5_WIDTH), 0)
    d_rows = jnp.where(own, d_ref[...], 0.0)
    spread = (lax.broadcasted_iota(jnp.int32, (S5_WIDTH, LANES), 0) % S5_GROUP
              == lax.broadcasted_iota(jnp.int32, (S5_WIDTH, LANES), 1) % S5_GROUP).astype(BF16)
    dt_scr[...] = sum(dot(t, spread) for t in _exact_bf16_terms(d_rows))


def _s5_discretise(p_ref, apre_ref, apim_ref, gb_scr, ca_scr):
    lre = p_ref[ROW_LAM_RE:ROW_LAM_RE + 2, :]
    lim = p_ref[ROW_LAM_IM:ROW_LAM_IM + 2, :]
    step = jnp.exp(p_ref[ROW_STEP:ROW_STEP + 2, :])
    mag = jnp.exp(lre * step)
    are = mag * jnp.cos(lim * step)
    aim = mag * jnp.sin(lim * step)
    nr = are - 1.0
    den = lre * lre + lim * lim
    fr = (nr * lre + aim * lim) / den
    fi = (aim * lre - nr * lim) / den

    pr = [jnp.ones_like(are)]
    pi = [jnp.zeros_like(are)]
    for _ in range(CHUNK):
        r, i = pr[-1], pi[-1]
        pr.append(r * are - i * aim)
        pi.append(r * aim + i * are)
    a16r, a16i = pr[CHUNK], pi[CHUNK]
    qr, qi = jnp.ones_like(are), jnp.zeros_like(are)
    for c in range(SEG_CHUNKS + 1):
        apre_ref[c] = qr
        apim_ref[c] = qi
        qr, qi = qr * a16r - qi * a16i, qr * a16i + qi * a16r

    for d in range(2):
        bre, bim = _param_rows(p_ref, ROW_B_RE, d), _param_rows(p_ref, ROW_B_IM, d)
        bbre = fr[d:d + 1] * bre - fi[d:d + 1] * bim
        bbim = fr[d:d + 1] * bim + fi[d:d + 1] * bre
        cre, cim = _param_rows(p_ref, ROW_C_RE, d), _param_rows(p_ref, ROW_C_IM, d)
        for s in range(CHUNK):
            k = CHUNK - 1 - s if d == 0 else s
            r, i = pr[k][d:d + 1], pi[k][d:d + 1]
            gb_scr[0, d, s] = r * bbre - i * bbim
            gb_scr[1, d, s] = r * bbim + i * bbre
            f = s + 1 if d == 0 else CHUNK - s
            r, i = pr[f][d:d + 1], pi[f][d:d + 1]
            ca_scr[0, d, s] = r * cre - i * cim
            ca_scr[1, d, s] = -(r * cim + i * cre)


def _s5_pair_operands(q, slot_q, p_ref, dt_ref, gb_scr, ca_scr, ts_scr, m_ref, w_ref, e_ref):
    dot_nt = lambda a, b: lax.dot_general(a, b, (((1,), (1,)), ((), ())), preferred_element_type=F32)
    lane_gi = lax.broadcasted_iota(jnp.int32, (CH, LANES), 1) // S5_STATE
    row_gi = lax.broadcasted_iota(jnp.int32, (LANES, CH), 0) // S5_STATE
    eye = (lax.broadcasted_iota(jnp.int32, (LANES, LANES), 0)
           == lax.broadcasted_iota(jnp.int32, (LANES, LANES), 1)).astype(BF16)
    slot = lax.broadcasted_iota(jnp.int32, (CH, LANES), 1) // S5_GROUP
    n_lag_rows = (2 * CHUNK - 1) * S5_GROUP
    lanes = pl.ds(pl.multiple_of(q * LANES, LANES), LANES)
    for d in range(2):
        for ri in range(2):
            col = slice((2 * d + ri) * LANES, (2 * d + ri + 1) * LANES)
            gb = gb_scr[ri, d, :, :, lanes].reshape(CH, LANES)
            ca = ca_scr[ri, d, :, :, lanes].reshape(CH, LANES).astype(BF16)
            ca_t = dot_nt(eye, ca)
            for gi in range(2):
                w_ref[slot_q, gi * CH:(gi + 1) * CH, col] = jnp.where(lane_gi == gi, gb, 0.0).astype(BF16)
                e_ref[slot_q, col, gi * CH:(gi + 1) * CH] = jnp.where(row_gi == gi, ca_t, 0.0).astype(BF16)
    for gi in range(2):
        g = 2 * q + gi
        lag = []
        for d in range(2):
            c_re = jnp.concatenate([_param_rows(p_ref, ROW_C_RE, d, lanes)] * OCT, axis=0)
            c_imn = jnp.concatenate([-_param_rows(p_ref, ROW_C_IM, d, lanes)] * OCT, axis=0)
            keep = lax.broadcasted_iota(jnp.int32, (LANES, LANES), 1) // S5_STATE == gi
            c_re = jnp.where(keep, c_re, 0.0).astype(BF16)
            c_imn = jnp.where(keep, c_imn, 0.0).astype(BF16)
            gre = gb_scr[0, d, :, :, lanes].reshape(CH, LANES).astype(BF16)
            gim = gb_scr[1, d, :, :, lanes].reshape(CH, LANES).astype(BF16)
            lag.append(dot_nt(gre, c_re) + dot_nt(gim, c_imn))
        zl = (CHUNK - 1) * S5_GROUP
        ts_scr[0:zl, :] = lag[0][0:zl]
        on_diag = (lax.broadcasted_iota(jnp.int32, (S5_GROUP, LANES), 1) % S5_GROUP
                   == lax.broadcasted_iota(jnp.int32, (S5_GROUP, LANES), 0))
        skip = jnp.where(on_diag, dt_ref[pl.ds(g, 1), :], 0.0)
        ts_scr[zl:zl + S5_GROUP, :] = lag[0][zl:] + lag[1][0:S5_GROUP] + skip
        ts_scr[zl + S5_GROUP:n_lag_rows, :] = lag[1][S5_GROUP:]
        for hf in range(CHUNK // OCT):
            acc = None
            for s in range(OCT):
                t = OCT * hf + s
                win = ts_scr[(CHUNK - 1 - t) * S5_GROUP:(CHUNK - 1 - t) * S5_GROUP + CH, :]
                acc = win if acc is None else jnp.where(slot == s, win, acc)
            m_ref[2 * slot_q + gi, :, hf * LANES:(hf + 1) * LANES] = acc.astype(BF16)


PAIRS_PER_STEP = 2


def _ada_tables_kernel(cctx_ref, c_ref, wada_ref, bada_ref, win_ref, lam_re_ref, lam_im_ref, lstep_ref, b_re_ref, b_im_ref,
                       c_re_ref, c_im_ref, d_ref, ind_ref, mod_ref, winb_ref, apre_ref, apim_ref, m_ref, w_ref, e_ref,
                       p_scr, dt_scr, gb_scr, ca_scr, ts_scr):
    j = pl.program_id(0)
    _cast_blocks([win_ref], [winb_ref])
    _ada_tile(cctx_ref, c_ref, wada_ref, bada_ref, mod_ref)

    @pl.when(j == 0)
    def _():
        _s5_params_to_lanes(lam_re_ref, lam_im_ref, lstep_ref, b_re_ref, b_im_ref, c_re_ref, c_im_ref, d_ref, ind_ref,
                            p_scr, dt_scr)
        _s5_discretise(p_scr, apre_ref, apim_ref, gb_scr, ca_scr)

    for lp in range(PAIRS_PER_STEP):
        _s5_pair_operands(PAIRS_PER_STEP * j + lp, lp, p_scr, dt_scr, gb_scr, ca_scr, ts_scr, m_ref, w_ref, e_ref)


def _adaln_and_s5_tables(c_ctx, c, w_ada, b_ada, w_in, lam_re, lam_im, log_step, b_re, b_im, c_re, c_im, s5_d):
    ind_np = np.zeros((S5_GROUPS, GP), np.float32)
    ind_np[np.arange(GP) // S5_STATE, np.arange(GP)] = 1.0
    s5_in = (lam_re, lam_im, log_step, jnp.swapaxes(b_re, -1, -2), jnp.swapaxes(b_im, -1, -2), c_re, c_im,
             s5_d.reshape(1, S5_WIDTH), jnp.asarray(ind_np).astype(BF16))

    n_steps = PAIRS // PAIRS_PER_STEP
    n_out = w_ada.shape[1]
    tn = n_out // n_steps
    cast_spec, cast_shape = _row_block_cast_specs(w_in, n_steps)
    whole = lambda a: pl.BlockSpec(a.shape, lambda j, nd=a.ndim: (0,) * nd)
    powers = jax.ShapeDtypeStruct((SEG_CHUNKS + 1, 2, GP), F32)
    tab = (2, 2, CHUNK, S5_GROUP, GP)
    mod, w_in_b, apre, apim, m, w, e = pl.pallas_call(
        _ada_tables_kernel,
        grid=(n_steps,),
        in_specs=[whole(c_ctx), whole(c),
                  pl.BlockSpec((D_MODEL, tn), lambda j: (0, j)),
                  pl.BlockSpec((1, tn), lambda j: (0, j)),
                  cast_spec] + [whole(a) for a in s5_in],
        out_specs=[pl.BlockSpec((8, tn), lambda j: (0, j)), cast_spec, whole(powers), whole(powers),
                   pl.BlockSpec((2 * PAIRS_PER_STEP, CH, CH), lambda j: (j, 0, 0)),
                   pl.BlockSpec((PAIRS_PER_STEP, 2 * CH, 4 * LANES), lambda j: (j, 0, 0)),
                   pl.BlockSpec((PAIRS_PER_STEP, 4 * LANES, 2 * CH), lambda j: (j, 0, 0))],
        out_shape=[jax.ShapeDtypeStruct((8, n_out), F32), cast_shape, powers, powers,
                   jax.ShapeDtypeStruct((S5_GROUPS, CH, CH), BF16),
                   jax.ShapeDtypeStruct((PAIRS, 2 * CH, 4 * LANES), BF16),
                   jax.ShapeDtypeStruct((PAIRS, 4 * LANES, 2 * CH), BF16)],
        scratch_shapes=[pltpu.VMEM((PARAM_ROWS, GP), F32),
                        pltpu.VMEM((S5_GROUPS, LANES), F32),
                        pltpu.VMEM(tab, F32),
                        pltpu.VMEM(tab, F32),
                        pltpu.VMEM(((2 * CHUNK - 1) * S5_GROUP, LANES), F32)],
        compiler_params=_cparams(("arbitrary",)),
        name="adaln_s5_tables",
    )(c_ctx, c, w_ada, b_ada.reshape(1, n_out), w_in, *s5_in)
    return mod, w_in_b, m, w, e, apre, apim


def _rms(x, g):
    return x * lax.rsqrt(jnp.mean(x * x, axis=-1, keepdims=True) + EPS) * g


def _slot_transpose(v):
    v = list(v)
    slot = lax.broadcasted_iota(jnp.int32, v[0].shape, 1) // S5_GROUP
    for k in (4, 2, 1):
        low = (slot & k) == 0
        for i in range(OCT):
            if i & k:
                continue
            a, b = v[i], v[i + k]
            v[i] = jnp.where(low, a, pltpu.roll(b, k * S5_GROUP, axis=1))
            v[i + k] = jnp.where(low, pltpu.roll(a, LANES - k * S5_GROUP, axis=1), b)
    return v


def _inproj_kernel(n_prompt_tiles, tiles_per_seq, n_cast, xp_ref, xs_ref, mod_ref, g_ref, win_ref, bdc_ref, bds_ref,
                   cm_ref, sm_ref, *rest):
    cast_in, rest = rest[:n_cast], rest[n_cast:]
    xg_ref, fc_ref, fs_ref, gs_ref, gf_ref, yfp_ref = rest[:6]
    cast_out, (h_scr, hb_scr, zs_scr) = rest[6:6 + n_cast], rest[6 + n_cast:]
    tm = xp_ref.shape[0]
    mod = _mod_vectors(mod_ref, n_prompt_tiles, tiles_per_seq)
    is_prompt = pl.program_id(0) < n_prompt_tiles
    dot = functools.partial(jnp.dot, preferred_element_type=F32)
    o = 2 * S5_WIDTH
    tiles = [slice(j * SEG, (j + 1) * SEG) for j in range(tm // SEG)]

    def norm_phase(j, r):
        x = jnp.where(is_prompt, xp_ref[r, :], xs_ref[r, :])
        h = _rms(x, g_ref[...]) * (1.0 + mod[1]) + mod[0]
        for k in range(D_MODEL // LANES):
            for c in range(SEG_CHUNKS):
                p0 = (j * SEG_CHUNKS + c) * PITCH
                h_scr[k, p0:p0 + CHUNK, :] = h[c * CHUNK:(c + 1) * CHUNK, k * LANES:(k + 1) * LANES]
        for s in range(CHUNK):
            r0 = j * SEG + s * SEG_CHUNKS
            for k in range(D_MODEL // LANES):
                hb_scr[r0:r0 + SEG_CHUNKS, k * LANES:(k + 1) * LANES] = (
                    h_scr[k, pl.ds(j * SEG_CHUNKS * PITCH + s, SEG_CHUNKS, stride=PITCH), :].astype(BF16))

    def s5_gate_phase(j, r):
        zs_scr[r, :] = dot(hb_scr[r, :], win_ref[:, 0:S5_WIDTH])
        gs_ref[r, :] = jax.nn.sigmoid(dot(hb_scr[r, :], win_ref[:, o:o + D_MODEL])).astype(BF16)

    def fold_phase(j, r):
        for b in range(S5_WIDTH // LANES):
            for hf in range(CHUNK // OCT):
                z = [zs_scr[j * SEG + s * SEG_CHUNKS:j * SEG + (s + 1) * SEG_CHUNKS, b * LANES:(b + 1) * LANES]
                     for s in range(OCT * hf, OCT * (hf + 1))]
                for i, xi in enumerate(_slot_transpose(z)):
                    xg_ref[OCT * b + i, j * SEG_CHUNKS:(j + 1) * SEG_CHUNKS,
                           hf * LANES:(hf + 1) * LANES] = xi.astype(BF16)

    def fourier_gate_phase(j, r):
        uf = dot(hb_scr[r, :], win_ref[:, S5_WIDTH:2 * S5_WIDTH]).astype(BF16)
        for n0 in range(0, FFT_WIDTH, MXU_DIM):
            cols = slice(n0, n0 + MXU_DIM)
            fc_ref[r, cols] = dot(uf[:, cols], bdc_ref[cols, cols]).astype(BF16)
            fs_ref[r, cols] = dot(uf[:, cols], bds_ref[cols, cols]).astype(BF16)
        gf_ref[r, :] = jax.nn.sigmoid(dot(hb_scr[r, :], win_ref[:, o + D_MODEL:o + 2 * D_MODEL])).astype(BF16)

    for phase in (norm_phase, s5_gate_phase, fold_phase, fourier_gate_phase):
        for j, r in enumerate(tiles):
            phase(j, r)
    _cast_blocks(cast_in, cast_out)

    @pl.when(is_prompt)
    def _():
        cm, sm = cm_ref[...].astype(BF16), sm_ref[...].astype(BF16)
        for r in tiles:
            yfp_ref[r, :] = (dot(cm, fc_ref[r, :]) - dot(sm, fs_ref[r, :])).astype(BF16)


def _channel_dft_mats():
    j = np.arange(FFT_GROUP)
    ang = 2.0 * np.pi * ((j[:, None] * j[None, :]) % FFT_GROUP) / FFT_GROUP
    blk_c = np.cos(ang) / math.sqrt(FFT_GROUP)
    blk_s = np.sin(ang) / math.sqrt(FFT_GROUP)
    bdc = np.kron(np.eye(FFT_GROUPS), blk_c)
    bds = np.kron(np.eye(FFT_GROUPS), blk_s)
    return jnp.asarray(bdc, F32).astype(BF16), jnp.asarray(bds, F32).astype(BF16)


def _const_spec(shape):
    nd = len(shape)
    return pl.BlockSpec(shape, lambda i: (0,) * nd, pipeline_mode=pl.Buffered(1))


def _two_part_specs(tm, width, n_prompt_tiles):
    return [pl.BlockSpec((tm, width), lambda i: (jnp.minimum(i, n_prompt_tiles - 1), 0)),
            pl.BlockSpec((tm, width), lambda i: (jnp.maximum(i - n_prompt_tiles, 0), 0))]


MOD_SPEC = pl.BlockSpec((8, N_MOD * D_MODEL), lambda i: (0, 0))


def _mod_vectors(mod_ref, n_prompt_tiles, tiles_per_seq):
    i = pl.program_id(0)
    row = jnp.where(i < n_prompt_tiles, 0, 1 + (i - n_prompt_tiles) // tiles_per_seq)
    full = mod_ref[pl.ds(row, 1), :]
    return [full[:, k * D_MODEL:(k + 1) * D_MODEL] for k in range(N_MOD)]


N_CAST_STEPS = 16


def _inproj(xp, xs, seq_len, mod, norm_g, w_in_b, later_weights, tm):
    t = xp.shape[0] + xs.shape[0]
    n_p = xp.shape[0] // tm
    assert t // tm >= N_CAST_STEPS
    bdc, bds = _channel_dft_mats()
    cm, sm = _tile_dft_tables(1.0 / math.sqrt(SEG))
    spare = t // tm - N_CAST_STEPS
    casts = [_row_block_cast_specs(w, N_CAST_STEPS, (k * spare) // max(len(later_weights) - 1, 1))
             for k, w in enumerate(later_weights)]
    tok = lambda w: pl.BlockSpec((tm, w), lambda i: (i, 0))
    out = lambda w: jax.ShapeDtypeStruct((t, w), BF16)
    return pl.pallas_call(
        functools.partial(_inproj_kernel, n_p, seq_len // tm, len(casts)),
        grid=(t // tm,),
        in_specs=_two_part_specs(tm, D_MODEL, n_p) + [
                  MOD_SPEC,
                  _const_spec((1, D_MODEL)),
                  _const_spec(w_in_b.shape),
                  _const_spec(bdc.shape),
                  _const_spec(bds.shape), _const_spec(cm.shape), _const_spec(sm.shape)] + [c[0] for c in casts],
        out_specs=[pl.BlockSpec((S5_GROUPS, tm // CHUNK, CH), lambda i: (0, i, 0)),
                   tok(FFT_WIDTH), tok(FFT_WIDTH), tok(D_MODEL), tok(D_MODEL),
                   _two_part_specs(tm, FFT_WIDTH, n_p)[0]] + [c[0] for c in casts],
        out_shape=[jax.ShapeDtypeStruct((S5_GROUPS, t // CHUNK, CH), BF16),
                   out(FFT_WIDTH), out(FFT_WIDTH), out(D_MODEL), out(D_MODEL),
                   jax.ShapeDtypeStruct((xp.shape[0], FFT_WIDTH), BF16)] + [c[1] for c in casts],
        scratch_shapes=[pltpu.VMEM((D_MODEL // LANES, tm // CHUNK * PITCH, LANES), F32),
                        pltpu.VMEM((tm, D_MODEL), BF16),
                        pltpu.VMEM((tm, S5_WIDTH), F32)],
        compiler_params=_cparams(("arbitrary",)),
        name="inproj",
    )(xp, xs, mod, norm_g.reshape(1, D_MODEL), w_in_b, bdc, bds, cm, sm, *later_weights)


N_PROMPT_SEG = 16
N_SAMPLE_SEG = 32
N_SAMPLE_SEQ = 2
N_SEG = N_PROMPT_SEG + N_SAMPLE_SEG
ROWS = SEG_CHUNKS * N_SEG
ROWS_P = SEG_CHUNKS * N_PROMPT_SEG
OCT_PAIRS = OCT // 2
SEG_PITCH = 56


def _cmul_add(ar, ai, hr, hi, sr, si):
    return ar * hr - ai * hi + sr, ar * hi + ai * hr + si


def _s5_kernel(x_ref, m_ref, w_ref, e_ref, apre_ref, apim_ref, h0_ref,
               y_ref, fin_ref, s_scr, hin_scr, hinp_scr, f_scr, hs_scr, y_scr):
    dot = functools.partial(jnp.dot, preferred_element_type=F32)
    parts = ((0, ROWS_P), (ROWS_P, ROWS - ROWS_P))
    seg_rows = lambda c: pl.ds(c, N_SEG, stride=PITCH)
    blk = lambda c: pl.ds(c * SEG_PITCH, N_SEG)
    seq_rows = lambda j: pl.ds(j, N_SAMPLE_SEQ, stride=SEG_CHUNKS)
    lat = lambda c: pl.ds(c * SEG_PITCH + N_PROMPT_SEG, N_SAMPLE_SEG)
    zero = jnp.zeros((N_SEG, LANES), F32)
    octet = pl.program_id(0)

    def entry_state(k, pr):
        g0 = 2 * (octet * OCT_PAIRS + pr)
        rows = [jnp.concatenate([h0_ref[pl.ds(n * 4 * S5_GROUPS + k * S5_GROUPS + g0 + gi, 1), :] for gi in range(2)],
                                axis=1) for n in range(N_SAMPLE_SEQ)]
        return jnp.concatenate(rows, axis=0)

    def store_final(k, pr, v):
        for gi in range(2):
            fin_ref[:, k, 2 * pr + gi, :] = v[0:N_PROMPT_SEG, gi * S5_STATE:(gi + 1) * S5_STATE]

    for pr in range(OCT_PAIRS):
        ln = slice(pr * LANES, (pr + 1) * LANES)
        g0, g1 = 2 * pr, 2 * pr + 1
        for r0, nr in parts:
            s = (dot(x_ref[g0, r0:r0 + nr, :], w_ref[pr, 0:CH, :])
                 + dot(x_ref[g1, r0:r0 + nr, :], w_ref[pr, CH:2 * CH, :]))
            for k in range(4):
                for sg in range(nr // SEG_CHUNKS):
                    p0 = (r0 // SEG_CHUNKS + sg) * PITCH
                    s_scr[k, p0:p0 + SEG_CHUNKS, :] = s[sg * SEG_CHUNKS:(sg + 1) * SEG_CHUNKS, k * LANES:(k + 1) * LANES]

        ar, ai = apre_ref[1, 0:1, ln], apim_ref[1, 0:1, ln]
        hr, hi = zero, zero
        for c in range(SEG_CHUNKS):
            hin_scr[0, blk(c), :] = hr
            hin_scr[1, blk(c), :] = hi
            hr, hi = _cmul_add(ar, ai, hr, hi, s_scr[0, seg_rows(c), :], s_scr[1, seg_rows(c), :])
        store_final(0, pr, hr)
        store_final(1, pr, hi)
        f_scr[0] = hr[N_PROMPT_SEG:]
        f_scr[1] = hi[N_PROMPT_SEG:]
        br, bi = apre_ref[1, 1:2, ln], apim_ref[1, 1:2, ln]
        gr, gi = zero, zero
        for c in range(SEG_CHUNKS - 1, -1, -1):
            hin_scr[2, blk(c), :] = gr
            hin_scr[3, blk(c), :] = gi
            gr, gi = _cmul_add(br, bi, gr, gi, s_scr[2, seg_rows(c), :], s_scr[3, seg_rows(c), :])
        store_final(2, pr, gr)
        store_final(3, pr, gi)
        f_scr[2] = gr[N_PROMPT_SEG:]
        f_scr[3] = gi[N_PROMPT_SEG:]

        a2r, a2i = apre_ref[SEG_CHUNKS, 0:1, ln], apim_ref[SEG_CHUNKS, 0:1, ln]
        hr, hi = entry_state(0, pr), entry_state(1, pr)
        for j in range(SEG_CHUNKS):
            hs_scr[0, seq_rows(j), :] = hr
            hs_scr[1, seq_rows(j), :] = hi
            hr, hi = _cmul_add(a2r, a2i, hr, hi, f_scr[0, seq_rows(j), :], f_scr[1, seq_rows(j), :])
        b2r, b2i = apre_ref[SEG_CHUNKS, 1:2, ln], apim_ref[SEG_CHUNKS, 1:2, ln]
        gr, gi = entry_state(2, pr), entry_state(3, pr)
        for j in range(SEG_CHUNKS - 1, -1, -1):
            hs_scr[2, seq_rows(j), :] = gr
            hs_scr[3, seq_rows(j), :] = gi
            gr, gi = _cmul_add(b2r, b2i, gr, gi, f_scr[2, seq_rows(j), :], f_scr[3, seq_rows(j), :])
        for c in range(SEG_CHUNKS):
            p_r, p_i = apre_ref[c, 0:1, ln], apim_ref[c, 0:1, ln]
            hr, hi = _cmul_add(p_r, p_i, hs_scr[0], hs_scr[1], hin_scr[0, lat(c), :], hin_scr[1, lat(c), :])
            hin_scr[0, lat(c), :] = hr
            hin_scr[1, lat(c), :] = hi
            cb = SEG_CHUNKS - 1 - c
            p_r, p_i = apre_ref[cb, 1:2, ln], apim_ref[cb, 1:2, ln]
            gr, gi = _cmul_add(p_r, p_i, hs_scr[2], hs_scr[3], hin_scr[2, lat(c), :], hin_scr[3, lat(c), :])
            hin_scr[2, lat(c), :] = gr
            hin_scr[3, lat(c), :] = gi

        for k in range(4):
            for sg in range(N_SEG):
                hinp_scr[sg * SEG_CHUNKS:(sg + 1) * SEG_CHUNKS, k * LANES:(k + 1) * LANES] = (
                    hin_scr[k, pl.ds(sg, SEG_CHUNKS, stride=SEG_PITCH), :].astype(BF16))
        for gi_, g in ((0, g0), (1, g1)):
            for r0, nr in parts:
                y_scr[g, r0:r0 + nr, :] = (dot(x_ref[g, r0:r0 + nr, :], m_ref[g])
                                           + dot(hinp_scr[r0:r0 + nr, :], e_ref[pr, :, gi_ * CH:(gi_ + 1) * CH]))

    rb_rows = 2 * SEG_CHUNKS
    for rb in range(ROWS // rb_rows):
        for hf in range(CHUNK // OCT):
            v = [y_scr[i, rb * rb_rows:(rb + 1) * rb_rows, hf * LANES:(hf + 1) * LANES] for i in range(OCT)]
            for s, acc in enumerate(_slot_transpose(v)):
                t = OCT * hf + s
                for sg in range(rb_rows // SEG_CHUNKS):
                    seg = rb * (rb_rows // SEG_CHUNKS) + sg
                    r0 = seg * SEG + t * SEG_CHUNKS
                    y_ref[0, r0:r0 + SEG_CHUNKS, :] = acc[sg * SEG_CHUNKS:(sg + 1) * SEG_CHUNKS]


def _s5_conv(xg, m, w, e, apre, apim, h0):
    n_oct = S5_GROUPS // OCT
    assert xg.shape[1] == ROWS
    nt = ROWS * CHUNK
    return pl.pallas_call(
        _s5_kernel,
        grid=(n_oct,),
        in_specs=[pl.BlockSpec((OCT, ROWS, CH), lambda o: (o, 0, 0)),
                  pl.BlockSpec((OCT, CH, CH), lambda o: (o, 0, 0)),
                  pl.BlockSpec((OCT_PAIRS, 2 * CH, 4 * LANES), lambda o: (o, 0, 0)),
                  pl.BlockSpec((OCT_PAIRS, 4 * LANES, 2 * CH), lambda o: (o, 0, 0)),
                  pl.BlockSpec((SEG_CHUNKS + 1, 2, OCT_PAIRS * LANES), lambda o: (0, 0, o)),
                  pl.BlockSpec((SEG_CHUNKS + 1, 2, OCT_PAIRS * LANES), lambda o: (0, 0, o)),
                  pl.BlockSpec((N_SAMPLE_SEQ * 4 * S5_GROUPS, S5_STATE), lambda o: (0, 0))],
        out_specs=[pl.BlockSpec((1, nt, LANES), lambda o: (o, 0, 0)),
                   pl.BlockSpec((N_PROMPT_SEG, 4, OCT, S5_STATE), lambda o: (0, 0, o, 0))],
        out_shape=[jax.ShapeDtypeStruct((n_oct, nt, LANES), F32),
                   jax.ShapeDtypeStruct((N_PROMPT_SEG, 4, S5_GROUPS, S5_STATE), F32)],
        scratch_shapes=[pltpu.VMEM((4, N_SEG * PITCH, LANES), F32),
                        pltpu.VMEM((4, SEG_CHUNKS * SEG_PITCH, LANES), F32),
                        pltpu.VMEM((ROWS, 4 * LANES), BF16),
                        pltpu.VMEM((4, N_SAMPLE_SEG, LANES), F32),
                        pltpu.VMEM((4, N_SAMPLE_SEG, LANES), F32),
                        pltpu.VMEM((OCT, ROWS, CH), F32)],
        compiler_params=_cparams(("parallel",)),
        name="s5_conv",
    )(xg, m, w, e, apre, apim, h0)


def _tile_order_positions(l):
    r = np.arange(l)
    tile, wi = r // SEG, r % SEG
    return tile * SEG + (wi % SEG_CHUNKS) * CHUNK + wi // SEG_CHUNKS


def _tile_dft_tables(scale, rows_in_tile_order=True):
    pos = _tile_order_positions(SEG)
    freq = pos if rows_in_tile_order else np.arange(SEG)
    ang = 2.0 * np.pi * ((freq[:, None] * pos[None, :]) % SEG) / SEG
    return jnp.asarray(np.cos(ang) * scale, F32), jnp.asarray(np.sin(ang) * scale, F32)


N_TILES = 16


def _fft16(xr, xi):
    n = len(xr)
    rev = [int(format(i, "04b")[::-1], 2) for i in range(n)]
    ar = [xr[r] for r in rev]
    ai = [xi[r] for r in rev]
    size = 2
    while size <= n:
        half = size // 2
        for start in range(0, n, size):
            for k in range(half):
                wr = math.cos(2.0 * math.pi * k / size)
                wi = -math.sin(2.0 * math.pi * k / size)
                i0, i1 = start + k, start + k + half
                if k == 0:
                    tr, ti = ar[i1], ai[i1]
                elif 4 * k == size:
                    tr, ti = ai[i1], -ar[i1]
                else:
                    tr = ar[i1] * wr - ai[i1] * wi
                    ti = ar[i1] * wi + ai[i1] * wr
                ar[i1], ai[i1] = ar[i0] - tr, ai[i0] - ti
                ar[i0], ai[i0] = ar[i0] + tr, ai[i0] + ti
        size *= 2
    return ar, ai


def _pos_dft_long_kernel(cb_ref, sb_ref, ca_ref, sa_ref, fc_ref, fs_ref, o_ref, a_scr):
    wdt = fc_ref.shape[2]

    def rows_fft(r, carry):
        rows = pl.ds(pl.multiple_of(r * CHUNK, CHUNK), CHUNK)
        for b in range(wdt // LANES):
            lns = slice(b * LANES, (b + 1) * LANES)
            xr = [fc_ref[j, rows, lns].astype(F32) for j in range(N_TILES)]
            xi = [-fs_ref[j, rows, lns].astype(F32) for j in range(N_TILES)]
            ar, ai = _fft16(xr, xi)
            for k in range(N_TILES):
                a_scr[k, 0, rows, lns] = ar[k].astype(BF16)
                a_scr[k, 1, rows, lns] = ai[k].astype(BF16)
        return carry

    lax.fori_loop(0, SEG // CHUNK, rows_fft, 0)

    dot = functools.partial(jnp.dot, preferred_element_type=F32)
    cb, sb = cb_ref[...], sb_ref[...]
    for k1 in range(N_TILES):
        ca, sa = ca_ref[k1:k1 + 1, :], sa_ref[k1:k1 + 1, :]
        dc = (cb * ca - sb * sa).astype(BF16)
        ds = (sb * ca + cb * sa).astype(BF16)
        out = dot(dc, a_scr[k1, 0]) + dot(ds, a_scr[k1, 1])
        o_ref[:, k1 * SEG_CHUNKS:(k1 + 1) * SEG_CHUNKS, :] = out.astype(BF16).reshape(N_TILES, SEG_CHUNKS, wdt)


def _pos_dft_long(fc, fs, first, n):
    wdt = fc.shape[-1]
    l = N_TILES * SEG
    wb = 2 * LANES
    cb, sb = _tile_dft_tables(1.0 / math.sqrt(l), rows_in_tile_order=False)
    pos = _tile_order_positions(SEG)
    ang = 2.0 * np.pi * (np.arange(N_TILES)[:, None] * pos[None, :]) / l
    ca, sa = jnp.asarray(np.cos(ang), F32), jnp.asarray(np.sin(ang), F32)
    const2 = lambda shape: pl.BlockSpec(shape, lambda b, k: (0, 0), pipeline_mode=pl.Buffered(1))
    seq_in = pl.BlockSpec((None, N_TILES, SEG, wb), lambda b, k: (first + b, 0, 0, k))
    seq_out = pl.BlockSpec((None, N_TILES, SEG, wb), lambda b, k: (b, 0, 0, k))
    out = pl.pallas_call(
        _pos_dft_long_kernel,
        grid=(n, wdt // wb),
        in_specs=[const2((SEG, SEG)), const2((SEG, SEG)), const2((N_TILES, SEG)), const2((N_TILES, SEG)),
                  seq_in, seq_in],
        out_specs=seq_out,
        out_shape=jax.ShapeDtypeStruct((n, N_TILES, SEG, wdt), BF16),
        scratch_shapes=[pltpu.VMEM((N_TILES, 2, SEG, wb), BF16)],
        compiler_params=_cparams(("parallel", "parallel")),
        name="pos_dft_long",
    )(cb, sb, ca, sa, fc, fs)
    return out.reshape(n * l, wdt)


def _mix_ffn_kernel(n_prompt_tiles, tiles_per_seq, xp_ref, xs_ref, mod_ref, ys_ref, yfp_ref, yfs_ref, gs_ref, gf_ref,
                    wglu_ref, bglu_ref, wps_ref, wpf_ref, wout_ref, n2_ref,
                    wg_ref, wu_ref, wd_ref, fn_ref, op_ref, os_ref, m_scr, mb_scr):
    tm = xp_ref.shape[0]
    dot = functools.partial(jnp.dot, preferred_element_type=F32)
    mod = _mod_vectors(mod_ref, n_prompt_tiles, tiles_per_seq)
    is_prompt = pl.program_id(0) < n_prompt_tiles
    tiles = [slice(j * SEG, (j + 1) * SEG) for j in range(tm // SEG)]
    st = [dict() for _ in tiles]

    def gelu_phase(j, r, s):
        y = jnp.concatenate([ys_ref[b, r, :] for b in range(S5_WIDTH // LANES)], axis=1)
        s["z"] = jax.nn.gelu(y)

    def glu_phase(j, r, s):
        z = s["z"]
        s["z"] = (z * jax.nn.sigmoid(dot(z.astype(BF16), wglu_ref[...]) + bglu_ref[...])).astype(BF16)

    def proj_phase(j, r, s):
        yf = jnp.where(is_prompt, yfp_ref[r, :], yfs_ref[r, :])
        m = (gs_ref[r, :].astype(F32) * dot(s.pop("z"), wps_ref[...])
             + gf_ref[r, :].astype(F32) * dot(yf, wpf_ref[...]))
        for k in range(D_MODEL // LANES):
            for t in range(CHUNK):
                p0 = (j * CHUNK + t) * PITCH
                m_scr[k, p0:p0 + SEG_CHUNKS, :] = m[t * SEG_CHUNKS:(t + 1) * SEG_CHUNKS, k * LANES:(k + 1) * LANES]
        for c in range(SEG_CHUNKS):
            r0 = j * SEG + c * CHUNK
            for k in range(D_MODEL // LANES):
                mb_scr[r0:r0 + CHUNK, k * LANES:(k + 1) * LANES] = (
                    m_scr[k, pl.ds(j * CHUNK * PITCH + c, CHUNK, stride=PITCH), :].astype(BF16))

    def out_phase(j, r, s):
        x = jnp.where(is_prompt, xp_ref[r, :], xs_ref[r, :])
        x1 = x + mod[2] * dot(mb_scr[r, :], wout_ref[...])
        s["x1"] = x1
        s["h2"] = (_rms(x1, n2_ref[...]) * (1.0 + mod[4]) + mod[3]).astype(BF16)

    def ffn_phase(n0, n1, j, r, s):
        gate = dot(s["h2"], wg_ref[:, n0:n1])
        up = dot(s["h2"], wu_ref[:, n0:n1])
        part = dot((gate * jax.nn.sigmoid(gate) * up).astype(BF16), wd_ref[n0:n1, :])
        s["ff"] = part if "ff" not in s else s["ff"] + part

    def final_phase(j, r, s):
        x2 = s.pop("x1") + mod[5] * s.pop("ff")
        s["res"] = _rms(x2, fn_ref[...])

    phases = [gelu_phase, glu_phase, proj_phase, out_phase]
    phases += [functools.partial(ffn_phase, n0, n1) for n0, n1 in zip(FF_SPLITS[:-1], FF_SPLITS[1:])]
    phases += [final_phase]
    for phase in phases:
        for j, r in enumerate(tiles):
            phase(j, r, st[j])

    @pl.when(is_prompt)
    def _():
        for j, r in enumerate(tiles):
            op_ref[r, :] = st[j]["res"]

    @pl.when(jnp.logical_not(is_prompt))
    def _():
        for j, r in enumerate(tiles):
            os_ref[r, :] = st[j]["res"]


def _mix_ffn(xp, xs, seq_len, mod, ys, yf_p, yf_s, gs, gf, wts, tm):
    tp, ts = xp.shape[0], xs.shape[0]
    n_p = tp // tm
    tok = lambda w: pl.BlockSpec((tm, w), lambda i: (i, 0))
    return pl.pallas_call(
        functools.partial(_mix_ffn_kernel, n_p, seq_len // tm),
        grid=((tp + ts) // tm,),
        in_specs=_two_part_specs(tm, D_MODEL, n_p)
                 + [MOD_SPEC,
                    pl.BlockSpec((S5_WIDTH // LANES, tm, LANES), lambda i: (0, i, 0))]
                 + _two_part_specs(tm, FFT_WIDTH, n_p)
                 + [tok(D_MODEL), tok(D_MODEL)]
                 + [_const_spec(w.shape) for w in wts],
        out_specs=_two_part_specs(tm, D_MODEL, n_p),
        out_shape=[jax.ShapeDtypeStruct((tp, D_MODEL), F32), jax.ShapeDtypeStruct((ts, D_MODEL), F32)],
        scratch_shapes=[pltpu.VMEM((D_MODEL // LANES, tm // CHUNK * PITCH, LANES), F32),
                        pltpu.VMEM((tm, D_MODEL), BF16)],
        compiler_params=_cparams(("arbitrary",)),
        name="mix_ffn",
    )(xp, xs, mod, ys, yf_p, yf_s, gs, gf, *wts)


def kernel(x_prompt, x_sample, state_s5, c, c_ctx, norm1_g, norm2_g, w_ada, b_ada, w_in,
           s5_lambda_re, s5_lambda_im, s5_log_step, s5_b_re, s5_b_im, s5_c_re, s5_c_im,
           s5_d, w_glu, b_glu, w_proj_s5, w_proj_fft, w_out, w_ffn_gate, w_ffn_up,
           w_ffn_down, final_norm_g):
    nb, sl, _ = x_prompt.shape
    db, dl, _ = x_sample.shape
    assert w_in.shape[0] == 1 and sl == SEG and nb == N_PROMPT_SEG
    assert db == N_SAMPLE_SEQ and dl == SEG * SEG_CHUNKS

    mod, w_in_b, m, w, e, apre, apim = _adaln_and_s5_tables(
        c_ctx[None], c, w_ada[0], b_ada[0], w_in[0], s5_lambda_re[0], s5_lambda_im[0], s5_log_step[0], s5_b_re[0],
        s5_b_im[0], s5_c_re[0], s5_c_im[0], s5_d)

    tm = 512
    xp = x_prompt.reshape(nb * sl, D_MODEL)
    xs = x_sample.reshape(db * dl, D_MODEL)
    later = (w_glu[0], w_proj_s5[0], w_proj_fft[0], w_out[0], w_ffn_gate[0], w_ffn_up[0], w_ffn_down[0])
    xg, fc, fs, gs, gf, yf_p, *later_b = _inproj(xp, xs, dl, mod, norm1_g[0], w_in_b, later, tm)
    wglu_b, wps_b, wpf_b, wout_b, wg_b, wu_b, wd_b = later_b

    ys, fin = _s5_conv(xg, m, w, e, apre, apim, state_s5.astype(F32).reshape(db * 4 * S5_GROUPS, S5_STATE))
    new_state = fin.reshape(nb, 1, 2, 2, S5_GROUPS, S5_STATE)

    n_prompt_units = nb // N_TILES
    yf_s = _pos_dft_long(fc.reshape(-1, N_TILES, SEG, FFT_WIDTH), fs.reshape(-1, N_TILES, SEG, FFT_WIDTH),
                         n_prompt_units, db)

    wts = (wglu_b, b_glu[0].reshape(1, S5_WIDTH), wps_b, wpf_b, wout_b, norm2_g[0].reshape(1, D_MODEL),
           wg_b, wu_b, wd_b, final_norm_g.reshape(1, D_MODEL))
    tm2 = 2 * SEG
    y_p, y_s = _mix_ffn(xp, xs, dl, mod, ys, yf_p.reshape(nb * sl, FFT_WIDTH), yf_s, gs, gf, wts, tm2)
    return (y_p.reshape(nb, sl, D_MODEL), y_s.reshape(db, dl, D_MODEL), new_state)
```

```python
import functools
import math

import numpy as np
import jax
import jax.numpy as jnp
from jax import lax
from jax.experimental import pallas as pl
from jax.experimental.pallas import tpu as pltpu

F32 = jnp.float32
BF16 = jnp.bfloat16

D_MODEL = 1024
S5_WIDTH = 512
S5_GROUPS = 32
S5_GROUP = 16
S5_STATE = 64
FFT_WIDTH = 512
FFT_GROUPS = 4
FFT_GROUP = 128
D_FF = 2816
N_MOD = 6
EPS = 1e-6

LANES = 128
CHUNK = 16
SEG_CHUNKS = 16
SEG = CHUNK * SEG_CHUNKS
GP = S5_GROUPS * S5_STATE
PAIRS = S5_GROUPS // 2
CH = CHUNK * S5_GROUP
OCT = LANES // S5_GROUP
PITCH = 24

MXU_DIM = 256
FF_SPLITS = (0, 3 * MXU_DIM, 6 * MXU_DIM, 9 * MXU_DIM, D_FF)
VMEM_LIMIT = 58 * 1024 * 1024


def _cparams(sem):
    return pltpu.CompilerParams(dimension_semantics=sem, vmem_limit_bytes=VMEM_LIMIT)


def _row_block_cast_specs(w, n_steps):
    rows = w.shape[0] // n_steps
    assert rows * n_steps == w.shape[0] and rows % 16 == 0, (w.shape, n_steps)
    spec = pl.BlockSpec((rows, w.shape[1]), lambda i: (jnp.minimum(i, n_steps - 1), 0))
    return spec, jax.ShapeDtypeStruct(w.shape, BF16)


def _cast_blocks(in_refs, out_refs):
    for src, dst in zip(in_refs, out_refs):
        dst[...] = src[...].astype(BF16)


def _ada_tile(cctx_ref, c_ref, w_ref, b_ref, o_ref):
    dot = functools.partial(jnp.dot, preferred_element_type=F32)
    n_pad = o_ref.shape[0] - 1 - c_ref.shape[0]
    c = jnp.concatenate([cctx_ref[...], c_ref[...], jnp.zeros((n_pad, c_ref.shape[1]), F32)], axis=0)
    s = c * jax.nn.sigmoid(c)
    s_hi = s.astype(BF16)
    s_lo = (s - s_hi.astype(F32)).astype(BF16)
    w = w_ref[...]
    w_hi = w.astype(BF16)
    w_lo = (w - w_hi.astype(F32)).astype(BF16)
    r = dot(jnp.concatenate([s_hi, s_lo], axis=0), w_hi)
    n = s.shape[0]
    o_ref[...] = r[0:n] + r[n:2 * n] + dot(s_hi, w_lo) + b_ref[...]


ROW_LAM_RE, ROW_LAM_IM, ROW_STEP, ROW_B_RE = 0, 2, 4, 8
ROW_B_IM = ROW_B_RE + 2 * S5_GROUP
ROW_C_RE = ROW_B_IM + 2 * S5_GROUP
ROW_C_IM = ROW_C_RE + 2 * S5_GROUP
PARAM_ROWS = ROW_C_IM + 2 * S5_GROUP


def _param_rows(p_ref, row0, d, lanes=slice(None)):
    return p_ref[row0 + d * S5_GROUP:row0 + (d + 1) * S5_GROUP, lanes]


def _exact_bf16_terms(x):
    hi = x.astype(BF16)
    r = x - hi.astype(F32)
    mid = r.astype(BF16)
    return hi, mid, (r - mid.astype(F32)).astype(BF16)


def _s5_params_to_lanes(lam_re_ref, lam_im_ref, lstep_ref, b_re_ref, b_im_ref, c_re_ref, c_im_ref, d_ref, ind_ref,
                        p_scr, dt_scr):
    dot = functools.partial(jnp.dot, preferred_element_type=F32)
    p_scr[ROW_STEP + 2:ROW_B_RE, :] = jnp.zeros((ROW_B_RE - ROW_STEP - 2, GP), F32)
    p_scr[ROW_STEP:ROW_STEP + 2, :] = sum(dot(t, ind_ref[...]) for t in _exact_bf16_terms(lstep_ref[...]))
    for d in range(2):
        for row0, src in ((ROW_LAM_RE, lam_re_ref), (ROW_LAM_IM, lam_im_ref)):
            for q in range(PAIRS):
                p_scr[row0 + d:row0 + d + 1, q * LANES:(q + 1) * LANES] = jnp.concatenate(
                    [src[d, 2 * q:2 * q + 1, :], src[d, 2 * q + 1:2 * q + 2, :]], axis=1)
        for row0, src in ((ROW_B_RE, b_re_ref), (ROW_B_IM, b_im_ref), (ROW_C_RE, c_re_ref), (ROW_C_IM, c_im_ref)):
            for q in range(PAIRS):
                p_scr[row0 + d * S5_GROUP:row0 + (d + 1) * S5_GROUP, q * LANES:(q + 1) * LANES] = jnp.concatenate(
                    [src[d, 2 * q], src[d, 2 * q + 1]], axis=1)
    own = lax.broadcasted_iota(jnp.int32, (S5_GROUPS, S5_WIDTH), 1) // S5_GROUP \
        == lax.broadcasted_iota(jnp.int32, (S5_GROUPS, S5_WIDTH), 0)
    d_rows = jnp.where(own, d_ref[...], 0.0)
    spread = (lax.broadcasted_iota(jnp.int32, (S5_WIDTH, LANES), 0) % S5_GROUP
              == lax.broadcasted_iota(jnp.int32, (S5_WIDTH, LANES), 1) % S5_GROUP).astype(BF16)
    dt_scr[...] = sum(dot(t, spread) for t in _exact_bf16_terms(d_rows))


def _s5_discretise(p_ref, apre_ref, apim_ref, gb_scr, ca_scr):
    lre = p_ref[ROW_LAM_RE:ROW_LAM_RE + 2, :]
    lim = p_ref[ROW_LAM_IM:ROW_LAM_IM + 2, :]
    step = jnp.exp(p_ref[ROW_STEP:ROW_STEP + 2, :])
    mag = jnp.exp(lre * step)
    are = mag * jnp.cos(lim * step)
    aim = mag * jnp.sin(lim * step)
    nr = are - 1.0
    den = lre * lre + lim * lim
    fr = (nr * lre + aim * lim) / den
    fi = (aim * lre - nr * lim) / den

    pr = [jnp.ones_like(are)]
    pi = [jnp.zeros_like(are)]
    for _ in range(CHUNK):
        r, i = pr[-1], pi[-1]
        pr.append(r * are - i * aim)
        pi.append(r * aim + i * are)
    a16r, a16i = pr[CHUNK], pi[CHUNK]
    qr, qi = jnp.ones_like(are), jnp.zeros_like(are)
    for c in range(SEG_CHUNKS + 1):
        apre_ref[c] = qr
        apim_ref[c] = qi
        qr, qi = qr * a16r - qi * a16i, qr * a16i + qi * a16r

    for d in range(2):
        bre, bim = _param_rows(p_ref, ROW_B_RE, d), _param_rows(p_ref, ROW_B_IM, d)
        bbre = fr[d:d + 1] * bre - fi[d:d + 1] * bim
        bbim = fr[d:d + 1] * bim + fi[d:d + 1] * bre
        cre, cim = _param_rows(p_ref, ROW_C_RE, d), _param_rows(p_ref, ROW_C_IM, d)
        for s in range(CHUNK):
            k = CHUNK - 1 - s if d == 0 else s
            r, i = pr[k][d:d + 1], pi[k][d:d + 1]
            gb_scr[0, d, s] = r * bbre - i * bbim
            gb_scr[1, d, s] = r * bbim + i * bbre
            f = s + 1 if d == 0 else CHUNK - s
            r, i = pr[f][d:d + 1], pi[f][d:d + 1]
            ca_scr[0, d, s] = r * cre - i * cim
            ca_scr[1, d, s] = -(r * cim + i * cre)


def _s5_pair_operands(q, slot_q, p_ref, dt_ref, gb_scr, ca_scr, ts_scr, m_ref, w_ref, e_ref):
    dot_nt = lambda a, b: lax.dot_general(a, b, (((1,), (1,)), ((), ())), preferred_element_type=F32)
    lane_gi = lax.broadcasted_iota(jnp.int32, (CH, LANES), 1) // S5_STATE
    row_gi = lax.broadcasted_iota(jnp.int32, (LANES, CH), 0) // S5_STATE
    eye = (lax.broadcasted_iota(jnp.int32, (LANES, LANES), 0)
           == lax.broadcasted_iota(jnp.int32, (LANES, LANES), 1)).astype(BF16)
    slot = lax.broadcasted_iota(jnp.int32, (CH, LANES), 1) // S5_GROUP
    n_lag_rows = (2 * CHUNK - 1) * S5_GROUP
    lanes = pl.ds(pl.multiple_of(q * LANES, LANES), LANES)
    for d in range(2):
        for ri in range(2):
            col = slice((2 * d + ri) * LANES, (2 * d + ri + 1) * LANES)
            gb = gb_scr[ri, d, :, :, lanes].reshape(CH, LANES)
            ca = ca_scr[ri, d, :, :, lanes].reshape(CH, LANES).astype(BF16)
            ca_t = dot_nt(eye, ca)
            for gi in range(2):
                w_ref[slot_q, gi * CH:(gi + 1) * CH, col] = jnp.where(lane_gi == gi, gb, 0.0).astype(BF16)
                e_ref[slot_q, col, gi * CH:(gi + 1) * CH] = jnp.where(row_gi == gi, ca_t, 0.0).astype(BF16)
    for gi in range(2):
        g = 2 * q + gi
        lag = []
        for d in range(2):
            c_re = jnp.concatenate([_param_rows(p_ref, ROW_C_RE, d, lanes)] * OCT, axis=0)
            c_imn = jnp.concatenate([-_param_rows(p_ref, ROW_C_IM, d, lanes)] * OCT, axis=0)
            keep = lax.broadcasted_iota(jnp.int32, (LANES, LANES), 1) // S5_STATE == gi
            c_re = jnp.where(keep, c_re, 0.0).astype(BF16)
            c_imn = jnp.where(keep, c_imn, 0.0).astype(BF16)
            gre = gb_scr[0, d, :, :, lanes].reshape(CH, LANES).astype(BF16)
            gim = gb_scr[1, d, :, :, lanes].reshape(CH, LANES).astype(BF16)
            lag.append(dot_nt(gre, c_re) + dot_nt(gim, c_imn))
        zl = (CHUNK - 1) * S5_GROUP
        ts_scr[0:zl, :] = lag[0][0:zl]
        on_diag = (lax.broadcasted_iota(jnp.int32, (S5_GROUP, LANES), 1) % S5_GROUP
                   == lax.broadcasted_iota(jnp.int32, (S5_GROUP, LANES), 0))
        skip = jnp.where(on_diag, dt_ref[pl.ds(g, 1), :], 0.0)
        ts_scr[zl:zl + S5_GROUP, :] = lag[0][zl:] + lag[1][0:S5_GROUP] + skip
        ts_scr[zl + S5_GROUP:n_lag_rows, :] = lag[1][S5_GROUP:]
        for hf in range(CHUNK // OCT):
            acc = None
            for s in range(OCT):
                t = OCT * hf + s
                win = ts_scr[(CHUNK - 1 - t) * S5_GROUP:(CHUNK - 1 - t) * S5_GROUP + CH, :]
                acc = win if acc is None else jnp.where(slot == s, win, acc)
            m_ref[2 * slot_q + gi, :, hf * LANES:(hf + 1) * LANES] = acc.astype(BF16)


PAIRS_PER_STEP = 2


def _ada_tables_kernel(cctx_ref, c_ref, wada_ref, bada_ref, win_ref, lam_re_ref, lam_im_ref, lstep_ref, b_re_ref, b_im_ref,
                       c_re_ref, c_im_ref, d_ref, ind_ref, mod_ref, winb_ref, apre_ref, apim_ref, m_ref, w_ref, e_ref,
                       p_scr, dt_scr, gb_scr, ca_scr, ts_scr):
    j = pl.program_id(0)
    _cast_blocks([win_ref], [winb_ref])
    _ada_tile(cctx_ref, c_ref, wada_ref, bada_ref, mod_ref)

    @pl.when(j == 0)
    def _():
        _s5_params_to_lanes(lam_re_ref, lam_im_ref, lstep_ref, b_re_ref, b_im_ref, c_re_ref, c_im_ref, d_ref, ind_ref,
                            p_scr, dt_scr)
        _s5_discretise(p_scr, apre_ref, apim_ref, gb_scr, ca_scr)

    for lp in range(PAIRS_PER_STEP):
        _s5_pair_operands(PAIRS_PER_STEP * j + lp, lp, p_scr, dt_scr, gb_scr, ca_scr, ts_scr, m_ref, w_ref, e_ref)


def _adaln_and_s5_tables(c_ctx, c, w_ada, b_ada, w_in, lam_re, lam_im, log_step, b_re, b_im, c_re, c_im, s5_d):
    ind_np = np.zeros((S5_GROUPS, GP), np.float32)
    ind_np[np.arange(GP) // S5_STATE, np.arange(GP)] = 1.0
    s5_in = (lam_re, lam_im, log_step, jnp.swapaxes(b_re, -1, -2), jnp.swapaxes(b_im, -1, -2), c_re, c_im,
             s5_d.reshape(1, S5_WIDTH), jnp.asarray(ind_np).astype(BF16))

    n_steps = PAIRS // PAIRS_PER_STEP
    n_out = w_ada.shape[1]
    tn = n_out // n_steps
    cast_spec, cast_shape = _row_block_cast_specs(w_in, n_steps)
    whole = lambda a: pl.BlockSpec(a.shape, lambda j, nd=a.ndim: (0,) * nd)
    powers = jax.ShapeDtypeStruct((SEG_CHUNKS + 1, 2, GP), F32)
    tab = (2, 2, CHUNK, S5_GROUP, GP)
    mod, w_in_b, apre, apim, m, w, e = pl.pallas_call(
        _ada_tables_kernel,
        grid=(n_steps,),
        in_specs=[whole(c_ctx), whole(c),
                  pl.BlockSpec((D_MODEL, tn), lambda j: (0, j)),
                  pl.BlockSpec((1, tn), lambda j: (0, j)),
                  cast_spec] + [whole(a) for a in s5_in],
        out_specs=[pl.BlockSpec((8, tn), lambda j: (0, j)), cast_spec, whole(powers), whole(powers),
                   pl.BlockSpec((2 * PAIRS_PER_STEP, CH, CH), lambda j: (j, 0, 0)),
                   pl.BlockSpec((PAIRS_PER_STEP, 2 * CH, 4 * LANES), lambda j: (j, 0, 0)),
                   pl.BlockSpec((PAIRS_PER_STEP, 4 * LANES, 2 * CH), lambda j: (j, 0, 0))],
        out_shape=[jax.ShapeDtypeStruct((8, n_out), F32), cast_shape, powers, powers,
                   jax.ShapeDtypeStruct((S5_GROUPS, CH, CH), BF16),
                   jax.ShapeDtypeStruct((PAIRS, 2 * CH, 4 * LANES), BF16),
                   jax.ShapeDtypeStruct((PAIRS, 4 * LANES, 2 * CH), BF16)],
        scratch_shapes=[pltpu.VMEM((PARAM_ROWS, GP), F32),
                        pltpu.VMEM((S5_GROUPS, LANES), F32),
                        pltpu.VMEM(tab, F32),
                        pltpu.VMEM(tab, F32),
                        pltpu.VMEM(((2 * CHUNK - 1) * S5_GROUP, LANES), F32)],
        compiler_params=_cparams(("arbitrary",)),
        name="adaln_s5_tables",
    )(c_ctx, c, w_ada, b_ada.reshape(1, n_out), w_in, *s5_in)
    return mod, w_in_b, m, w, e, apre, apim


def _rms(x, g):
    return x * lax.rsqrt(jnp.mean(x * x, axis=-1, keepdims=True) + EPS) * g


def _slot_transpose(v):
    v = list(v)
    slot = lax.broadcasted_iota(jnp.int32, v[0].shape, 1) // S5_GROUP
    for k in (4, 2, 1):
        low = (slot & k) == 0
        for i in range(OCT):
            if i & k:
                continue
            a, b = v[i], v[i + k]
            v[i] = jnp.where(low, a, pltpu.roll(b, k * S5_GROUP, axis=1))
            v[i + k] = jnp.where(low, pltpu.roll(a, LANES - k * S5_GROUP, axis=1), b)
    return v


def _inproj_kernel(n_prompt_tiles, tiles_per_seq, n_cast, xp_ref, xs_ref, mod_ref, g_ref, win_ref, bdc_ref, bds_ref,
                   cm_ref, sm_ref, *rest):
    cast_in, rest = rest[:n_cast], rest[n_cast:]
    xg_ref, fc_ref, fs_ref, gs_ref, gf_ref, yfp_ref = rest[:6]
    cast_out, (h_scr, hb_scr, zs_scr) = rest[6:6 + n_cast], rest[6 + n_cast:]
    tm = xp_ref.shape[0]
    mod = _mod_vectors(mod_ref, n_prompt_tiles, tiles_per_seq)
    is_prompt = pl.program_id(0) < n_prompt_tiles
    dot = functools.partial(jnp.dot, preferred_element_type=F32)
    o = 2 * S5_WIDTH
    tiles = [slice(j * SEG, (j + 1) * SEG) for j in range(tm // SEG)]

    def norm_phase(j, r):
        x = jnp.where(is_prompt, xp_ref[r, :], xs_ref[r, :])
        h = _rms(x, g_ref[...]) * (1.0 + mod[1]) + mod[0]
        for k in range(D_MODEL // LANES):
            for c in range(SEG_CHUNKS):
                p0 = (j * SEG_CHUNKS + c) * PITCH
                h_scr[k, p0:p0 + CHUNK, :] = h[c * CHUNK:(c + 1) * CHUNK, k * LANES:(k + 1) * LANES]
        for s in range(CHUNK):
            r0 = j * SEG + s * SEG_CHUNKS
            for k in range(D_MODEL // LANES):
                hb_scr[r0:r0 + SEG_CHUNKS, k * LANES:(k + 1) * LANES] = (
                    h_scr[k, pl.ds(j * SEG_CHUNKS * PITCH + s, SEG_CHUNKS, stride=PITCH), :].astype(BF16))

    def s5_gate_phase(j, r):
        zs_scr[r, :] = dot(hb_scr[r, :], win_ref[:, 0:S5_WIDTH])
        gs_ref[r, :] = jax.nn.sigmoid(dot(hb_scr[r, :], win_ref[:, o:o + D_MODEL])).astype(BF16)

    def fold_phase(j, r):
        for b in range(S5_WIDTH // LANES):
            for hf in range(CHUNK // OCT):
                z = [zs_scr[j * SEG + s * SEG_CHUNKS:j * SEG + (s + 1) * SEG_CHUNKS, b * LANES:(b + 1) * LANES]
                     for s in range(OCT * hf, OCT * (hf + 1))]
                for i, xi in enumerate(_slot_transpose(z)):
                    xg_ref[OCT * b + i, j * SEG_CHUNKS:(j + 1) * SEG_CHUNKS,
                           hf * LANES:(hf + 1) * LANES] = xi.astype(BF16)

    def fourier_gate_phase(j, r):
        uf = dot(hb_scr[r, :], win_ref[:, S5_WIDTH:2 * S5_WIDTH]).astype(BF16)
        for n0 in range(0, FFT_WIDTH, MXU_DIM):
            cols = slice(n0, n0 + MXU_DIM)
            fc_ref[r, cols] = dot(uf[:, cols], bdc_ref[cols, cols]).astype(BF16)
            fs_ref[r, cols] = dot(uf[:, cols], bds_ref[cols, cols]).astype(BF16)
        gf_ref[r, :] = jax.nn.sigmoid(dot(hb_scr[r, :], win_ref[:, o + D_MODEL:o + 2 * D_MODEL])).astype(BF16)

    for phase in (norm_phase, s5_gate_phase, fold_phase, fourier_gate_phase):
        for j, r in enumerate(tiles):
            phase(j, r)
    _cast_blocks(cast_in, cast_out)

    @pl.when(is_prompt)
    def _():
        cm, sm = cm_ref[...].astype(BF16), sm_ref[...].astype(BF16)
        for r in tiles:
            yfp_ref[r, :] = (dot(cm, fc_ref[r, :]) - dot(sm, fs_ref[r, :])).astype(BF16)


def _channel_dft_mats():
    j = np.arange(FFT_GROUP)
    ang = 2.0 * np.pi * ((j[:, None] * j[None, :]) % FFT_GROUP) / FFT_GROUP
    blk_c = np.cos(ang) / math.sqrt(FFT_GROUP)
    blk_s = np.sin(ang) / math.sqrt(FFT_GROUP)
    bdc = np.kron(np.eye(FFT_GROUPS), blk_c)
    bds = np.kron(np.eye(FFT_GROUPS), blk_s)
    return jnp.asarray(bdc, F32).astype(BF16), jnp.asarray(bds, F32).astype(BF16)


def _const_spec(shape):
    nd = len(shape)
    return pl.BlockSpec(shape, lambda i: (0,) * nd, pipeline_mode=pl.Buffered(1))


def _two_part_specs(tm, width, n_prompt_tiles):
    return [pl.BlockSpec((tm, width), lambda i: (jnp.minimum(i, n_prompt_tiles - 1), 0)),
            pl.BlockSpec((tm, width), lambda i: (jnp.maximum(i - n_prompt_tiles, 0), 0))]


MOD_SPEC = pl.BlockSpec((8, N_MOD * D_MODEL), lambda i: (0, 0))


def _mod_vectors(mod_ref, n_prompt_tiles, tiles_per_seq):
    i = pl.program_id(0)
    row = jnp.where(i < n_prompt_tiles, 0, 1 + (i - n_prompt_tiles) // tiles_per_seq)
    full = mod_ref[pl.ds(row, 1), :]
    return [full[:, k * D_MODEL:(k + 1) * D_MODEL] for k in range(N_MOD)]


N_CAST_STEPS = 16


def _inproj(xp, xs, seq_len, mod, norm_g, w_in_b, later_weights, tm):
    t = xp.shape[0] + xs.shape[0]
    n_p = xp.shape[0] // tm
    assert t // tm >= N_CAST_STEPS
    bdc, bds = _channel_dft_mats()
    cm, sm = _tile_dft_tables(1.0 / math.sqrt(SEG))
    casts = [_row_block_cast_specs(w, N_CAST_STEPS) for w in later_weights]
    tok = lambda w: pl.BlockSpec((tm, w), lambda i: (i, 0))
    out = lambda w: jax.ShapeDtypeStruct((t, w), BF16)
    return pl.pallas_call(
        functools.partial(_inproj_kernel, n_p, seq_len // tm, len(casts)),
        grid=(t // tm,),
        in_specs=_two_part_specs(tm, D_MODEL, n_p) + [
                  MOD_SPEC,
                  _const_spec((1, D_MODEL)),
                  _const_spec(w_in_b.shape),
                  _const_spec(bdc.shape),
                  _const_spec(bds.shape), _const_spec(cm.shape), _const_spec(sm.shape)] + [c[0] for c in casts],
        out_specs=[pl.BlockSpec((S5_GROUPS, tm // CHUNK, CH), lambda i: (0, i, 0)),
                   tok(FFT_WIDTH), tok(FFT_WIDTH), tok(D_MODEL), tok(D_MODEL),
                   _two_part_specs(tm, FFT_WIDTH, n_p)[0]] + [c[0] for c in casts],
        out_shape=[jax.ShapeDtypeStruct((S5_GROUPS, t // CHUNK, CH), BF16),
                   out(FFT_WIDTH), out(FFT_WIDTH), out(D_MODEL), out(D_MODEL),
                   jax.ShapeDtypeStruct((xp.shape[0], FFT_WIDTH), BF16)] + [c[1] for c in casts],
        scratch_shapes=[pltpu.VMEM((D_MODEL // LANES, tm // CHUNK * PITCH, LANES), F32),
                        pltpu.VMEM((tm, D_MODEL), BF16),
                        pltpu.VMEM((tm, S5_WIDTH), F32)],
        compiler_params=_cparams(("arbitrary",)),
        name="inproj",
    )(xp, xs, mod, norm_g.reshape(1, D_MODEL), w_in_b, bdc, bds, cm, sm, *later_weights)


N_PROMPT_SEG = 16
N_SAMPLE_SEG = 32
N_SAMPLE_SEQ = 2
N_SEG = N_PROMPT_SEG + N_SAMPLE_SEG
ROWS = SEG_CHUNKS * N_SEG
ROWS_P = SEG_CHUNKS * N_PROMPT_SEG
OCT_PAIRS = OCT // 2
SEG_PITCH = 56


def _cmul_add(ar, ai, hr, hi, sr, si):
    return ar * hr - ai * hi + sr, ar * hi + ai * hr + si


def _s5_kernel(x_ref, m_ref, w_ref, e_ref, apre_ref, apim_ref, h0_ref,
               y_ref, fin_ref, s_scr, hin_scr, hinp_scr, f_scr, hs_scr, y_scr):
    dot = functools.partial(jnp.dot, preferred_element_type=F32)
    parts = ((0, ROWS_P), (ROWS_P, ROWS - ROWS_P))
    seg_rows = lambda c: pl.ds(c, N_SEG, stride=PITCH)
    blk = lambda c: pl.ds(c * SEG_PITCH, N_SEG)
    seq_rows = lambda j: pl.ds(j, N_SAMPLE_SEQ, stride=SEG_CHUNKS)
    lat = lambda c: pl.ds(c * SEG_PITCH + N_PROMPT_SEG, N_SAMPLE_SEG)
    zero = jnp.zeros((N_SEG, LANES), F32)
    octet = pl.program_id(0)

    def entry_state(k, pr):
        g0 = 2 * (octet * OCT_PAIRS + pr)
        rows = [jnp.concatenate([h0_ref[pl.ds(n * 4 * S5_GROUPS + k * S5_GROUPS + g0 + gi, 1), :] for gi in range(2)],
                                axis=1) for n in range(N_SAMPLE_SEQ)]
        return jnp.concatenate(rows, axis=0)

    def store_final(k, pr, v):
        for gi in range(2):
            fin_ref[:, k, 2 * pr + gi, :] = v[0:N_PROMPT_SEG, gi * S5_STATE:(gi + 1) * S5_STATE]

    for pr in range(OCT_PAIRS):
        ln = slice(pr * LANES, (pr + 1) * LANES)
        g0, g1 = 2 * pr, 2 * pr + 1
        for r0, nr in parts:
            s = (dot(x_ref[g0, r0:r0 + nr, :], w_ref[pr, 0:CH, :])
                 + dot(x_ref[g1, r0:r0 + nr, :], w_ref[pr, CH:2 * CH, :]))
            for k in range(4):
                for sg in range(nr // SEG_CHUNKS):
                    p0 = (r0 // SEG_CHUNKS + sg) * PITCH
                    s_scr[k, p0:p0 + SEG_CHUNKS, :] = s[sg * SEG_CHUNKS:(sg + 1) * SEG_CHUNKS, k * LANES:(k + 1) * LANES]

        ar, ai = apre_ref[1, 0:1, ln], apim_ref[1, 0:1, ln]
        hr, hi = zero, zero
        for c in range(SEG_CHUNKS):
            hin_scr[0, blk(c), :] = hr
            hin_scr[1, blk(c), :] = hi
            hr, hi = _cmul_add(ar, ai, hr, hi, s_scr[0, seg_rows(c), :], s_scr[1, seg_rows(c), :])
        store_final(0, pr, hr)
        store_final(1, pr, hi)
        f_scr[0] = hr[N_PROMPT_SEG:]
        f_scr[1] = hi[N_PROMPT_SEG:]
        br, bi = apre_ref[1, 1:2, ln], apim_ref[1, 1:2, ln]
        gr, gi = zero, zero
        for c in range(SEG_CHUNKS - 1, -1, -1):
            hin_scr[2, blk(c), :] = gr
            hin_scr[3, blk(c), :] = gi
            gr, gi = _cmul_add(br, bi, gr, gi, s_scr[2, seg_rows(c), :], s_scr[3, seg_rows(c), :])
        store_final(2, pr, gr)
        store_final(3, pr, gi)
        f_scr[2] = gr[N_PROMPT_SEG:]
        f_scr[3] = gi[N_PROMPT_SEG:]

        a2r, a2i = apre_ref[SEG_CHUNKS, 0:1, ln], apim_ref[SEG_CHUNKS, 0:1, ln]
        hr, hi = entry_state(0, pr), entry_state(1, pr)
        for j in range(SEG_CHUNKS):
            hs_scr[0, seq_rows(j), :] = hr
            hs_scr[1, seq_rows(j), :] = hi
            hr, hi = _cmul_add(a2r, a2i, hr, hi, f_scr[0, seq_rows(j), :], f_scr[1, seq_rows(j), :])
        b2r, b2i = apre_ref[SEG_CHUNKS, 1:2, ln], apim_ref[SEG_CHUNKS, 1:2, ln]
        gr, gi = entry_state(2, pr), entry_state(3, pr)
        for j in range(SEG_CHUNKS - 1, -1, -1):
            hs_scr[2, seq_rows(j), :] = gr
            hs_scr[3, seq_rows(j), :] = gi
            gr, gi = _cmul_add(b2r, b2i, gr, gi, f_scr[2, seq_rows(j), :], f_scr[3, seq_rows(j), :])
        for c in range(SEG_CHUNKS):
            p_r, p_i = apre_ref[c, 0:1, ln], apim_ref[c, 0:1, ln]
            hr, hi = _cmul_add(p_r, p_i, hs_scr[0], hs_scr[1], hin_scr[0, lat(c), :], hin_scr[1, lat(c), :])
            hin_scr[0, lat(c), :] = hr
            hin_scr[1, lat(c), :] = hi
            cb = SEG_CHUNKS - 1 - c
            p_r, p_i = apre_ref[cb, 1:2, ln], apim_ref[cb, 1:2, ln]
            gr, gi = _cmul_add(p_r, p_i, hs_scr[2], hs_scr[3], hin_scr[2, lat(c), :], hin_scr[3, lat(c), :])
            hin_scr[2, lat(c), :] = gr
            hin_scr[3, lat(c), :] = gi

        for k in range(4):
            for sg in range(N_SEG):
                hinp_scr[sg * SEG_CHUNKS:(sg + 1) * SEG_CHUNKS, k * LANES:(k + 1) * LANES] = (
                    hin_scr[k, pl.ds(sg, SEG_CHUNKS, stride=SEG_PITCH), :].astype(BF16))
        for gi_, g in ((0, g0), (1, g1)):
            for r0, nr in parts:
                y_scr[g, r0:r0 + nr, :] = (dot(x_ref[g, r0:r0 + nr, :], m_ref[g])
                                           + dot(hinp_scr[r0:r0 + nr, :], e_ref[pr, :, gi_ * CH:(gi_ + 1) * CH]))

    rb_rows = 2 * SEG_CHUNKS
    for rb in range(ROWS // rb_rows):
        for hf in range(CHUNK // OCT):
            v = [y_scr[i, rb * rb_rows:(rb + 1) * rb_rows, hf * LANES:(hf + 1) * LANES] for i in range(OCT)]
            for s, acc in enumerate(_slot_transpose(v)):
                t = OCT * hf + s
                for sg in range(rb_rows // SEG_CHUNKS):
                    seg = rb * (rb_rows // SEG_CHUNKS) + sg
                    r0 = seg * SEG + t * SEG_CHUNKS
                    y_ref[0, r0:r0 + SEG_CHUNKS, :] = acc[sg * SEG_CHUNKS:(sg + 1) * SEG_CHUNKS]


def _s5_conv(xg, m, w, e, apre, apim, h0):
    n_oct = S5_GROUPS // OCT
    assert xg.shape[1] == ROWS
    nt = ROWS * CHUNK
    return pl.pallas_call(
        _s5_kernel,
        grid=(n_oct,),
        in_specs=[pl.BlockSpec((OCT, ROWS, CH), lambda o: (o, 0, 0)),
                  pl.BlockSpec((OCT, CH, CH), lambda o: (o, 0, 0)),
                  pl.BlockSpec((OCT_PAIRS, 2 * CH, 4 * LANES), lambda o: (o, 0, 0)),
                  pl.BlockSpec((OCT_PAIRS, 4 * LANES, 2 * CH), lambda o: (o, 0, 0)),
                  pl.BlockSpec((SEG_CHUNKS + 1, 2, OCT_PAIRS * LANES), lambda o: (0, 0, o)),
                  pl.BlockSpec((SEG_CHUNKS + 1, 2, OCT_PAIRS * LANES), lambda o: (0, 0, o)),
                  pl.BlockSpec((N_SAMPLE_SEQ * 4 * S5_GROUPS, S5_STATE), lambda o: (0, 0))],
        out_specs=[pl.BlockSpec((1, nt, LANES), lambda o: (o, 0, 0)),
                   pl.BlockSpec((N_PROMPT_SEG, 4, OCT, S5_STATE), lambda o: (0, 0, o, 0))],
        out_shape=[jax.ShapeDtypeStruct((n_oct, nt, LANES), F32),
                   jax.ShapeDtypeStruct((N_PROMPT_SEG, 4, S5_GROUPS, S5_STATE), F32)],
        scratch_shapes=[pltpu.VMEM((4, N_SEG * PITCH, LANES), F32),
                        pltpu.VMEM((4, SEG_CHUNKS * SEG_PITCH, LANES), F32),
                        pltpu.VMEM((ROWS, 4 * LANES), BF16),
                        pltpu.VMEM((4, N_SAMPLE_SEG, LANES), F32),
                        pltpu.VMEM((4, N_SAMPLE_SEG, LANES), F32),
                        pltpu.VMEM((OCT, ROWS, CH), F32)],
        compiler_params=_cparams(("parallel",)),
        name="s5_conv",
    )(xg, m, w, e, apre, apim, h0)


def _tile_order_positions(l):
    r = np.arange(l)
    tile, wi = r // SEG, r % SEG
    return tile * SEG + (wi % SEG_CHUNKS) * CHUNK + wi // SEG_CHUNKS


def _tile_dft_tables(scale, rows_in_tile_order=True):
    pos = _tile_order_positions(SEG)
    freq = pos if rows_in_tile_order else np.arange(SEG)
    ang = 2.0 * np.pi * ((freq[:, None] * pos[None, :]) % SEG) / SEG
    return jnp.asarray(np.cos(ang) * scale, F32), jnp.asarray(np.sin(ang) * scale, F32)


N_TILES = 16


def _fft16(xr, xi):
    n = len(xr)
    rev = [int(format(i, "04b")[::-1], 2) for i in range(n)]
    ar = [xr[r] for r in rev]
    ai = [xi[r] for r in rev]
    size = 2
    while size <= n:
        half = size // 2
        for start in range(0, n, size):
            for k in range(half):
                wr = math.cos(2.0 * math.pi * k / size)
                wi = -math.sin(2.0 * math.pi * k / size)
                i0, i1 = start + k, start + k + half
                if k == 0:
                    tr, ti = ar[i1], ai[i1]
                elif 4 * k == size:
                    tr, ti = ai[i1], -ar[i1]
                else:
                    tr = ar[i1] * wr - ai[i1] * wi
                    ti = ar[i1] * wi + ai[i1] * wr
                ar[i1], ai[i1] = ar[i0] - tr, ai[i0] - ti
                ar[i0], ai[i0] = ar[i0] + tr, ai[i0] + ti
        size *= 2
    return ar, ai


def _pos_dft_long_kernel(cb_ref, sb_ref, ca_ref, sa_ref, fc_ref, fs_ref, o_ref, a_scr):
    wdt = fc_ref.shape[2]

    def rows_fft(r, carry):
        rows = pl.ds(pl.multiple_of(r * CHUNK, CHUNK), CHUNK)
        for b in range(wdt // LANES):
            lns = slice(b * LANES, (b + 1) * LANES)
            xr = [fc_ref[j, rows, lns].astype(F32) for j in range(N_TILES)]
            xi = [-fs_ref[j, rows, lns].astype(F32) for j in range(N_TILES)]
            ar, ai = _fft16(xr, xi)
            for k in range(N_TILES):
                a_scr[k, 0, rows, lns] = ar[k].astype(BF16)
                a_scr[k, 1, rows, lns] = ai[k].astype(BF16)
        return carry

    lax.fori_loop(0, SEG // CHUNK, rows_fft, 0)

    dot = functools.partial(jnp.dot, preferred_element_type=F32)
    cb, sb = cb_ref[...], sb_ref[...]
    for k1 in range(N_TILES):
        ca, sa = ca_ref[k1:k1 + 1, :], sa_ref[k1:k1 + 1, :]
        dc = (cb * ca - sb * sa).astype(BF16)
        ds = (sb * ca + cb * sa).astype(BF16)
        out = dot(dc, a_scr[k1, 0]) + dot(ds, a_scr[k1, 1])
        o_ref[:, k1 * SEG_CHUNKS:(k1 + 1) * SEG_CHUNKS, :] = out.astype(BF16).reshape(N_TILES, SEG_CHUNKS, wdt)


def _pos_dft_long(fc, fs, first, n):
    wdt = fc.shape[-1]
    l = N_TILES * SEG
    wb = 2 * LANES
    cb, sb = _tile_dft_tables(1.0 / math.sqrt(l), rows_in_tile_order=False)
    pos = _tile_order_positions(SEG)
    ang = 2.0 * np.pi * (np.arange(N_TILES)[:, None] * pos[None, :]) / l
    ca, sa = jnp.asarray(np.cos(ang), F32), jnp.asarray(np.sin(ang), F32)
    const2 = lambda shape: pl.BlockSpec(shape, lambda b, k: (0, 0), pipeline_mode=pl.Buffered(1))
    seq_in = pl.BlockSpec((None, N_TILES, SEG, wb), lambda b, k: (first + b, 0, 0, k))
    seq_out = pl.BlockSpec((None, N_TILES, SEG, wb), lambda b, k: (b, 0, 0, k))
    out = pl.pallas_call(
        _pos_dft_long_kernel,
        grid=(n, wdt // wb),
        in_specs=[const2((SEG, SEG)), const2((SEG, SEG)), const2((N_TILES, SEG)), const2((N_TILES, SEG)),
                  seq_in, seq_in],
        out_specs=seq_out,
        out_shape=jax.ShapeDtypeStruct((n, N_TILES, SEG, wdt), BF16),
        scratch_shapes=[pltpu.VMEM((N_TILES, 2, SEG, wb), BF16)],
        compiler_params=_cparams(("parallel", "parallel")),
        name="pos_dft_long",
    )(cb, sb, ca, sa, fc, fs)
    return out.reshape(n * l, wdt)


def _mix_ffn_kernel(n_prompt_tiles, tiles_per_seq, xp_ref, xs_ref, mod_ref, ys_ref, yfp_ref, yfs_ref, gs_ref, gf_ref,
                    wglu_ref, bglu_ref, wps_ref, wpf_ref, wout_ref, n2_ref,
                    wg_ref, wu_ref, wd_ref, fn_ref, op_ref, os_ref, m_scr, mb_scr):
    tm = xp_ref.shape[0]
    dot = functools.partial(jnp.dot, preferred_element_type=F32)
    mod = _mod_vectors(mod_ref, n_prompt_tiles, tiles_per_seq)
    is_prompt = pl.program_id(0) < n_prompt_tiles
    tiles = [slice(j * SEG, (j + 1) * SEG) for j in range(tm // SEG)]
    st = [dict() for _ in tiles]

    def gelu_phase(j, r, s):
        y = jnp.concatenate([ys_ref[b, r, :] for b in range(S5_WIDTH // LANES)], axis=1)
        s["z"] = jax.nn.gelu(y)

    def glu_phase(j, r, s):
        z = s["z"]
        s["z"] = (z * jax.nn.sigmoid(dot(z.astype(BF16), wglu_ref[...]) + bglu_ref[...])).astype(BF16)

    def proj_phase(j, r, s):
        yf = jnp.where(is_prompt, yfp_ref[r, :], yfs_ref[r, :])
        m = (gs_ref[r, :].astype(F32) * dot(s.pop("z"), wps_ref[...])
             + gf_ref[r, :].astype(F32) * dot(yf, wpf_ref[...]))
        for k in range(D_MODEL // LANES):
            for t in range(CHUNK):
                p0 = (j * CHUNK + t) * PITCH
                m_scr[k, p0:p0 + SEG_CHUNKS, :] = m[t * SEG_CHUNKS:(t + 1) * SEG_CHUNKS, k * LANES:(k + 1) * LANES]
        for c in range(SEG_CHUNKS):
            r0 = j * SEG + c * CHUNK
            for k in range(D_MODEL // LANES):
                mb_scr[r0:r0 + CHUNK, k * LANES:(k + 1) * LANES] = (
                    m_scr[k, pl.ds(j * CHUNK * PITCH + c, CHUNK, stride=PITCH), :].astype(BF16))

    def out_phase(j, r, s):
        x = jnp.where(is_prompt, xp_ref[r, :], xs_ref[r, :])
        x1 = x + mod[2] * dot(mb_scr[r, :], wout_ref[...])
        s["x1"] = x1
        s["h2"] = (_rms(x1, n2_ref[...]) * (1.0 + mod[4]) + mod[3]).astype(BF16)

    def ffn_phase(n0, n1, j, r, s):
        gate = dot(s["h2"], wg_ref[:, n0:n1])
        up = dot(s["h2"], wu_ref[:, n0:n1])
        part = dot((gate * jax.nn.sigmoid(gate) * up).astype(BF16), wd_ref[n0:n1, :])
        s["ff"] = part if "ff" not in s else s["ff"] + part

    def final_phase(j, r, s):
        x2 = s.pop("x1") + mod[5] * s.pop("ff")
        s["res"] = _rms(x2, fn_ref[...])

    phases = [gelu_phase, glu_phase, proj_phase, out_phase]
    phases += [functools.partial(ffn_phase, n0, n1) for n0, n1 in zip(FF_SPLITS[:-1], FF_SPLITS[1:])]
    phases += [final_phase]
    for phase in phases:
        for j, r in enumerate(tiles):
            phase(j, r, st[j])

    @pl.when(is_prompt)
    def _():
        for j, r in enumerate(tiles):
            op_ref[r, :] = st[j]["res"]

    @pl.when(jnp.logical_not(is_prompt))
    def _():
        for j, r in enumerate(tiles):
            os_ref[r, :] = st[j]["res"]


def _mix_ffn(xp, xs, seq_len, mod, ys, yf_p, yf_s, gs, gf, wts, tm):
    tp, ts = xp.shape[0], xs.shape[0]
    n_p = tp // tm
    tok = lambda w: pl.BlockSpec((tm, w), lambda i: (i, 0))
    return pl.pallas_call(
        functools.partial(_mix_ffn_kernel, n_p, seq_len // tm),
        grid=((tp + ts) // tm,),
        in_specs=_two_part_specs(tm, D_MODEL, n_p)
                 + [MOD_SPEC,
                    pl.BlockSpec((S5_WIDTH // LANES, tm, LANES), lambda i: (0, i, 0))]
                 + _two_part_specs(tm, FFT_WIDTH, n_p)
                 + [tok(D_MODEL), tok(D_MODEL)]
                 + [_const_spec(w.shape) for w in wts],
        out_specs=_two_part_specs(tm, D_MODEL, n_p),
        out_shape=[jax.ShapeDtypeStruct((tp, D_MODEL), F32), jax.ShapeDtypeStruct((ts, D_MODEL), F32)],
        scratch_shapes=[pltpu.VMEM((D_MODEL // LANES, tm // CHUNK * PITCH, LANES), F32),
                        pltpu.VMEM((tm, D_MODEL), BF16)],
        compiler_params=_cparams(("arbitrary",)),
        name="mix_ffn",
    )(xp, xs, mod, ys, yf_p, yf_s, gs, gf, *wts)


def kernel(x_prompt, x_sample, state_s5, c, c_ctx, norm1_g, norm2_g, w_ada, b_ada, w_in,
           s5_lambda_re, s5_lambda_im, s5_log_step, s5_b_re, s5_b_im, s5_c_re, s5_c_im,
           s5_d, w_glu, b_glu, w_proj_s5, w_proj_fft, w_out, w_ffn_gate, w_ffn_up,
           w_ffn_down, final_norm_g):
    nb, sl, _ = x_prompt.shape
    db, dl, _ = x_sample.shape
    assert w_in.shape[0] == 1 and sl == SEG and nb == N_PROMPT_SEG
    assert db == N_SAMPLE_SEQ and dl == SEG * SEG_CHUNKS

    mod, w_in_b, m, w, e, apre, apim = _adaln_and_s5_tables(
        c_ctx[None], c, w_ada[0], b_ada[0], w_in[0], s5_lambda_re[0], s5_lambda_im[0], s5_log_step[0], s5_b_re[0],
        s5_b_im[0], s5_c_re[0], s5_c_im[0], s5_d)

    tm = 512
    xp = x_prompt.reshape(nb * sl, D_MODEL)
    xs = x_sample.reshape(db * dl, D_MODEL)
    later = (w_glu[0], w_proj_s5[0], w_proj_fft[0], w_out[0], w_ffn_gate[0], w_ffn_up[0], w_ffn_down[0])
    xg, fc, fs, gs, gf, yf_p, *later_b = _inproj(xp, xs, dl, mod, norm1_g[0], w_in_b, later, tm)
    wglu_b, wps_b, wpf_b, wout_b, wg_b, wu_b, wd_b = later_b

    ys, fin = _s5_conv(xg, m, w, e, apre, apim, state_s5.astype(F32).reshape(db * 4 * S5_GROUPS, S5_STATE))
    new_state = fin.reshape(nb, 1, 2, 2, S5_GROUPS, S5_STATE)

    n_prompt_units = nb // N_TILES
    yf_s = _pos_dft_long(fc.reshape(-1, N_TILES, SEG, FFT_WIDTH), fs.reshape(-1, N_TILES, SEG, FFT_WIDTH),
                         n_prompt_units, db)

    wts = (wglu_b, b_glu[0].reshape(1, S5_WIDTH), wps_b, wpf_b, wout_b, norm2_g[0].reshape(1, D_MODEL),
           wg_b, wu_b, wd_b, final_norm_g.reshape(1, D_MODEL))
    tm2 = 2 * SEG
    y_p, y_s = _mix_ffn(xp, xs, dl, mod, ys, yf_p.reshape(nb * sl, FFT_WIDTH), yf_s, gs, gf, wts, tm2)
    return (y_p.reshape(nb, sl, D_MODEL), y_s.reshape(db, dl, D_MODEL), new_state)
```

```python
import functools
import math

import numpy as np
import jax
import jax.numpy as jnp
from jax import lax
from jax.experimental import pallas as pl
from jax.experimental.pallas import tpu as pltpu

F32 = jnp.float32
BF16 = jnp.bfloat16

D_MODEL = 1024
S5_WIDTH = 512
S5_GROUPS = 32
S5_GROUP = 16
S5_STATE = 64
FFT_WIDTH = 512
FFT_GROUPS = 4
FFT_GROUP = 128
D_FF = 2816
N_MOD = 6
EPS = 1e-6

LANES = 128
CHUNK = 16
SEG_CHUNKS = 16
SEG = CHUNK * SEG_CHUNKS
GP = S5_GROUPS * S5_STATE
PAIRS = S5_GROUPS // 2
CH = CHUNK * S5_GROUP
OCT = LANES // S5_GROUP
PITCH = 24

MXU_DIM = 256
FF_SPLITS = (0, 3 * MXU_DIM, 6 * MXU_DIM, 9 * MXU_DIM, D_FF)
VMEM_LIMIT = 58 * 1024 * 1024


def _cparams(sem):
    return pltpu.CompilerParams(dimension_semantics=sem, vmem_limit_bytes=VMEM_LIMIT)


def _row_block_cast_specs(w, n_steps):
    rows = w.shape[0] // n_steps
    assert rows * n_steps == w.shape[0] and rows % 16 == 0, (w.shape, n_steps)
    spec = pl.BlockSpec((rows, w.shape[1]), lambda i: (jnp.minimum(i, n_steps - 1), 0))
    return spec, jax.ShapeDtypeStruct(w.shape, BF16)


def _cast_blocks(in_refs, out_refs):
    for src, dst in zip(in_refs, out_refs):
        dst[...] = src[...].astype(BF16)


def _ada_tile(cctx_ref, c_ref, w_ref, b_ref, o_ref):
    dot = functools.partial(jnp.dot, preferred_element_type=F32)
    n_pad = o_ref.shape[0] - 1 - c_ref.shape[0]
    c = jnp.concatenate([cctx_ref[...], c_ref[...], jnp.zeros((n_pad, c_ref.shape[1]), F32)], axis=0)
    s = c * jax.nn.sigmoid(c)
    s_hi = s.astype(BF16)
    s_lo = (s - s_hi.astype(F32)).astype(BF16)
    w = w_ref[...]
    w_hi = w.astype(BF16)
    w_lo = (w - w_hi.astype(F32)).astype(BF16)
    r = dot(jnp.concatenate([s_hi, s_lo], axis=0), w_hi)
    n = s.shape[0]
    o_ref[...] = r[0:n] + r[n:2 * n] + dot(s_hi, w_lo) + b_ref[...]


ROW_LAM_RE, ROW_LAM_IM, ROW_STEP, ROW_B_RE = 0, 2, 4, 8
ROW_B_IM = ROW_B_RE + 2 * S5_GROUP
ROW_C_RE = ROW_B_IM + 2 * S5_GROUP
ROW_C_IM = ROW_C_RE + 2 * S5_GROUP
PARAM_ROWS = ROW_C_IM + 2 * S5_GROUP


def _param_rows(p_ref, row0, d, lanes=slice(None)):
    return p_ref[row0 + d * S5_GROUP:row0 + (d + 1) * S5_GROUP, lanes]


def _exact_bf16_terms(x):
    hi = x.astype(BF16)
    r = x - hi.astype(F32)
    mid = r.astype(BF16)
    return hi, mid, (r - mid.astype(F32)).astype(BF16)


def _s5_params_to_lanes(lam_re_ref, lam_im_ref, lstep_ref, b_re_ref, b_im_ref, c_re_ref, c_im_ref, d_ref, ind_ref,
                        p_scr, dt_scr):
    dot = functools.partial(jnp.dot, preferred_element_type=F32)
    p_scr[ROW_STEP + 2:ROW_B_RE, :] = jnp.zeros((ROW_B_RE - ROW_STEP - 2, GP), F32)
    p_scr[ROW_STEP:ROW_STEP + 2, :] = sum(dot(t, ind_ref[...]) for t in _exact_bf16_terms(lstep_ref[...]))
    for d in range(2):
        for row0, src in ((ROW_LAM_RE, lam_re_ref), (ROW_LAM_IM, lam_im_ref)):
            for q in range(PAIRS):
                p_scr[row0 + d:row0 + d + 1, q * LANES:(q + 1) * LANES] = jnp.concatenate(
                    [src[d, 2 * q:2 * q + 1, :], src[d, 2 * q + 1:2 * q + 2, :]], axis=1)
        for row0, src in ((ROW_B_RE, b_re_ref), (ROW_B_IM, b_im_ref), (ROW_C_RE, c_re_ref), (ROW_C_IM, c_im_ref)):
            for q in range(PAIRS):
                p_scr[row0 + d * S5_GROUP:row0 + (d + 1) * S5_GROUP, q * LANES:(q + 1) * LANES] = jnp.concatenate(
                    [src[d, 2 * q], src[d, 2 * q + 1]], axis=1)
    own = lax.broadcasted_iota(jnp.int32, (S5_GROUPS, S5_WIDTH), 1) // S5_GROUP \
        == lax.broadcasted_iota(jnp.int32, (S5_GROUPS, S5_WIDTH), 0)
    d_rows = jnp.where(own, d_ref[...], 0.0)
    spread = (lax.broadcasted_iota(jnp.int32, (S5_WIDTH, LANES), 0) % S5_GROUP
              == lax.broadcasted_iota(jnp.int32, (S5_WIDTH, LANES), 1) % S5_GROUP).astype(BF16)
    dt_scr[...] = sum(dot(t, spread) for t in _exact_bf16_terms(d_rows))


def _s5_discretise(p_ref, apre_ref, apim_ref, gb_scr, ca_scr):
    lre = p_ref[ROW_LAM_RE:ROW_LAM_RE + 2, :]
    lim = p_ref[ROW_LAM_IM:ROW_LAM_IM + 2, :]
    step = jnp.exp(p_ref[ROW_STEP:ROW_STEP + 2, :])
    mag = jnp.exp(lre * step)
    are = mag * jnp.cos(lim * step)
    aim = mag * jnp.sin(lim * step)
    nr = are - 1.0
    den = lre * lre + lim * lim
    fr = (nr * lre + aim * lim) / den
    fi = (aim * lre - nr * lim) / den

    pr = [jnp.ones_like(are)]
    pi = [jnp.zeros_like(are)]
    for _ in range(CHUNK):
        r, i = pr[-1], pi[-1]
        pr.append(r * are - i * aim)
        pi.append(r * aim + i * are)
    a16r, a16i = pr[CHUNK], pi[CHUNK]
    qr, qi = jnp.ones_like(are), jnp.zeros_like(are)
    for c in range(SEG_CHUNKS + 1):
        apre_ref[c] = qr
        apim_ref[c] = qi
        qr, qi = qr * a16r - qi * a16i, qr * a16i + qi * a16r

    for d in range(2):
        bre, bim = _param_rows(p_ref, ROW_B_RE, d), _param_rows(p_ref, ROW_B_IM, d)
        bbre = fr[d:d + 1] * bre - fi[d:d + 1] * bim
        bbim = fr[d:d + 1] * bim + fi[d:d + 1] * bre
        cre, cim = _param_rows(p_ref, ROW_C_RE, d), _param_rows(p_ref, ROW_C_IM, d)
        for s in range(CHUNK):
            k = CHUNK - 1 - s if d == 0 else s
            r, i = pr[k][d:d + 1], pi[k][d:d + 1]
            gb_scr[0, d, s] = r * bbre - i * bbim
            gb_scr[1, d, s] = r * bbim + i * bbre
            f = s + 1 if d == 0 else CHUNK - s
            r, i = pr[f][d:d + 1], pi[f][d:d + 1]
            ca_scr[0, d, s] = r * cre - i * cim
            ca_scr[1, d, s] = -(r * cim + i * cre)


def _s5_pair_operands(q, slot_q, p_ref, dt_ref, gb_scr, ca_scr, ts_scr, m_ref, w_ref, e_ref):
    dot_nt = lambda a, b: lax.dot_general(a, b, (((1,), (1,)), ((), ())), preferred_element_type=F32)
    lane_gi = lax.broadcasted_iota(jnp.int32, (CH, LANES), 1) // S5_STATE
    row_gi = lax.broadcasted_iota(jnp.int32, (LANES, CH), 0) // S5_STATE
    eye = (lax.broadcasted_iota(jnp.int32, (LANES, LANES), 0)
           == lax.broadcasted_iota(jnp.int32, (LANES, LANES), 1)).astype(BF16)
    slot = lax.broadcasted_iota(jnp.int32, (CH, LANES), 1) // S5_GROUP
    n_lag_rows = (2 * CHUNK - 1) * S5_GROUP
    lanes = pl.ds(pl.multiple_of(q * LANES, LANES), LANES)
    for d in range(2):
        for ri in range(2):
            col = slice((2 * d + ri) * LANES, (2 * d + ri + 1) * LANES)
            gb = gb_scr[ri, d, :, :, lanes].reshape(CH, LANES)
            ca = ca_scr[ri, d, :, :, lanes].reshape(CH, LANES).astype(BF16)
            ca_t = dot_nt(eye, ca)
            for gi in range(2):
                w_ref[slot_q, gi * CH:(gi + 1) * CH, col] = jnp.where(lane_gi == gi, gb, 0.0).astype(BF16)
                e_ref[slot_q, col, gi * CH:(gi + 1) * CH] = jnp.where(row_gi == gi, ca_t, 0.0).astype(BF16)
    for gi in range(2):
        g = 2 * q + gi
        lag = []
        for d in range(2):
            c_re = jnp.concatenate([_param_rows(p_ref, ROW_C_RE, d, lanes)] * OCT, axis=0)
            c_imn = jnp.concatenate([-_param_rows(p_ref, ROW_C_IM, d, lanes)] * OCT, axis=0)
            keep = lax.broadcasted_iota(jnp.int32, (LANES, LANES), 1) // S5_STATE == gi
            c_re = jnp.where(keep, c_re, 0.0).astype(BF16)
            c_imn = jnp.where(keep, c_imn, 0.0).astype(BF16)
            gre = gb_scr[0, d, :, :, lanes].reshape(CH, LANES).astype(BF16)
            gim = gb_scr[1, d, :, :, lanes].reshape(CH, LANES).astype(BF16)
            lag.append(dot_nt(gre, c_re) + dot_nt(gim, c_imn))
        zl = (CHUNK - 1) * S5_GROUP
        ts_scr[0:zl, :] = lag[0][0:zl]
        on_diag = (lax.broadcasted_iota(jnp.int32, (S5_GROUP, LANES), 1) % S5_GROUP
                   == lax.broadcasted_iota(jnp.int32, (S5_GROUP, LANES), 0))
        skip = jnp.where(on_diag, dt_ref[pl.ds(g, 1), :], 0.0)
        ts_scr[zl:zl + S5_GROUP, :] = lag[0][zl:] + lag[1][0:S5_GROUP] + skip
        ts_scr[zl + S5_GROUP:n_lag_rows, :] = lag[1][S5_GROUP:]
        for hf in range(CHUNK // OCT):
            acc = None
            for s in range(OCT):
                t = OCT * hf + s
                win = ts_scr[(CHUNK - 1 - t) * S5_GROUP:(CHUNK - 1 - t) * S5_GROUP + CH, :]
                acc = win if acc is None else jnp.where(slot == s, win, acc)
            m_ref[2 * slot_q + gi, :, hf * LANES:(hf + 1) * LANES] = acc.astype(BF16)


PAIRS_PER_STEP = 2
WADA_SLOTS = 3


def _ada_tables_kernel(cctx_ref, c_ref, wada_ref, bada_ref, win_ref, lam_re_ref, lam_im_ref, lstep_ref, b_re_ref, b_im_ref,
                       c_re_ref, c_im_ref, d_ref, ind_ref, mod_ref, winb_ref, apre_ref, apim_ref, m_ref, w_ref, e_ref,
                       p_scr, dt_scr, gb_scr, ca_scr, ts_scr, wada_buf, wada_sem):
    j = pl.program_id(0)
    n_steps = pl.num_programs(0)
    tn = wada_buf.shape[2]

    def tile_copy(step):
        slot = step % WADA_SLOTS
        cols = pl.ds(pl.multiple_of(step * tn, LANES), tn)
        return pltpu.make_async_copy(wada_ref.at[:, cols], wada_buf.at[slot], wada_sem.at[slot])

    @pl.when(j == 0)
    def _():
        for s in range(WADA_SLOTS - 1):
            tile_copy(s).start()

    @pl.when(j + WADA_SLOTS - 1 < n_steps)
    def _():
        tile_copy(j + WADA_SLOTS - 1).start()

    tile_copy(j).wait()
    _cast_blocks([win_ref], [winb_ref])
    _ada_tile(cctx_ref, c_ref, wada_buf.at[j % WADA_SLOTS], bada_ref, mod_ref)

    @pl.when(j == 0)
    def _():
        _s5_params_to_lanes(lam_re_ref, lam_im_ref, lstep_ref, b_re_ref, b_im_ref, c_re_ref, c_im_ref, d_ref, ind_ref,
                            p_scr, dt_scr)
        _s5_discretise(p_scr, apre_ref, apim_ref, gb_scr, ca_scr)

    for lp in range(PAIRS_PER_STEP):
        _s5_pair_operands(PAIRS_PER_STEP * j + lp, lp, p_scr, dt_scr, gb_scr, ca_scr, ts_scr, m_ref, w_ref, e_ref)


def _adaln_and_s5_tables(c_ctx, c, w_ada, b_ada, w_in, lam_re, lam_im, log_step, b_re, b_im, c_re, c_im, s5_d):
    ind_np = np.zeros((S5_GROUPS, GP), np.float32)
    ind_np[np.arange(GP) // S5_STATE, np.arange(GP)] = 1.0
    s5_in = (lam_re, lam_im, log_step, jnp.swapaxes(b_re, -1, -2), jnp.swapaxes(b_im, -1, -2), c_re, c_im,
             s5_d.reshape(1, S5_WIDTH), jnp.asarray(ind_np).astype(BF16))

    n_steps = PAIRS // PAIRS_PER_STEP
    n_out = w_ada.shape[1]
    tn = n_out // n_steps
    cast_spec, cast_shape = _row_block_cast_specs(w_in, n_steps)
    whole = lambda a: pl.BlockSpec(a.shape, lambda j, nd=a.ndim: (0,) * nd)
    powers = jax.ShapeDtypeStruct((SEG_CHUNKS + 1, 2, GP), F32)
    tab = (2, 2, CHUNK, S5_GROUP, GP)
    mod, w_in_b, apre, apim, m, w, e = pl.pallas_call(
        _ada_tables_kernel,
        grid=(n_steps,),
        in_specs=[whole(c_ctx), whole(c),
                  pl.BlockSpec(memory_space=pl.ANY),
                  pl.BlockSpec((1, tn), lambda j: (0, j)),
                  cast_spec] + [whole(a) for a in s5_in],
        out_specs=[pl.BlockSpec((8, tn), lambda j: (0, j)), cast_spec, whole(powers), whole(powers),
                   pl.BlockSpec((2 * PAIRS_PER_STEP, CH, CH), lambda j: (j, 0, 0)),
                   pl.BlockSpec((PAIRS_PER_STEP, 2 * CH, 4 * LANES), lambda j: (j, 0, 0)),
                   pl.BlockSpec((PAIRS_PER_STEP, 4 * LANES, 2 * CH), lambda j: (j, 0, 0))],
        out_shape=[jax.ShapeDtypeStruct((8, n_out), F32), cast_shape, powers, powers,
                   jax.ShapeDtypeStruct((S5_GROUPS, CH, CH), BF16),
                   jax.ShapeDtypeStruct((PAIRS, 2 * CH, 4 * LANES), BF16),
                   jax.ShapeDtypeStruct((PAIRS, 4 * LANES, 2 * CH), BF16)],
        scratch_shapes=[pltpu.VMEM((PARAM_ROWS, GP), F32),
                        pltpu.VMEM((S5_GROUPS, LANES), F32),
                        pltpu.VMEM(tab, F32),
                        pltpu.VMEM(tab, F32),
                        pltpu.VMEM(((2 * CHUNK - 1) * S5_GROUP, LANES), F32),
                        pltpu.VMEM((WADA_SLOTS, D_MODEL, tn), F32),
                        pltpu.SemaphoreType.DMA((WADA_SLOTS,))],
        compiler_params=_cparams(("arbitrary",)),
        name="adaln_s5_tables",
    )(c_ctx, c, w_ada, b_ada.reshape(1, n_out), w_in, *s5_in)
    return mod, w_in_b, m, w, e, apre, apim


def _rms(x, g):
    return x * lax.rsqrt(jnp.mean(x * x, axis=-1, keepdims=True) + EPS) * g


def _slot_transpose(v):
    v = list(v)
    slot = lax.broadcasted_iota(jnp.int32, v[0].shape, 1) // S5_GROUP
    for k in (4, 2, 1):
        low = (slot & k) == 0
        for i in range(OCT):
            if i & k:
                continue
            a, b = v[i], v[i + k]
            v[i] = jnp.where(low, a, pltpu.roll(b, k * S5_GROUP, axis=1))
            v[i + k] = jnp.where(low, pltpu.roll(a, LANES - k * S5_GROUP, axis=1), b)
    return v


def _inproj_kernel(n_prompt_tiles, tiles_per_seq, n_cast, xp_ref, xs_ref, mod_ref, g_ref, win_ref, bdc_ref, bds_ref,
                   cm_ref, sm_ref, *rest):
    cast_in, rest = rest[:n_cast], rest[n_cast:]
    xg_ref, fc_ref, fs_ref, gs_ref, gf_ref, yfp_ref = rest[:6]
    cast_out, (h_scr, hb_scr, zs_scr) = rest[6:6 + n_cast], rest[6 + n_cast:]
    tm = xp_ref.shape[0]
    mod = _mod_vectors(mod_ref, n_prompt_tiles, tiles_per_seq)
    is_prompt = pl.program_id(0) < n_prompt_tiles
    dot = functools.partial(jnp.dot, preferred_element_type=F32)
    o = 2 * S5_WIDTH
    tiles = [slice(j * SEG, (j + 1) * SEG) for j in range(tm // SEG)]

    def norm_phase(j, r):
        x = jnp.where(is_prompt, xp_ref[r, :], xs_ref[r, :])
        h = _rms(x, g_ref[...]) * (1.0 + mod[1]) + mod[0]
        for k in range(D_MODEL // LANES):
            for c in range(SEG_CHUNKS):
                p0 = (j * SEG_CHUNKS + c) * PITCH
                h_scr[k, p0:p0 + CHUNK, :] = h[c * CHUNK:(c + 1) * CHUNK, k * LANES:(k + 1) * LANES]
        for s in range(CHUNK):
            r0 = j * SEG + s * SEG_CHUNKS
            for k in range(D_MODEL // LANES):
                hb_scr[r0:r0 + SEG_CHUNKS, k * LANES:(k + 1) * LANES] = (
                    h_scr[k, pl.ds(j * SEG_CHUNKS * PITCH + s, SEG_CHUNKS, stride=PITCH), :].astype(BF16))

    def s5_gate_phase(j, r):
        zs_scr[r, :] = dot(hb_scr[r, :], win_ref[:, 0:S5_WIDTH])
        gs_ref[r, :] = jax.nn.sigmoid(dot(hb_scr[r, :], win_ref[:, o:o + D_MODEL])).astype(BF16)

    def fold_phase(j, r):
        for b in range(S5_WIDTH // LANES):
            for hf in range(CHUNK // OCT):
                z = [zs_scr[j * SEG + s * SEG_CHUNKS:j * SEG + (s + 1) * SEG_CHUNKS, b * LANES:(b + 1) * LANES]
                     for s in range(OCT * hf, OCT * (hf + 1))]
                for i, xi in enumerate(_slot_transpose(z)):
                    xg_ref[OCT * b + i, j * SEG_CHUNKS:(j + 1) * SEG_CHUNKS,
                           hf * LANES:(hf + 1) * LANES] = xi.astype(BF16)

    def fourier_gate_phase(j, r):
        uf = dot(hb_scr[r, :], win_ref[:, S5_WIDTH:2 * S5_WIDTH]).astype(BF16)
        for n0 in range(0, FFT_WIDTH, MXU_DIM):
            cols = slice(n0, n0 + MXU_DIM)
            fc_ref[r, cols] = dot(uf[:, cols], bdc_ref[cols, cols]).astype(BF16)
            fs_ref[r, cols] = dot(uf[:, cols], bds_ref[cols, cols]).astype(BF16)
        gf_ref[r, :] = jax.nn.sigmoid(dot(hb_scr[r, :], win_ref[:, o + D_MODEL:o + 2 * D_MODEL])).astype(BF16)

    for phase in (norm_phase, s5_gate_phase, fold_phase, fourier_gate_phase):
        for j, r in enumerate(tiles):
            phase(j, r)
    _cast_blocks(cast_in, cast_out)

    @pl.when(is_prompt)
    def _():
        cm, sm = cm_ref[...].astype(BF16), sm_ref[...].astype(BF16)
        for r in tiles:
            yfp_ref[r, :] = (dot(cm, fc_ref[r, :]) - dot(sm, fs_ref[r, :])).astype(BF16)


def _channel_dft_mats():
    j = np.arange(FFT_GROUP)
    ang = 2.0 * np.pi * ((j[:, None] * j[None, :]) % FFT_GROUP) / FFT_GROUP
    blk_c = np.cos(ang) / math.sqrt(FFT_GROUP)
    blk_s = np.sin(ang) / math.sqrt(FFT_GROUP)
    bdc = np.kron(np.eye(FFT_GROUPS), blk_c)
    bds = np.kron(np.eye(FFT_GROUPS), blk_s)
    return jnp.asarray(bdc, F32).astype(BF16), jnp.asarray(bds, F32).astype(BF16)


def _const_spec(shape):
    nd = len(shape)
    return pl.BlockSpec(shape, lambda i: (0,) * nd, pipeline_mode=pl.Buffered(1))


def _two_part_specs(tm, width, n_prompt_tiles):
    return [pl.BlockSpec((tm, width), lambda i: (jnp.minimum(i, n_prompt_tiles - 1), 0)),
            pl.BlockSpec((tm, width), lambda i: (jnp.maximum(i - n_prompt_tiles, 0), 0))]


MOD_SPEC = pl.BlockSpec((8, N_MOD * D_MODEL), lambda i: (0, 0))


def _mod_vectors(mod_ref, n_prompt_tiles, tiles_per_seq):
    i = pl.program_id(0)
    row = jnp.where(i < n_prompt_tiles, 0, 1 + (i - n_prompt_tiles) // tiles_per_seq)
    full = mod_ref[pl.ds(row, 1), :]
    return [full[:, k * D_MODEL:(k + 1) * D_MODEL] for k in range(N_MOD)]


N_CAST_STEPS = 16


def _inproj(xp, xs, seq_len, mod, norm_g, w_in_b, later_weights, tm):
    t = xp.shape[0] + xs.shape[0]
    n_p = xp.shape[0] // tm
    assert t // tm >= N_CAST_STEPS
    bdc, bds = _channel_dft_mats()
    cm, sm = _tile_dft_tables(1.0 / math.sqrt(SEG))
    casts = [_row_block_cast_specs(w, N_CAST_STEPS) for w in later_weights]
    tok = lambda w: pl.BlockSpec((tm, w), lambda i: (i, 0))
    out = lambda w: jax.ShapeDtypeStruct((t, w), BF16)
    return pl.pallas_call(
        functools.partial(_inproj_kernel, n_p, seq_len // tm, len(casts)),
        grid=(t // tm,),
        in_specs=_two_part_specs(tm, D_MODEL, n_p) + [
                  MOD_SPEC,
                  _const_spec((1, D_MODEL)),
                  _const_spec(w_in_b.shape),
                  _const_spec(bdc.shape),
                  _const_spec(bds.shape), _const_spec(cm.shape), _const_spec(sm.shape)] + [c[0] for c in casts],
        out_specs=[pl.BlockSpec((S5_GROUPS, tm // CHUNK, CH), lambda i: (0, i, 0)),
                   tok(FFT_WIDTH), tok(FFT_WIDTH), tok(D_MODEL), tok(D_MODEL),
                   _two_part_specs(tm, FFT_WIDTH, n_p)[0]] + [c[0] for c in casts],
        out_shape=[jax.ShapeDtypeStruct((S5_GROUPS, t // CHUNK, CH), BF16),
                   out(FFT_WIDTH), out(FFT_WIDTH), out(D_MODEL), out(D_MODEL),
                   jax.ShapeDtypeStruct((xp.shape[0], FFT_WIDTH), BF16)] + [c[1] for c in casts],
        scratch_shapes=[pltpu.VMEM((D_MODEL // LANES, tm // CHUNK * PITCH, LANES), F32),
                        pltpu.VMEM((tm, D_MODEL), BF16),
                        pltpu.VMEM((tm, S5_WIDTH), F32)],
        compiler_params=_cparams(("arbitrary",)),
        name="inproj",
    )(xp, xs, mod, norm_g.reshape(1, D_MODEL), w_in_b, bdc, bds, cm, sm, *later_weights)


N_PROMPT_SEG = 16
N_SAMPLE_SEG = 32
N_SAMPLE_SEQ = 2
N_SEG = N_PROMPT_SEG + N_SAMPLE_SEG
ROWS = SEG_CHUNKS * N_SEG
ROWS_P = SEG_CHUNKS * N_PROMPT_SEG
OCT_PAIRS = OCT // 2
SEG_PITCH = 56


def _cmul_add(ar, ai, hr, hi, sr, si):
    return ar * hr - ai * hi + sr, ar * hi + ai * hr + si


def _s5_kernel(x_ref, m_ref, w_ref, e_ref, apre_ref, apim_ref, h0_ref,
               y_ref, fin_ref, s_scr, hin_scr, hinp_scr, f_scr, hs_scr, y_scr):
    dot = functools.partial(jnp.dot, preferred_element_type=F32)
    parts = ((0, ROWS_P), (ROWS_P, ROWS - ROWS_P))
    seg_rows = lambda c: pl.ds(c, N_SEG, stride=PITCH)
    blk = lambda c: pl.ds(c * SEG_PITCH, N_SEG)
    seq_rows = lambda j: pl.ds(j, N_SAMPLE_SEQ, stride=SEG_CHUNKS)
    lat = lambda c: pl.ds(c * SEG_PITCH + N_PROMPT_SEG, N_SAMPLE_SEG)
    zero = jnp.zeros((N_SEG, LANES), F32)
    octet = pl.program_id(0)

    def entry_state(k, pr):
        g0 = 2 * (octet * OCT_PAIRS + pr)
        rows = [jnp.concatenate([h0_ref[pl.ds(n * 4 * S5_GROUPS + k * S5_GROUPS + g0 + gi, 1), :] for gi in range(2)],
                                axis=1) for n in range(N_SAMPLE_SEQ)]
        return jnp.concatenate(rows, axis=0)

    def store_final(k, pr, v):
        for gi in range(2):
            fin_ref[:, k, 2 * pr + gi, :] = v[0:N_PROMPT_SEG, gi * S5_STATE:(gi + 1) * S5_STATE]

    for pr in range(OCT_PAIRS):
        ln = slice(pr * LANES, (pr + 1) * LANES)
        g0, g1 = 2 * pr, 2 * pr + 1
        for r0, nr in parts:
            s = (dot(x_ref[g0, r0:r0 + nr, :], w_ref[pr, 0:CH, :])
                 + dot(x_ref[g1, r0:r0 + nr, :], w_ref[pr, CH:2 * CH, :]))
            for k in range(4):
                for sg in range(nr // SEG_CHUNKS):
                    p0 = (r0 // SEG_CHUNKS + sg) * PITCH
                    s_scr[k, p0:p0 + SEG_CHUNKS, :] = s[sg * SEG_CHUNKS:(sg + 1) * SEG_CHUNKS, k * LANES:(k + 1) * LANES]

        ar, ai = apre_ref[1, 0:1, ln], apim_ref[1, 0:1, ln]
        hr, hi = zero, zero
        for c in range(SEG_CHUNKS):
            hin_scr[0, blk(c), :] = hr
            hin_scr[1, blk(c), :] = hi
            hr, hi = _cmul_add(ar, ai, hr, hi, s_scr[0, seg_rows(c), :], s_scr[1, seg_rows(c), :])
        store_final(0, pr, hr)
        store_final(1, pr, hi)
        f_scr[0] = hr[N_PROMPT_SEG:]
        f_scr[1] = hi[N_PROMPT_SEG:]
        br, bi = apre_ref[1, 1:2, ln], apim_ref[1, 1:2, ln]
        gr, gi = zero, zero
        for c in range(SEG_CHUNKS - 1, -1, -1):
            hin_scr[2, blk(c), :] = gr
            hin_scr[3, blk(c), :] = gi
            gr, gi = _cmul_add(br, bi, gr, gi, s_scr[2, seg_rows(c), :], s_scr[3, seg_rows(c), :])
        store_final(2, pr, gr)
        store_final(3, pr, gi)
        f_scr[2] = gr[N_PROMPT_SEG:]
        f_scr[3] = gi[N_PROMPT_SEG:]

        a2r, a2i = apre_ref[SEG_CHUNKS, 0:1, ln], apim_ref[SEG_CHUNKS, 0:1, ln]
        hr, hi = entry_state(0, pr), entry_state(1, pr)
        for j in range(SEG_CHUNKS):
            hs_scr[0, seq_rows(j), :] = hr
            hs_scr[1, seq_rows(j), :] = hi
            hr, hi = _cmul_add(a2r, a2i, hr, hi, f_scr[0, seq_rows(j), :], f_scr[1, seq_rows(j), :])
        b2r, b2i = apre_ref[SEG_CHUNKS, 1:2, ln], apim_ref[SEG_CHUNKS, 1:2, ln]
        gr, gi = entry_state(2, pr), entry_state(3, pr)
        for j in range(SEG_CHUNKS - 1, -1, -1):
            hs_scr[2, seq_rows(j), :] = gr
            hs_scr[3, seq_rows(j), :] = gi
            gr, gi = _cmul_add(b2r, b2i, gr, gi, f_scr[2, seq_rows(j), :], f_scr[3, seq_rows(j), :])
        for c in range(SEG_CHUNKS):
            p_r, p_i = apre_ref[c, 0:1, ln], apim_ref[c, 0:1, ln]
            hr, hi = _cmul_add(p_r, p_i, hs_scr[0], hs_scr[1], hin_scr[0, lat(c), :], hin_scr[1, lat(c), :])
            hin_scr[0, lat(c), :] = hr
            hin_scr[1, lat(c), :] = hi
            cb = SEG_CHUNKS - 1 - c
            p_r, p_i = apre_ref[cb, 1:2, ln], apim_ref[cb, 1:2, ln]
            gr, gi = _cmul_add(p_r, p_i, hs_scr[2], hs_scr[3], hin_scr[2, lat(c), :], hin_scr[3, lat(c), :])
            hin_scr[2, lat(c), :] = gr
            hin_scr[3, lat(c), :] = gi

        for k in range(4):
            for sg in range(N_SEG):
                hinp_scr[sg * SEG_CHUNKS:(sg + 1) * SEG_CHUNKS, k * LANES:(k + 1) * LANES] = (
                    hin_scr[k, pl.ds(sg, SEG_CHUNKS, stride=SEG_PITCH), :].astype(BF16))
        for gi_, g in ((0, g0), (1, g1)):
            for r0, nr in parts:
                y_scr[g, r0:r0 + nr, :] = (dot(x_ref[g, r0:r0 + nr, :], m_ref[g])
                                           + dot(hinp_scr[r0:r0 + nr, :], e_ref[pr, :, gi_ * CH:(gi_ + 1) * CH]))

    rb_rows = 2 * SEG_CHUNKS
    for rb in range(ROWS // rb_rows):
        for hf in range(CHUNK // OCT):
            v = [y_scr[i, rb * rb_rows:(rb + 1) * rb_rows, hf * LANES:(hf + 1) * LANES] for i in range(OCT)]
            for s, acc in enumerate(_slot_transpose(v)):
                t = OCT * hf + s
                for sg in range(rb_rows // SEG_CHUNKS):
                    seg = rb * (rb_rows // SEG_CHUNKS) + sg
                    r0 = seg * SEG + t * SEG_CHUNKS
                    y_ref[0, r0:r0 + SEG_CHUNKS, :] = acc[sg * SEG_CHUNKS:(sg + 1) * SEG_CHUNKS]


def _s5_conv(xg, m, w, e, apre, apim, h0):
    n_oct = S5_GROUPS // OCT
    assert xg.shape[1] == ROWS
    nt = ROWS * CHUNK
    return pl.pallas_call(
        _s5_kernel,
        grid=(n_oct,),
        in_specs=[pl.BlockSpec((OCT, ROWS, CH), lambda o: (o, 0, 0)),
                  pl.BlockSpec((OCT, CH, CH), lambda o: (o, 0, 0)),
                  pl.BlockSpec((OCT_PAIRS, 2 * CH, 4 * LANES), lambda o: (o, 0, 0)),
                  pl.BlockSpec((OCT_PAIRS, 4 * LANES, 2 * CH), lambda o: (o, 0, 0)),
                  pl.BlockSpec((SEG_CHUNKS + 1, 2, OCT_PAIRS * LANES), lambda o: (0, 0, o)),
                  pl.BlockSpec((SEG_CHUNKS + 1, 2, OCT_PAIRS * LANES), lambda o: (0, 0, o)),
                  pl.BlockSpec((N_SAMPLE_SEQ * 4 * S5_GROUPS, S5_STATE), lambda o: (0, 0))],
        out_specs=[pl.BlockSpec((1, nt, LANES), lambda o: (o, 0, 0)),
                   pl.BlockSpec((N_PROMPT_SEG, 4, OCT, S5_STATE), lambda o: (0, 0, o, 0))],
        out_shape=[jax.ShapeDtypeStruct((n_oct, nt, LANES), F32),
                   jax.ShapeDtypeStruct((N_PROMPT_SEG, 4, S5_GROUPS, S5_STATE), F32)],
        scratch_shapes=[pltpu.VMEM((4, N_SEG * PITCH, LANES), F32),
                        pltpu.VMEM((4, SEG_CHUNKS * SEG_PITCH, LANES), F32),
                        pltpu.VMEM((ROWS, 4 * LANES), BF16),
                        pltpu.VMEM((4, N_SAMPLE_SEG, LANES), F32),
                        pltpu.VMEM((4, N_SAMPLE_SEG, LANES), F32),
                        pltpu.VMEM((OCT, ROWS, CH), F32)],
        compiler_params=_cparams(("parallel",)),
        name="s5_conv",
    )(xg, m, w, e, apre, apim, h0)


def _tile_order_positions(l):
    r = np.arange(l)
    tile, wi = r // SEG, r % SEG
    return tile * SEG + (wi % SEG_CHUNKS) * CHUNK + wi // SEG_CHUNKS


def _tile_dft_tables(scale, rows_in_tile_order=True):
    pos = _tile_order_positions(SEG)
    freq = pos if rows_in_tile_order else np.arange(SEG)
    ang = 2.0 * np.pi * ((freq[:, None] * pos[None, :]) % SEG) / SEG
    return jnp.asarray(np.cos(ang) * scale, F32), jnp.asarray(np.sin(ang) * scale, F32)


N_TILES = 16


def _fft16(xr, xi):
    n = len(xr)
    rev = [int(format(i, "04b")[::-1], 2) for i in range(n)]
    ar = [xr[r] for r in rev]
    ai = [xi[r] for r in rev]
    size = 2
    while size <= n:
        half = size // 2
        for start in range(0, n, size):
            for k in range(half):
                wr = math.cos(2.0 * math.pi * k / size)
                wi = -math.sin(2.0 * math.pi * k / size)
                i0, i1 = start + k, start + k + half
                if k == 0:
                    tr, ti = ar[i1], ai[i1]
                elif 4 * k == size:
                    tr, ti = ai[i1], -ar[i1]
                else:
                    tr = ar[i1] * wr - ai[i1] * wi
                    ti = ar[i1] * wi + ai[i1] * wr
                ar[i1], ai[i1] = ar[i0] - tr, ai[i0] - ti
                ar[i0], ai[i0] = ar[i0] + tr, ai[i0] + ti
        size *= 2
    return ar, ai


def _pos_dft_long_kernel(cb_ref, sb_ref, ca_ref, sa_ref, fc_ref, fs_ref, o_ref, a_scr):
    wdt = fc_ref.shape[2]

    def rows_fft(r, carry):
        rows = pl.ds(pl.multiple_of(r * CHUNK, CHUNK), CHUNK)
        for b in range(wdt // LANES):
            lns = slice(b * LANES, (b + 1) * LANES)
            xr = [fc_ref[j, rows, lns].astype(F32) for j in range(N_TILES)]
            xi = [-fs_ref[j, rows, lns].astype(F32) for j in range(N_TILES)]
            ar, ai = _fft16(xr, xi)
            for k in range(N_TILES):
                a_scr[k, 0, rows, lns] = ar[k].astype(BF16)
                a_scr[k, 1, rows, lns] = ai[k].astype(BF16)
        return carry

    lax.fori_loop(0, SEG // CHUNK, rows_fft, 0)

    dot = functools.partial(jnp.dot, preferred_element_type=F32)
    cb, sb = cb_ref[...], sb_ref[...]
    for k1 in range(N_TILES):
        ca, sa = ca_ref[k1:k1 + 1, :], sa_ref[k1:k1 + 1, :]
        dc = (cb * ca - sb * sa).astype(BF16)
        ds = (sb * ca + cb * sa).astype(BF16)
        out = dot(dc, a_scr[k1, 0]) + dot(ds, a_scr[k1, 1])
        o_ref[:, k1 * SEG_CHUNKS:(k1 + 1) * SEG_CHUNKS, :] = out.astype(BF16).reshape(N_TILES, SEG_CHUNKS, wdt)


def _pos_dft_long(fc, fs, first, n):
    wdt = fc.shape[-1]
    l = N_TILES * SEG
    wb = 2 * LANES
    cb, sb = _tile_dft_tables(1.0 / math.sqrt(l), rows_in_tile_order=False)
    pos = _tile_order_positions(SEG)
    ang = 2.0 * np.pi * (np.arange(N_TILES)[:, None] * pos[None, :]) / l
    ca, sa = jnp.asarray(np.cos(ang), F32), jnp.asarray(np.sin(ang), F32)
    const2 = lambda shape: pl.BlockSpec(shape, lambda b, k: (0, 0), pipeline_mode=pl.Buffered(1))
    seq_in = pl.BlockSpec((None, N_TILES, SEG, wb), lambda b, k: (first + b, 0, 0, k))
    seq_out = pl.BlockSpec((None, N_TILES, SEG, wb), lambda b, k: (b, 0, 0, k))
    out = pl.pallas_call(
        _pos_dft_long_kernel,
        grid=(n, wdt // wb),
        in_specs=[const2((SEG, SEG)), const2((SEG, SEG)), const2((N_TILES, SEG)), const2((N_TILES, SEG)),
                  seq_in, seq_in],
        out_specs=seq_out,
        out_shape=jax.ShapeDtypeStruct((n, N_TILES, SEG, wdt), BF16),
        scratch_shapes=[pltpu.VMEM((N_TILES, 2, SEG, wb), BF16)],
        compiler_params=_cparams(("parallel", "parallel")),
        name="pos_dft_long",
    )(cb, sb, ca, sa, fc, fs)
    return out.reshape(n * l, wdt)


def _mix_ffn_kernel(n_prompt_tiles, tiles_per_seq, xp_ref, xs_ref, mod_ref, ys_ref, yfp_ref, yfs_ref, gs_ref, gf_ref,
                    wglu_ref, bglu_ref, wps_ref, wpf_ref, wout_ref, n2_ref,
                    wg_ref, wu_ref, wd_ref, fn_ref, op_ref, os_ref, m_scr, mb_scr):
    tm = xp_ref.shape[0]
    dot = functools.partial(jnp.dot, preferred_element_type=F32)
    mod = _mod_vectors(mod_ref, n_prompt_tiles, tiles_per_seq)
    is_prompt = pl.program_id(0) < n_prompt_tiles
    tiles = [slice(j * SEG, (j + 1) * SEG) for j in range(tm // SEG)]
    st = [dict() for _ in tiles]

    def gelu_phase(j, r, s):
        y = jnp.concatenate([ys_ref[b, r, :] for b in range(S5_WIDTH // LANES)], axis=1)
        s["z"] = jax.nn.gelu(y)

    def glu_phase(j, r, s):
        z = s["z"]
        s["z"] = (z * jax.nn.sigmoid(dot(z.astype(BF16), wglu_ref[...]) + bglu_ref[...])).astype(BF16)

    def proj_phase(j, r, s):
        yf = jnp.where(is_prompt, yfp_ref[r, :], yfs_ref[r, :])
        m = (gs_ref[r, :].astype(F32) * dot(s.pop("z"), wps_ref[...])
             + gf_ref[r, :].astype(F32) * dot(yf, wpf_ref[...]))
        for k in range(D_MODEL // LANES):
            for t in range(CHUNK):
                p0 = (j * CHUNK + t) * PITCH
                m_scr[k, p0:p0 + SEG_CHUNKS, :] = m[t * SEG_CHUNKS:(t + 1) * SEG_CHUNKS, k * LANES:(k + 1) * LANES]
        for c in range(SEG_CHUNKS):
            r0 = j * SEG + c * CHUNK
            for k in range(D_MODEL // LANES):
                mb_scr[r0:r0 + CHUNK, k * LANES:(k + 1) * LANES] = (
                    m_scr[k, pl.ds(j * CHUNK * PITCH + c, CHUNK, stride=PITCH), :].astype(BF16))

    def out_phase(j, r, s):
        x = jnp.where(is_prompt, xp_ref[r, :], xs_ref[r, :])
        x1 = x + mod[2] * dot(mb_scr[r, :], wout_ref[...])
        s["x1"] = x1
        s["h2"] = (_rms(x1, n2_ref[...]) * (1.0 + mod[4]) + mod[3]).astype(BF16)

    def ffn_phase(n0, n1, j, r, s):
        gate = dot(s["h2"], wg_ref[:, n0:n1])
        up = dot(s["h2"], wu_ref[:, n0:n1])
        part = dot((gate * jax.nn.sigmoid(gate) * up).astype(BF16), wd_ref[n0:n1, :])
        s["ff"] = part if "ff" not in s else s["ff"] + part

    def final_phase(j, r, s):
        x2 = s.pop("x1") + mod[5] * s.pop("ff")
        s["res"] = _rms(x2, fn_ref[...])

    phases = [gelu_phase, glu_phase, proj_phase, out_phase]
    phases += [functools.partial(ffn_phase, n0, n1) for n0, n1 in zip(FF_SPLITS[:-1], FF_SPLITS[1:])]
    phases += [final_phase]
    for phase in phases:
        for j, r in enumerate(tiles):
            phase(j, r, st[j])

    @pl.when(is_prompt)
    def _():
        for j, r in enumerate(tiles):
            op_ref[r, :] = st[j]["res"]

    @pl.when(jnp.logical_not(is_prompt))
    def _():
        for j, r in enumerate(tiles):
            os_ref[r, :] = st[j]["res"]


def _mix_ffn(xp, xs, seq_len, mod, ys, yf_p, yf_s, gs, gf, wts, tm):
    tp, ts = xp.shape[0], xs.shape[0]
    n_p = tp // tm
    tok = lambda w: pl.BlockSpec((tm, w), lambda i: (i, 0))
    return pl.pallas_call(
        functools.partial(_mix_ffn_kernel, n_p, seq_len // tm),
        grid=((tp + ts) // tm,),
        in_specs=_two_part_specs(tm, D_MODEL, n_p)
                 + [MOD_SPEC,
                    pl.BlockSpec((S5_WIDTH // LANES, tm, LANES), lambda i: (0, i, 0))]
                 + _two_part_specs(tm, FFT_WIDTH, n_p)
                 + [tok(D_MODEL), tok(D_MODEL)]
                 + [_const_spec(w.shape) for w in wts],
        out_specs=_two_part_specs(tm, D_MODEL, n_p),
        out_shape=[jax.ShapeDtypeStruct((tp, D_MODEL), F32), jax.ShapeDtypeStruct((ts, D_MODEL), F32)],
        scratch_shapes=[pltpu.VMEM((D_MODEL // LANES, tm // CHUNK * PITCH, LANES), F32),
                        pltpu.VMEM((tm, D_MODEL), BF16)],
        compiler_params=_cparams(("arbitrary",)),
        name="mix_ffn",
    )(xp, xs, mod, ys, yf_p, yf_s, gs, gf, *wts)


def kernel(x_prompt, x_sample, state_s5, c, c_ctx, norm1_g, norm2_g, w_ada, b_ada, w_in,
           s5_lambda_re, s5_lambda_im, s5_log_step, s5_b_re, s5_b_im, s5_c_re, s5_c_im,
           s5_d, w_glu, b_glu, w_proj_s5, w_proj_fft, w_out, w_ffn_gate, w_ffn_up,
           w_ffn_down, final_norm_g):
    nb, sl, _ = x_prompt.shape
    db, dl, _ = x_sample.shape
    assert w_in.shape[0] == 1 and sl == SEG and nb == N_PROMPT_SEG
    assert db == N_SAMPLE_SEQ and dl == SEG * SEG_CHUNKS

    mod, w_in_b, m, w, e, apre, apim = _adaln_and_s5_tables(
        c_ctx[None], c, w_ada[0], b_ada[0], w_in[0], s5_lambda_re[0], s5_lambda_im[0], s5_log_step[0], s5_b_re[0],
        s5_b_im[0], s5_c_re[0], s5_c_im[0], s5_d)

    tm = 512
    xp = x_prompt.reshape(nb * sl, D_MODEL)
    xs = x_sample.reshape(db * dl, D_MODEL)
    later = (w_glu[0], w_proj_s5[0], w_proj_fft[0], w_out[0], w_ffn_gate[0], w_ffn_up[0], w_ffn_down[0])
    xg, fc, fs, gs, gf, yf_p, *later_b = _inproj(xp, xs, dl, mod, norm1_g[0], w_in_b, later, tm)
    wglu_b, wps_b, wpf_b, wout_b, wg_b, wu_b, wd_b = later_b

    ys, fin = _s5_conv(xg, m, w, e, apre, apim, state_s5.astype(F32).reshape(db * 4 * S5_GROUPS, S5_STATE))
    new_state = fin.reshape(nb, 1, 2, 2, S5_GROUPS, S5_STATE)

    n_prompt_units = nb // N_TILES
    yf_s = _pos_dft_long(fc.reshape(-1, N_TILES, SEG, FFT_WIDTH), fs.reshape(-1, N_TILES, SEG, FFT_WIDTH),
                         n_prompt_units, db)

    wts = (wglu_b, b_glu[0].reshape(1, S5_WIDTH), wps_b, wpf_b, wout_b, norm2_g[0].reshape(1, D_MODEL),
           wg_b, wu_b, wd_b, final_norm_g.reshape(1, D_MODEL))
    tm2 = 2 * SEG
    y_p, y_s = _mix_ffn(xp, xs, dl, mod, ys, yf_p.reshape(nb * sl, FFT_WIDTH), yf_s, gs, gf, wts, tm2)
    return (y_p.reshape(nb, sl, D_MODEL), y_s.reshape(db, dl, D_MODEL), new_state)
```
